```python
import math
import jax, jax.numpy as jnp
from jax import lax
import numpy as np

D_MODEL = 1024
BATCH = 8
SEQ = 2048
DEPTH = 1
DEC_BATCH = 128
DEC_SEQ = 1
PAST_LEN = 16384
PAGE_SIZE = 128

N_META = 16
LN_EPS = 1e-5
GLA_HEADS = 4
GLA_DK = (D_MODEL // 2) // GLA_HEADS
GLA_DV = D_MODEL // GLA_HEADS
GLA_RANK = 16
GLA_TAU = 16.0
GLA_CHUNK = 64
HEAD_DIM = 64
ATT_HEADS = D_MODEL // HEAD_DIM
ATT_KV_HEADS = 4
GQA_GROUP = ATT_HEADS // ATT_KV_HEADS
WINDOW = 128
ATT_BLOCK = 128
REL_BUCKETS = 32
REL_MAX_DIST = 128
N_GROUPS = 4
EXPERTS_PER_GROUP = 8
N_EXPERTS = N_GROUPS * EXPERTS_PER_GROUP
TOP_K = 2
D_EXPERT = D_MODEL // 2
MOE_BLOCK = 128
ALPHA = (2.0 * DEPTH) ** 0.25
BETA = (8.0 * DEPTH) ** -0.25
PROJ_SIZES = (GLA_HEADS * GLA_DK, GLA_HEADS * GLA_DK, GLA_HEADS * GLA_DV, GLA_HEADS * GLA_DV, GLA_RANK,
              ATT_HEADS * HEAD_DIM, ATT_KV_HEADS * HEAD_DIM, ATT_KV_HEADS * HEAD_DIM, D_MODEL, D_MODEL)
PROJ_WIDTH = sum(PROJ_SIZES)

kernel_name = "hybrid_gla_swa_sink_hmoe_step"


def layer_norm(x, g, b):
    xf = x.astype(jnp.float32)
    mu = jnp.mean(xf, -1, keepdims=True)
    var = jnp.mean(jnp.square(xf - mu), -1, keepdims=True)
    return ((xf - mu) * lax.rsqrt(var + LN_EPS) * g.astype(jnp.float32) + b.astype(jnp.float32)).astype(x.dtype)


def rel_bucket(dist):
    max_exact = REL_BUCKETS // 2
    d = jnp.maximum(dist, 0)
    large = max_exact + (jnp.log(jnp.maximum(d, 1).astype(jnp.float32) / max_exact)
                         / math.log(REL_MAX_DIST / max_exact) * (REL_BUCKETS - max_exact)).astype(jnp.int32)
    large = jnp.minimum(large, REL_BUCKETS - 1)
    return jnp.where(d < max_exact, d, large)


def rel_bias_for(dist, table):
    b = table[rel_bucket(dist)].astype(jnp.float32)
    b = b.reshape(dist.shape + (ATT_KV_HEADS, GQA_GROUP))
    return jnp.moveaxis(b, (-2, -1), (0, 1))


def sink_softmax(scores, mask, sinks):
    sink = sinks.astype(jnp.float32).reshape(ATT_KV_HEADS, GQA_GROUP, 1, 1)
    s = jnp.where(mask, scores, -jnp.inf)
    m = jnp.maximum(jnp.max(s, -1, keepdims=True), sink)
    p = jnp.exp(s - m)
    return p / (jnp.sum(p, -1, keepdims=True) + jnp.exp(sink - m))


def combined_projection(h, w_in, gk_up, gk_bias):
    n, t, _ = h.shape
    idx = np.cumsum(PROJ_SIZES)[:-1].tolist()
    q_g, k_g, v_g, r_g, lr_g, q_a, k_a, v_a, gate_a, gate_b = jnp.split(h @ w_in, idx, axis=-1)
    q_g = q_g.reshape(n, t, GLA_HEADS, GLA_DK) * (GLA_DK ** -0.5)
    k_g = k_g.reshape(n, t, GLA_HEADS, GLA_DK)
    v_g = v_g.reshape(n, t, GLA_HEADS, GLA_DV)
    log_a = (jax.nn.log_sigmoid((lr_g @ gk_up + gk_bias).astype(jnp.float32)) / GLA_TAU)
    log_a = log_a.reshape(n, t, GLA_HEADS, GLA_DK)
    q_a = q_a.reshape(n, t, ATT_HEADS, HEAD_DIM)
    k_a = k_a.reshape(n, t, ATT_KV_HEADS, HEAD_DIM)
    v_a = v_a.reshape(n, t, ATT_KV_HEADS, HEAD_DIM)
    return q_g, k_g, v_g, r_g, log_a, q_a, k_a, v_a, gate_a, gate_b


def gla_chunked(q, k, v, log_a, s0, chunk):
    n, t, h, dk = q.shape
    dv = v.shape[-1]
    nc = t // chunk

    def to_chunks(x):
        return x.astype(jnp.float32).reshape(n, nc, chunk, h, x.shape[-1]).transpose(1, 0, 3, 2, 4)

    causal = jnp.tril(jnp.ones((chunk, chunk), dtype=bool))

    def step(s, inp):
        qc, kc, vc, gc = inp
        b = jnp.cumsum(gc, axis=2)
        o_inter = jnp.einsum('nhid,nhde->nhie', qc * jnp.exp(b), s)
        diff = b[:, :, :, None, :] - b[:, :, None, :, :]
        decay = jnp.exp(jnp.where(causal[:, :, None], diff, -jnp.inf))
        attn = jnp.einsum('nhid,nhjd,nhijd->nhij', qc, kc, decay)
        o = o_inter + jnp.einsum('nhij,nhje->nhie', attn, vc)
        b_last = b[:, :, -1:, :]
        s_new = jnp.exp(b_last[:, :, 0, :])[..., None] * s + jnp.einsum('nhjd,nhje->nhde', kc * jnp.exp(b_last - b), vc)
        return s_new, o

    s_fin, o = lax.scan(step, s0, (to_chunks(q), to_chunks(k), to_chunks(v), to_chunks(log_a)))
    o = o.transpose(1, 0, 3, 2, 4).reshape(n, t, h, dv)
    return o, s_fin


def gla_run(q, k, v, log_a, s0, chunk, pad_front):
    t = q.shape[1]
    pad_back = (-(t + pad_front)) % chunk
    padt = lambda x: jnp.pad(x, ((0, 0), (pad_front, pad_back), (0, 0), (0, 0)))
    o, s = gla_chunked(padt(q), padt(k), padt(v), padt(log_a), s0, chunk)
    return o[:, pad_front:pad_front + t], s


def gla_output(o, r, g):
    o = o * lax.rsqrt(jnp.mean(jnp.square(o), -1, keepdims=True) + LN_EPS) * g.astype(jnp.float32)
    n, t = o.shape[:2]
    return (o.reshape(n, t, -1) * jax.nn.silu(r.astype(jnp.float32))).astype(r.dtype)


def swa_prompt(q, k, v, rel_bias, sinks):
    n, l = q.shape[:2]
    pad = (-l) % ATT_BLOCK
    nb = (l + pad) // ATT_BLOCK
    padt = lambda x: jnp.pad(x, ((0, 0), (pad, 0), (0, 0), (0, 0)))
    qb = padt(q).reshape(n, nb, ATT_BLOCK, ATT_KV_HEADS, GQA_GROUP, HEAD_DIM)
    kb = padt(k).reshape(n, nb, ATT_BLOCK, ATT_KV_HEADS, HEAD_DIM)
    vb = padt(v).reshape(n, nb, ATT_BLOCK, ATT_KV_HEADS, HEAD_DIM)
    prev = lambda x: jnp.pad(x, ((0, 0), (1, 0), (0, 0), (0, 0), (0, 0)))[:, :-1]
    kk = jnp.concatenate([prev(kb), kb], axis=2)
    vv = jnp.concatenate([prev(vb), vb], axis=2)
    dist = ATT_BLOCK + jnp.arange(ATT_BLOCK)[:, None] - jnp.arange(2 * ATT_BLOCK)[None, :]
    k_pos = (jnp.arange(nb)[:, None] - 1) * ATT_BLOCK + jnp.arange(2 * ATT_BLOCK)[None, :] - pad
    mask = ((dist >= 0) & (dist < WINDOW))[None] & (k_pos >= 0)[:, None, :]
    scores = jnp.einsum('nbqkgd,nbckd->nbkgqc', qb, kk).astype(jnp.float32) * (HEAD_DIM ** -0.5)
    scores = scores + rel_bias_for(dist, rel_bias)
    p = sink_softmax(scores, mask[None, :, None, None], sinks)
    o = jnp.einsum('nbkgqc,nbckd->nbqkgd', p.astype(v.dtype), vv)
    return o.reshape(n, nb * ATT_BLOCK, ATT_HEADS * HEAD_DIM)[:, pad:]


def swa_sample(q, k, v, cache_k, cache_v, rel_bias, sinks):
    n, s = q.shape[:2]
    wb = cache_k.shape[1]
    kk = jnp.concatenate([cache_k.astype(k.dtype), k], axis=1)
    vv = jnp.concatenate([cache_v.astype(v.dtype), v], axis=1)
    qg = q.reshape(n, s, ATT_KV_HEADS, GQA_GROUP, HEAD_DIM)
    dist = wb + jnp.arange(s)[:, None] - jnp.arange(wb + s)[None, :]
    mask = (dist >= 0) & (dist < WINDOW)
    scores = jnp.einsum('nqkgd,nckd->nkgqc', qg, kk).astype(jnp.float32) * (HEAD_DIM ** -0.5)
    scores = scores + rel_bias_for(dist, rel_bias)
    p = sink_softmax(scores, mask, sinks)
    o = jnp.einsum('nkgqc,nckd->nqkgd', p.astype(v.dtype), vv).reshape(n, s, ATT_HEADS * HEAD_DIM)
    return o, kk[:, -wb:], vv[:, -wb:]


def hier_moe(x, w_rg, b_rg, w_re, b_re, w_gate, w_up, w_down):
    t = x.shape[0]
    xf = x.astype(jnp.float32)
    g_logits = xf @ w_rg.astype(jnp.float32) + b_rg.astype(jnp.float32)
    g_prob = jax.nn.softmax(g_logits, axis=-1)
    grp = jnp.argmax(g_logits, axis=-1)
    p_grp = jnp.take_along_axis(g_prob, grp[:, None], axis=1)[:, 0]
    e_logits = (xf @ w_re.astype(jnp.float32) + b_re.astype(jnp.float32)).reshape(t, N_GROUPS, EXPERTS_PER_GROUP)
    e_in = jnp.take_along_axis(e_logits, grp[:, None, None], axis=1)[:, 0]
    top_v, top_i = lax.top_k(e_in, TOP_K)
    w_top = jax.nn.softmax(top_v, axis=-1) * p_grp[:, None]
    expert = grp[:, None] * EXPERTS_PER_GROUP + top_i

    a = t * TOP_K
    e_flat = expert.reshape(a).astype(jnp.int32)
    tok_flat = jnp.repeat(jnp.arange(t, dtype=jnp.int32), TOP_K)
    w_flat = w_top.reshape(a)
    order = jnp.argsort(e_flat)
    e_s, tok_s, w_s = e_flat[order], tok_flat[order], w_flat[order]
    counts = jnp.zeros((N_EXPERTS,), jnp.int32).at[e_flat].add(1)
    starts = jnp.cumsum(counts) - counts
    padded = (counts + MOE_BLOCK - 1) // MOE_BLOCK * MOE_BLOCK
    pend = jnp.cumsum(padded)
    pstart = pend - padded
    dest = pstart[e_s] + (jnp.arange(a, dtype=jnp.int32) - starts[e_s])
    n_blocks = -(-a // MOE_BLOCK) + N_EXPERTS
    buf = jnp.zeros((n_blocks * MOE_BLOCK, x.shape[1]), x.dtype).at[dest].set(x[tok_s])
    block_start = jnp.arange(n_blocks, dtype=jnp.int32) * MOE_BLOCK
    block_e = jnp.minimum(jnp.searchsorted(pend, block_start, side='right'), N_EXPERTS - 1)

    def expert_block(args):
        xb, e = args
        hb = jax.nn.silu(xb @ w_gate[e]) * (xb @ w_up[e])
        return hb @ w_down[e]

    out = lax.map(expert_block, (buf.reshape(n_blocks, MOE_BLOCK, -1), block_e)).reshape(n_blocks * MOE_BLOCK, -1)
    return jax.ops.segment_sum(out[dest] * w_s[:, None].astype(out.dtype), tok_s, num_segments=t)


def layer_forward(h, gla_s0, win_k, win_v, rel_bias, w_in, gk_up, gk_bias, gla_norm_g, sinks, w_out,
                  ln1_g, ln1_b, w_rg, b_rg, w_re, b_re, w_gate, w_up, w_down, ln2_g, ln2_b, state_dtype):
    n, t, d = h.shape
    q_g, k_g, v_g, r_g, log_a, q_a, k_a, v_a, gate_a, gate_b = combined_projection(h, w_in, gk_up, gk_bias)
    if win_k is None:
        s0 = jnp.zeros((n, GLA_HEADS, GLA_DK, GLA_DV), jnp.float32)
        o_g, s_new = gla_run(q_g, k_g, v_g, log_a, s0, GLA_CHUNK, (-N_META) % GLA_CHUNK)
        o_a = swa_prompt(q_a, k_a, v_a, rel_bias, sinks)
        wb = min(WINDOW, t)
        k_new, v_new = k_a[:, -wb:], v_a[:, -wb:]
    else:
        o_g, s_new = gla_run(q_g, k_g, v_g, log_a, gla_s0.astype(jnp.float32), min(GLA_CHUNK, t), 0)
        o_a, k_new, v_new = swa_sample(q_a, k_a, v_a, win_k, win_v, rel_bias, sinks)
    y_g = gla_output(o_g, r_g, gla_norm_g)
    merged = jax.nn.sigmoid(gate_a) * y_g + jax.nn.sigmoid(gate_b) * o_a.astype(h.dtype)
    h1 = layer_norm(ALPHA * h + merged @ w_out, ln1_g, ln1_b)
    f = hier_moe(h1.reshape(n * t, d), w_rg, b_rg, w_re, b_re, w_gate, w_up, w_down).reshape(n, t, d)
    h2 = layer_norm(ALPHA * h1 + f, ln2_g, ln2_b)
    return h2, s_new.astype(state_dtype), k_new.astype(state_dtype), v_new.astype(state_dtype)


def setup_inputs(seed: int = 0) -> dict:
    key = jax.random.key(seed)
    ks = jax.random.split(key, 32)
    nrm = lambda k, shape, scale: jax.random.normal(k, shape, jnp.float32) * scale
    wb = min(WINDOW, PAST_LEN)
    L = DEPTH
    return {
        "x_prompt": nrm(ks[0], (BATCH, SEQ, D_MODEL), 1.0),
        "x_sample": nrm(ks[1], (DEC_BATCH, DEC_SEQ, D_MODEL), 1.0),
        "state_gla": nrm(ks[2], (L, DEC_BATCH, GLA_HEADS, GLA_DK, GLA_DV), 1.0),
        "cache_swa_k": nrm(ks[3], (L, DEC_BATCH, wb, ATT_KV_HEADS, HEAD_DIM), 1.0),
        "cache_swa_v": nrm(ks[4], (L, DEC_BATCH, wb, ATT_KV_HEADS, HEAD_DIM), 1.0),
        "meta_tokens": nrm(ks[5], (N_META, D_MODEL), 1.0),
        "ln_emb_g": 1.0 + nrm(ks[6], (D_MODEL,), 0.02),
        "ln_emb_b": nrm(ks[7], (D_MODEL,), 0.02),
        "rel_bias": nrm(ks[8], (REL_BUCKETS, ATT_HEADS), 0.1),
        "w_in": nrm(ks[9], (L, D_MODEL, PROJ_WIDTH), D_MODEL ** -0.5),
        "gk_up": nrm(ks[10], (L, GLA_RANK, GLA_HEADS * GLA_DK), GLA_RANK ** -0.5),
        "gk_bias": nrm(ks[11], (L, GLA_HEADS * GLA_DK), 0.1),
        "gla_norm_g": 1.0 + nrm(ks[12], (L, GLA_DV), 0.02),
        "sinks": nrm(ks[13], (L, ATT_HEADS), 0.5),
        "w_out": nrm(ks[14], (L, D_MODEL, D_MODEL), BETA * D_MODEL ** -0.5),
        "ln1_g": 1.0 + nrm(ks[15], (L, D_MODEL), 0.02),
        "ln1_b": nrm(ks[16], (L, D_MODEL), 0.02),
        "w_router_group": nrm(ks[17], (L, D_MODEL, N_GROUPS), D_MODEL ** -0.5),
        "b_router_group": nrm(ks[18], (L, N_GROUPS), 0.01),
        "w_router_expert": nrm(ks[19], (L, D_MODEL, N_EXPERTS), D_MODEL ** -0.5),
        "b_router_expert": nrm(ks[20], (L, N_EXPERTS), 0.01),
        "w_gate": nrm(ks[21], (L, N_EXPERTS, D_MODEL, D_EXPERT), D_MODEL ** -0.5),
        "w_up": nrm(ks[22], (L, N_EXPERTS, D_MODEL, D_EXPERT), D_MODEL ** -0.5),
        "w_down": nrm(ks[23], (L, N_EXPERTS, D_EXPERT, D_MODEL), BETA * D_EXPERT ** -0.5),
        "ln2_g": 1.0 + nrm(ks[24], (L, D_MODEL), 0.02),
        "ln2_b": nrm(ks[25], (L, D_MODEL), 0.02),
    }


def reference(x_prompt, x_sample, state_gla, cache_swa_k, cache_swa_v, meta_tokens, ln_emb_g, ln_emb_b,
              rel_bias, w_in, gk_up, gk_bias, gla_norm_g, sinks, w_out, ln1_g, ln1_b, w_router_group,
              b_router_group, w_router_expert, b_router_expert, w_gate, w_up, w_down, ln2_g, ln2_b):
    b = x_prompt.shape[0]
    meta = jnp.broadcast_to(meta_tokens.astype(x_prompt.dtype)[None], (b, N_META, x_prompt.shape[-1]))
    hp = layer_norm(jnp.concatenate([meta, x_prompt], axis=1), ln_emb_g, ln_emb_b)
    hs = layer_norm(x_sample, ln_emb_g, ln_emb_b)
    sd = state_gla.dtype
    gp, kp, vp, gs, kss, vss = [], [], [], [], [], []
    for l in range(DEPTH):
        lp = (w_in[l], gk_up[l], gk_bias[l], gla_norm_g[l], sinks[l], w_out[l], ln1_g[l], ln1_b[l],
              w_router_group[l], b_router_group[l], w_router_expert[l], b_router_expert[l],
              w_gate[l], w_up[l], w_down[l], ln2_g[l], ln2_b[l])
        hp, s_p, k_p, v_p = layer_forward(hp, None, None, None, rel_bias, *lp, sd)
        hs, s_s, k_s, v_s = layer_forward(hs, state_gla[l], cache_swa_k[l], cache_swa_v[l], rel_bias, *lp, sd)
        gp.append(s_p); kp.append(k_p); vp.append(v_p)
        gs.append(s_s); kss.append(k_s); vss.append(v_s)
    y_prompt = hp[:, N_META:]
    return (y_prompt, hs, jnp.stack(gp), jnp.stack(kp), jnp.stack(vp), jnp.stack(gs), jnp.stack(kss), jnp.stack(vss))
```

```python
import functools
import math

import jax
import jax.numpy as jnp
import numpy as np
from jax import lax
from jax.experimental import pallas as pl
from jax.experimental.pallas import tpu as pltpu

F32 = jnp.float32
BF16 = jnp.bfloat16

D_MODEL = 1024
N_META = 16
LN_EPS = 1e-5
GLA_HEADS = 4
GLA_DK = 128
GLA_DV = 256
GLA_RANK = 16
GLA_TAU = 16.0
HEAD_DIM = 64
ATT_HEADS = 16
ATT_KV_HEADS = 4
GQA_GROUP = 4
WINDOW = 128
REL_BUCKETS = 32
REL_MAX_DIST = 128
N_GROUPS = 4
EXPERTS_PER_GROUP = 8
N_EXPERTS = 32
TOP_K = 2
D_EXPERT = 512
DEPTH = 1
ALPHA = (2.0 * DEPTH) ** 0.25

LANES = 128
BLK = 128
VMEM_LIMIT = 56 * 1024 * 1024


def _cparams(sem):
    return pltpu.CompilerParams(dimension_semantics=sem, vmem_limit_bytes=VMEM_LIMIT)


def _layer_norm_rows(x, g, b):
    mu = jnp.mean(x, axis=-1, keepdims=True)
    xc = x - mu
    var = jnp.mean(xc * xc, axis=-1, keepdims=True)
    return xc * lax.rsqrt(var + LN_EPS) * g + b


_PROJ_OUTS = (
    ("qg", GLA_HEADS * GLA_DK, BF16, GLA_DK ** -0.5),
    ("kg", GLA_HEADS * GLA_DK, BF16, None),
    ("vg", GLA_HEADS * GLA_DV, BF16, None),
    ("rg", GLA_HEADS * GLA_DV, BF16, None),
    ("qa", ATT_HEADS * HEAD_DIM, BF16, HEAD_DIM ** -0.5),
    ("ka", ATT_KV_HEADS * HEAD_DIM, F32, None),
    ("va", ATT_KV_HEADS * HEAD_DIM, F32, None),
    ("ga", D_MODEL, BF16, None),
    ("gb", D_MODEL, BF16, None),
    ("lr", LANES, F32, None),
)
_PROJ_W = sum(w for _, w, _, _ in _PROJ_OUTS)


def _prep_w_in(w_in):
    sizes = (512, 512, 1024, 1024, GLA_RANK, 1024, 256, 256, 1024, 1024)
    offs = np.cumsum((0,) + sizes)
    a = w_in[:, : offs[4]]
    lr = w_in[:, offs[4]: offs[5]]
    b = w_in[:, offs[5]:]
    pad = jnp.zeros((w_in.shape[0], LANES - GLA_RANK), w_in.dtype)
    return jnp.concatenate([a, b, lr, pad], axis=1).astype(BF16)


def _ln_proj_body(x_ref, g_ref, b_ref, w_ref, *out_refs):
    xn = _layer_norm_rows(x_ref[...], g_ref[...], b_ref[...]).astype(BF16)
    c0 = 0
    for (_, width, dtype, scale), o_ref in zip(_PROJ_OUTS, out_refs):
        acc = jnp.dot(xn, w_ref[:, c0:c0 + width], preferred_element_type=F32)
        if scale is not None:
            acc = acc * scale
        o_ref[...] = acc.astype(dtype)
        c0 += width


def _ln_proj(x2d, ln_g, ln_b, w_cat, tm):
    m = x2d.shape[0]
    assert m % tm == 0
    out_shape = [jax.ShapeDtypeStruct((m, w), dt) for _, w, dt, _ in _PROJ_OUTS]
    out_specs = [pl.BlockSpec((tm, w), lambda i: (i, 0)) for _, w, _, _ in _PROJ_OUTS]
    outs = pl.pallas_call(
        _ln_proj_body,
        grid=(m // tm,),
        in_specs=[
            pl.BlockSpec((tm, D_MODEL), lambda i: (i, 0)),
            pl.BlockSpec((1, D_MODEL), lambda i: (0, 0)),
            pl.BlockSpec((1, D_MODEL), lambda i: (0, 0)),
            pl.BlockSpec((D_MODEL, _PROJ_W), lambda i: (0, 0), pipeline_mode=pl.Buffered(1)),
        ],
        out_specs=out_specs,
        out_shape=out_shape,
        compiler_params=_cparams(("arbitrary",)),
        name="ln_proj",
    )(x2d, ln_g.reshape(1, -1), ln_b.reshape(1, -1), w_cat)
    return dict(zip([n for n, _, _, _ in _PROJ_OUTS], outs))


_GLA_LEVELS = tuple(2 ** i for i in range(int(math.log2(BLK))))


def _log_sigmoid(x):
    return jnp.minimum(x, 0.0) - jnp.log1p(jnp.exp(-jnp.abs(x)))


def _sigmoid(x):
    return 1.0 / (1.0 + jnp.exp(-x))


def _split_dot(a01, x):
    hi = x.astype(BF16)
    lo = (x - hi.astype(F32)).astype(BF16)
    return (jnp.dot(a01, hi, preferred_element_type=F32) + jnp.dot(a01, lo, preferred_element_type=F32))


def _gla_anchor_exponent(b, la, s, row):
    if s == 1:
        return jnp.where(row % 2 == 1, la, 0.0)
    if s == 2:
        la_dn = pltpu.roll(la, 1, axis=0)
        la_up = pltpu.roll(la, BLK - 1, axis=0)
        r = row % 4
        return jnp.where(r == 0, la_up, jnp.where(r == 1, 0.0, jnp.where(r == 2, la, la + la_dn)))
    nb = BLK // (2 * s)
    b3 = b.reshape(nb, 2 * s, b.shape[-1])
    anchor = jnp.broadcast_to(b3[:, s - 1:s, :], b3.shape).reshape(b.shape)
    return -jnp.abs(b - anchor)


def _gla_body(nblk, qm, km, vm, rm, lrm, gam, qp, kp, vp, rp, lrp, gap, gkup_ref, gkb_ref, gn_ref, tri_ref,
              y_ref, s_out_ref, s_ref):
    c = pl.program_id(1)
    is_meta = c == 0

    @pl.when(is_meta)
    def _():
        s_ref[...] = jnp.zeros_like(s_ref)

    def pick(m_ref, p_ref):
        return jnp.where(is_meta, m_ref[...], p_ref[...])

    row = lax.broadcasted_iota(jnp.int32, (BLK, GLA_DK), 0)
    col_t = lax.broadcasted_iota(jnp.int32, (BLK, BLK), 1)
    row_t = lax.broadcasted_iota(jnp.int32, (BLK, BLK), 0)
    live = jnp.logical_or(jnp.logical_not(is_meta), row >= BLK - N_META)
    tri = tri_ref[...]
    q_all, k_all, v_all, r_all = pick(qm, qp), pick(km, kp), pick(vm, vp), pick(rm, rp)
    ga_all = pick(gam, gap)
    lr = pick(lrm, lrp).astype(BF16)
    for h in range(GLA_HEADS):
        dk = slice(h * GLA_DK, (h + 1) * GLA_DK)
        dv = slice(h * GLA_DV, (h + 1) * GLA_DV)
        x = jnp.dot(lr, gkup_ref[:, dk], preferred_element_type=F32) + gkb_ref[:, dk]
        la = jnp.where(live, _log_sigmoid(x) * (1.0 / GLA_TAU), 0.0)
        q = q_all[:, dk].astype(F32)
        k = jnp.where(live, k_all[:, dk].astype(F32), 0.0)
        v = v_all[:, dv]
        b = _split_dot(tri, la)
        b_last = b[BLK - 1:BLK, :]
        s_old = s_ref[h]
        o = jnp.dot((q * jnp.exp(b)).astype(BF16), s_old.astype(BF16), preferred_element_type=F32)
        kd = (k * jnp.exp(b_last - b)).astype(BF16)
        kv = lax.dot_general(kd, v, (((0,), (0,)), ((), ())), preferred_element_type=F32)
        decay_col = jnp.transpose(jnp.broadcast_to(jnp.exp(b_last), (BLK, GLA_DK)))[:, :1]
        s_new = decay_col * s_old + kv
        s_ref[h] = s_new
        a = jnp.where(row_t == col_t, jnp.dot(q.astype(BF16), k.astype(BF16).T, preferred_element_type=F32), 0.0)
        for s in _GLA_LEVELS:
            e = jnp.exp(_gla_anchor_exponent(b, la, s, row))
            upper = (row // s) % 2 == 1
            q_s = jnp.where(upper, q * e, 0.0).astype(BF16)
            k_s = jnp.where(upper, 0.0, k * e).astype(BF16)
            p = lax.dot_general(q_s, k_s, (((1,), (1,)), ((), ())), preferred_element_type=F32)
            a = a + jnp.where(row_t // (2 * s) == col_t // (2 * s), p, 0.0)
        o = o + jnp.dot(a.astype(BF16), v, preferred_element_type=F32)
        o = o * lax.rsqrt(jnp.mean(o * o, axis=-1, keepdims=True) + LN_EPS) * gn_ref[...]
        r = r_all[:, dv].astype(F32)
        y = o * (r * _sigmoid(r)) * _sigmoid(ga_all[:, dv].astype(F32))
        y_ref[:, dv] = y.astype(y_ref.dtype)

    @pl.when(c == nblk)
    def _():
        s_out_ref[...] = s_ref[...]


def _tri_incl():
    i = np.arange(BLK)
    return jnp.asarray((i[None, :] <= i[:, None]).astype(np.float32), dtype=BF16)


def _gla_prompt(pp, pe, gk_up, gk_bias, gnorm, nbatch, nblk):
    names = ("qg", "kg", "vg", "rg", "lr", "ga")
    gkup = jnp.concatenate([gk_up, jnp.zeros((LANES - GLA_RANK, gk_up.shape[1]), gk_up.dtype)], axis=0).astype(BF16)
    m_specs = [pl.BlockSpec((BLK, pe[n].shape[1]), lambda b, c: (1, 0)) for n in names]
    p_specs = [pl.BlockSpec((BLK, pp[n].shape[1]), lambda b, c: (b * nblk + jnp.maximum(c - 1, 0), 0)) for n in names]
    w_specs = [
        pl.BlockSpec((LANES, GLA_HEADS * GLA_DK), lambda b, c: (0, 0)),
        pl.BlockSpec((1, GLA_HEADS * GLA_DK), lambda b, c: (0, 0)),
        pl.BlockSpec((1, GLA_DV), lambda b, c: (0, 0)),
        pl.BlockSpec((BLK, BLK), lambda b, c: (0, 0)),
    ]
    y, s_fin = pl.pallas_call(
        functools.partial(_gla_body, nblk),
        grid=(nbatch, nblk + 1),
        in_specs=m_specs + p_specs + w_specs,
        out_specs=[
            pl.BlockSpec((BLK, D_MODEL), lambda b, c: (b * nblk + jnp.maximum(c - 1, 0), 0)),
            pl.BlockSpec((None, GLA_HEADS, GLA_DK, GLA_DV), lambda b, c: (b, 0, 0, 0)),
        ],
        out_shape=[
            jax.ShapeDtypeStruct((nbatch * nblk * BLK, D_MODEL), BF16),
            jax.ShapeDtypeStruct((nbatch, GLA_HEADS, GLA_DK, GLA_DV), F32),
        ],
        scratch_shapes=[pltpu.VMEM((GLA_HEADS, GLA_DK, GLA_DV), F32)],
        compiler_params=_cparams(("arbitrary", "arbitrary")),
        name="gla_prompt",
    )(*[pe[n] for n in names], *[pp[n] for n in names], gkup, gk_bias.reshape(1, -1), gnorm.reshape(1, -1), _tri_incl())
    return y, s_fin


GLA_STEP_SEQS = 16


def _gla_step_body(q_ref, k_ref, v_ref, r_ref, lr_ref, ga_ref, gkup_ref, gkb_ref, gn_ref, s_in_ref,
                   y_ref, s_out_ref, at_ref, kt_ref, qt_ref):
    g = pl.program_id(0)
    nseq = q_ref.shape[0]

    @pl.when(g == 0)
    def _():
        x = jnp.dot(lr_ref[...].astype(BF16), gkup_ref[...], preferred_element_type=F32) + gkb_ref[...]
        a = jnp.exp(_log_sigmoid(x) * (1.0 / GLA_TAU))
        for h in range(GLA_HEADS):
            dk = slice(h * GLA_DK, (h + 1) * GLA_DK)
            at_ref[h] = jnp.transpose(a[:, dk])
            kt_ref[h] = jnp.transpose(k_ref[:, dk].astype(F32))
            qt_ref[h] = jnp.transpose(q_ref[:, dk].astype(F32))

    lane = lax.broadcasted_iota(jnp.int32, (GLA_DK, nseq), 1)
    ones = jnp.ones((nseq, GLA_DV), BF16)
    grp = pl.ds(pl.multiple_of(g * GLA_STEP_SEQS, GLA_STEP_SEQS), GLA_STEP_SEQS)
    r_grp = r_ref[grp, :].astype(F32)
    ga_grp = ga_ref[grp, :].astype(F32)
    for i in range(GLA_STEP_SEQS):
        n = g * GLA_STEP_SEQS + i
        sel = lane == n
        for h in range(GLA_HEADS):
            dv = slice(h * GLA_DV, (h + 1) * GLA_DV)
            a_sel = jnp.where(sel, at_ref[h], 0.0)
            k_sel = jnp.where(sel, kt_ref[h], 0.0).astype(BF16)
            q_sel = jnp.where(sel, qt_ref[h], 0.0).astype(BF16)
            decay = _split_dot_rhs(a_sel, ones)
            kv = jnp.dot(k_sel, v_ref[:, dv], preferred_element_type=F32)
            q_b = jnp.dot(q_sel, ones, preferred_element_type=F32)
            s_new = decay * s_in_ref[i, h] + kv
            s_out_ref[i, h] = s_new
            o = jnp.sum(q_b * s_new, axis=0, keepdims=True)
            o = o * lax.rsqrt(jnp.mean(o * o, axis=-1, keepdims=True) + LN_EPS) * gn_ref[...]
            r = r_grp[i:i + 1, dv]
            ga = ga_grp[i:i + 1, dv]
            y_ref[i:i + 1, dv] = (o * (r * _sigmoid(r)) * _sigmoid(ga)).astype(y_ref.dtype)


def _split_dot_rhs(x, b01):
    hi = x.astype(BF16)
    lo = (x - hi.astype(F32)).astype(BF16)
    return jnp.dot(hi, b01, preferred_element_type=F32) + jnp.dot(lo, b01, preferred_element_type=F32)


def _gla_step(pe, gk_up, gk_bias, gnorm, state):
    nseq = state.shape[0]
    assert nseq == BLK and nseq % GLA_STEP_SEQS == 0
    names = ("qg", "kg", "vg", "rg", "lr", "ga")
    gkup = jnp.concatenate([gk_up, jnp.zeros((LANES - GLA_RANK, gk_up.shape[1]), gk_up.dtype)], axis=0).astype(BF16)
    t_specs = [pl.BlockSpec((nseq, pe[n].shape[1]), lambda g: (0, 0)) for n in names]
    st_spec = pl.BlockSpec((GLA_STEP_SEQS, GLA_HEADS, GLA_DK, GLA_DV), lambda g: (g, 0, 0, 0))
    return pl.pallas_call(
        _gla_step_body,
        grid=(nseq // GLA_STEP_SEQS,),
        in_specs=t_specs + [
            pl.BlockSpec((LANES, GLA_HEADS * GLA_DK), lambda g: (0, 0)),
            pl.BlockSpec((1, GLA_HEADS * GLA_DK), lambda g: (0, 0)),
            pl.BlockSpec((1, GLA_DV), lambda g: (0, 0)),
            st_spec,
        ],
        out_specs=[pl.BlockSpec((GLA_STEP_SEQS, D_MODEL), lambda g: (g, 0)), st_spec],
        out_shape=[jax.ShapeDtypeStruct((nseq, D_MODEL), F32), jax.ShapeDtypeStruct(state.shape, F32)],
        scratch_shapes=[pltpu.VMEM((GLA_HEADS, GLA_DK, nseq), F32) for _ in range(3)],
        compiler_params=_cparams(("arbitrary",)),
        name="gla_step",
    )(*[pe[n] for n in names], gkup, gk_bias.reshape(1, -1), gnorm.reshape(1, -1), state)


HALF = LANES // 2


def _rel_bucket(dist):
    max_exact = REL_BUCKETS // 2
    d = jnp.maximum(dist, 0)
    large = max_exact + (jnp.log(jnp.maximum(d, 1).astype(F32) / max_exact)
                         / math.log(REL_MAX_DIST / max_exact) * (REL_BUCKETS - max_exact)).astype(jnp.int32)
    large = jnp.minimum(large, REL_BUCKETS - 1)
    return jnp.where(d < max_exact, d, large)


def _swa_bias_tables(rel_bias):
    q = jnp.arange(BLK)[:, None]
    c = jnp.arange(2 * BLK)[None, :]
    dist = BLK + q - c
    bias = jnp.moveaxis(rel_bias[_rel_bucket(dist)].astype(F32), -1, 0)
    inside = (dist >= 0) & (dist < WINDOW)
    first = inside & (c >= BLK - N_META)
    neg = jnp.float32(-jnp.inf)
    return jnp.stack([jnp.where(first[None], bias, neg), jnp.where(inside[None], bias, neg)])


def _half_tiles(x):
    lane = lax.broadcasted_iota(jnp.int32, (x.shape[0], LANES), 1)
    low = lane < HALF
    out = []
    for t in range(2):
        tile = x[:, t * LANES:(t + 1) * LANES]
        swapped = pltpu.roll(tile, HALF, axis=1)
        zero = jnp.zeros_like(tile)
        even = (jnp.where(low, tile, zero).astype(BF16), jnp.where(low, zero, swapped).astype(BF16))
        odd = (jnp.where(low, swapped, zero).astype(BF16), jnp.where(low, zero, tile).astype(BF16))
        out += [even, odd]
    return out


def _swa_body(nblk, q_ref, kc_ref, vc_ref, kp_ref, vp_ref, km_ref, vm_ref, gb_ref, yg_ref, tb_ref, sink_ref,
              o_ref):
    blk = pl.program_id(1)
    first = blk == 0
    k_prev = jnp.where(first, km_ref[...], kp_ref[...])
    v_prev = jnp.where(first, vm_ref[...], vp_ref[...])
    k_tiles = _half_tiles(jnp.concatenate([k_prev, kc_ref[...]], axis=0))
    v_tiles = _half_tiles(jnp.concatenate([v_prev, vc_ref[...]], axis=0))
    variant = jnp.minimum(blk, 1)
    lane = lax.broadcasted_iota(jnp.int32, (BLK, LANES), 1)
    low = lane < HALF
    nt = (((1,), (1,)), ((), ()))
    for j in range(ATT_KV_HEADS):
        for pair in range(GQA_GROUP // 2):
            t = j * (GQA_GROUP // 2) + pair
            q_t = q_ref[:, t * LANES:(t + 1) * LANES]
            acc = None
            inv = []
            for half in range(2):
                h = 2 * t + half
                s = lax.dot_general(q_t, k_tiles[j][half], nt, preferred_element_type=F32)
                s = s + tb_ref[variant, h]
                sink = sink_ref[h]
                m = jnp.maximum(jnp.max(s, axis=-1, keepdims=True), sink)
                p = jnp.exp(s - m)
                inv.append(1.0 / (jnp.sum(p, axis=-1, keepdims=True) + jnp.exp(sink - m)))
                pv = jnp.dot(p.astype(BF16), v_tiles[j][half], preferred_element_type=F32)
                acc = pv if acc is None else acc + pv
            o = acc * jnp.where(low, inv[0], inv[1])
            cols = slice(t * LANES, (t + 1) * LANES)
            gate = _sigmoid(gb_ref[:, cols].astype(F32))
            o_ref[:, cols] = (gate * o + yg_ref[:, cols].astype(F32)).astype(o_ref.dtype)


def _swa_prompt(pp, pe, yg, rel_bias, sinks, nbatch, nblk):
    tb = _swa_bias_tables(rel_bias)
    kvw = ATT_KV_HEADS * HEAD_DIM
    cur = lambda b, c: (b * nblk + c, 0)
    prev = lambda b, c: (b * nblk + jnp.maximum(c - 1, 0), 0)
    return pl.pallas_call(
        functools.partial(_swa_body, nblk),
        grid=(nbatch, nblk),
        in_specs=[
            pl.BlockSpec((BLK, D_MODEL), cur),
            pl.BlockSpec((BLK, kvw), cur), pl.BlockSpec((BLK, kvw), cur),
            pl.BlockSpec((BLK, kvw), prev), pl.BlockSpec((BLK, kvw), prev),
            pl.BlockSpec((BLK, kvw), lambda b, c: (1, 0)), pl.BlockSpec((BLK, kvw), lambda b, c: (1, 0)),
            pl.BlockSpec((BLK, D_MODEL), cur),
            pl.BlockSpec((BLK, D_MODEL), cur),
            pl.BlockSpec(tb.shape, lambda b, c: (0, 0, 0, 0)),
            pl.BlockSpec(memory_space=pltpu.SMEM),
        ],
        out_specs=pl.BlockSpec((BLK, D_MODEL), cur),
        out_shape=jax.ShapeDtypeStruct((nbatch * nblk * BLK, D_MODEL), BF16),
        compiler_params=_cparams(("arbitrary", "arbitrary")),
        name="swa_prompt",
    )(pp["qa"], pp["ka"], pp["va"], pp["ka"], pp["va"], pe["ka"], pe["va"], pp["gb"], yg, tb, sinks)


SWA_STEP_SEQS = 8
Q_TILES = ATT_HEADS // 2


def _swa_step_body(q_ref, kn_ref, vn_ref, ck_ref, cv_ref, gb_ref, yg_ref, tb_ref, sink_ref,
                   o_ref, ko_ref, vo_ref):
    row = lax.broadcasted_iota(jnp.int32, (WINDOW, ATT_KV_HEADS * HEAD_DIM), 0)
    trow = lax.broadcasted_iota(jnp.int32, (Q_TILES, LANES), 0)
    low = lax.broadcasted_iota(jnp.int32, (Q_TILES, LANES), 1) < HALF
    nt = (((1,), (1,)), ((), ()))
    for i in range(SWA_STEP_SEQS):
        k_win = jnp.where(row == WINDOW - 1, kn_ref[i:i + 1, :], pltpu.roll(ck_ref[i], WINDOW - 1, axis=0))
        v_win = jnp.where(row == WINDOW - 1, vn_ref[i:i + 1, :], pltpu.roll(cv_ref[i], WINDOW - 1, axis=0))
        ko_ref[i] = k_win
        vo_ref[i] = v_win
        k_tiles = _half_tiles(k_win)
        v_tiles = _half_tiles(v_win)
        q8 = q_ref[i].astype(BF16)
        s = [jnp.zeros((Q_TILES, WINDOW), F32), jnp.zeros((Q_TILES, WINDOW), F32)]
        for j in range(ATT_KV_HEADS):
            mine = trow // (GQA_GROUP // 2) == j
            for half in range(2):
                sj = lax.dot_general(q8, k_tiles[j][half], nt, preferred_element_type=F32)
                s[half] = jnp.where(mine, sj, s[half])
        p, inv = [], []
        for half in range(2):
            sh = s[half] + tb_ref[half]
            sink = sink_ref[half]
            m = jnp.maximum(jnp.max(sh, axis=-1, keepdims=True), sink)
            ph = jnp.exp(sh - m)
            inv.append(1.0 / (jnp.sum(ph, axis=-1, keepdims=True) + jnp.exp(sink - m)))
            p.append(ph.astype(BF16))
        o = jnp.zeros((Q_TILES, LANES), F32)
        for j in range(ATT_KV_HEADS):
            mine = trow // (GQA_GROUP // 2) == j
            pv = (jnp.dot(p[0], v_tiles[j][0], preferred_element_type=F32)
                  + jnp.dot(p[1], v_tiles[j][1], preferred_element_type=F32))
            o = jnp.where(mine, pv, o)
        o = o * jnp.where(low, inv[0], inv[1])
        o_ref[i] = _sigmoid(gb_ref[i]) * o + yg_ref[i]


def _swa_step(pe, yg_s, cache_k, cache_v, rel_bias, sinks):
    nseq = cache_k.shape[0]
    kvw = ATT_KV_HEADS * HEAD_DIM
    as_tiles = lambda x: x[:nseq].astype(F32).reshape(nseq, Q_TILES, LANES)
    dist = (WINDOW - 1) - jnp.arange(WINDOW)
    bias = rel_bias[_rel_bucket(dist)].astype(F32).T
    tb = jnp.stack([bias[0::2], bias[1::2]])
    sk = jnp.broadcast_to(jnp.stack([sinks[0::2], sinks[1::2]])[:, :, None], (2, Q_TILES, 1)).astype(F32)
    g = SWA_STEP_SEQS
    tile_spec = pl.BlockSpec((g, Q_TILES, LANES), lambda s: (s, 0, 0))
    win_spec = pl.BlockSpec((g, WINDOW, kvw), lambda s: (s, 0, 0))
    new_spec = pl.BlockSpec((g, kvw), lambda s: (s, 0))
    o, ko, vo = pl.pallas_call(
        _swa_step_body,
        grid=(nseq // g,),
        in_specs=[tile_spec, new_spec, new_spec, win_spec, win_spec, tile_spec, tile_spec,
                  pl.BlockSpec(tb.shape, lambda s: (0, 0, 0)), pl.BlockSpec(sk.shape, lambda s: (0, 0, 0))],
        out_specs=[tile_spec, win_spec, win_spec],
        out_shape=[jax.ShapeDtypeStruct((nseq, Q_TILES, LANES), F32),
                   jax.ShapeDtypeStruct(cache_k.shape, F32), jax.ShapeDtypeStruct(cache_v.shape, F32)],
        compiler_params=_cparams(("arbitrary",)),
        name="swa_step",
    )(as_tiles(pe["qa"]), pe["ka"], pe["va"], cache_k, cache_v, as_tiles(pe["gb"]),
      yg_s.reshape(nseq, Q_TILES, LANES), tb, sk)
    return o.reshape(nseq, D_MODEL), ko, vo


ROUTER_ROWS = 40
META_ROWS = 8


def _split3_nt(a_hi, a_lo, x):
    nt = (((1,), (1,)), ((), ()))
    x_hi = x.astype(BF16)
    x_lo = (x - x_hi.astype(F32)).astype(BF16)
    return (lax.dot_general(a_hi, x_hi, nt, preferred_element_type=F32)
            + lax.dot_general(a_hi, x_lo, nt, preferred_element_type=F32)
            + lax.dot_general(a_lo, x_hi, nt, preferred_element_type=F32))


def _first_argmax_rows(v, ridx, nrows):
    vmax = jnp.max(v, axis=0, keepdims=True)
    idx = jnp.min(jnp.where(v == vmax, ridx, nrows), axis=0, keepdims=True)
    return vmax, idx


def _post_body(nsteps, *refs):
    h1_ref, meta_ref, wcol_ref = refs[-5:-2]
    i = pl.program_id(0)

    @pl.when(i < nsteps)
    def _():
        _post_tile(i, *refs)

    @pl.when(i >= nsteps)
    def _():
        h1_ref[...] = jnp.zeros_like(h1_ref)
        meta_ref[...] = jnp.zeros_like(meta_ref)
        wcol_ref[...] = jnp.zeros_like(wcol_ref)


def _post_tile(i, mg_ref, x_ref, lng_ref, lnb_ref, wo_ref, g1_ref, b1_ref, wrh_ref, wrl_ref, rb_ref, ut_ref,
               cin_ref, *rest):
    h1_ref, meta_ref, wcol_ref, cout_ref, carry_ref = rest[-5:]

    @pl.when(i == 0)
    def _():
        carry_ref[...] = cin_ref[...]

    tm = x_ref.shape[0]
    h = _layer_norm_rows(x_ref[...], lng_ref[...], lnb_ref[...])
    acc = jnp.dot(mg_ref[...].astype(BF16), wo_ref[...], preferred_element_type=F32)
    h1 = _layer_norm_rows(ALPHA * h + acc, g1_ref[...], b1_ref[...])
    h1_ref[...] = h1

    lt = _split3_nt(wrh_ref[...], wrl_ref[...], h1) + rb_ref[:, :1]
    ridx = lax.broadcasted_iota(jnp.int32, (EXPERTS_PER_GROUP, tm), 0)
    neg = jnp.float32(-jnp.inf)
    g_log = jnp.where(ridx < N_GROUPS, lt[N_EXPERTS:N_EXPERTS + EXPERTS_PER_GROUP], neg)
    g_max, grp = _first_argmax_rows(g_log, ridx, EXPERTS_PER_GROUP)
    p_grp = 1.0 / jnp.sum(jnp.exp(g_log - g_max), axis=0, keepdims=True)
    e_in = lt[0:EXPERTS_PER_GROUP]
    for gi in range(1, N_GROUPS):
        e_in = jnp.where(grp == gi, lt[gi * EXPERTS_PER_GROUP:(gi + 1) * EXPERTS_PER_GROUP], e_in)
    v0, i0 = _first_argmax_rows(e_in, ridx, EXPERTS_PER_GROUP)
    v1, i1 = _first_argmax_rows(jnp.where(ridx == i0, neg, e_in), ridx, EXPERTS_PER_GROUP)
    t = jnp.exp(v1 - v0)
    w0 = p_grp / (1.0 + t)
    w1 = p_grp * t / (1.0 + t)
    e0 = grp * EXPERTS_PER_GROUP + i0
    e1 = grp * EXPERTS_PER_GROUP + i1

    eidx = lax.broadcasted_iota(jnp.int32, (N_EXPERTS, tm), 0)
    hit0 = eidx == e0
    hit1 = eidx == e1
    oh = jnp.where(jnp.logical_or(hit0, hit1), 1.0, 0.0)
    before = jnp.dot(oh.astype(BF16), ut_ref[...], preferred_element_type=F32) + carry_ref[:, :1]
    r0 = jnp.sum(jnp.where(hit0, before, 0.0), axis=0, keepdims=True).astype(jnp.int32)
    r1 = jnp.sum(jnp.where(hit1, before, 0.0), axis=0, keepdims=True).astype(jnp.int32)
    carry_ref[...] = carry_ref[...] + jnp.sum(oh, axis=1, keepdims=True)
    cout_ref[...] = carry_ref[...]

    zi = jnp.zeros((META_ROWS - 4, tm), jnp.int32)
    meta_ref[...] = jnp.concatenate([e0, e1, r0, r1, zi], axis=0)
    wt = jnp.concatenate([w0, w1, jnp.zeros((LANES - 2, tm), F32)], axis=0)
    wcol_ref[...] = jnp.transpose(wt)


def _post(mg, x2d, prm, tm, row0, total_rows, carry_in, prev=None, zero_tail=False):
    m = x2d.shape[0]
    assert m % tm == 0 and row0 % tm == 0 and total_rows % tm == 0
    off = row0 // tm
    nsteps = m // tm
    last = nsteps - 1
    ut = jnp.asarray(np.triu(np.ones((tm, tm), np.float32), 1), dtype=BF16)
    full = lambda shape: pl.BlockSpec(shape, lambda i: (0,) * len(shape))
    in_specs = [
        pl.BlockSpec((tm, D_MODEL), lambda i: (jnp.minimum(i, last), 0)),
        pl.BlockSpec((tm, D_MODEL), lambda i: (jnp.minimum(i, last), 0)),
        full((1, D_MODEL)), full((1, D_MODEL)),
        full((D_MODEL, D_MODEL)),
        full((1, D_MODEL)), full((1, D_MODEL)),
        full((ROUTER_ROWS, D_MODEL)), full((ROUTER_ROWS, D_MODEL)), full((ROUTER_ROWS, LANES)),
        full((tm, tm)),
        full((N_EXPERTS, LANES)),
    ]
    args = [mg, x2d, prm["ln_emb_g"], prm["ln_emb_b"], prm["w_out"], prm["ln1_g"], prm["ln1_b"],
            prm["wr_hi"], prm["wr_lo"], prm["r_bias"], ut, carry_in]
    aliases = {}
    if prev is not None:
        for k, buf in enumerate(prev):
            in_specs.append(pl.BlockSpec(memory_space=pl.ANY))
            aliases[len(args)] = k
            args.append(buf)
    out_shape = [
        jax.ShapeDtypeStruct((total_rows, D_MODEL), F32),
        jax.ShapeDtypeStruct((META_ROWS, total_rows), jnp.int32),
        jax.ShapeDtypeStruct((total_rows, LANES), F32),
        jax.ShapeDtypeStruct((N_EXPERTS, LANES), F32),
    ]
    out_specs = [
        pl.BlockSpec((tm, D_MODEL), lambda i: (i + off, 0)),
        pl.BlockSpec((META_ROWS, tm), lambda i: (0, i + off)),
        pl.BlockSpec((tm, LANES), lambda i: (i + off, 0)),
        full((N_EXPERTS, LANES)),
    ]
    if prev is not None:
        assert len(prev) == 3
    return pl.pallas_call(
        functools.partial(_post_body, nsteps),
        grid=(nsteps + int(zero_tail),),
        in_specs=in_specs,
        out_specs=out_specs,
        out_shape=out_shape,
        input_output_aliases=aliases,
        scratch_shapes=[pltpu.VMEM((N_EXPERTS, LANES), F32)],
        compiler_params=_cparams(("arbitrary",)),
        name="post_attn",
    )(*args)


def _prep_post_params(ln_emb_g, ln_emb_b, w_out, ln1_g, ln1_b, w_rg, b_rg, w_re, b_re):
    row = lambda v: v.reshape(1, -1)
    wr = jnp.concatenate([w_re.T, w_rg.T, jnp.zeros((ROUTER_ROWS - N_EXPERTS - N_GROUPS, D_MODEL), F32)], axis=0)
    wr_hi = wr.astype(BF16)
    wr_lo = (wr - wr_hi.astype(F32)).astype(BF16)
    rb = jnp.concatenate([b_re, b_rg, jnp.zeros((ROUTER_ROWS - N_EXPERTS - N_GROUPS,), F32)])
    return dict(ln_emb_g=row(ln_emb_g), ln_emb_b=row(ln_emb_b), w_out=w_out.astype(BF16), ln1_g=row(ln1_g),
                ln1_b=row(ln1_b), wr_hi=wr_hi, wr_lo=wr_lo,
                r_bias=jnp.broadcast_to(rb[:, None], (ROUTER_ROWS, LANES)))


MOE_ROWS = 256
PACKED = D_MODEL // 2
U32 = jnp.uint32


def _pack_bf16_pairs(x):
    half = x.shape[1] // 2
    hi = pltpu.bitcast(x[:, :half].astype(BF16).astype(F32), U32)
    lo = pltpu.bitcast(x[:, half:].astype(BF16).astype(F32), U32)
    return hi | (lo >> 16)


def _unpack_bf16_pairs(u):
    hi = pltpu.bitcast(u & jnp.uint32(0xFFFF0000), F32)
    lo = pltpu.bitcast(u << 16, F32)
    return hi, lo


def _moe_plan(counts, total_assign):
    nb_max = -(-total_assign // MOE_ROWS) + N_EXPERTS
    padded = (counts + MOE_ROWS - 1) // MOE_ROWS * MOE_ROWS
    pend = jnp.cumsum(padded)
    pstart = (pend - padded).astype(jnp.int32)
    block_start = jnp.arange(nb_max, dtype=jnp.int32) * MOE_ROWS
    block_e = jnp.minimum(jnp.searchsorted(pend, block_start, side="right"), N_EXPERTS - 1).astype(jnp.int32)
    n_used = (pend[-1] // MOE_ROWS).astype(jnp.int32).reshape(1)
    return pstart, block_e, n_used, nb_max


def _dispatch_body(pstart_ref, meta_ref, h_ref, xs_in_ref, xs_ref, pk_ref, sem):
    del xs_in_ref
    tm = h_ref.shape[0]
    pk_ref[...] = _pack_bf16_pairs(h_ref[...])

    def row_copy(t, k):
        dest = pstart_ref[meta_ref[k, t]] + meta_ref[2 + k, t]
        return pltpu.make_async_copy(pk_ref.at[pl.ds(t, 1)], xs_ref.at[pl.ds(dest, 1)], sem)

    def issue(t, c):
        row_copy(t, 0).start()
        row_copy(t, 1).start()
        return c

    lax.fori_loop(0, tm, issue, 0)

    def drain(t, c):
        row_copy(t, 0).wait()
        row_copy(t, 1).wait()
        return c

    lax.fori_loop(0, tm, drain, 0)


def _dispatch(h1, meta, pstart, nslots, tm, total):
    assert total % tm == 0 and total <= h1.shape[0]
    zeros = jnp.zeros((nslots, PACKED), U32)
    return pl.pallas_call(
        _dispatch_body,
        grid_spec=pltpu.PrefetchScalarGridSpec(
            num_scalar_prefetch=1,
            grid=(total // tm,),
            in_specs=[
                pl.BlockSpec((META_ROWS, tm), lambda i, ps: (0, i), memory_space=pltpu.SMEM),
                pl.BlockSpec((tm, D_MODEL), lambda i, ps: (i, 0)),
                pl.BlockSpec(memory_space=pl.ANY),
            ],
            out_specs=pl.BlockSpec(memory_space=pl.ANY),
            scratch_shapes=[pltpu.VMEM((tm, PACKED), U32), pltpu.SemaphoreType.DMA(())],
        ),
        out_shape=jax.ShapeDtypeStruct((nslots, PACKED), U32),
        input_output_aliases={3: 0},
        compiler_params=_cparams(("arbitrary",)),
        name="moe_dispatch",
    )(pstart, meta, h1, zeros)


def _expert_body(be_ref, nu_ref, xs_ref, wg_ref, wu_ref, wd_ref, ys_ref, wgb_ref, wub_ref, wdb_ref):
    i = pl.program_id(0)
    changed = jnp.logical_or(i == 0, be_ref[i] != be_ref[jnp.maximum(i - 1, 0)])

    @pl.when(changed)
    def _():
        wgb_ref[...] = wg_ref[...].astype(BF16)
        wub_ref[...] = wu_ref[...].astype(BF16)
        wdb_ref[...] = wd_ref[...].astype(BF16)

    @pl.when(i < nu_ref[0])
    def _():
        hi, lo = _unpack_bf16_pairs(xs_ref[...])
        hi = hi.astype(BF16)
        lo = lo.astype(BF16)
        half = PACKED

        def proj(w_ref):
            return (jnp.dot(hi, w_ref[:half, :], preferred_element_type=F32)
                    + jnp.dot(lo, w_ref[half:, :], preferred_element_type=F32))

        g = proj(wgb_ref)
        hb = (g * _sigmoid(g)) * proj(wub_ref)
        y = jnp.dot(hb.astype(BF16), wdb_ref[...], preferred_element_type=F32)
        ys_ref[...] = _pack_bf16_pairs(y)

    @pl.when(i >= nu_ref[0])
    def _():
        ys_ref[...] = jnp.zeros_like(ys_ref)


def _experts(xs, block_e, n_used, w_gate, w_up, w_down, nb_max):
    wmap = lambda i, be, nu: (be[i], 0, 0)
    return pl.pallas_call(
        _expert_body,
        grid_spec=pltpu.PrefetchScalarGridSpec(
            num_scalar_prefetch=2,
            grid=(nb_max,),
            in_specs=[
                pl.BlockSpec((MOE_ROWS, PACKED), lambda i, be, nu: (i, 0)),
                pl.BlockSpec((None, D_MODEL, D_EXPERT), wmap),
                pl.BlockSpec((None, D_MODEL, D_EXPERT), wmap),
                pl.BlockSpec((None, D_EXPERT, D_MODEL), wmap),
            ],
            out_specs=pl.BlockSpec((MOE_ROWS, PACKED), lambda i, be, nu: (i, 0)),
            scratch_shapes=[pltpu.VMEM((D_MODEL, D_EXPERT), BF16), pltpu.VMEM((D_MODEL, D_EXPERT), BF16),
                            pltpu.VMEM((D_EXPERT, D_MODEL), BF16)],
        ),
        out_shape=jax.ShapeDtypeStruct(xs.shape, U32),
        compiler_params=_cparams(("arbitrary",)),
        name="moe_experts",
    )(block_e, n_used, xs, w_gate, w_up, w_down)


def _combine_body(pstart_ref, meta_ref, h_ref, w_ref, g2_ref, b2_ref, ys_ref, o_ref, buf_ref, sem):
    tm = h_ref.shape[0]

    def row_copy(t, k):
        src = pstart_ref[meta_ref[k, t]] + meta_ref[2 + k, t]
        return pltpu.make_async_copy(ys_ref.at[pl.ds(src, 1)], buf_ref.at[k, pl.ds(t, 1)], sem)

    def issue(t, c):
        row_copy(t, 0).start()
        row_copy(t, 1).start()
        return c

    lax.fori_loop(0, tm, issue, 0)

    def drain(t, c):
        row_copy(t, 0).wait()
        row_copy(t, 1).wait()
        return c

    lax.fori_loop(0, tm, drain, 0)
    w = w_ref[...]
    hi0, lo0 = _unpack_bf16_pairs(buf_ref[0])
    hi1, lo1 = _unpack_bf16_pairs(buf_ref[1])
    w0 = w[:, 0:1]
    w1 = w[:, 1:2]
    f = jnp.concatenate([w0 * hi0 + w1 * hi1, w0 * lo0 + w1 * lo1], axis=1)
    o_ref[...] = _layer_norm_rows(ALPHA * h_ref[...] + f, g2_ref[...], b2_ref[...])


def _combine(h1, wcol, meta, pstart, ys, ln2_g, ln2_b, tm, row0, nrows):
    assert nrows % tm == 0 and row0 % tm == 0
    off = row0 // tm
    return pl.pallas_call(
        _combine_body,
        grid_spec=pltpu.PrefetchScalarGridSpec(
            num_scalar_prefetch=1,
            grid=(nrows // tm,),
            in_specs=[
                pl.BlockSpec((META_ROWS, tm), lambda i, ps: (0, i + off), memory_space=pltpu.SMEM),
                pl.BlockSpec((tm, D_MODEL), lambda i, ps: (i + off, 0)),
                pl.BlockSpec((tm, LANES), lambda i, ps: (i + off, 0)),
                pl.BlockSpec((1, D_MODEL), lambda i, ps: (0, 0)),
                pl.BlockSpec((1, D_MODEL), lambda i, ps: (0, 0)),
                pl.BlockSpec(memory_space=pl.ANY),
            ],
            out_specs=pl.BlockSpec((tm, D_MODEL), lambda i, ps: (i, 0)),
            scratch_shapes=[pltpu.VMEM((TOP_K, tm, PACKED), U32), pltpu.SemaphoreType.DMA(())],
        ),
        out_shape=jax.ShapeDtypeStruct((nrows, D_MODEL), F32),
        compiler_params=_cparams(("arbitrary",)),
        name="moe_combine",
    )(pstart, meta, h1, wcol, ln2_g.reshape(1, -1), ln2_b.reshape(1, -1), ys)


PROJ_ROWS = 512
POST_ROWS = 512
DISPATCH_ROWS = 384
COMBINE_ROWS = 256


def kernel(x_prompt, x_sample, state_gla, cache_swa_k, cache_swa_v, meta_tokens, ln_emb_g, ln_emb_b, rel_bias, w_in,
           gk_up, gk_bias, gla_norm_g, sinks, w_out, ln1_g, ln1_b, w_router_group, b_router_group, w_router_expert,
           b_router_expert, w_gate, w_up, w_down, ln2_g, ln2_b):
    nbatch, seq, d = x_prompt.shape
    nseq = x_sample.shape[0]
    assert w_in.shape[0] == DEPTH == 1 and d == D_MODEL and x_sample.shape[1] == 1
    assert seq % BLK == 0 and nseq == BLK and meta_tokens.shape[0] == N_META
    nblk = seq // BLK
    n_prompt = nbatch * seq
    total = n_prompt + nseq
    kvw = ATT_KV_HEADS * HEAD_DIM

    xp = x_prompt.reshape(n_prompt, d)
    xs = x_sample.reshape(nseq, d)
    extra = jnp.concatenate([xs, jnp.zeros((BLK - N_META, d), xs.dtype), meta_tokens.astype(xs.dtype)], axis=0)
    w_cat = _prep_w_in(w_in[0])
    pp = _ln_proj(xp, ln_emb_g, ln_emb_b, w_cat, PROJ_ROWS)
    pe = _ln_proj(extra, ln_emb_g, ln_emb_b, w_cat, 2 * BLK)

    yg, gla_p = _gla_prompt(pp, pe, gk_up[0], gk_bias[0], gla_norm_g[0], nbatch, nblk)
    yg_s, gla_s = _gla_step(pe, gk_up[0], gk_bias[0], gla_norm_g[0], state_gla[0])
    mg = _swa_prompt(pp, pe, yg, rel_bias, sinks[0], nbatch, nblk)
    mg_s, k_s, v_s = _swa_step(pe, yg_s, cache_swa_k[0].reshape(nseq, WINDOW, kvw),
                               cache_swa_v[0].reshape(nseq, WINDOW, kvw), rel_bias, sinks[0])

    prm = _prep_post_params(ln_emb_g, ln_emb_b, w_out[0], ln1_g[0], ln1_b[0], w_router_group[0], b_router_group[0],
                            w_router_expert[0], b_router_expert[0])
    carry0 = jnp.zeros((N_EXPERTS, LANES), F32)
    rows_alloc = n_prompt + POST_ROWS
    h1, meta, wcol, carry1 = _post(mg, xp, prm, POST_ROWS, 0, rows_alloc, carry0, zero_tail=True)
    h1, meta, wcol, carry2 = _post(mg_s, xs, prm, nseq, n_prompt, rows_alloc, carry1, prev=(h1, meta, wcol))

    counts = carry2[:, 0].astype(jnp.int32)
    pstart, block_e, n_used, nb_max = _moe_plan(counts, TOP_K * total)
    slots = _dispatch(h1, meta, pstart, nb_max * MOE_ROWS, DISPATCH_ROWS, total)
    ys = _experts(slots, block_e, n_used, w_gate[0], w_up[0], w_down[0], nb_max)
    y_p = _combine(h1, wcol, meta, pstart, ys, ln2_g[0], ln2_b[0], COMBINE_ROWS, 0, n_prompt)
    y_s = _combine(h1, wcol, meta, pstart, ys, ln2_g[0], ln2_b[0], nseq, n_prompt, nseq)

    kv_shape = (1, nbatch, WINDOW, ATT_KV_HEADS, HEAD_DIM)
    k_p = pp["ka"].reshape(nbatch, seq, kvw)[:, seq - WINDOW:].reshape(kv_shape)
    v_p = pp["va"].reshape(nbatch, seq, kvw)[:, seq - WINDOW:].reshape(kv_shape)
    return (y_p.reshape(nbatch, seq, d), y_s.reshape(nseq, 1, d), gla_p[None], k_p, v_p, gla_s[None],
            k_s.reshape(cache_swa_k.shape), v_s.reshape(cache_swa_v.shape))
```

```python
import functools
import math

import jax
import jax.numpy as jnp
import numpy as np
from jax import lax
from jax.experimental import pallas as pl
from jax.experimental.pallas import tpu as pltpu

F32 = jnp.float32
BF16 = jnp.bfloat16

D_MODEL = 1024
N_META = 16
LN_EPS = 1e-5
GLA_HEADS = 4
GLA_DK = 128
GLA_DV = 256
GLA_RANK = 16
GLA_TAU = 16.0
HEAD_DIM = 64
ATT_HEADS = 16
ATT_KV_HEADS = 4
GQA_GROUP = 4
WINDOW = 128
REL_BUCKETS = 32
REL_MAX_DIST = 128
N_GROUPS = 4
EXPERTS_PER_GROUP = 8
N_EXPERTS = 32
TOP_K = 2
D_EXPERT = 512
DEPTH = 1
ALPHA = (2.0 * DEPTH) ** 0.25

LANES = 128
BLK = 128
VMEM_LIMIT = 56 * 1024 * 1024


def _cparams(sem):
    return pltpu.CompilerParams(dimension_semantics=sem, vmem_limit_bytes=VMEM_LIMIT)


def _layer_norm_rows(x, g, b):
    mu = jnp.mean(x, axis=-1, keepdims=True)
    xc = x - mu
    var = jnp.mean(xc * xc, axis=-1, keepdims=True)
    return xc * lax.rsqrt(var + LN_EPS) * g + b


_PROJ_OUTS = (
    ("qg", GLA_HEADS * GLA_DK, BF16, GLA_DK ** -0.5),
    ("kg", GLA_HEADS * GLA_DK, BF16, None),
    ("vg", GLA_HEADS * GLA_DV, BF16, None),
    ("rg", GLA_HEADS * GLA_DV, BF16, None),
    ("qa", ATT_HEADS * HEAD_DIM, BF16, HEAD_DIM ** -0.5),
    ("ka", ATT_KV_HEADS * HEAD_DIM, F32, None),
    ("va", ATT_KV_HEADS * HEAD_DIM, F32, None),
    ("ga", D_MODEL, BF16, None),
    ("gb", D_MODEL, BF16, None),
    ("lr", LANES, F32, None),
)
_PROJ_W = sum(w for _, w, _, _ in _PROJ_OUTS)


def _prep_w_in(w_in):
    sizes = (512, 512, 1024, 1024, GLA_RANK, 1024, 256, 256, 1024, 1024)
    offs = np.cumsum((0,) + sizes)
    a = w_in[:, : offs[4]]
    lr = w_in[:, offs[4]: offs[5]]
    b = w_in[:, offs[5]:]
    pad = jnp.zeros((w_in.shape[0], LANES - GLA_RANK), w_in.dtype)
    return jnp.concatenate([a, b, lr, pad], axis=1).astype(BF16)


def _ln_proj_body(x_ref, g_ref, b_ref, w_ref, *out_refs):
    xn = _layer_norm_rows(x_ref[...], g_ref[...], b_ref[...]).astype(BF16)
    c0 = 0
    for (_, width, dtype, scale), o_ref in zip(_PROJ_OUTS, out_refs):
        acc = jnp.dot(xn, w_ref[:, c0:c0 + width], preferred_element_type=F32)
        if scale is not None:
            acc = acc * scale
        o_ref[...] = acc.astype(dtype)
        c0 += width


def _ln_proj(x2d, ln_g, ln_b, w_cat, tm):
    m = x2d.shape[0]
    assert m % tm == 0
    out_shape = [jax.ShapeDtypeStruct((m, w), dt) for _, w, dt, _ in _PROJ_OUTS]
    out_specs = [pl.BlockSpec((tm, w), lambda i: (i, 0)) for _, w, _, _ in _PROJ_OUTS]
    outs = pl.pallas_call(
        _ln_proj_body,
        grid=(m // tm,),
        in_specs=[
            pl.BlockSpec((tm, D_MODEL), lambda i: (i, 0)),
            pl.BlockSpec((1, D_MODEL), lambda i: (0, 0)),
            pl.BlockSpec((1, D_MODEL), lambda i: (0, 0)),
            pl.BlockSpec((D_MODEL, _PROJ_W), lambda i: (0, 0), pipeline_mode=pl.Buffered(1)),
        ],
        out_specs=out_specs,
        out_shape=out_shape,
        compiler_params=_cparams(("arbitrary",)),
        name="ln_proj",
    )(x2d, ln_g.reshape(1, -1), ln_b.reshape(1, -1), w_cat)
    return dict(zip([n for n, _, _, _ in _PROJ_OUTS], outs))


_GLA_LEVELS = tuple(2 ** i for i in range(int(math.log2(BLK))))


def _log_sigmoid(x):
    return jnp.minimum(x, 0.0) - jnp.log1p(jnp.exp(-jnp.abs(x)))


def _sigmoid(x):
    return 1.0 / (1.0 + jnp.exp(-x))


def _split_dot(a01, x):
    hi = x.astype(BF16)
    lo = (x - hi.astype(F32)).astype(BF16)
    return (jnp.dot(a01, hi, preferred_element_type=F32) + jnp.dot(a01, lo, preferred_element_type=F32))


def _gla_anchor_exponent(b, la, s, row):
    if s == 1:
        return jnp.where(row % 2 == 1, la, 0.0)
    if s == 2:
        la_dn = pltpu.roll(la, 1, axis=0)
        la_up = pltpu.roll(la, BLK - 1, axis=0)
        r = row % 4
        return jnp.where(r == 0, la_up, jnp.where(r == 1, 0.0, jnp.where(r == 2, la, la + la_dn)))
    nb = BLK // (2 * s)
    b3 = b.reshape(nb, 2 * s, b.shape[-1])
    anchor = jnp.broadcast_to(b3[:, s - 1:s, :], b3.shape).reshape(b.shape)
    return -jnp.abs(b - anchor)


def _gla_body(nblk, qm, km, vm, rm, lrm, gam, qp, kp, vp, rp, lrp, gap, gkup_ref, gkb_ref, gn_ref, tri_ref,
              y_ref, s_out_ref, s_ref):
    c = pl.program_id(1)
    is_meta = c == 0

    @pl.when(is_meta)
    def _():
        s_ref[...] = jnp.zeros_like(s_ref)

    def pick(m_ref, p_ref):
        return jnp.where(is_meta, m_ref[...], p_ref[...])

    row = lax.broadcasted_iota(jnp.int32, (BLK, GLA_DK), 0)
    col_t = lax.broadcasted_iota(jnp.int32, (BLK, BLK), 1)
    row_t = lax.broadcasted_iota(jnp.int32, (BLK, BLK), 0)
    live = jnp.logical_or(jnp.logical_not(is_meta), row >= BLK - N_META)
    tri = tri_ref[...]
    q_all, k_all, v_all, r_all = pick(qm, qp), pick(km, kp), pick(vm, vp), pick(rm, rp)
    ga_all = pick(gam, gap)
    lr = pick(lrm, lrp).astype(BF16)
    for h in range(GLA_HEADS):
        dk = slice(h * GLA_DK, (h + 1) * GLA_DK)
        dv = slice(h * GLA_DV, (h + 1) * GLA_DV)
        x = jnp.dot(lr, gkup_ref[:, dk], preferred_element_type=F32) + gkb_ref[:, dk]
        la = jnp.where(live, _log_sigmoid(x) * (1.0 / GLA_TAU), 0.0)
        q = q_all[:, dk].astype(F32)
        k = jnp.where(live, k_all[:, dk].astype(F32), 0.0)
        v = v_all[:, dv]
        b = _split_dot(tri, la)
        b_last = b[BLK - 1:BLK, :]
        s_old = s_ref[h]
        o = jnp.dot((q * jnp.exp(b)).astype(BF16), s_old.astype(BF16), preferred_element_type=F32)
        kd = (k * jnp.exp(b_last - b)).astype(BF16)
        kv = lax.dot_general(kd, v, (((0,), (0,)), ((), ())), preferred_element_type=F32)
        decay_col = jnp.transpose(jnp.broadcast_to(jnp.exp(b_last), (BLK, GLA_DK)))[:, :1]
        s_new = decay_col * s_old + kv
        s_ref[h] = s_new
        a = jnp.where(row_t == col_t, jnp.dot(q.astype(BF16), k.astype(BF16).T, preferred_element_type=F32), 0.0)
        for s in _GLA_LEVELS:
            e = jnp.exp(_gla_anchor_exponent(b, la, s, row))
            upper = (row // s) % 2 == 1
            q_s = jnp.where(upper, q * e, 0.0).astype(BF16)
            k_s = jnp.where(upper, 0.0, k * e).astype(BF16)
            p = lax.dot_general(q_s, k_s, (((1,), (1,)), ((), ())), preferred_element_type=F32)
            a = a + jnp.where(row_t // (2 * s) == col_t // (2 * s), p, 0.0)
        o = o + jnp.dot(a.astype(BF16), v, preferred_element_type=F32)
        o = o * lax.rsqrt(jnp.mean(o * o, axis=-1, keepdims=True) + LN_EPS) * gn_ref[...]
        r = r_all[:, dv].astype(F32)
        y = o * (r * _sigmoid(r)) * _sigmoid(ga_all[:, dv].astype(F32))
        y_ref[:, dv] = y.astype(y_ref.dtype)

    @pl.when(c == nblk)
    def _():
        s_out_ref[...] = s_ref[...]


def _tri_incl():
    i = np.arange(BLK)
    return jnp.asarray((i[None, :] <= i[:, None]).astype(np.float32), dtype=BF16)


def _gla_prompt(pp, pe, gk_up, gk_bias, gnorm, nbatch, nblk):
    names = ("qg", "kg", "vg", "rg", "lr", "ga")
    gkup = jnp.concatenate([gk_up, jnp.zeros((LANES - GLA_RANK, gk_up.shape[1]), gk_up.dtype)], axis=0).astype(BF16)
    m_specs = [pl.BlockSpec((BLK, pe[n].shape[1]), lambda b, c: (1, 0)) for n in names]
    p_specs = [pl.BlockSpec((BLK, pp[n].shape[1]), lambda b, c: (b * nblk + jnp.maximum(c - 1, 0), 0)) for n in names]
    w_specs = [
        pl.BlockSpec((LANES, GLA_HEADS * GLA_DK), lambda b, c: (0, 0)),
        pl.BlockSpec((1, GLA_HEADS * GLA_DK), lambda b, c: (0, 0)),
        pl.BlockSpec((1, GLA_DV), lambda b, c: (0, 0)),
        pl.BlockSpec((BLK, BLK), lambda b, c: (0, 0)),
    ]
    y, s_fin = pl.pallas_call(
        functools.partial(_gla_body, nblk),
        grid=(nbatch, nblk + 1),
        in_specs=m_specs + p_specs + w_specs,
        out_specs=[
            pl.BlockSpec((BLK, D_MODEL), lambda b, c: (b * nblk + jnp.maximum(c - 1, 0), 0)),
            pl.BlockSpec((None, GLA_HEADS, GLA_DK, GLA_DV), lambda b, c: (b, 0, 0, 0)),
        ],
        out_shape=[
            jax.ShapeDtypeStruct((nbatch * nblk * BLK, D_MODEL), BF16),
            jax.ShapeDtypeStruct((nbatch, GLA_HEADS, GLA_DK, GLA_DV), F32),
        ],
        scratch_shapes=[pltpu.VMEM((GLA_HEADS, GLA_DK, GLA_DV), F32)],
        compiler_params=_cparams(("arbitrary", "arbitrary")),
        name="gla_prompt",
    )(*[pe[n] for n in names], *[pp[n] for n in names], gkup, gk_bias.reshape(1, -1), gnorm.reshape(1, -1), _tri_incl())
    return y, s_fin


GLA_STEP_SEQS = 16


def _gla_step_body(q_ref, k_ref, v_ref, r_ref, lr_ref, ga_ref, gkup_ref, gkb_ref, gn_ref, s_in_ref,
                   y_ref, s_out_ref, at_ref, kt_ref, qt_ref):
    g = pl.program_id(0)
    nseq = q_ref.shape[0]

    @pl.when(g == 0)
    def _():
        x = jnp.dot(lr_ref[...].astype(BF16), gkup_ref[...], preferred_element_type=F32) + gkb_ref[...]
        a = jnp.exp(_log_sigmoid(x) * (1.0 / GLA_TAU))
        for h in range(GLA_HEADS):
            dk = slice(h * GLA_DK, (h + 1) * GLA_DK)
            at_ref[h] = jnp.transpose(a[:, dk])
            kt_ref[h] = jnp.transpose(k_ref[:, dk].astype(F32))
            qt_ref[h] = jnp.transpose(q_ref[:, dk].astype(F32))

    lane = lax.broadcasted_iota(jnp.int32, (GLA_DK, nseq), 1)
    ones = jnp.ones((nseq, GLA_DV), BF16)
    grp = pl.ds(pl.multiple_of(g * GLA_STEP_SEQS, GLA_STEP_SEQS), GLA_STEP_SEQS)
    r_grp = r_ref[grp, :].astype(F32)
    ga_grp = ga_ref[grp, :].astype(F32)
    for i in range(GLA_STEP_SEQS):
        n = g * GLA_STEP_SEQS + i
        sel = lane == n
        for h in range(GLA_HEADS):
            dv = slice(h * GLA_DV, (h + 1) * GLA_DV)
            a_sel = jnp.where(sel, at_ref[h], 0.0)
            k_sel = jnp.where(sel, kt_ref[h], 0.0).astype(BF16)
            q_sel = jnp.where(sel, qt_ref[h], 0.0).astype(BF16)
            decay = _split_dot_rhs(a_sel, ones)
            kv = jnp.dot(k_sel, v_ref[:, dv], preferred_element_type=F32)
            q_b = jnp.dot(q_sel, ones, preferred_element_type=F32)
            s_new = decay * s_in_ref[i, h] + kv
            s_out_ref[i, h] = s_new
            o = jnp.sum(q_b * s_new, axis=0, keepdims=True)
            o = o * lax.rsqrt(jnp.mean(o * o, axis=-1, keepdims=True) + LN_EPS) * gn_ref[...]
            r = r_grp[i:i + 1, dv]
            ga = ga_grp[i:i + 1, dv]
            y_ref[i:i + 1, dv] = (o * (r * _sigmoid(r)) * _sigmoid(ga)).astype(y_ref.dtype)


def _split_dot_rhs(x, b01):
    hi = x.astype(BF16)
    lo = (x - hi.astype(F32)).astype(BF16)
    return jnp.dot(hi, b01, preferred_element_type=F32) + jnp.dot(lo, b01, preferred_element_type=F32)


def _gla_step(pe, gk_up, gk_bias, gnorm, state):
    nseq = state.shape[0]
    assert nseq == BLK and nseq % GLA_STEP_SEQS == 0
    names = ("qg", "kg", "vg", "rg", "lr", "ga")
    gkup = jnp.concatenate([gk_up, jnp.zeros((LANES - GLA_RANK, gk_up.shape[1]), gk_up.dtype)], axis=0).astype(BF16)
    t_specs = [pl.BlockSpec((nseq, pe[n].shape[1]), lambda g: (0, 0)) for n in names]
    st_spec = pl.BlockSpec((GLA_STEP_SEQS, GLA_HEADS, GLA_DK, GLA_DV), lambda g: (g, 0, 0, 0))
    return pl.pallas_call(
        _gla_step_body,
        grid=(nseq // GLA_STEP_SEQS,),
        in_specs=t_specs + [
            pl.BlockSpec((LANES, GLA_HEADS * GLA_DK), lambda g: (0, 0)),
            pl.BlockSpec((1, GLA_HEADS * GLA_DK), lambda g: (0, 0)),
            pl.BlockSpec((1, GLA_DV), lambda g: (0, 0)),
            st_spec,
        ],
        out_specs=[pl.BlockSpec((GLA_STEP_SEQS, D_MODEL), lambda g: (g, 0)), st_spec],
        out_shape=[jax.ShapeDtypeStruct((nseq, D_MODEL), F32), jax.ShapeDtypeStruct(state.shape, F32)],
        scratch_shapes=[pltpu.VMEM((GLA_HEADS, GLA_DK, nseq), F32) for _ in range(3)],
        compiler_params=_cparams(("arbitrary",)),
        name="gla_step",
    )(*[pe[n] for n in names], gkup, gk_bias.reshape(1, -1), gnorm.reshape(1, -1), state)


HALF = LANES // 2


def _rel_bucket(dist):
    max_exact = REL_BUCKETS // 2
    d = jnp.maximum(dist, 0)
    large = max_exact + (jnp.log(jnp.maximum(d, 1).astype(F32) / max_exact)
                         / math.log(REL_MAX_DIST / max_exact) * (REL_BUCKETS - max_exact)).astype(jnp.int32)
    large = jnp.minimum(large, REL_BUCKETS - 1)
    return jnp.where(d < max_exact, d, large)


def _bias_lookup(rel_bias, dist):
    onehot = (_rel_bucket(dist)[..., None] == jnp.arange(REL_BUCKETS)).astype(F32)
    return jnp.einsum("...b,bh->h...", onehot, rel_bias.astype(F32), precision=lax.Precision.HIGHEST)


def _swa_bias_tables(rel_bias):
    q = jnp.arange(BLK)[:, None]
    c = jnp.arange(2 * BLK)[None, :]
    dist = BLK + q - c
    bias = _bias_lookup(rel_bias, dist)
    inside = (dist >= 0) & (dist < WINDOW)
    first = inside & (c >= BLK - N_META)
    neg = jnp.float32(-jnp.inf)
    return jnp.stack([jnp.where(first[None], bias, neg), jnp.where(inside[None], bias, neg)])


def _half_tiles(x):
    lane = lax.broadcasted_iota(jnp.int32, (x.shape[0], LANES), 1)
    low = lane < HALF
    out = []
    for t in range(2):
        tile = x[:, t * LANES:(t + 1) * LANES]
        swapped = pltpu.roll(tile, HALF, axis=1)
        zero = jnp.zeros_like(tile)
        even = (jnp.where(low, tile, zero).astype(BF16), jnp.where(low, zero, swapped).astype(BF16))
        odd = (jnp.where(low, swapped, zero).astype(BF16), jnp.where(low, zero, tile).astype(BF16))
        out += [even, odd]
    return out


def _swa_body(nblk, q_ref, kc_ref, vc_ref, kp_ref, vp_ref, km_ref, vm_ref, gb_ref, yg_ref, tb_ref, sink_ref,
              o_ref):
    blk = pl.program_id(1)
    first = blk == 0
    k_prev = jnp.where(first, km_ref[...], kp_ref[...])
    v_prev = jnp.where(first, vm_ref[...], vp_ref[...])
    k_tiles = _half_tiles(jnp.concatenate([k_prev, kc_ref[...]], axis=0))
    v_tiles = _half_tiles(jnp.concatenate([v_prev, vc_ref[...]], axis=0))
    variant = jnp.minimum(blk, 1)
    lane = lax.broadcasted_iota(jnp.int32, (BLK, LANES), 1)
    low = lane < HALF
    nt = (((1,), (1,)), ((), ()))
    for j in range(ATT_KV_HEADS):
        for pair in range(GQA_GROUP // 2):
            t = j * (GQA_GROUP // 2) + pair
            q_t = q_ref[:, t * LANES:(t + 1) * LANES]
            acc = None
            inv = []
            for half in range(2):
                h = 2 * t + half
                s = lax.dot_general(q_t, k_tiles[j][half], nt, preferred_element_type=F32)
                s = s + tb_ref[variant, h]
                sink = sink_ref[h]
                m = jnp.maximum(jnp.max(s, axis=-1, keepdims=True), sink)
                p = jnp.exp(s - m)
                inv.append(1.0 / (jnp.sum(p, axis=-1, keepdims=True) + jnp.exp(sink - m)))
                pv = jnp.dot(p.astype(BF16), v_tiles[j][half], preferred_element_type=F32)
                acc = pv if acc is None else acc + pv
            o = acc * jnp.where(low, inv[0], inv[1])
            cols = slice(t * LANES, (t + 1) * LANES)
            gate = _sigmoid(gb_ref[:, cols].astype(F32))
            o_ref[:, cols] = (gate * o + yg_ref[:, cols].astype(F32)).astype(o_ref.dtype)


def _swa_prompt(pp, pe, yg, rel_bias, sinks, nbatch, nblk):
    tb = _swa_bias_tables(rel_bias)
    kvw = ATT_KV_HEADS * HEAD_DIM
    cur = lambda b, c: (b * nblk + c, 0)
    prev = lambda b, c: (b * nblk + jnp.maximum(c - 1, 0), 0)
    return pl.pallas_call(
        functools.partial(_swa_body, nblk),
        grid=(nbatch, nblk),
        in_specs=[
            pl.BlockSpec((BLK, D_MODEL), cur),
            pl.BlockSpec((BLK, kvw), cur), pl.BlockSpec((BLK, kvw), cur),
            pl.BlockSpec((BLK, kvw), prev), pl.BlockSpec((BLK, kvw), prev),
            pl.BlockSpec((BLK, kvw), lambda b, c: (1, 0)), pl.BlockSpec((BLK, kvw), lambda b, c: (1, 0)),
            pl.BlockSpec((BLK, D_MODEL), cur),
            pl.BlockSpec((BLK, D_MODEL), cur),
            pl.BlockSpec(tb.shape, lambda b, c: (0, 0, 0, 0)),
            pl.BlockSpec(memory_space=pltpu.SMEM),
        ],
        out_specs=pl.BlockSpec((BLK, D_MODEL), cur),
        out_shape=jax.ShapeDtypeStruct((nbatch * nblk * BLK, D_MODEL), BF16),
        compiler_params=_cparams(("arbitrary", "arbitrary")),
        name="swa_prompt",
    )(pp["qa"], pp["ka"], pp["va"], pp["ka"], pp["va"], pe["ka"], pe["va"], pp["gb"], yg, tb, sinks)


SWA_STEP_SEQS = 8
Q_TILES = ATT_HEADS // 2


def _swa_step_body(q_ref, kn_ref, vn_ref, ck_ref, cv_ref, gb_ref, yg_ref, tb_ref, sink_ref,
                   o_ref, ko_ref, vo_ref):
    row = lax.broadcasted_iota(jnp.int32, (WINDOW, ATT_KV_HEADS * HEAD_DIM), 0)
    trow = lax.broadcasted_iota(jnp.int32, (Q_TILES, LANES), 0)
    low = lax.broadcasted_iota(jnp.int32, (Q_TILES, LANES), 1) < HALF
    nt = (((1,), (1,)), ((), ()))
    for i in range(SWA_STEP_SEQS):
        k_win = jnp.where(row == WINDOW - 1, kn_ref[i:i + 1, :], pltpu.roll(ck_ref[i], WINDOW - 1, axis=0))
        v_win = jnp.where(row == WINDOW - 1, vn_ref[i:i + 1, :], pltpu.roll(cv_ref[i], WINDOW - 1, axis=0))
        ko_ref[i] = k_win
        vo_ref[i] = v_win
        k_tiles = _half_tiles(k_win)
        v_tiles = _half_tiles(v_win)
        q8 = q_ref[i].astype(BF16)
        s = [jnp.zeros((Q_TILES, WINDOW), F32), jnp.zeros((Q_TILES, WINDOW), F32)]
        for j in range(ATT_KV_HEADS):
            mine = trow // (GQA_GROUP // 2) == j
            for half in range(2):
                sj = lax.dot_general(q8, k_tiles[j][half], nt, preferred_element_type=F32)
                s[half] = jnp.where(mine, sj, s[half])
        p, inv = [], []
        for half in range(2):
            sh = s[half] + tb_ref[half]
            sink = sink_ref[half]
            m = jnp.maximum(jnp.max(sh, axis=-1, keepdims=True), sink)
            ph = jnp.exp(sh - m)
            inv.append(1.0 / (jnp.sum(ph, axis=-1, keepdims=True) + jnp.exp(sink - m)))
            p.append(ph.astype(BF16))
        o = jnp.zeros((Q_TILES, LANES), F32)
        for j in range(ATT_KV_HEADS):
            mine = trow // (GQA_GROUP // 2) == j
            pv = (jnp.dot(p[0], v_tiles[j][0], preferred_element_type=F32)
                  + jnp.dot(p[1], v_tiles[j][1], preferred_element_type=F32))
            o = jnp.where(mine, pv, o)
        o = o * jnp.where(low, inv[0], inv[1])
        o_ref[i] = _sigmoid(gb_ref[i]) * o + yg_ref[i]


def _swa_step(pe, yg_s, cache_k, cache_v, rel_bias, sinks):
    nseq = cache_k.shape[0]
    kvw = ATT_KV_HEADS * HEAD_DIM
    as_tiles = lambda x: x[:nseq].astype(F32).reshape(nseq, Q_TILES, LANES)
    dist = (WINDOW - 1) - jnp.arange(WINDOW)
    bias = _bias_lookup(rel_bias, dist)
    tb = jnp.stack([bias[0::2], bias[1::2]])
    sk = jnp.broadcast_to(jnp.stack([sinks[0::2], sinks[1::2]])[:, :, None], (2, Q_TILES, 1)).astype(F32)
    g = SWA_STEP_SEQS
    tile_spec = pl.BlockSpec((g, Q_TILES, LANES), lambda s: (s, 0, 0))
    win_spec = pl.BlockSpec((g, WINDOW, kvw), lambda s: (s, 0, 0))
    new_spec = pl.BlockSpec((g, kvw), lambda s: (s, 0))
    o, ko, vo = pl.pallas_call(
        _swa_step_body,
        grid=(nseq // g,),
        in_specs=[tile_spec, new_spec, new_spec, win_spec, win_spec, tile_spec, tile_spec,
                  pl.BlockSpec(tb.shape, lambda s: (0, 0, 0)), pl.BlockSpec(sk.shape, lambda s: (0, 0, 0))],
        out_specs=[tile_spec, win_spec, win_spec],
        out_shape=[jax.ShapeDtypeStruct((nseq, Q_TILES, LANES), F32),
                   jax.ShapeDtypeStruct(cache_k.shape, F32), jax.ShapeDtypeStruct(cache_v.shape, F32)],
        compiler_params=_cparams(("arbitrary",)),
        name="swa_step",
    )(as_tiles(pe["qa"]), pe["ka"], pe["va"], cache_k, cache_v, as_tiles(pe["gb"]),
      yg_s.reshape(nseq, Q_TILES, LANES), tb, sk)
    return o.reshape(nseq, D_MODEL), ko, vo


ROUTER_ROWS = 40
META_ROWS = 8


def _split3_nt(a_hi, a_lo, x):
    nt = (((1,), (1,)), ((), ()))
    x_hi = x.astype(BF16)
    x_lo = (x - x_hi.astype(F32)).astype(BF16)
    return (lax.dot_general(a_hi, x_hi, nt, preferred_element_type=F32)
            + lax.dot_general(a_hi, x_lo, nt, preferred_element_type=F32)
            + lax.dot_general(a_lo, x_hi, nt, preferred_element_type=F32))


def _first_argmax_rows(v, ridx, nrows):
    vmax = jnp.max(v, axis=0, keepdims=True)
    idx = jnp.min(jnp.where(v == vmax, ridx, nrows), axis=0, keepdims=True)
    return vmax, idx


def _post_body(nsteps, *refs):
    h1_ref, meta_ref, wcol_ref = refs[-5:-2]
    i = pl.program_id(0)

    @pl.when(i < nsteps)
    def _():
        _post_tile(i, *refs)

    @pl.when(i >= nsteps)
    def _():
        h1_ref[...] = jnp.zeros_like(h1_ref)
        meta_ref[...] = jnp.zeros_like(meta_ref)
        wcol_ref[...] = jnp.zeros_like(wcol_ref)


def _post_tile(i, mg_ref, x_ref, lng_ref, lnb_ref, wo_ref, g1_ref, b1_ref, wrh_ref, wrl_ref, rb_ref, ut_ref,
               cin_ref, *rest):
    h1_ref, meta_ref, wcol_ref, cout_ref, carry_ref = rest[-5:]

    @pl.when(i == 0)
    def _():
        carry_ref[...] = cin_ref[...]

    tm = x_ref.shape[0]
    h = _layer_norm_rows(x_ref[...], lng_ref[...], lnb_ref[...])
    acc = jnp.dot(mg_ref[...].astype(BF16), wo_ref[...], preferred_element_type=F32)
    h1 = _layer_norm_rows(ALPHA * h + acc, g1_ref[...], b1_ref[...])
    h1_ref[...] = h1

    lt = _split3_nt(wrh_ref[...], wrl_ref[...], h1) + rb_ref[:, :1]
    ridx = lax.broadcasted_iota(jnp.int32, (EXPERTS_PER_GROUP, tm), 0)
    neg = jnp.float32(-jnp.inf)
    g_log = jnp.where(ridx < N_GROUPS, lt[N_EXPERTS:N_EXPERTS + EXPERTS_PER_GROUP], neg)
    g_max, grp = _first_argmax_rows(g_log, ridx, EXPERTS_PER_GROUP)
    p_grp = 1.0 / jnp.sum(jnp.exp(g_log - g_max), axis=0, keepdims=True)
    e_in = lt[0:EXPERTS_PER_GROUP]
    for gi in range(1, N_GROUPS):
        e_in = jnp.where(grp == gi, lt[gi * EXPERTS_PER_GROUP:(gi + 1) * EXPERTS_PER_GROUP], e_in)
    v0, i0 = _first_argmax_rows(e_in, ridx, EXPERTS_PER_GROUP)
    v1, i1 = _first_argmax_rows(jnp.where(ridx == i0, neg, e_in), ridx, EXPERTS_PER_GROUP)
    t = jnp.exp(v1 - v0)
    w0 = p_grp / (1.0 + t)
    w1 = p_grp * t / (1.0 + t)
    e0 = grp * EXPERTS_PER_GROUP + i0
    e1 = grp * EXPERTS_PER_GROUP + i1

    eidx = lax.broadcasted_iota(jnp.int32, (N_EXPERTS, tm), 0)
    hit0 = eidx == e0
    hit1 = eidx == e1
    oh = jnp.where(jnp.logical_or(hit0, hit1), 1.0, 0.0)
    before = jnp.dot(oh.astype(BF16), ut_ref[...], preferred_element_type=F32) + carry_ref[:, :1]
    r0 = jnp.sum(jnp.where(hit0, before, 0.0), axis=0, keepdims=True).astype(jnp.int32)
    r1 = jnp.sum(jnp.where(hit1, before, 0.0), axis=0, keepdims=True).astype(jnp.int32)
    carry_ref[...] = carry_ref[...] + jnp.sum(oh, axis=1, keepdims=True)
    cout_ref[...] = carry_ref[...]

    zi = jnp.zeros((META_ROWS - 4, tm), jnp.int32)
    meta_ref[...] = jnp.concatenate([e0, e1, r0, r1, zi], axis=0)
    wt = jnp.concatenate([w0, w1, jnp.zeros((LANES - 2, tm), F32)], axis=0)
    wcol_ref[...] = jnp.transpose(wt)


def _post(mg, x2d, prm, tm, row0, total_rows, carry_in, prev=None, zero_tail=False):
    m = x2d.shape[0]
    assert m % tm == 0 and row0 % tm == 0 and total_rows % tm == 0
    off = row0 // tm
    nsteps = m // tm
    last = nsteps - 1
    ut = jnp.asarray(np.triu(np.ones((tm, tm), np.float32), 1), dtype=BF16)
    full = lambda shape: pl.BlockSpec(shape, lambda i: (0,) * len(shape))
    in_specs = [
        pl.BlockSpec((tm, D_MODEL), lambda i: (jnp.minimum(i, last), 0)),
        pl.BlockSpec((tm, D_MODEL), lambda i: (jnp.minimum(i, last), 0)),
        full((1, D_MODEL)), full((1, D_MODEL)),
        full((D_MODEL, D_MODEL)),
        full((1, D_MODEL)), full((1, D_MODEL)),
        full((ROUTER_ROWS, D_MODEL)), full((ROUTER_ROWS, D_MODEL)), full((ROUTER_ROWS, LANES)),
        full((tm, tm)),
        full((N_EXPERTS, LANES)),
    ]
    args = [mg, x2d, prm["ln_emb_g"], prm["ln_emb_b"], prm["w_out"], prm["ln1_g"], prm["ln1_b"],
            prm["wr_hi"], prm["wr_lo"], prm["r_bias"], ut, carry_in]
    aliases = {}
    if prev is not None:
        for k, buf in enumerate(prev):
            in_specs.append(pl.BlockSpec(memory_space=pl.ANY))
            aliases[len(args)] = k
            args.append(buf)
    out_shape = [
        jax.ShapeDtypeStruct((total_rows, D_MODEL), F32),
        jax.ShapeDtypeStruct((META_ROWS, total_rows), jnp.int32),
        jax.ShapeDtypeStruct((total_rows, LANES), F32),
        jax.ShapeDtypeStruct((N_EXPERTS, LANES), F32),
    ]
    out_specs = [
        pl.BlockSpec((tm, D_MODEL), lambda i: (i + off, 0)),
        pl.BlockSpec((META_ROWS, tm), lambda i: (0, i + off)),
        pl.BlockSpec((tm, LANES), lambda i: (i + off, 0)),
        full((N_EXPERTS, LANES)),
    ]
    if prev is not None:
        assert len(prev) == 3
    return pl.pallas_call(
        functools.partial(_post_body, nsteps),
        grid=(nsteps + int(zero_tail),),
        in_specs=in_specs,
        out_specs=out_specs,
        out_shape=out_shape,
        input_output_aliases=aliases,
        scratch_shapes=[pltpu.VMEM((N_EXPERTS, LANES), F32)],
        compiler_params=_cparams(("arbitrary",)),
        name="post_attn",
    )(*args)


def _prep_post_params(ln_emb_g, ln_emb_b, w_out, ln1_g, ln1_b, w_rg, b_rg, w_re, b_re):
    row = lambda v: v.reshape(1, -1)
    wr = jnp.concatenate([w_re.T, w_rg.T, jnp.zeros((ROUTER_ROWS - N_EXPERTS - N_GROUPS, D_MODEL), F32)], axis=0)
    wr_hi = wr.astype(BF16)
    wr_lo = (wr - wr_hi.astype(F32)).astype(BF16)
    rb = jnp.concatenate([b_re, b_rg, jnp.zeros((ROUTER_ROWS - N_EXPERTS - N_GROUPS,), F32)])
    return dict(ln_emb_g=row(ln_emb_g), ln_emb_b=row(ln_emb_b), w_out=w_out.astype(BF16), ln1_g=row(ln1_g),
                ln1_b=row(ln1_b), wr_hi=wr_hi, wr_lo=wr_lo,
                r_bias=jnp.broadcast_to(rb[:, None], (ROUTER_ROWS, LANES)))


MOE_ROWS = 256
PACKED = D_MODEL // 2
U32 = jnp.uint32


def _pack_bf16_pairs(x):
    half = x.shape[1] // 2
    hi = pltpu.bitcast(x[:, :half].astype(BF16).astype(F32), U32)
    lo = pltpu.bitcast(x[:, half:].astype(BF16).astype(F32), U32)
    return hi | (lo >> 16)


def _unpack_bf16_pairs(u):
    hi = pltpu.bitcast(u & jnp.uint32(0xFFFF0000), F32)
    lo = pltpu.bitcast(u << 16, F32)
    return hi, lo


def _moe_plan(counts, total_assign):
    nb_max = -(-total_assign // MOE_ROWS) + N_EXPERTS
    padded = (counts + MOE_ROWS - 1) // MOE_ROWS * MOE_ROWS
    pend = jnp.cumsum(padded)
    pstart = (pend - padded).astype(jnp.int32)
    block_start = jnp.arange(nb_max, dtype=jnp.int32) * MOE_ROWS
    n_ended = jnp.sum((pend[None, :] <= block_start[:, None]).astype(jnp.int32), axis=1)
    block_e = jnp.minimum(n_ended, N_EXPERTS - 1).astype(jnp.int32)
    n_used = (pend[-1] // MOE_ROWS).astype(jnp.int32).reshape(1)
    return pstart, block_e, n_used, nb_max


ROW_UNROLL = 8


def _slot_ids(meta, pstart):
    experts = meta[0:TOP_K]
    ranks = meta[TOP_K:2 * TOP_K]
    onehot = experts[..., None] == jnp.arange(N_EXPERTS, dtype=jnp.int32)
    return ranks + jnp.sum(jnp.where(onehot, pstart, 0), axis=-1)


def _for_row_groups(tm, fn):
    def group(g, c):
        t0 = pl.multiple_of(g * ROW_UNROLL, ROW_UNROLL)
        for r in range(ROW_UNROLL):
            for k in range(TOP_K):
                fn(t0 + r, k)
        return c

    lax.fori_loop(0, tm // ROW_UNROLL, group, 0)


def _dispatch_body(s0_ref, s1_ref, h_ref, xs_in_ref, xs_ref, pk_ref, sem):
    del xs_in_ref
    tm = h_ref.shape[0]
    slot_refs = (s0_ref, s1_ref)
    pk_ref[...] = _pack_bf16_pairs(h_ref[...])

    def row_copy(t, k):
        return pltpu.make_async_copy(pk_ref.at[pl.ds(t, 1)], xs_ref.at[pl.ds(slot_refs[k][0, t], 1)], sem)

    _for_row_groups(tm, lambda t, k: row_copy(t, k).start(priority=k))
    _for_row_groups(tm, lambda t, k: row_copy(t, k).wait())


def _dispatch(h1, slot_ids, nslots, tm, total):
    assert total % tm == 0 and total <= h1.shape[0] and tm % ROW_UNROLL == 0
    zeros = jnp.zeros((nslots, PACKED), U32)
    slot_spec = pl.BlockSpec((1, tm), lambda i: (0, i), memory_space=pltpu.SMEM)
    return pl.pallas_call(
        _dispatch_body,
        grid=(total // tm,),
        in_specs=[slot_spec, slot_spec, pl.BlockSpec((tm, D_MODEL), lambda i: (i, 0)),
                  pl.BlockSpec(memory_space=pl.ANY)],
        out_specs=pl.BlockSpec(memory_space=pl.ANY),
        scratch_shapes=[pltpu.VMEM((tm, PACKED), U32), pltpu.SemaphoreType.DMA(())],
        out_shape=jax.ShapeDtypeStruct((nslots, PACKED), U32),
        input_output_aliases={3: 0},
        compiler_params=_cparams(("arbitrary",)),
        name="moe_dispatch",
    )(slot_ids[0:1], slot_ids[1:2], h1, zeros)


def _expert_body(be_ref, nu_ref, xs_ref, wg_ref, wu_ref, wd_ref, ys_ref, wgb_ref, wub_ref, wdb_ref):
    i = pl.program_id(0)
    changed = jnp.logical_or(i == 0, be_ref[i] != be_ref[jnp.maximum(i - 1, 0)])

    @pl.when(changed)
    def _():
        wgb_ref[...] = wg_ref[...].astype(BF16)
        wub_ref[...] = wu_ref[...].astype(BF16)
        wdb_ref[...] = wd_ref[...].astype(BF16)

    @pl.when(i < nu_ref[0])
    def _():
        hi, lo = _unpack_bf16_pairs(xs_ref[...])
        hi = hi.astype(BF16)
        lo = lo.astype(BF16)
        half = PACKED

        def proj(w_ref):
            return (jnp.dot(hi, w_ref[:half, :], preferred_element_type=F32)
                    + jnp.dot(lo, w_ref[half:, :], preferred_element_type=F32))

        g = proj(wgb_ref)
        hb = (g * _sigmoid(g)) * proj(wub_ref)
        y = jnp.dot(hb.astype(BF16), wdb_ref[...], preferred_element_type=F32)
        ys_ref[...] = _pack_bf16_pairs(y)

    @pl.when(i >= nu_ref[0])
    def _():
        ys_ref[...] = jnp.zeros_like(ys_ref)


def _experts(xs, block_e, n_used, w_gate, w_up, w_down, nb_max):
    wmap = lambda i, be, nu: (be[i], 0, 0)
    return pl.pallas_call(
        _expert_body,
        grid_spec=pltpu.PrefetchScalarGridSpec(
            num_scalar_prefetch=2,
            grid=(nb_max,),
            in_specs=[
                pl.BlockSpec((MOE_ROWS, PACKED), lambda i, be, nu: (i, 0)),
                pl.BlockSpec((None, D_MODEL, D_EXPERT), wmap),
                pl.BlockSpec((None, D_MODEL, D_EXPERT), wmap),
                pl.BlockSpec((None, D_EXPERT, D_MODEL), wmap),
            ],
            out_specs=pl.BlockSpec((MOE_ROWS, PACKED), lambda i, be, nu: (i, 0)),
            scratch_shapes=[pltpu.VMEM((D_MODEL, D_EXPERT), BF16), pltpu.VMEM((D_MODEL, D_EXPERT), BF16),
                            pltpu.VMEM((D_EXPERT, D_MODEL), BF16)],
        ),
        out_shape=jax.ShapeDtypeStruct(xs.shape, U32),
        compiler_params=_cparams(("arbitrary",)),
        name="moe_experts",
    )(block_e, n_used, xs, w_gate, w_up, w_down)


def _combine_body(s0_ref, s1_ref, h_ref, w_ref, g2_ref, b2_ref, ys_ref, o_ref, buf_ref, sem):
    tm = h_ref.shape[0]
    slot_refs = (s0_ref, s1_ref)

    def row_copy(t, k):
        return pltpu.make_async_copy(ys_ref.at[pl.ds(slot_refs[k][0, t], 1)], buf_ref.at[k, pl.ds(t, 1)], sem)

    _for_row_groups(tm, lambda t, k: row_copy(t, k).start(priority=k))
    _for_row_groups(tm, lambda t, k: row_copy(t, k).wait())
    w = w_ref[...]
    hi0, lo0 = _unpack_bf16_pairs(buf_ref[0])
    hi1, lo1 = _unpack_bf16_pairs(buf_ref[1])
    w0 = w[:, 0:1]
    w1 = w[:, 1:2]
    f = jnp.concatenate([w0 * hi0 + w1 * hi1, w0 * lo0 + w1 * lo1], axis=1)
    o_ref[...] = _layer_norm_rows(ALPHA * h_ref[...] + f, g2_ref[...], b2_ref[...])


def _combine(h1, wcol, slot_ids, ys, ln2_g, ln2_b, tm, row0, nrows):
    assert nrows % tm == 0 and row0 % tm == 0 and tm % ROW_UNROLL == 0
    off = row0 // tm
    slot_spec = pl.BlockSpec((1, tm), lambda i: (0, i + off), memory_space=pltpu.SMEM)
    return pl.pallas_call(
        _combine_body,
        grid=(nrows // tm,),
        in_specs=[
            slot_spec, slot_spec,
            pl.BlockSpec((tm, D_MODEL), lambda i: (i + off, 0)),
            pl.BlockSpec((tm, LANES), lambda i: (i + off, 0)),
            pl.BlockSpec((1, D_MODEL), lambda i: (0, 0)),
            pl.BlockSpec((1, D_MODEL), lambda i: (0, 0)),
            pl.BlockSpec(memory_space=pl.ANY),
        ],
        out_specs=pl.BlockSpec((tm, D_MODEL), lambda i: (i, 0)),
        scratch_shapes=[pltpu.VMEM((TOP_K, tm, PACKED), U32), pltpu.SemaphoreType.DMA(())],
        out_shape=jax.ShapeDtypeStruct((nrows, D_MODEL), F32),
        compiler_params=_cparams(("arbitrary",)),
        name="moe_combine",
    )(slot_ids[0:1], slot_ids[1:2], h1, wcol, ln2_g.reshape(1, -1), ln2_b.reshape(1, -1), ys)


PROJ_ROWS = 512
POST_ROWS = 512
DISPATCH_ROWS = 384
COMBINE_ROWS = 256


def kernel(x_prompt, x_sample, state_gla, cache_swa_k, cache_swa_v, meta_tokens, ln_emb_g, ln_emb_b, rel_bias, w_in,
           gk_up, gk_bias, gla_norm_g, sinks, w_out, ln1_g, ln1_b, w_router_group, b_router_group, w_router_expert,
           b_router_expert, w_gate, w_up, w_down, ln2_g, ln2_b):
    nbatch, seq, d = x_prompt.shape
    nseq = x_sample.shape[0]
    assert w_in.shape[0] == DEPTH == 1 and d == D_MODEL and x_sample.shape[1] == 1
    assert seq % BLK == 0 and nseq == BLK and meta_tokens.shape[0] == N_META
    nblk = seq // BLK
    n_prompt = nbatch * seq
    total = n_prompt + nseq
    kvw = ATT_KV_HEADS * HEAD_DIM

    xp = x_prompt.reshape(n_prompt, d)
    xs = x_sample.reshape(nseq, d)
    extra = jnp.concatenate([xs, jnp.zeros((BLK - N_META, d), xs.dtype), meta_tokens.astype(xs.dtype)], axis=0)
    w_cat = _prep_w_in(w_in[0])
    pp = _ln_proj(xp, ln_emb_g, ln_emb_b, w_cat, PROJ_ROWS)
    pe = _ln_proj(extra, ln_emb_g, ln_emb_b, w_cat, 2 * BLK)

    yg, gla_p = _gla_prompt(pp, pe, gk_up[0], gk_bias[0], gla_norm_g[0], nbatch, nblk)
    yg_s, gla_s = _gla_step(pe, gk_up[0], gk_bias[0], gla_norm_g[0], state_gla[0])
    mg = _swa_prompt(pp, pe, yg, rel_bias, sinks[0], nbatch, nblk)
    mg_s, k_s, v_s = _swa_step(pe, yg_s, cache_swa_k[0].reshape(nseq, WINDOW, kvw),
                               cache_swa_v[0].reshape(nseq, WINDOW, kvw), rel_bias, sinks[0])

    prm = _prep_post_params(ln_emb_g, ln_emb_b, w_out[0], ln1_g[0], ln1_b[0], w_router_group[0], b_router_group[0],
                            w_router_expert[0], b_router_expert[0])
    carry0 = jnp.zeros((N_EXPERTS, LANES), F32)
    rows_alloc = n_prompt + POST_ROWS
    h1, meta, wcol, carry1 = _post(mg, xp, prm, POST_ROWS, 0, rows_alloc, carry0, zero_tail=True)
    h1, meta, wcol, carry2 = _post(mg_s, xs, prm, nseq, n_prompt, rows_alloc, carry1, prev=(h1, meta, wcol))

    counts = carry2[:, 0].astype(jnp.int32)
    pstart, block_e, n_used, nb_max = _moe_plan(counts, TOP_K * total)
    slot_ids = _slot_ids(meta, pstart)
    xs_sorted = _dispatch(h1, slot_ids, nb_max * MOE_ROWS, DISPATCH_ROWS, total)
    ys = _experts(xs_sorted, block_e, n_used, w_gate[0], w_up[0], w_down[0], nb_max)
    y_p = _combine(h1, wcol, slot_ids, ys, ln2_g[0], ln2_b[0], COMBINE_ROWS, 0, n_prompt)
    y_s = _combine(h1, wcol, slot_ids, ys, ln2_g[0], ln2_b[0], nseq, n_prompt, nseq)

    kv_shape = (1, nbatch, WINDOW, ATT_KV_HEADS, HEAD_DIM)
    k_p = pp["ka"].reshape(nbatch, seq, kvw)[:, seq - WINDOW:].reshape(kv_shape)
    v_p = pp["va"].reshape(nbatch, seq, kvw)[:, seq - WINDOW:].reshape(kv_shape)
    return (y_p.reshape(nbatch, seq, d), y_s.reshape(nseq, 1, d), gla_p[None], k_p, v_p, gla_s[None],
            k_s.reshape(cache_swa_k.shape), v_s.reshape(cache_swa_v.shape))
```

```python
import functools
import math

import jax
import jax.numpy as jnp
import numpy as np
from jax import lax
from jax.experimental import pallas as pl
from jax.experimental.pallas import tpu as pltpu

F32 = jnp.float32
BF16 = jnp.bfloat16

D_MODEL = 1024
N_META = 16
LN_EPS = 1e-5
GLA_HEADS = 4
GLA_DK = 128
GLA_DV = 256
GLA_RANK = 16
GLA_TAU = 16.0
HEAD_DIM = 64
ATT_HEADS = 16
ATT_KV_HEADS = 4
GQA_GROUP = 4
WINDOW = 128
REL_BUCKETS = 32
REL_MAX_DIST = 128
N_GROUPS = 4
EXPERTS_PER_GROUP = 8
N_EXPERTS = 32
TOP_K = 2
D_EXPERT = 512
DEPTH = 1
ALPHA = (2.0 * DEPTH) ** 0.25

LANES = 128
BLK = 128
VMEM_LIMIT = 56 * 1024 * 1024


def _cparams(sem):
    return pltpu.CompilerParams(dimension_semantics=sem, vmem_limit_bytes=VMEM_LIMIT)


def _layer_norm_rows(x, g, b):
    mu = jnp.mean(x, axis=-1, keepdims=True)
    xc = x - mu
    var = jnp.mean(xc * xc, axis=-1, keepdims=True)
    return xc * lax.rsqrt(var + LN_EPS) * g + b


_PROJ_OUTS = (
    ("qg", GLA_HEADS * GLA_DK, BF16, GLA_DK ** -0.5),
    ("kg", GLA_HEADS * GLA_DK, BF16, None),
    ("vg", GLA_HEADS * GLA_DV, BF16, None),
    ("rg", GLA_HEADS * GLA_DV, BF16, None),
    ("qa", ATT_HEADS * HEAD_DIM, BF16, HEAD_DIM ** -0.5),
    ("ka", ATT_KV_HEADS * HEAD_DIM, F32, None),
    ("va", ATT_KV_HEADS * HEAD_DIM, F32, None),
    ("ga", D_MODEL, BF16, None),
    ("gb", D_MODEL, BF16, None),
    ("lr", LANES, F32, None),
)
_PROJ_W = sum(w for _, w, _, _ in _PROJ_OUTS)


def _prep_w_in(w_in):
    sizes = (512, 512, 1024, 1024, GLA_RANK, 1024, 256, 256, 1024, 1024)
    offs = np.cumsum((0,) + sizes)
    a = w_in[:, : offs[4]]
    lr = w_in[:, offs[4]: offs[5]]
    b = w_in[:, offs[5]:]
    pad = jnp.zeros((w_in.shape[0], LANES - GLA_RANK), w_in.dtype)
    return jnp.concatenate([a, b, lr, pad], axis=1).astype(BF16)


def _ln_proj_body(x_ref, g_ref, b_ref, w_ref, *out_refs):
    xn = _layer_norm_rows(x_ref[...], g_ref[...], b_ref[...]).astype(BF16)
    c0 = 0
    for (_, width, dtype, scale), o_ref in zip(_PROJ_OUTS, out_refs):
        acc = jnp.dot(xn, w_ref[:, c0:c0 + width], preferred_element_type=F32)
        if scale is not None:
            acc = acc * scale
        o_ref[...] = acc.astype(dtype)
        c0 += width


def _ln_proj(x2d, ln_g, ln_b, w_cat, tm):
    m = x2d.shape[0]
    assert m % tm == 0
    out_shape = [jax.ShapeDtypeStruct((m, w), dt) for _, w, dt, _ in _PROJ_OUTS]
    out_specs = [pl.BlockSpec((tm, w), lambda i: (i, 0)) for _, w, _, _ in _PROJ_OUTS]
    outs = pl.pallas_call(
        _ln_proj_body,
        grid=(m // tm,),
        in_specs=[
            pl.BlockSpec((tm, D_MODEL), lambda i: (i, 0)),
            pl.BlockSpec((1, D_MODEL), lambda i: (0, 0)),
            pl.BlockSpec((1, D_MODEL), lambda i: (0, 0)),
            pl.BlockSpec((D_MODEL, _PROJ_W), lambda i: (0, 0), pipeline_mode=pl.Buffered(1)),
        ],
        out_specs=out_specs,
        out_shape=out_shape,
        compiler_params=_cparams(("arbitrary",)),
        name="ln_proj",
    )(x2d, ln_g.reshape(1, -1), ln_b.reshape(1, -1), w_cat)
    return dict(zip([n for n, _, _, _ in _PROJ_OUTS], outs))


_GLA_LEVELS = tuple(2 ** i for i in range(int(math.log2(BLK))))
GLA_SAFE_EXPONENT = 60.0


def _log_sigmoid(x):
    return jnp.minimum(x, 0.0) - jnp.log(1.0 + jnp.exp(-jnp.abs(x)))


def _sigmoid(x):
    return 1.0 / (1.0 + jnp.exp(-x))


def _split_dot(a01, x):
    hi = x.astype(BF16)
    lo = (x - hi.astype(F32)).astype(BF16)
    return (jnp.dot(a01, hi, preferred_element_type=F32) + jnp.dot(a01, lo, preferred_element_type=F32))


def _gla_anchor_exponent(b, la, s, row):
    if s == 1:
        return jnp.where(row % 2 == 1, la, 0.0)
    if s == 2:
        la_dn = pltpu.roll(la, 1, axis=0)
        la_up = pltpu.roll(la, BLK - 1, axis=0)
        r = row % 4
        return jnp.where(r == 0, la_up, jnp.where(r == 1, 0.0, jnp.where(r == 2, la, la + la_dn)))
    nb = BLK // (2 * s)
    b3 = b.reshape(nb, 2 * s, b.shape[-1])
    anchor = jnp.broadcast_to(b3[:, s - 1:s, :], b3.shape).reshape(b.shape)
    return -jnp.abs(b - anchor)


def _gla_body(nblk, qm, km, vm, rm, lrm, gam, qp, kp, vp, rp, lrp, gap, gkup_ref, gkb_ref, gn_ref, tri_ref,
              y_ref, s_out_ref, s_ref):
    c = pl.program_id(1)
    is_meta = c == 0

    @pl.when(is_meta)
    def _():
        s_ref[...] = jnp.zeros_like(s_ref)

    def pick(m_ref, p_ref):
        return jnp.where(is_meta, m_ref[...], p_ref[...])

    row = lax.broadcasted_iota(jnp.int32, (BLK, GLA_DK), 0)
    col_t = lax.broadcasted_iota(jnp.int32, (BLK, BLK), 1)
    row_t = lax.broadcasted_iota(jnp.int32, (BLK, BLK), 0)
    live = jnp.logical_or(jnp.logical_not(is_meta), row >= BLK - N_META)
    tri = tri_ref[...]
    q_all, k_all, v_all, r_all = pick(qm, qp), pick(km, kp), pick(vm, vp), pick(rm, rp)
    ga_all = pick(gam, gap)
    lr = pick(lrm, lrp).astype(BF16)
    x_all = jnp.dot(lr, gkup_ref[...], preferred_element_type=F32) + gkb_ref[...]
    la_all = _log_sigmoid(x_all) * (1.0 / GLA_TAU)
    nt = (((1,), (1,)), ((), ()))
    mid = BLK // 2 - 1

    def head(h, single_anchor):
        dk = slice(h * GLA_DK, (h + 1) * GLA_DK)
        dv = slice(h * GLA_DV, (h + 1) * GLA_DV)
        la = jnp.where(live, la_all[:, dk], 0.0)
        q = q_all[:, dk].astype(F32)
        k = jnp.where(live, k_all[:, dk].astype(F32), 0.0)
        v = v_all[:, dv]
        b = _split_dot(tri, la)
        b_last = b[BLK - 1:BLK, :]
        s_old = s_ref[h]
        if single_anchor:
            b_mid = b[mid:mid + 1, :]
            qe = q * jnp.exp(b - b_mid)
            ke = k * jnp.exp(b_mid - b)
            a = jnp.where(row_t >= col_t,
                          lax.dot_general(qe.astype(BF16), ke.astype(BF16), nt, preferred_element_type=F32), 0.0)
            qg = qe * jnp.exp(b_mid)
            kd = ke * jnp.exp(b_last - b_mid)
        else:
            a = jnp.where(row_t == col_t,
                          lax.dot_general(q.astype(BF16), k.astype(BF16), nt, preferred_element_type=F32), 0.0)
            for s in _GLA_LEVELS:
                e = jnp.exp(_gla_anchor_exponent(b, la, s, row))
                upper = (row // s) % 2 == 1
                q_s = jnp.where(upper, q * e, 0.0).astype(BF16)
                k_s = jnp.where(upper, 0.0, k * e).astype(BF16)
                p = lax.dot_general(q_s, k_s, nt, preferred_element_type=F32)
                a = a + jnp.where(row_t // (2 * s) == col_t // (2 * s), p, 0.0)
            qg = q * jnp.exp(b)
            kd = k * jnp.exp(b_last - b)
        o = jnp.dot(qg.astype(BF16), s_old.astype(BF16), preferred_element_type=F32)
        kv = lax.dot_general(kd.astype(BF16), v, (((0,), (0,)), ((), ())), preferred_element_type=F32)
        decay_col = jnp.transpose(jnp.broadcast_to(jnp.exp(b_last), (BLK, GLA_DK)))[:, :1]
        s_ref[h] = decay_col * s_old + kv
        o = o + jnp.dot(a.astype(BF16), v, preferred_element_type=F32)
        o = o * lax.rsqrt(jnp.mean(o * o, axis=-1, keepdims=True) + LN_EPS) * gn_ref[...]
        r = r_all[:, dv].astype(F32)
        y = o * (r * _sigmoid(r)) * _sigmoid(ga_all[:, dv].astype(F32))
        y_ref[:, dv] = y.astype(y_ref.dtype)

    mild = jnp.max(jnp.abs(la_all)) * (BLK // 2) <= GLA_SAFE_EXPONENT

    @pl.when(mild)
    def _():
        for h in range(GLA_HEADS):
            head(h, True)

    @pl.when(jnp.logical_not(mild))
    def _():
        for h in range(GLA_HEADS):
            head(h, False)

    @pl.when(c == nblk)
    def _():
        s_out_ref[...] = s_ref[...]


def _tri_incl():
    i = np.arange(BLK)
    return jnp.asarray((i[None, :] <= i[:, None]).astype(np.float32), dtype=BF16)


def _gla_prompt(pp, pe, gk_up, gk_bias, gnorm, nbatch, nblk):
    names = ("qg", "kg", "vg", "rg", "lr", "ga")
    gkup = jnp.concatenate([gk_up, jnp.zeros((LANES - GLA_RANK, gk_up.shape[1]), gk_up.dtype)], axis=0).astype(BF16)
    m_specs = [pl.BlockSpec((BLK, pe[n].shape[1]), lambda b, c: (1, 0)) for n in names]
    p_specs = [pl.BlockSpec((BLK, pp[n].shape[1]), lambda b, c: (b * nblk + jnp.maximum(c - 1, 0), 0)) for n in names]
    w_specs = [
        pl.BlockSpec((LANES, GLA_HEADS * GLA_DK), lambda b, c: (0, 0)),
        pl.BlockSpec((1, GLA_HEADS * GLA_DK), lambda b, c: (0, 0)),
        pl.BlockSpec((1, GLA_DV), lambda b, c: (0, 0)),
        pl.BlockSpec((BLK, BLK), lambda b, c: (0, 0)),
    ]
    y, s_fin = pl.pallas_call(
        functools.partial(_gla_body, nblk),
        grid=(nbatch, nblk + 1),
        in_specs=m_specs + p_specs + w_specs,
        out_specs=[
            pl.BlockSpec((BLK, D_MODEL), lambda b, c: (b * nblk + jnp.maximum(c - 1, 0), 0)),
            pl.BlockSpec((None, GLA_HEADS, GLA_DK, GLA_DV), lambda b, c: (b, 0, 0, 0)),
        ],
        out_shape=[
            jax.ShapeDtypeStruct((nbatch * nblk * BLK, D_MODEL), BF16),
            jax.ShapeDtypeStruct((nbatch, GLA_HEADS, GLA_DK, GLA_DV), F32),
        ],
        scratch_shapes=[pltpu.VMEM((GLA_HEADS, GLA_DK, GLA_DV), F32)],
        compiler_params=_cparams(("arbitrary", "arbitrary")),
        name="gla_prompt",
    )(*[pe[n] for n in names], *[pp[n] for n in names], gkup, gk_bias.reshape(1, -1), gnorm.reshape(1, -1), _tri_incl())
    return y, s_fin


GLA_STEP_SEQS = 16


def _gla_step_body(q_ref, k_ref, v_ref, r_ref, lr_ref, ga_ref, gkup_ref, gkb_ref, gn_ref, s_in_ref,
                   y_ref, s_out_ref, at_ref, kt_ref, qt_ref):
    g = pl.program_id(0)
    nseq = q_ref.shape[0]

    @pl.when(g == 0)
    def _():
        x = jnp.dot(lr_ref[...].astype(BF16), gkup_ref[...], preferred_element_type=F32) + gkb_ref[...]
        a = jnp.exp(_log_sigmoid(x) * (1.0 / GLA_TAU))
        for h in range(GLA_HEADS):
            dk = slice(h * GLA_DK, (h + 1) * GLA_DK)
            at_ref[h] = jnp.transpose(a[:, dk])
            kt_ref[h] = jnp.transpose(k_ref[:, dk].astype(F32))
            qt_ref[h] = jnp.transpose(q_ref[:, dk].astype(F32))

    lane = lax.broadcasted_iota(jnp.int32, (GLA_DK, nseq), 1)
    ones = jnp.ones((nseq, GLA_DV), BF16)
    grp = pl.ds(pl.multiple_of(g * GLA_STEP_SEQS, GLA_STEP_SEQS), GLA_STEP_SEQS)
    r_grp = r_ref[grp, :].astype(F32)
    ga_grp = ga_ref[grp, :].astype(F32)
    for i in range(GLA_STEP_SEQS):
        n = g * GLA_STEP_SEQS + i
        sel = lane == n
        for h in range(GLA_HEADS):
            dv = slice(h * GLA_DV, (h + 1) * GLA_DV)
            a_sel = jnp.where(sel, at_ref[h], 0.0)
            k_sel = jnp.where(sel, kt_ref[h], 0.0).astype(BF16)
            q_sel = jnp.where(sel, qt_ref[h], 0.0).astype(BF16)
            decay = _split_dot_rhs(a_sel, ones)
            kv = jnp.dot(k_sel, v_ref[:, dv], preferred_element_type=F32)
            q_b = jnp.dot(q_sel, ones, preferred_element_type=F32)
            s_new = decay * s_in_ref[i, h] + kv
            s_out_ref[i, h] = s_new
            o = jnp.sum(q_b * s_new, axis=0, keepdims=True)
            o = o * lax.rsqrt(jnp.mean(o * o, axis=-1, keepdims=True) + LN_EPS) * gn_ref[...]
            r = r_grp[i:i + 1, dv]
            ga = ga_grp[i:i + 1, dv]
            y_ref[i:i + 1, dv] = (o * (r * _sigmoid(r)) * _sigmoid(ga)).astype(y_ref.dtype)


def _split_dot_rhs(x, b01):
    hi = x.astype(BF16)
    lo = (x - hi.astype(F32)).astype(BF16)
    return jnp.dot(hi, b01, preferred_element_type=F32) + jnp.dot(lo, b01, preferred_element_type=F32)


def _gla_step(pe, gk_up, gk_bias, gnorm, state):
    nseq = state.shape[0]
    assert nseq == BLK and nseq % GLA_STEP_SEQS == 0
    names = ("qg", "kg", "vg", "rg", "lr", "ga")
    gkup = jnp.concatenate([gk_up, jnp.zeros((LANES - GLA_RANK, gk_up.shape[1]), gk_up.dtype)], axis=0).astype(BF16)
    t_specs = [pl.BlockSpec((nseq, pe[n].shape[1]), lambda g: (0, 0)) for n in names]
    st_spec = pl.BlockSpec((GLA_STEP_SEQS, GLA_HEADS, GLA_DK, GLA_DV), lambda g: (g, 0, 0, 0))
    return pl.pallas_call(
        _gla_step_body,
        grid=(nseq // GLA_STEP_SEQS,),
        in_specs=t_specs + [
            pl.BlockSpec((LANES, GLA_HEADS * GLA_DK), lambda g: (0, 0)),
            pl.BlockSpec((1, GLA_HEADS * GLA_DK), lambda g: (0, 0)),
            pl.BlockSpec((1, GLA_DV), lambda g: (0, 0)),
            st_spec,
        ],
        out_specs=[pl.BlockSpec((GLA_STEP_SEQS, D_MODEL), lambda g: (g, 0)), st_spec],
        out_shape=[jax.ShapeDtypeStruct((nseq, D_MODEL), F32), jax.ShapeDtypeStruct(state.shape, F32)],
        scratch_shapes=[pltpu.VMEM((GLA_HEADS, GLA_DK, nseq), F32) for _ in range(3)],
        compiler_params=_cparams(("arbitrary",)),
        name="gla_step",
    )(*[pe[n] for n in names], gkup, gk_bias.reshape(1, -1), gnorm.reshape(1, -1), state)


HALF = LANES // 2


def _rel_bucket(dist):
    max_exact = REL_BUCKETS // 2
    d = jnp.maximum(dist, 0)
    large = max_exact + (jnp.log(jnp.maximum(d, 1).astype(F32) / max_exact)
                         / math.log(REL_MAX_DIST / max_exact) * (REL_BUCKETS - max_exact)).astype(jnp.int32)
    large = jnp.minimum(large, REL_BUCKETS - 1)
    return jnp.where(d < max_exact, d, large)


def _bias_lookup(rel_bias, dist):
    onehot = (_rel_bucket(dist)[..., None] == jnp.arange(REL_BUCKETS)).astype(F32)
    return jnp.einsum("...b,bh->h...", onehot, rel_bias.astype(F32), precision=lax.Precision.HIGHEST)


def _swa_bias_tables(rel_bias):
    q = jnp.arange(BLK)[:, None]
    c = jnp.arange(2 * BLK)[None, :]
    dist = BLK + q - c
    bias = _bias_lookup(rel_bias, dist)
    inside = (dist >= 0) & (dist < WINDOW)
    first = inside & (c >= BLK - N_META)
    neg = jnp.float32(-jnp.inf)
    return jnp.stack([jnp.where(first[None], bias, neg), jnp.where(inside[None], bias, neg)])


def _half_tiles(x):
    lane = lax.broadcasted_iota(jnp.int32, (x.shape[0], LANES), 1)
    low = lane < HALF
    out = []
    for t in range(2):
        tile = x[:, t * LANES:(t + 1) * LANES]
        swapped = pltpu.roll(tile, HALF, axis=1)
        zero = jnp.zeros_like(tile)
        even = (jnp.where(low, tile, zero).astype(BF16), jnp.where(low, zero, swapped).astype(BF16))
        odd = (jnp.where(low, swapped, zero).astype(BF16), jnp.where(low, zero, tile).astype(BF16))
        out += [even, odd]
    return out


def _swa_body(nblk, q_ref, kc_ref, vc_ref, kp_ref, vp_ref, km_ref, vm_ref, gb_ref, yg_ref, tb_ref, sink_ref,
              o_ref):
    blk = pl.program_id(1)
    first = blk == 0
    k_prev = jnp.where(first, km_ref[...], kp_ref[...])
    v_prev = jnp.where(first, vm_ref[...], vp_ref[...])
    k_tiles = _half_tiles(jnp.concatenate([k_prev, kc_ref[...]], axis=0))
    v_tiles = _half_tiles(jnp.concatenate([v_prev, vc_ref[...]], axis=0))
    variant = jnp.minimum(blk, 1)
    lane = lax.broadcasted_iota(jnp.int32, (BLK, LANES), 1)
    low = lane < HALF
    nt = (((1,), (1,)), ((), ()))
    for j in range(ATT_KV_HEADS):
        for pair in range(GQA_GROUP // 2):
            t = j * (GQA_GROUP // 2) + pair
            q_t = q_ref[:, t * LANES:(t + 1) * LANES]
            acc = None
            inv = []
            for half in range(2):
                h = 2 * t + half
                s = lax.dot_general(q_t, k_tiles[j][half], nt, preferred_element_type=F32)
                s = s + tb_ref[variant, h]
                sink = sink_ref[h]
                m = jnp.maximum(jnp.max(s, axis=-1, keepdims=True), sink)
                p = jnp.exp(s - m)
                inv.append(1.0 / (jnp.sum(p, axis=-1, keepdims=True) + jnp.exp(sink - m)))
                pv = jnp.dot(p.astype(BF16), v_tiles[j][half], preferred_element_type=F32)
                acc = pv if acc is None else acc + pv
            o = acc * jnp.where(low, inv[0], inv[1])
            cols = slice(t * LANES, (t + 1) * LANES)
            gate = _sigmoid(gb_ref[:, cols].astype(F32))
            o_ref[:, cols] = (gate * o + yg_ref[:, cols].astype(F32)).astype(o_ref.dtype)


def _swa_prompt(pp, pe, yg, rel_bias, sinks, nbatch, nblk):
    tb = _swa_bias_tables(rel_bias)
    kvw = ATT_KV_HEADS * HEAD_DIM
    cur = lambda b, c: (b * nblk + c, 0)
    prev = lambda b, c: (b * nblk + jnp.maximum(c - 1, 0), 0)
    return pl.pallas_call(
        functools.partial(_swa_body, nblk),
        grid=(nbatch, nblk),
        in_specs=[
            pl.BlockSpec((BLK, D_MODEL), cur),
            pl.BlockSpec((BLK, kvw), cur), pl.BlockSpec((BLK, kvw), cur),
            pl.BlockSpec((BLK, kvw), prev), pl.BlockSpec((BLK, kvw), prev),
            pl.BlockSpec((BLK, kvw), lambda b, c: (1, 0)), pl.BlockSpec((BLK, kvw), lambda b, c: (1, 0)),
            pl.BlockSpec((BLK, D_MODEL), cur),
            pl.BlockSpec((BLK, D_MODEL), cur),
            pl.BlockSpec(tb.shape, lambda b, c: (0, 0, 0, 0)),
            pl.BlockSpec(memory_space=pltpu.SMEM),
        ],
        out_specs=pl.BlockSpec((BLK, D_MODEL), cur),
        out_shape=jax.ShapeDtypeStruct((nbatch * nblk * BLK, D_MODEL), BF16),
        compiler_params=_cparams(("arbitrary", "arbitrary")),
        name="swa_prompt",
    )(pp["qa"], pp["ka"], pp["va"], pp["ka"], pp["va"], pe["ka"], pe["va"], pp["gb"], yg, tb, sinks)


SWA_STEP_SEQS = 8
Q_TILES = ATT_HEADS // 2


def _swa_step_body(q_ref, kn_ref, vn_ref, ck_ref, cv_ref, gb_ref, yg_ref, tb_ref, sink_ref,
                   o_ref, ko_ref, vo_ref):
    row = lax.broadcasted_iota(jnp.int32, (WINDOW, ATT_KV_HEADS * HEAD_DIM), 0)
    trow = lax.broadcasted_iota(jnp.int32, (Q_TILES, LANES), 0)
    low = lax.broadcasted_iota(jnp.int32, (Q_TILES, LANES), 1) < HALF
    nt = (((1,), (1,)), ((), ()))
    for i in range(SWA_STEP_SEQS):
        k_win = jnp.where(row == WINDOW - 1, kn_ref[i:i + 1, :], pltpu.roll(ck_ref[i], WINDOW - 1, axis=0))
        v_win = jnp.where(row == WINDOW - 1, vn_ref[i:i + 1, :], pltpu.roll(cv_ref[i], WINDOW - 1, axis=0))
        ko_ref[i] = k_win
        vo_ref[i] = v_win
        k_tiles = _half_tiles(k_win)
        v_tiles = _half_tiles(v_win)
        q8 = q_ref[i].astype(BF16)
        s = [jnp.zeros((Q_TILES, WINDOW), F32), jnp.zeros((Q_TILES, WINDOW), F32)]
        for j in range(ATT_KV_HEADS):
            mine = trow // (GQA_GROUP // 2) == j
            for half in range(2):
                sj = lax.dot_general(q8, k_tiles[j][half], nt, preferred_element_type=F32)
                s[half] = jnp.where(mine, sj, s[half])
        p, inv = [], []
        for half in range(2):
            sh = s[half] + tb_ref[half]
            sink = sink_ref[half]
            m = jnp.maximum(jnp.max(sh, axis=-1, keepdims=True), sink)
            ph = jnp.exp(sh - m)
            inv.append(1.0 / (jnp.sum(ph, axis=-1, keepdims=True) + jnp.exp(sink - m)))
            p.append(ph.astype(BF16))
        o = jnp.zeros((Q_TILES, LANES), F32)
        for j in range(ATT_KV_HEADS):
            mine = trow // (GQA_GROUP // 2) == j
            pv = (jnp.dot(p[0], v_tiles[j][0], preferred_element_type=F32)
                  + jnp.dot(p[1], v_tiles[j][1], preferred_element_type=F32))
            o = jnp.where(mine, pv, o)
        o = o * jnp.where(low, inv[0], inv[1])
        o_ref[i] = _sigmoid(gb_ref[i]) * o + yg_ref[i]


def _swa_step(pe, yg_s, cache_k, cache_v, rel_bias, sinks):
    nseq = cache_k.shape[0]
    kvw = ATT_KV_HEADS * HEAD_DIM
    as_tiles = lambda x: x[:nseq].astype(F32).reshape(nseq, Q_TILES, LANES)
    dist = (WINDOW - 1) - jnp.arange(WINDOW)
    bias = _bias_lookup(rel_bias, dist)
    tb = jnp.stack([bias[0::2], bias[1::2]])
    sk = jnp.broadcast_to(jnp.stack([sinks[0::2], sinks[1::2]])[:, :, None], (2, Q_TILES, 1)).astype(F32)
    g = SWA_STEP_SEQS
    tile_spec = pl.BlockSpec((g, Q_TILES, LANES), lambda s: (s, 0, 0))
    win_spec = pl.BlockSpec((g, WINDOW, kvw), lambda s: (s, 0, 0))
    new_spec = pl.BlockSpec((g, kvw), lambda s: (s, 0))
    o, ko, vo = pl.pallas_call(
        _swa_step_body,
        grid=(nseq // g,),
        in_specs=[tile_spec, new_spec, new_spec, win_spec, win_spec, tile_spec, tile_spec,
                  pl.BlockSpec(tb.shape, lambda s: (0, 0, 0)), pl.BlockSpec(sk.shape, lambda s: (0, 0, 0))],
        out_specs=[tile_spec, win_spec, win_spec],
        out_shape=[jax.ShapeDtypeStruct((nseq, Q_TILES, LANES), F32),
                   jax.ShapeDtypeStruct(cache_k.shape, F32), jax.ShapeDtypeStruct(cache_v.shape, F32)],
        compiler_params=_cparams(("arbitrary",)),
        name="swa_step",
    )(as_tiles(pe["qa"]), pe["ka"], pe["va"], cache_k, cache_v, as_tiles(pe["gb"]),
      yg_s.reshape(nseq, Q_TILES, LANES), tb, sk)
    return o.reshape(nseq, D_MODEL), ko, vo


ROUTER_ROWS = 40
META_ROWS = 8


def _split3_nt(a_hi, a_lo, x):
    nt = (((1,), (1,)), ((), ()))
    x_hi = x.astype(BF16)
    x_lo = (x - x_hi.astype(F32)).astype(BF16)
    return (lax.dot_general(a_hi, x_hi, nt, preferred_element_type=F32)
            + lax.dot_general(a_hi, x_lo, nt, preferred_element_type=F32)
            + lax.dot_general(a_lo, x_hi, nt, preferred_element_type=F32))


def _first_argmax_rows(v, ridx, nrows):
    vmax = jnp.max(v, axis=0, keepdims=True)
    idx = jnp.min(jnp.where(v == vmax, ridx, nrows), axis=0, keepdims=True)
    return vmax, idx


def _post_body(nsteps, *refs):
    h1_ref, meta_ref, wcol_ref = refs[-5:-2]
    i = pl.program_id(0)

    @pl.when(i < nsteps)
    def _():
        _post_tile(i, *refs)

    @pl.when(i >= nsteps)
    def _():
        h1_ref[...] = jnp.zeros_like(h1_ref)
        meta_ref[...] = jnp.zeros_like(meta_ref)
        wcol_ref[...] = jnp.zeros_like(wcol_ref)


def _post_tile(i, mg_ref, x_ref, lng_ref, lnb_ref, wo_ref, g1_ref, b1_ref, wrh_ref, wrl_ref, rb_ref, ut_ref,
               cin_ref, *rest):
    h1_ref, meta_ref, wcol_ref, cout_ref, carry_ref = rest[-5:]

    @pl.when(i == 0)
    def _():
        carry_ref[...] = cin_ref[...]

    tm = x_ref.shape[0]
    h = _layer_norm_rows(x_ref[...], lng_ref[...], lnb_ref[...])
    acc = jnp.dot(mg_ref[...].astype(BF16), wo_ref[...], preferred_element_type=F32)
    h1 = _layer_norm_rows(ALPHA * h + acc, g1_ref[...], b1_ref[...])
    h1_ref[...] = h1

    lt = _split3_nt(wrh_ref[...], wrl_ref[...], h1) + rb_ref[:, :1]
    ridx = lax.broadcasted_iota(jnp.int32, (EXPERTS_PER_GROUP, tm), 0)
    neg = jnp.float32(-jnp.inf)
    g_log = jnp.where(ridx < N_GROUPS, lt[N_EXPERTS:N_EXPERTS + EXPERTS_PER_GROUP], neg)
    g_max, grp = _first_argmax_rows(g_log, ridx, EXPERTS_PER_GROUP)
    p_grp = 1.0 / jnp.sum(jnp.exp(g_log - g_max), axis=0, keepdims=True)
    e_in = lt[0:EXPERTS_PER_GROUP]
    for gi in range(1, N_GROUPS):
        e_in = jnp.where(grp == gi, lt[gi * EXPERTS_PER_GROUP:(gi + 1) * EXPERTS_PER_GROUP], e_in)
    v0, i0 = _first_argmax_rows(e_in, ridx, EXPERTS_PER_GROUP)
    v1, i1 = _first_argmax_rows(jnp.where(ridx == i0, neg, e_in), ridx, EXPERTS_PER_GROUP)
    t = jnp.exp(v1 - v0)
    w0 = p_grp / (1.0 + t)
    w1 = p_grp * t / (1.0 + t)
    e0 = grp * EXPERTS_PER_GROUP + i0
    e1 = grp * EXPERTS_PER_GROUP + i1

    eidx = lax.broadcasted_iota(jnp.int32, (N_EXPERTS, tm), 0)
    hit0 = eidx == e0
    hit1 = eidx == e1
    oh = jnp.where(jnp.logical_or(hit0, hit1), 1.0, 0.0)
    before = jnp.dot(oh.astype(BF16), ut_ref[...], preferred_element_type=F32) + carry_ref[:, :1]
    r0 = jnp.sum(jnp.where(hit0, before, 0.0), axis=0, keepdims=True).astype(jnp.int32)
    r1 = jnp.sum(jnp.where(hit1, before, 0.0), axis=0, keepdims=True).astype(jnp.int32)
    carry_ref[...] = carry_ref[...] + jnp.sum(oh, axis=1, keepdims=True)
    cout_ref[...] = carry_ref[...]

    zi = jnp.zeros((META_ROWS - 4, tm), jnp.int32)
    meta_ref[...] = jnp.concatenate([e0, e1, r0, r1, zi], axis=0)
    wt = jnp.concatenate([w0, w1, jnp.zeros((LANES - 2, tm), F32)], axis=0)
    wcol_ref[...] = jnp.transpose(wt)


def _post(mg, x2d, prm, tm, row0, total_rows, carry_in, prev=None, zero_tail=False):
    m = x2d.shape[0]
    assert m % tm == 0 and row0 % tm == 0 and total_rows % tm == 0
    off = row0 // tm
    nsteps = m // tm
    last = nsteps - 1
    ut = jnp.asarray(np.triu(np.ones((tm, tm), np.float32), 1), dtype=BF16)
    full = lambda shape: pl.BlockSpec(shape, lambda i: (0,) * len(shape))
    in_specs = [
        pl.BlockSpec((tm, D_MODEL), lambda i: (jnp.minimum(i, last), 0)),
        pl.BlockSpec((tm, D_MODEL), lambda i: (jnp.minimum(i, last), 0)),
        full((1, D_MODEL)), full((1, D_MODEL)),
        full((D_MODEL, D_MODEL)),
        full((1, D_MODEL)), full((1, D_MODEL)),
        full((ROUTER_ROWS, D_MODEL)), full((ROUTER_ROWS, D_MODEL)), full((ROUTER_ROWS, LANES)),
        full((tm, tm)),
        full((N_EXPERTS, LANES)),
    ]
    args = [mg, x2d, prm["ln_emb_g"], prm["ln_emb_b"], prm["w_out"], prm["ln1_g"], prm["ln1_b"],
            prm["wr_hi"], prm["wr_lo"], prm["r_bias"], ut, carry_in]
    aliases = {}
    if prev is not None:
        for k, buf in enumerate(prev):
            in_specs.append(pl.BlockSpec(memory_space=pl.ANY))
            aliases[len(args)] = k
            args.append(buf)
    out_shape = [
        jax.ShapeDtypeStruct((total_rows, D_MODEL), F32),
        jax.ShapeDtypeStruct((META_ROWS, total_rows), jnp.int32),
        jax.ShapeDtypeStruct((total_rows, LANES), F32),
        jax.ShapeDtypeStruct((N_EXPERTS, LANES), F32),
    ]
    out_specs = [
        pl.BlockSpec((tm, D_MODEL), lambda i: (i + off, 0)),
        pl.BlockSpec((META_ROWS, tm), lambda i: (0, i + off)),
        pl.BlockSpec((tm, LANES), lambda i: (i + off, 0)),
        full((N_EXPERTS, LANES)),
    ]
    if prev is not None:
        assert len(prev) == 3
    return pl.pallas_call(
        functools.partial(_post_body, nsteps),
        grid=(nsteps + int(zero_tail),),
        in_specs=in_specs,
        out_specs=out_specs,
        out_shape=out_shape,
        input_output_aliases=aliases,
        scratch_shapes=[pltpu.VMEM((N_EXPERTS, LANES), F32)],
        compiler_params=_cparams(("arbitrary",)),
        name="post_attn",
    )(*args)


def _prep_post_params(ln_emb_g, ln_emb_b, w_out, ln1_g, ln1_b, w_rg, b_rg, w_re, b_re):
    row = lambda v: v.reshape(1, -1)
    wr = jnp.concatenate([w_re.T, w_rg.T, jnp.zeros((ROUTER_ROWS - N_EXPERTS - N_GROUPS, D_MODEL), F32)], axis=0)
    wr_hi = wr.astype(BF16)
    wr_lo = (wr - wr_hi.astype(F32)).astype(BF16)
    rb = jnp.concatenate([b_re, b_rg, jnp.zeros((ROUTER_ROWS - N_EXPERTS - N_GROUPS,), F32)])
    return dict(ln_emb_g=row(ln_emb_g), ln_emb_b=row(ln_emb_b), w_out=w_out.astype(BF16), ln1_g=row(ln1_g),
                ln1_b=row(ln1_b), wr_hi=wr_hi, wr_lo=wr_lo,
                r_bias=jnp.broadcast_to(rb[:, None], (ROUTER_ROWS, LANES)))


MOE_ROWS = 256
PACKED = D_MODEL // 2
U32 = jnp.uint32


def _pack_bf16_pairs(x):
    half = x.shape[1] // 2
    hi = pltpu.bitcast(x[:, :half].astype(BF16).astype(F32), U32)
    lo = pltpu.bitcast(x[:, half:].astype(BF16).astype(F32), U32)
    return hi | (lo >> 16)


def _unpack_bf16_pairs(u):
    hi = pltpu.bitcast(u & jnp.uint32(0xFFFF0000), F32)
    lo = pltpu.bitcast(u << 16, F32)
    return hi, lo


def _moe_plan(counts, total_assign):
    nb_max = -(-total_assign // MOE_ROWS) + N_EXPERTS
    padded = (counts + MOE_ROWS - 1) // MOE_ROWS * MOE_ROWS
    pend = jnp.cumsum(padded)
    pstart = (pend - padded).astype(jnp.int32)
    block_start = jnp.arange(nb_max, dtype=jnp.int32) * MOE_ROWS
    n_ended = jnp.sum((pend[None, :] <= block_start[:, None]).astype(jnp.int32), axis=1)
    block_e = jnp.minimum(n_ended, N_EXPERTS - 1).astype(jnp.int32)
    n_used = (pend[-1] // MOE_ROWS).astype(jnp.int32).reshape(1)
    return pstart, block_e, n_used, nb_max


ROW_UNROLL = 8


def _slot_ids(meta, pstart):
    experts = meta[0:TOP_K]
    ranks = meta[TOP_K:2 * TOP_K]
    onehot = experts[..., None] == jnp.arange(N_EXPERTS, dtype=jnp.int32)
    return ranks + jnp.sum(jnp.where(onehot, pstart, 0), axis=-1)


def _for_row_groups(tm, fn):
    def group(g, c):
        t0 = pl.multiple_of(g * ROW_UNROLL, ROW_UNROLL)
        for r in range(ROW_UNROLL):
            for k in range(TOP_K):
                fn(t0 + r, k)
        return c

    lax.fori_loop(0, tm // ROW_UNROLL, group, 0)


def _dispatch_body(nsteps, s0_ref, s1_ref, h_ref, xs_in_ref, xs_ref, pk_ref, sems):
    del xs_in_ref
    i = pl.program_id(0)
    tm = h_ref.shape[0]
    slot_refs = (s0_ref, s1_ref)
    cur = i % 2
    pk_ref[cur] = _pack_bf16_pairs(h_ref[...])

    def send(t, k):
        pltpu.make_async_copy(pk_ref.at[cur, pl.ds(t, 1)], xs_ref.at[pl.ds(slot_refs[k][0, t], 1)],
                              sems.at[cur]).start(priority=k)

    def wait_buffer(buf):
        _for_row_groups(tm, lambda t, k: pltpu.make_async_copy(
            pk_ref.at[buf, pl.ds(t, 1)], xs_ref.at[pl.ds(0, 1)], sems.at[buf]).wait())

    _for_row_groups(tm, send)

    @pl.when(i > 0)
    def _():
        wait_buffer(1 - cur)

    @pl.when(i == nsteps - 1)
    def _():
        wait_buffer(cur)


def _dispatch(h1, slot_ids, nslots, tm, total):
    assert total % tm == 0 and total <= h1.shape[0] and tm % ROW_UNROLL == 0
    zeros = jnp.zeros((nslots, PACKED), U32)
    slot_spec = pl.BlockSpec((1, tm), lambda i: (0, i), memory_space=pltpu.SMEM)
    return pl.pallas_call(
        functools.partial(_dispatch_body, total // tm),
        grid=(total // tm,),
        in_specs=[slot_spec, slot_spec, pl.BlockSpec((tm, D_MODEL), lambda i: (i, 0)),
                  pl.BlockSpec(memory_space=pl.ANY)],
        out_specs=pl.BlockSpec(memory_space=pl.ANY),
        scratch_shapes=[pltpu.VMEM((2, tm, PACKED), U32), pltpu.SemaphoreType.DMA((2,))],
        out_shape=jax.ShapeDtypeStruct((nslots, PACKED), U32),
        input_output_aliases={3: 0},
        compiler_params=_cparams(("arbitrary",)),
        name="moe_dispatch",
    )(slot_ids[0:1], slot_ids[1:2], h1, zeros)


def _expert_body(be_ref, nu_ref, xs_ref, wg_ref, wu_ref, wd_ref, ys_ref, wgb_ref, wub_ref, wdb_ref):
    i = pl.program_id(0)
    changed = jnp.logical_or(i == 0, be_ref[i] != be_ref[jnp.maximum(i - 1, 0)])

    @pl.when(changed)
    def _():
        wgb_ref[...] = wg_ref[...].astype(BF16)
        wub_ref[...] = wu_ref[...].astype(BF16)
        wdb_ref[...] = wd_ref[...].astype(BF16)

    @pl.when(i < nu_ref[0])
    def _():
        hi, lo = _unpack_bf16_pairs(xs_ref[...])
        hi = hi.astype(BF16)
        lo = lo.astype(BF16)
        half = PACKED

        def proj(w_ref):
            return (jnp.dot(hi, w_ref[:half, :], preferred_element_type=F32)
                    + jnp.dot(lo, w_ref[half:, :], preferred_element_type=F32))

        g = proj(wgb_ref)
        hb = (g * _sigmoid(g)) * proj(wub_ref)
        y = jnp.dot(hb.astype(BF16), wdb_ref[...], preferred_element_type=F32)
        ys_ref[...] = _pack_bf16_pairs(y)

    @pl.when(i >= nu_ref[0])
    def _():
        ys_ref[...] = jnp.zeros_like(ys_ref)


def _experts(xs, block_e, n_used, w_gate, w_up, w_down, nb_max):
    wmap = lambda i, be, nu: (be[i], 0, 0)
    return pl.pallas_call(
        _expert_body,
        grid_spec=pltpu.PrefetchScalarGridSpec(
            num_scalar_prefetch=2,
            grid=(nb_max,),
            in_specs=[
                pl.BlockSpec((MOE_ROWS, PACKED), lambda i, be, nu: (i, 0)),
                pl.BlockSpec((None, D_MODEL, D_EXPERT), wmap),
                pl.BlockSpec((None, D_MODEL, D_EXPERT), wmap),
                pl.BlockSpec((None, D_EXPERT, D_MODEL), wmap),
            ],
            out_specs=pl.BlockSpec((MOE_ROWS, PACKED), lambda i, be, nu: (i, 0)),
            scratch_shapes=[pltpu.VMEM((D_MODEL, D_EXPERT), BF16), pltpu.VMEM((D_MODEL, D_EXPERT), BF16),
                            pltpu.VMEM((D_EXPERT, D_MODEL), BF16)],
        ),
        out_shape=jax.ShapeDtypeStruct(xs.shape, U32),
        compiler_params=_cparams(("arbitrary",)),
        name="moe_experts",
    )(block_e, n_used, xs, w_gate, w_up, w_down)


def _combine_body(nsteps, s0_ref, s1_ref, n0_ref, n1_ref, h_ref, w_ref, g2_ref, b2_ref, ys_ref, o_ref, buf_ref, sems):
    i = pl.program_id(0)
    tm = h_ref.shape[0]
    cur = i % 2

    def fetch(slot_refs, buf):
        _for_row_groups(tm, lambda t, k: pltpu.make_async_copy(
            ys_ref.at[pl.ds(slot_refs[k][0, t], 1)], buf_ref.at[buf, k, pl.ds(t, 1)], sems.at[buf]).start(priority=k))

    @pl.when(i == 0)
    def _():
        fetch((s0_ref, s1_ref), cur)

    @pl.when(i + 1 < nsteps)
    def _():
        fetch((n0_ref, n1_ref), 1 - cur)

    _for_row_groups(tm, lambda t, k: pltpu.make_async_copy(
        ys_ref.at[pl.ds(0, 1)], buf_ref.at[cur, k, pl.ds(t, 1)], sems.at[cur]).wait())
    w = w_ref[...]
    hi0, lo0 = _unpack_bf16_pairs(buf_ref[cur, 0])
    hi1, lo1 = _unpack_bf16_pairs(buf_ref[cur, 1])
    w0 = w[:, 0:1]
    w1 = w[:, 1:2]
    f = jnp.concatenate([w0 * hi0 + w1 * hi1, w0 * lo0 + w1 * lo1], axis=1)
    o_ref[...] = _layer_norm_rows(ALPHA * h_ref[...] + f, g2_ref[...], b2_ref[...])


def _combine(h1, wcol, slot_ids, ys, ln2_g, ln2_b, tm, row0, nrows):
    assert nrows % tm == 0 and row0 % tm == 0 and tm % ROW_UNROLL == 0
    off = row0 // tm
    nsteps = nrows // tm
    slot_spec = pl.BlockSpec((1, tm), lambda i: (0, i + off), memory_space=pltpu.SMEM)
    next_spec = pl.BlockSpec((1, tm), lambda i: (0, jnp.minimum(i + 1, nsteps - 1) + off), memory_space=pltpu.SMEM)
    return pl.pallas_call(
        functools.partial(_combine_body, nsteps),
        grid=(nsteps,),
        in_specs=[
            slot_spec, slot_spec, next_spec, next_spec,
            pl.BlockSpec((tm, D_MODEL), lambda i: (i + off, 0)),
            pl.BlockSpec((tm, LANES), lambda i: (i + off, 0)),
            pl.BlockSpec((1, D_MODEL), lambda i: (0, 0)),
            pl.BlockSpec((1, D_MODEL), lambda i: (0, 0)),
            pl.BlockSpec(memory_space=pl.ANY),
        ],
        out_specs=pl.BlockSpec((tm, D_MODEL), lambda i: (i, 0)),
        scratch_shapes=[pltpu.VMEM((2, TOP_K, tm, PACKED), U32), pltpu.SemaphoreType.DMA((2,))],
        out_shape=jax.ShapeDtypeStruct((nrows, D_MODEL), F32),
        compiler_params=_cparams(("arbitrary",)),
        name="moe_combine",
    )(slot_ids[0:1], slot_ids[1:2], slot_ids[0:1], slot_ids[1:2], h1, wcol, ln2_g.reshape(1, -1),
      ln2_b.reshape(1, -1), ys)


PROJ_ROWS = 512
POST_ROWS = 512
DISPATCH_ROWS = 384
COMBINE_ROWS = 256


def kernel(x_prompt, x_sample, state_gla, cache_swa_k, cache_swa_v, meta_tokens, ln_emb_g, ln_emb_b, rel_bias, w_in,
           gk_up, gk_bias, gla_norm_g, sinks, w_out, ln1_g, ln1_b, w_router_group, b_router_group, w_router_expert,
           b_router_expert, w_gate, w_up, w_down, ln2_g, ln2_b):
    nbatch, seq, d = x_prompt.shape
    nseq = x_sample.shape[0]
    assert w_in.shape[0] == DEPTH == 1 and d == D_MODEL and x_sample.shape[1] == 1
    assert seq % BLK == 0 and nseq == BLK and meta_tokens.shape[0] == N_META
    nblk = seq // BLK
    n_prompt = nbatch * seq
    total = n_prompt + nseq
    kvw = ATT_KV_HEADS * HEAD_DIM

    xp = x_prompt.reshape(n_prompt, d)
    xs = x_sample.reshape(nseq, d)
    extra = jnp.concatenate([xs, jnp.zeros((BLK - N_META, d), xs.dtype), meta_tokens.astype(xs.dtype)], axis=0)
    w_cat = _prep_w_in(w_in[0])
    pp = _ln_proj(xp, ln_emb_g, ln_emb_b, w_cat, PROJ_ROWS)
    pe = _ln_proj(extra, ln_emb_g, ln_emb_b, w_cat, 2 * BLK)

    yg, gla_p = _gla_prompt(pp, pe, gk_up[0], gk_bias[0], gla_norm_g[0], nbatch, nblk)
    yg_s, gla_s = _gla_step(pe, gk_up[0], gk_bias[0], gla_norm_g[0], state_gla[0])
    mg = _swa_prompt(pp, pe, yg, rel_bias, sinks[0], nbatch, nblk)
    mg_s, k_s, v_s = _swa_step(pe, yg_s, cache_swa_k[0].reshape(nseq, WINDOW, kvw),
                               cache_swa_v[0].reshape(nseq, WINDOW, kvw), rel_bias, sinks[0])

    prm = _prep_post_params(ln_emb_g, ln_emb_b, w_out[0], ln1_g[0], ln1_b[0], w_router_group[0], b_router_group[0],
                            w_router_expert[0], b_router_expert[0])
    carry0 = jnp.zeros((N_EXPERTS, LANES), F32)
    rows_alloc = n_prompt + POST_ROWS
    h1, meta, wcol, carry1 = _post(mg, xp, prm, POST_ROWS, 0, rows_alloc, carry0, zero_tail=True)
    h1, meta, wcol, carry2 = _post(mg_s, xs, prm, nseq, n_prompt, rows_alloc, carry1, prev=(h1, meta, wcol))

    counts = carry2[:, 0].astype(jnp.int32)
    pstart, block_e, n_used, nb_max = _moe_plan(counts, TOP_K * total)
    slot_ids = _slot_ids(meta, pstart)
    xs_sorted = _dispatch(h1, slot_ids, nb_max * MOE_ROWS, DISPATCH_ROWS, total)
    ys = _experts(xs_sorted, block_e, n_used, w_gate[0], w_up[0], w_down[0], nb_max)
    y_p = _combine(h1, wcol, slot_ids, ys, ln2_g[0], ln2_b[0], COMBINE_ROWS, 0, n_prompt)
    y_s = _combine(h1, wcol, slot_ids, ys, ln2_g[0], ln2_b[0], nseq, n_prompt, nseq)

    kv_shape = (1, nbatch, WINDOW, ATT_KV_HEADS, HEAD_DIM)
    k_p = pp["ka"].reshape(nbatch, seq, kvw)[:, seq - WINDOW:].reshape(kv_shape)
    v_p = pp["va"].reshape(nbatch, seq, kvw)[:, seq - WINDOW:].reshape(kv_shape)
    return (y_p.reshape(nbatch, seq, d), y_s.reshape(nseq, 1, d), gla_p[None], k_p, v_p, gla_s[None],
            k_s.reshape(cache_swa_k.shape), v_s.reshape(cache_swa_v.shape))
```

```python
import functools
import math

import jax
import jax.numpy as jnp
import numpy as np
from jax import lax
from jax.experimental import pallas as pl
from jax.experimental.pallas import tpu as pltpu

F32 = jnp.float32
BF16 = jnp.bfloat16

D_MODEL = 1024
N_META = 16
LN_EPS = 1e-5
GLA_HEADS = 4
GLA_DK = 128
GLA_DV = 256
GLA_RANK = 16
GLA_TAU = 16.0
HEAD_DIM = 64
ATT_HEADS = 16
ATT_KV_HEADS = 4
GQA_GROUP = 4
WINDOW = 128
REL_BUCKETS = 32
REL_MAX_DIST = 128
N_GROUPS = 4
EXPERTS_PER_GROUP = 8
N_EXPERTS = 32
TOP_K = 2
D_EXPERT = 512
DEPTH = 1
ALPHA = (2.0 * DEPTH) ** 0.25

LANES = 128
BLK = 128
VMEM_LIMIT = 56 * 1024 * 1024


def _cparams(sem):
    return pltpu.CompilerParams(dimension_semantics=sem, vmem_limit_bytes=VMEM_LIMIT)


def _layer_norm_rows(x, g, b):
    mu = jnp.mean(x, axis=-1, keepdims=True)
    xc = x - mu
    var = jnp.mean(xc * xc, axis=-1, keepdims=True)
    return xc * lax.rsqrt(var + LN_EPS) * g + b


_PROJ_OUTS = (
    ("qg", GLA_HEADS * GLA_DK, BF16, GLA_DK ** -0.5),
    ("kg", GLA_HEADS * GLA_DK, BF16, None),
    ("vg", GLA_HEADS * GLA_DV, BF16, None),
    ("rg", GLA_HEADS * GLA_DV, BF16, None),
    ("qa", ATT_HEADS * HEAD_DIM, BF16, HEAD_DIM ** -0.5),
    ("ka", ATT_KV_HEADS * HEAD_DIM, F32, None),
    ("va", ATT_KV_HEADS * HEAD_DIM, F32, None),
    ("ga", D_MODEL, BF16, None),
    ("gb", D_MODEL, BF16, None),
    ("lr", LANES, F32, None),
)
_PROJ_W = sum(w for _, w, _, _ in _PROJ_OUTS)


def _prep_w_in(w_in):
    sizes = (512, 512, 1024, 1024, GLA_RANK, 1024, 256, 256, 1024, 1024)
    offs = np.cumsum((0,) + sizes)
    a = w_in[:, : offs[4]]
    lr = w_in[:, offs[4]: offs[5]]
    b = w_in[:, offs[5]:]
    pad = jnp.zeros((w_in.shape[0], LANES - GLA_RANK), w_in.dtype)
    return jnp.concatenate([a, b, lr, pad], axis=1).astype(BF16)


def _ln_proj_body(x_ref, g_ref, b_ref, w_ref, *out_refs):
    xn = _layer_norm_rows(x_ref[...], g_ref[...], b_ref[...]).astype(BF16)
    c0 = 0
    for (_, width, dtype, scale), o_ref in zip(_PROJ_OUTS, out_refs):
        acc = jnp.dot(xn, w_ref[:, c0:c0 + width], preferred_element_type=F32)
        if scale is not None:
            acc = acc * scale
        o_ref[...] = acc.astype(dtype)
        c0 += width


def _ln_proj(x2d, ln_g, ln_b, w_cat, tm):
    m = x2d.shape[0]
    assert m % tm == 0
    out_shape = [jax.ShapeDtypeStruct((m, w), dt) for _, w, dt, _ in _PROJ_OUTS]
    out_specs = [pl.BlockSpec((tm, w), lambda i: (i, 0)) for _, w, _, _ in _PROJ_OUTS]
    outs = pl.pallas_call(
        _ln_proj_body,
        grid=(m // tm,),
        in_specs=[
            pl.BlockSpec((tm, D_MODEL), lambda i: (i, 0)),
            pl.BlockSpec((1, D_MODEL), lambda i: (0, 0)),
            pl.BlockSpec((1, D_MODEL), lambda i: (0, 0)),
            pl.BlockSpec((D_MODEL, _PROJ_W), lambda i: (0, 0), pipeline_mode=pl.Buffered(1)),
        ],
        out_specs=out_specs,
        out_shape=out_shape,
        compiler_params=_cparams(("arbitrary",)),
        name="ln_proj",
    )(x2d, ln_g.reshape(1, -1), ln_b.reshape(1, -1), w_cat)
    return dict(zip([n for n, _, _, _ in _PROJ_OUTS], outs))


_GLA_LEVELS = tuple(2 ** i for i in range(int(math.log2(BLK))))
GLA_SAFE_EXPONENT = 60.0


def _log_sigmoid(x):
    return jnp.minimum(x, 0.0) - jnp.log(1.0 + jnp.exp(-jnp.abs(x)))


def _sigmoid(x):
    return 1.0 / (1.0 + jnp.exp(-x))


def _split_dot(a01, x):
    hi = x.astype(BF16)
    lo = (x - hi.astype(F32)).astype(BF16)
    n = x.shape[1]
    both = jnp.dot(a01, jnp.concatenate([hi, lo], axis=1), preferred_element_type=F32)
    return both[:, :n] + both[:, n:]


def _gla_anchor_exponent(b, la, s, row):
    if s == 1:
        return jnp.where(row % 2 == 1, la, 0.0)
    if s == 2:
        la_dn = pltpu.roll(la, 1, axis=0)
        la_up = pltpu.roll(la, BLK - 1, axis=0)
        r = row % 4
        return jnp.where(r == 0, la_up, jnp.where(r == 1, 0.0, jnp.where(r == 2, la, la + la_dn)))
    nb = BLK // (2 * s)
    b3 = b.reshape(nb, 2 * s, b.shape[-1])
    anchor = jnp.broadcast_to(b3[:, s - 1:s, :], b3.shape).reshape(b.shape)
    return -jnp.abs(b - anchor)


def _gla_body(nblk, qm, km, vm, rm, lrm, gam, qp, kp, vp, rp, lrp, gap, gkup_ref, gkb_ref, gn_ref, tri_ref,
              y_ref, s_out_ref, s_ref):
    c = pl.program_id(1)
    is_meta = c == 0

    @pl.when(is_meta)
    def _():
        s_ref[...] = jnp.zeros_like(s_ref)

    def pick(m_ref, p_ref):
        return jnp.where(is_meta, m_ref[...], p_ref[...])

    row = lax.broadcasted_iota(jnp.int32, (BLK, GLA_DK), 0)
    col_t = lax.broadcasted_iota(jnp.int32, (BLK, BLK), 1)
    row_t = lax.broadcasted_iota(jnp.int32, (BLK, BLK), 0)
    live = jnp.logical_or(jnp.logical_not(is_meta), row >= BLK - N_META)
    tri = tri_ref[...]
    q_all, k_all, v_all, r_all = pick(qm, qp), pick(km, kp), pick(vm, vp), pick(rm, rp)
    ga_all = pick(gam, gap)
    lr = pick(lrm, lrp).astype(BF16)
    x_all = jnp.dot(lr, gkup_ref[...], preferred_element_type=F32) + gkb_ref[...]
    la_all = _log_sigmoid(x_all) * (1.0 / GLA_TAU)
    nt = (((1,), (1,)), ((), ()))
    mid = BLK // 2 - 1

    def head(h, single_anchor):
        dk = slice(h * GLA_DK, (h + 1) * GLA_DK)
        dv = slice(h * GLA_DV, (h + 1) * GLA_DV)
        la = jnp.where(live, la_all[:, dk], 0.0)
        q = q_all[:, dk].astype(F32)
        k = jnp.where(live, k_all[:, dk].astype(F32), 0.0)
        v = v_all[:, dv]
        b = _split_dot(tri, la)
        b_last = b[BLK - 1:BLK, :]
        s_old = s_ref[h]
        if single_anchor:
            b_mid = b[mid:mid + 1, :]
            qe = q * jnp.exp(b - b_mid)
            ke = k * jnp.exp(b_mid - b)
            a = jnp.where(row_t >= col_t,
                          lax.dot_general(qe.astype(BF16), ke.astype(BF16), nt, preferred_element_type=F32), 0.0)
            qg = qe * jnp.exp(b_mid)
            kd = ke * jnp.exp(b_last - b_mid)
        else:
            a = jnp.where(row_t == col_t,
                          lax.dot_general(q.astype(BF16), k.astype(BF16), nt, preferred_element_type=F32), 0.0)
            for s in _GLA_LEVELS:
                e = jnp.exp(_gla_anchor_exponent(b, la, s, row))
                upper = (row // s) % 2 == 1
                q_s = jnp.where(upper, q * e, 0.0).astype(BF16)
                k_s = jnp.where(upper, 0.0, k * e).astype(BF16)
                p = lax.dot_general(q_s, k_s, nt, preferred_element_type=F32)
                a = a + jnp.where(row_t // (2 * s) == col_t // (2 * s), p, 0.0)
            qg = q * jnp.exp(b)
            kd = k * jnp.exp(b_last - b)
        o = jnp.dot(qg.astype(BF16), s_old.astype(BF16), preferred_element_type=F32)
        lhs = jnp.concatenate([jnp.transpose(kd).astype(BF16), a.astype(BF16)], axis=0)
        both = jnp.dot(lhs, v, preferred_element_type=F32)
        decay_col = jnp.transpose(jnp.broadcast_to(jnp.exp(b_last), (BLK, GLA_DK)))[:, :1]
        s_ref[h] = decay_col * s_old + both[:GLA_DK]
        o = o + both[GLA_DK:]
        o = o * lax.rsqrt(jnp.mean(o * o, axis=-1, keepdims=True) + LN_EPS) * gn_ref[...]
        r = r_all[:, dv].astype(F32)
        y = o * (r * _sigmoid(r)) * _sigmoid(ga_all[:, dv].astype(F32))
        y_ref[:, dv] = y.astype(y_ref.dtype)

    mild = jnp.max(jnp.abs(la_all)) * (BLK // 2) <= GLA_SAFE_EXPONENT

    @pl.when(mild)
    def _():
        for h in range(GLA_HEADS):
            head(h, True)

    @pl.when(jnp.logical_not(mild))
    def _():
        for h in range(GLA_HEADS):
            head(h, False)

    @pl.when(c == nblk)
    def _():
        s_out_ref[...] = s_ref[...]


def _tri_incl():
    i = np.arange(BLK)
    return jnp.asarray((i[None, :] <= i[:, None]).astype(np.float32), dtype=BF16)


def _gla_prompt(pp, pe, gk_up, gk_bias, gnorm, nbatch, nblk):
    names = ("qg", "kg", "vg", "rg", "lr", "ga")
    gkup = jnp.concatenate([gk_up, jnp.zeros((LANES - GLA_RANK, gk_up.shape[1]), gk_up.dtype)], axis=0).astype(BF16)
    m_specs = [pl.BlockSpec((BLK, pe[n].shape[1]), lambda b, c: (1, 0)) for n in names]
    p_specs = [pl.BlockSpec((BLK, pp[n].shape[1]), lambda b, c: (b * nblk + jnp.maximum(c - 1, 0), 0)) for n in names]
    w_specs = [
        pl.BlockSpec((LANES, GLA_HEADS * GLA_DK), lambda b, c: (0, 0)),
        pl.BlockSpec((1, GLA_HEADS * GLA_DK), lambda b, c: (0, 0)),
        pl.BlockSpec((1, GLA_DV), lambda b, c: (0, 0)),
        pl.BlockSpec((BLK, BLK), lambda b, c: (0, 0)),
    ]
    y, s_fin = pl.pallas_call(
        functools.partial(_gla_body, nblk),
        grid=(nbatch, nblk + 1),
        in_specs=m_specs + p_specs + w_specs,
        out_specs=[
            pl.BlockSpec((BLK, D_MODEL), lambda b, c: (b * nblk + jnp.maximum(c - 1, 0), 0)),
            pl.BlockSpec((None, GLA_HEADS, GLA_DK, GLA_DV), lambda b, c: (b, 0, 0, 0)),
        ],
        out_shape=[
            jax.ShapeDtypeStruct((nbatch * nblk * BLK, D_MODEL), BF16),
            jax.ShapeDtypeStruct((nbatch, GLA_HEADS, GLA_DK, GLA_DV), F32),
        ],
        scratch_shapes=[pltpu.VMEM((GLA_HEADS, GLA_DK, GLA_DV), F32)],
        compiler_params=_cparams(("arbitrary", "arbitrary")),
        name="gla_prompt",
    )(*[pe[n] for n in names], *[pp[n] for n in names], gkup, gk_bias.reshape(1, -1), gnorm.reshape(1, -1), _tri_incl())
    return y, s_fin


GLA_STEP_SEQS = 16


def _gla_step_body(q_ref, k_ref, v_ref, r_ref, lr_ref, ga_ref, gkup_ref, gkb_ref, gn_ref, s_in_ref,
                   y_ref, s_out_ref, at_ref, kt_ref, qt_ref):
    g = pl.program_id(0)
    nseq = q_ref.shape[0]

    @pl.when(g == 0)
    def _():
        x = jnp.dot(lr_ref[...].astype(BF16), gkup_ref[...], preferred_element_type=F32) + gkb_ref[...]
        a = jnp.exp(_log_sigmoid(x) * (1.0 / GLA_TAU))
        for h in range(GLA_HEADS):
            dk = slice(h * GLA_DK, (h + 1) * GLA_DK)
            at_ref[h] = jnp.transpose(a[:, dk])
            kt_ref[h] = jnp.transpose(k_ref[:, dk].astype(F32))
            qt_ref[h] = jnp.transpose(q_ref[:, dk].astype(F32))

    lane = lax.broadcasted_iota(jnp.int32, (GLA_DK, nseq), 1)
    ones = jnp.ones((nseq, GLA_DV), BF16)
    grp = pl.ds(pl.multiple_of(g * GLA_STEP_SEQS, GLA_STEP_SEQS), GLA_STEP_SEQS)
    r_grp = r_ref[grp, :].astype(F32)
    ga_grp = ga_ref[grp, :].astype(F32)
    for i in range(GLA_STEP_SEQS):
        n = g * GLA_STEP_SEQS + i
        sel = lane == n
        for h in range(GLA_HEADS):
            dv = slice(h * GLA_DV, (h + 1) * GLA_DV)
            a_sel = jnp.where(sel, at_ref[h], 0.0)
            k_sel = jnp.where(sel, kt_ref[h], 0.0).astype(BF16)
            q_sel = jnp.where(sel, qt_ref[h], 0.0).astype(BF16)
            decay = _split_dot_rhs(a_sel, ones)
            kv = jnp.dot(k_sel, v_ref[:, dv], preferred_element_type=F32)
            q_b = jnp.dot(q_sel, ones, preferred_element_type=F32)
            s_new = decay * s_in_ref[i, h] + kv
            s_out_ref[i, h] = s_new
            o = jnp.sum(q_b * s_new, axis=0, keepdims=True)
            o = o * lax.rsqrt(jnp.mean(o * o, axis=-1, keepdims=True) + LN_EPS) * gn_ref[...]
            r = r_grp[i:i + 1, dv]
            ga = ga_grp[i:i + 1, dv]
            y_ref[i:i + 1, dv] = (o * (r * _sigmoid(r)) * _sigmoid(ga)).astype(y_ref.dtype)


def _split_dot_rhs(x, b01):
    hi = x.astype(BF16)
    lo = (x - hi.astype(F32)).astype(BF16)
    return jnp.dot(hi, b01, preferred_element_type=F32) + jnp.dot(lo, b01, preferred_element_type=F32)


def _gla_step(pe, gk_up, gk_bias, gnorm, state):
    nseq = state.shape[0]
    assert nseq == BLK and nseq % GLA_STEP_SEQS == 0
    names = ("qg", "kg", "vg", "rg", "lr", "ga")
    gkup = jnp.concatenate([gk_up, jnp.zeros((LANES - GLA_RANK, gk_up.shape[1]), gk_up.dtype)], axis=0).astype(BF16)
    t_specs = [pl.BlockSpec((nseq, pe[n].shape[1]), lambda g: (0, 0)) for n in names]
    st_spec = pl.BlockSpec((GLA_STEP_SEQS, GLA_HEADS, GLA_DK, GLA_DV), lambda g: (g, 0, 0, 0))
    return pl.pallas_call(
        _gla_step_body,
        grid=(nseq // GLA_STEP_SEQS,),
        in_specs=t_specs + [
            pl.BlockSpec((LANES, GLA_HEADS * GLA_DK), lambda g: (0, 0)),
            pl.BlockSpec((1, GLA_HEADS * GLA_DK), lambda g: (0, 0)),
            pl.BlockSpec((1, GLA_DV), lambda g: (0, 0)),
            st_spec,
        ],
        out_specs=[pl.BlockSpec((GLA_STEP_SEQS, D_MODEL), lambda g: (g, 0)), st_spec],
        out_shape=[jax.ShapeDtypeStruct((nseq, D_MODEL), F32), jax.ShapeDtypeStruct(state.shape, F32)],
        scratch_shapes=[pltpu.VMEM((GLA_HEADS, GLA_DK, nseq), F32) for _ in range(3)],
        compiler_params=_cparams(("arbitrary",)),
        name="gla_step",
    )(*[pe[n] for n in names], gkup, gk_bias.reshape(1, -1), gnorm.reshape(1, -1), state)


HALF = LANES // 2


def _rel_bucket(dist):
    max_exact = REL_BUCKETS // 2
    d = jnp.maximum(dist, 0)
    large = max_exact + (jnp.log(jnp.maximum(d, 1).astype(F32) / max_exact)
                         / math.log(REL_MAX_DIST / max_exact) * (REL_BUCKETS - max_exact)).astype(jnp.int32)
    large = jnp.minimum(large, REL_BUCKETS - 1)
    return jnp.where(d < max_exact, d, large)


def _bias_lookup(rel_bias, dist):
    onehot = (_rel_bucket(dist)[..., None] == jnp.arange(REL_BUCKETS)).astype(F32)
    return jnp.einsum("...b,bh->h...", onehot, rel_bias.astype(F32), precision=lax.Precision.HIGHEST)


def _swa_bias_tables(rel_bias):
    q = jnp.arange(BLK)[:, None]
    c = jnp.arange(2 * BLK)[None, :]
    dist = BLK + q - c
    bias = _bias_lookup(rel_bias, dist)
    inside = (dist >= 0) & (dist < WINDOW)
    first = inside & (c >= BLK - N_META)
    neg = jnp.float32(-jnp.inf)
    return jnp.stack([jnp.where(first[None], bias, neg), jnp.where(inside[None], bias, neg)])


def _half_tiles(x):
    lane = lax.broadcasted_iota(jnp.int32, (x.shape[0], LANES), 1)
    low = lane < HALF
    out = []
    for t in range(2):
        tile = x[:, t * LANES:(t + 1) * LANES]
        swapped = pltpu.roll(tile, HALF, axis=1)
        zero = jnp.zeros_like(tile)
        even = (jnp.where(low, tile, zero).astype(BF16), jnp.where(low, zero, swapped).astype(BF16))
        odd = (jnp.where(low, swapped, zero).astype(BF16), jnp.where(low, zero, tile).astype(BF16))
        out += [even, odd]
    return out


def _dup_tiles(x):
    lane = lax.broadcasted_iota(jnp.int32, (x.shape[0], LANES), 1)
    low = lane < HALF
    out = []
    for t in range(2):
        tile = x[:, t * LANES:(t + 1) * LANES]
        swapped = pltpu.roll(tile, HALF, axis=1)
        out += [jnp.where(low, tile, swapped).astype(BF16), jnp.where(low, swapped, tile).astype(BF16)]
    return out


def _swa_body(nblk, q_ref, kc_ref, vc_ref, kp_ref, vp_ref, km_ref, vm_ref, gb_ref, yg_ref, tb_ref, sink_ref,
              o_ref):
    blk = pl.program_id(1)
    first = blk == 0
    k_prev = jnp.where(first, km_ref[...], kp_ref[...])
    v_prev = jnp.where(first, vm_ref[...], vp_ref[...])
    k_tiles = _dup_tiles(jnp.concatenate([k_prev, kc_ref[...]], axis=0))
    v_tiles = _dup_tiles(jnp.concatenate([v_prev, vc_ref[...]], axis=0))
    variant = jnp.minimum(blk, 1)
    npair = GQA_GROUP // 2
    low = lax.broadcasted_iota(jnp.int32, (BLK, LANES), 1) < HALF
    seg = lax.broadcasted_iota(jnp.int32, (GQA_GROUP * BLK, 1), 0) // BLK
    nt = (((1,), (1,)), ((), ()))
    for j in range(ATT_KV_HEADS):
        tiles = [j * npair + pair for pair in range(npair)]
        q_t = [q_ref[:, t * LANES:(t + 1) * LANES] for t in tiles]
        zero = jnp.zeros_like(q_t[0])
        q_st = jnp.concatenate([jnp.where(low, q, zero) for q in q_t] + [jnp.where(low, zero, q) for q in q_t], axis=0)
        heads = [2 * t for t in tiles] + [2 * t + 1 for t in tiles]
        s = lax.dot_general(q_st, k_tiles[j], nt, preferred_element_type=F32)
        s = s + jnp.concatenate([tb_ref[variant, h] for h in heads], axis=0)
        sink = jnp.full((GQA_GROUP * BLK, 1), sink_ref[heads[0]], F32)
        for i in range(1, GQA_GROUP):
            sink = jnp.where(seg == i, sink_ref[heads[i]], sink)
        m = jnp.maximum(jnp.max(s, axis=-1, keepdims=True), sink)
        p = jnp.exp(s - m)
        inv = 1.0 / (jnp.sum(p, axis=-1, keepdims=True) + jnp.exp(sink - m))
        o = jnp.dot(p.astype(BF16), v_tiles[j], preferred_element_type=F32) * inv
        for pair, t in enumerate(tiles):
            cols = slice(t * LANES, (t + 1) * LANES)
            gate = _sigmoid(gb_ref[:, cols].astype(F32))
            even = o[pair * BLK:(pair + 1) * BLK]
            odd = o[(npair + pair) * BLK:(npair + pair + 1) * BLK]
            o_ref[:, cols] = (gate * jnp.where(low, even, odd) + yg_ref[:, cols].astype(F32)).astype(o_ref.dtype)


def _swa_prompt(pp, pe, yg, rel_bias, sinks, nbatch, nblk):
    tb = _swa_bias_tables(rel_bias)
    kvw = ATT_KV_HEADS * HEAD_DIM
    cur = lambda b, c: (b * nblk + c, 0)
    prev = lambda b, c: (b * nblk + jnp.maximum(c - 1, 0), 0)
    return pl.pallas_call(
        functools.partial(_swa_body, nblk),
        grid=(nbatch, nblk),
        in_specs=[
            pl.BlockSpec((BLK, D_MODEL), cur),
            pl.BlockSpec((BLK, kvw), cur), pl.BlockSpec((BLK, kvw), cur),
            pl.BlockSpec((BLK, kvw), prev), pl.BlockSpec((BLK, kvw), prev),
            pl.BlockSpec((BLK, kvw), lambda b, c: (1, 0)), pl.BlockSpec((BLK, kvw), lambda b, c: (1, 0)),
            pl.BlockSpec((BLK, D_MODEL), cur),
            pl.BlockSpec((BLK, D_MODEL), cur),
            pl.BlockSpec(tb.shape, lambda b, c: (0, 0, 0, 0)),
            pl.BlockSpec(memory_space=pltpu.SMEM),
        ],
        out_specs=pl.BlockSpec((BLK, D_MODEL), cur),
        out_shape=jax.ShapeDtypeStruct((nbatch * nblk * BLK, D_MODEL), BF16),
        compiler_params=_cparams(("arbitrary", "arbitrary")),
        name="swa_prompt",
    )(pp["qa"], pp["ka"], pp["va"], pp["ka"], pp["va"], pe["ka"], pe["va"], pp["gb"], yg, tb, sinks)


SWA_STEP_SEQS = 8
Q_TILES = ATT_HEADS // 2


def _swa_step_body(q_ref, kn_ref, vn_ref, ck_ref, cv_ref, gb_ref, yg_ref, tb_ref, sink_ref,
                   o_ref, ko_ref, vo_ref):
    row = lax.broadcasted_iota(jnp.int32, (WINDOW, ATT_KV_HEADS * HEAD_DIM), 0)
    trow = lax.broadcasted_iota(jnp.int32, (Q_TILES, LANES), 0)
    low = lax.broadcasted_iota(jnp.int32, (Q_TILES, LANES), 1) < HALF
    nt = (((1,), (1,)), ((), ()))
    for i in range(SWA_STEP_SEQS):
        k_win = jnp.where(row == WINDOW - 1, kn_ref[i:i + 1, :], pltpu.roll(ck_ref[i], WINDOW - 1, axis=0))
        v_win = jnp.where(row == WINDOW - 1, vn_ref[i:i + 1, :], pltpu.roll(cv_ref[i], WINDOW - 1, axis=0))
        ko_ref[i] = k_win
        vo_ref[i] = v_win
        k_tiles = _half_tiles(k_win)
        v_tiles = _half_tiles(v_win)
        q8 = q_ref[i].astype(BF16)
        s = [jnp.zeros((Q_TILES, WINDOW), F32), jnp.zeros((Q_TILES, WINDOW), F32)]
        for j in range(ATT_KV_HEADS):
            mine = trow // (GQA_GROUP // 2) == j
            for half in range(2):
                sj = lax.dot_general(q8, k_tiles[j][half], nt, preferred_element_type=F32)
                s[half] = jnp.where(mine, sj, s[half])
        p, inv = [], []
        for half in range(2):
            sh = s[half] + tb_ref[half]
            sink = sink_ref[half]
            m = jnp.maximum(jnp.max(sh, axis=-1, keepdims=True), sink)
            ph = jnp.exp(sh - m)
            inv.append(1.0 / (jnp.sum(ph, axis=-1, keepdims=True) + jnp.exp(sink - m)))
            p.append(ph.astype(BF16))
        o = jnp.zeros((Q_TILES, LANES), F32)
        for j in range(ATT_KV_HEADS):
            mine = trow // (GQA_GROUP // 2) == j
            pv = (jnp.dot(p[0], v_tiles[j][0], preferred_element_type=F32)
                  + jnp.dot(p[1], v_tiles[j][1], preferred_element_type=F32))
            o = jnp.where(mine, pv, o)
        o = o * jnp.where(low, inv[0], inv[1])
        o_ref[i] = _sigmoid(gb_ref[i]) * o + yg_ref[i]


def _swa_step(pe, yg_s, cache_k, cache_v, rel_bias, sinks):
    nseq = cache_k.shape[0]
    kvw = ATT_KV_HEADS * HEAD_DIM
    as_tiles = lambda x: x[:nseq].astype(F32).reshape(nseq, Q_TILES, LANES)
    dist = (WINDOW - 1) - jnp.arange(WINDOW)
    bias = _bias_lookup(rel_bias, dist)
    tb = jnp.stack([bias[0::2], bias[1::2]])
    sk = jnp.broadcast_to(jnp.stack([sinks[0::2], sinks[1::2]])[:, :, None], (2, Q_TILES, 1)).astype(F32)
    g = SWA_STEP_SEQS
    tile_spec = pl.BlockSpec((g, Q_TILES, LANES), lambda s: (s, 0, 0))
    win_spec = pl.BlockSpec((g, WINDOW, kvw), lambda s: (s, 0, 0))
    new_spec = pl.BlockSpec((g, kvw), lambda s: (s, 0))
    o, ko, vo = pl.pallas_call(
        _swa_step_body,
        grid=(nseq // g,),
        in_specs=[tile_spec, new_spec, new_spec, win_spec, win_spec, tile_spec, tile_spec,
                  pl.BlockSpec(tb.shape, lambda s: (0, 0, 0)), pl.BlockSpec(sk.shape, lambda s: (0, 0, 0))],
        out_specs=[tile_spec, win_spec, win_spec],
        out_shape=[jax.ShapeDtypeStruct((nseq, Q_TILES, LANES), F32),
                   jax.ShapeDtypeStruct(cache_k.shape, F32), jax.ShapeDtypeStruct(cache_v.shape, F32)],
        compiler_params=_cparams(("arbitrary",)),
        name="swa_step",
    )(as_tiles(pe["qa"]), pe["ka"], pe["va"], cache_k, cache_v, as_tiles(pe["gb"]),
      yg_s.reshape(nseq, Q_TILES, LANES), tb, sk)
    return o.reshape(nseq, D_MODEL), ko, vo


ROUTER_ROWS = 40
META_ROWS = 8


def _split3_nt(a_hi, a_lo, x):
    nt = (((1,), (1,)), ((), ()))
    x_hi = x.astype(BF16)
    x_lo = (x - x_hi.astype(F32)).astype(BF16)
    return (lax.dot_general(a_hi, x_hi, nt, preferred_element_type=F32)
            + lax.dot_general(a_hi, x_lo, nt, preferred_element_type=F32)
            + lax.dot_general(a_lo, x_hi, nt, preferred_element_type=F32))


def _first_argmax_rows(v, ridx, nrows):
    vmax = jnp.max(v, axis=0, keepdims=True)
    idx = jnp.min(jnp.where(v == vmax, ridx, nrows), axis=0, keepdims=True)
    return vmax, idx


def _post_body(nsteps, *refs):
    h1_ref, meta_ref, wcol_ref = refs[-5:-2]
    i = pl.program_id(0)

    @pl.when(i < nsteps)
    def _():
        _post_tile(i, *refs)

    @pl.when(i >= nsteps)
    def _():
        h1_ref[...] = jnp.zeros_like(h1_ref)
        meta_ref[...] = jnp.zeros_like(meta_ref)
        wcol_ref[...] = jnp.zeros_like(wcol_ref)


def _post_tile(i, mg_ref, x_ref, lng_ref, lnb_ref, wo_ref, g1_ref, b1_ref, wrh_ref, wrl_ref, rb_ref, ut_ref,
               cin_ref, *rest):
    h1_ref, meta_ref, wcol_ref, cout_ref, carry_ref = rest[-5:]

    @pl.when(i == 0)
    def _():
        carry_ref[...] = cin_ref[...]

    tm = x_ref.shape[0]
    h = _layer_norm_rows(x_ref[...], lng_ref[...], lnb_ref[...])
    acc = jnp.dot(mg_ref[...].astype(BF16), wo_ref[...], preferred_element_type=F32)
    h1 = _layer_norm_rows(ALPHA * h + acc, g1_ref[...], b1_ref[...])
    h1_ref[...] = h1

    lt = _split3_nt(wrh_ref[...], wrl_ref[...], h1) + rb_ref[:, :1]
    ridx = lax.broadcasted_iota(jnp.int32, (EXPERTS_PER_GROUP, tm), 0)
    neg = jnp.float32(-jnp.inf)
    g_log = jnp.where(ridx < N_GROUPS, lt[N_EXPERTS:N_EXPERTS + EXPERTS_PER_GROUP], neg)
    g_max, grp = _first_argmax_rows(g_log, ridx, EXPERTS_PER_GROUP)
    p_grp = 1.0 / jnp.sum(jnp.exp(g_log - g_max), axis=0, keepdims=True)
    e_in = lt[0:EXPERTS_PER_GROUP]
    for gi in range(1, N_GROUPS):
        e_in = jnp.where(grp == gi, lt[gi * EXPERTS_PER_GROUP:(gi + 1) * EXPERTS_PER_GROUP], e_in)
    v0, i0 = _first_argmax_rows(e_in, ridx, EXPERTS_PER_GROUP)
    v1, i1 = _first_argmax_rows(jnp.where(ridx == i0, neg, e_in), ridx, EXPERTS_PER_GROUP)
    t = jnp.exp(v1 - v0)
    w0 = p_grp / (1.0 + t)
    w1 = p_grp * t / (1.0 + t)
    e0 = grp * EXPERTS_PER_GROUP + i0
    e1 = grp * EXPERTS_PER_GROUP + i1

    eidx = lax.broadcasted_iota(jnp.int32, (N_EXPERTS, tm), 0)
    hit0 = eidx == e0
    hit1 = eidx == e1
    oh = jnp.where(jnp.logical_or(hit0, hit1), 1.0, 0.0)
    before = jnp.dot(oh.astype(BF16), ut_ref[...], preferred_element_type=F32) + carry_ref[:, :1]
    r0 = jnp.sum(jnp.where(hit0, before, 0.0), axis=0, keepdims=True).astype(jnp.int32)
    r1 = jnp.sum(jnp.where(hit1, before, 0.0), axis=0, keepdims=True).astype(jnp.int32)
    carry_ref[...] = carry_ref[...] + jnp.sum(oh, axis=1, keepdims=True)
    cout_ref[...] = carry_ref[...]

    zi = jnp.zeros((META_ROWS - 4, tm), jnp.int32)
    meta_ref[...] = jnp.concatenate([e0, e1, r0, r1, zi], axis=0)
    wt = jnp.concatenate([w0, w1, jnp.zeros((LANES - 2, tm), F32)], axis=0)
    wcol_ref[...] = jnp.transpose(wt)


def _post(mg, x2d, prm, tm, row0, total_rows, carry_in, prev=None, zero_tail=False):
    m = x2d.shape[0]
    assert m % tm == 0 and row0 % tm == 0 and total_rows % tm == 0
    off = row0 // tm
    nsteps = m // tm
    last = nsteps - 1
    ut = jnp.asarray(np.triu(np.ones((tm, tm), np.float32), 1), dtype=BF16)
    full = lambda shape: pl.BlockSpec(shape, lambda i: (0,) * len(shape))
    in_specs = [
        pl.BlockSpec((tm, D_MODEL), lambda i: (jnp.minimum(i, last), 0)),
        pl.BlockSpec((tm, D_MODEL), lambda i: (jnp.minimum(i, last), 0)),
        full((1, D_MODEL)), full((1, D_MODEL)),
        full((D_MODEL, D_MODEL)),
        full((1, D_MODEL)), full((1, D_MODEL)),
        full((ROUTER_ROWS, D_MODEL)), full((ROUTER_ROWS, D_MODEL)), full((ROUTER_ROWS, LANES)),
        full((tm, tm)),
        full((N_EXPERTS, LANES)),
    ]
    args = [mg, x2d, prm["ln_emb_g"], prm["ln_emb_b"], prm["w_out"], prm["ln1_g"], prm["ln1_b"],
            prm["wr_hi"], prm["wr_lo"], prm["r_bias"], ut, carry_in]
    aliases = {}
    if prev is not None:
        for k, buf in enumerate(prev):
            in_specs.append(pl.BlockSpec(memory_space=pl.ANY))
            aliases[len(args)] = k
            args.append(buf)
    out_shape = [
        jax.ShapeDtypeStruct((total_rows, D_MODEL), F32),
        jax.ShapeDtypeStruct((META_ROWS, total_rows), jnp.int32),
        jax.ShapeDtypeStruct((total_rows, LANES), F32),
        jax.ShapeDtypeStruct((N_EXPERTS, LANES), F32),
    ]
    out_specs = [
        pl.BlockSpec((tm, D_MODEL), lambda i: (i + off, 0)),
        pl.BlockSpec((META_ROWS, tm), lambda i: (0, i + off)),
        pl.BlockSpec((tm, LANES), lambda i: (i + off, 0)),
        full((N_EXPERTS, LANES)),
    ]
    if prev is not None:
        assert len(prev) == 3
    return pl.pallas_call(
        functools.partial(_post_body, nsteps),
        grid=(nsteps + int(zero_tail),),
        in_specs=in_specs,
        out_specs=out_specs,
        out_shape=out_shape,
        input_output_aliases=aliases,
        scratch_shapes=[pltpu.VMEM((N_EXPERTS, LANES), F32)],
        compiler_params=_cparams(("arbitrary",)),
        name="post_attn",
    )(*args)


def _prep_post_params(ln_emb_g, ln_emb_b, w_out, ln1_g, ln1_b, w_rg, b_rg, w_re, b_re):
    row = lambda v: v.reshape(1, -1)
    wr = jnp.concatenate([w_re.T, w_rg.T, jnp.zeros((ROUTER_ROWS - N_EXPERTS - N_GROUPS, D_MODEL), F32)], axis=0)
    wr_hi = wr.astype(BF16)
    wr_lo = (wr - wr_hi.astype(F32)).astype(BF16)
    rb = jnp.concatenate([b_re, b_rg, jnp.zeros((ROUTER_ROWS - N_EXPERTS - N_GROUPS,), F32)])
    return dict(ln_emb_g=row(ln_emb_g), ln_emb_b=row(ln_emb_b), w_out=w_out.astype(BF16), ln1_g=row(ln1_g),
                ln1_b=row(ln1_b), wr_hi=wr_hi, wr_lo=wr_lo,
                r_bias=jnp.broadcast_to(rb[:, None], (ROUTER_ROWS, LANES)))


MOE_ROWS = 256
PACKED = D_MODEL // 2
U32 = jnp.uint32


def _pack_bf16_pairs(x):
    half = x.shape[1] // 2
    hi = pltpu.bitcast(x[:, :half].astype(BF16).astype(F32), U32)
    lo = pltpu.bitcast(x[:, half:].astype(BF16).astype(F32), U32)
    return hi | (lo >> 16)


def _unpack_bf16_pairs(u):
    hi = pltpu.bitcast(u & jnp.uint32(0xFFFF0000), F32)
    lo = pltpu.bitcast(u << 16, F32)
    return hi, lo


def _moe_plan(counts, total_assign):
    nb_max = -(-total_assign // MOE_ROWS) + N_EXPERTS
    padded = (counts + MOE_ROWS - 1) // MOE_ROWS * MOE_ROWS
    pend = jnp.cumsum(padded)
    pstart = (pend - padded).astype(jnp.int32)
    block_start = jnp.arange(nb_max, dtype=jnp.int32) * MOE_ROWS
    n_ended = jnp.sum((pend[None, :] <= block_start[:, None]).astype(jnp.int32), axis=1)
    block_e = jnp.minimum(n_ended, N_EXPERTS - 1).astype(jnp.int32)
    n_used = (pend[-1] // MOE_ROWS).astype(jnp.int32).reshape(1)
    return pstart, block_e, n_used, nb_max


ROW_UNROLL = 8


def _slot_ids(meta, pstart):
    experts = meta[0:TOP_K]
    ranks = meta[TOP_K:2 * TOP_K]
    onehot = experts[..., None] == jnp.arange(N_EXPERTS, dtype=jnp.int32)
    return ranks + jnp.sum(jnp.where(onehot, pstart, 0), axis=-1)


def _for_row_groups(tm, fn):
    def group(g, c):
        t0 = pl.multiple_of(g * ROW_UNROLL, ROW_UNROLL)
        for r in range(ROW_UNROLL):
            for k in range(TOP_K):
                fn(t0 + r, k)
        return c

    lax.fori_loop(0, tm // ROW_UNROLL, group, 0)


def _dispatch_body(nsteps, s0_ref, s1_ref, h_ref, xs_in_ref, xs_ref, pk_ref, sems):
    del xs_in_ref
    i = pl.program_id(0)
    tm = h_ref.shape[0]
    slot_refs = (s0_ref, s1_ref)
    cur = i % 2
    pk_ref[cur] = _pack_bf16_pairs(h_ref[...])

    def send(t, k):
        pltpu.make_async_copy(pk_ref.at[cur, pl.ds(t, 1)], xs_ref.at[pl.ds(slot_refs[k][0, t], 1)],
                              sems.at[cur]).start(priority=k)

    def wait_buffer(buf):
        _for_row_groups(tm, lambda t, k: pltpu.make_async_copy(
            pk_ref.at[buf, pl.ds(t, 1)], xs_ref.at[pl.ds(0, 1)], sems.at[buf]).wait())

    _for_row_groups(tm, send)

    @pl.when(i > 0)
    def _():
        wait_buffer(1 - cur)

    @pl.when(i == nsteps - 1)
    def _():
        wait_buffer(cur)


def _dispatch(h1, slot_ids, nslots, tm, total):
    assert total % tm == 0 and total <= h1.shape[0] and tm % ROW_UNROLL == 0
    zeros = jnp.zeros((nslots, PACKED), U32)
    slot_spec = pl.BlockSpec((1, tm), lambda i: (0, i), memory_space=pltpu.SMEM)
    return pl.pallas_call(
        functools.partial(_dispatch_body, total // tm),
        grid=(total // tm,),
        in_specs=[slot_spec, slot_spec, pl.BlockSpec((tm, D_MODEL), lambda i: (i, 0)),
                  pl.BlockSpec(memory_space=pl.ANY)],
        out_specs=pl.BlockSpec(memory_space=pl.ANY),
        scratch_shapes=[pltpu.VMEM((2, tm, PACKED), U32), pltpu.SemaphoreType.DMA((2,))],
        out_shape=jax.ShapeDtypeStruct((nslots, PACKED), U32),
        input_output_aliases={3: 0},
        compiler_params=_cparams(("arbitrary",)),
        name="moe_dispatch",
    )(slot_ids[0:1], slot_ids[1:2], h1, zeros)


def _expert_body(be_ref, nu_ref, xs_ref, wg_ref, wu_ref, wd_ref, ys_ref, wgb_ref, wub_ref, wdb_ref):
    i = pl.program_id(0)
    changed = jnp.logical_or(i == 0, be_ref[i] != be_ref[jnp.maximum(i - 1, 0)])

    @pl.when(changed)
    def _():
        wgb_ref[...] = wg_ref[...].astype(BF16)
        wub_ref[...] = wu_ref[...].astype(BF16)
        wdb_ref[...] = wd_ref[...].astype(BF16)

    @pl.when(i < nu_ref[0])
    def _():
        hi, lo = _unpack_bf16_pairs(xs_ref[...])
        hi = hi.astype(BF16)
        lo = lo.astype(BF16)
        half = PACKED

        def proj(w_ref):
            return (jnp.dot(hi, w_ref[:half, :], preferred_element_type=F32)
                    + jnp.dot(lo, w_ref[half:, :], preferred_element_type=F32))

        g = proj(wgb_ref)
        hb = (g * _sigmoid(g)) * proj(wub_ref)
        y = jnp.dot(hb.astype(BF16), wdb_ref[...], preferred_element_type=F32)
        ys_ref[...] = _pack_bf16_pairs(y)

    @pl.when(i >= nu_ref[0])
    def _():
        ys_ref[...] = jnp.zeros_like(ys_ref)


def _experts(xs, block_e, n_used, w_gate, w_up, w_down, nb_max):
    wmap = lambda i, be, nu: (be[i], 0, 0)
    return pl.pallas_call(
        _expert_body,
        grid_spec=pltpu.PrefetchScalarGridSpec(
            num_scalar_prefetch=2,
            grid=(nb_max,),
            in_specs=[
                pl.BlockSpec((MOE_ROWS, PACKED), lambda i, be, nu: (i, 0)),
                pl.BlockSpec((None, D_MODEL, D_EXPERT), wmap),
                pl.BlockSpec((None, D_MODEL, D_EXPERT), wmap),
                pl.BlockSpec((None, D_EXPERT, D_MODEL), wmap),
            ],
            out_specs=pl.BlockSpec((MOE_ROWS, PACKED), lambda i, be, nu: (i, 0)),
            scratch_shapes=[pltpu.VMEM((D_MODEL, D_EXPERT), BF16), pltpu.VMEM((D_MODEL, D_EXPERT), BF16),
                            pltpu.VMEM((D_EXPERT, D_MODEL), BF16)],
        ),
        out_shape=jax.ShapeDtypeStruct(xs.shape, U32),
        compiler_params=_cparams(("arbitrary",)),
        name="moe_experts",
    )(block_e, n_used, xs, w_gate, w_up, w_down)


def _combine_body(nsteps, s0_ref, s1_ref, n0_ref, n1_ref, h_ref, w_ref, g2_ref, b2_ref, ys_ref, o_ref, buf_ref, sems):
    i = pl.program_id(0)
    tm = h_ref.shape[0]
    cur = i % 2

    def fetch(slot_refs, buf):
        _for_row_groups(tm, lambda t, k: pltpu.make_async_copy(
            ys_ref.at[pl.ds(slot_refs[k][0, t], 1)], buf_ref.at[buf, k, pl.ds(t, 1)], sems.at[buf]).start(priority=k))

    @pl.when(i == 0)
    def _():
        fetch((s0_ref, s1_ref), cur)

    @pl.when(i + 1 < nsteps)
    def _():
        fetch((n0_ref, n1_ref), 1 - cur)

    _for_row_groups(tm, lambda t, k: pltpu.make_async_copy(
        ys_ref.at[pl.ds(0, 1)], buf_ref.at[cur, k, pl.ds(t, 1)], sems.at[cur]).wait())
    w = w_ref[...]
    hi0, lo0 = _unpack_bf16_pairs(buf_ref[cur, 0])
    hi1, lo1 = _unpack_bf16_pairs(buf_ref[cur, 1])
    w0 = w[:, 0:1]
    w1 = w[:, 1:2]
    f = jnp.concatenate([w0 * hi0 + w1 * hi1, w0 * lo0 + w1 * lo1], axis=1)
    o_ref[...] = _layer_norm_rows(ALPHA * h_ref[...] + f, g2_ref[...], b2_ref[...])


def _combine(h1, wcol, slot_ids, ys, ln2_g, ln2_b, tm, row0, nrows):
    assert nrows % tm == 0 and row0 % tm == 0 and tm % ROW_UNROLL == 0
    off = row0 // tm
    nsteps = nrows // tm
    slot_spec = pl.BlockSpec((1, tm), lambda i: (0, i + off), memory_space=pltpu.SMEM)
    next_spec = pl.BlockSpec((1, tm), lambda i: (0, jnp.minimum(i + 1, nsteps - 1) + off), memory_space=pltpu.SMEM)
    return pl.pallas_call(
        functools.partial(_combine_body, nsteps),
        grid=(nsteps,),
        in_specs=[
            slot_spec, slot_spec, next_spec, next_spec,
            pl.BlockSpec((tm, D_MODEL), lambda i: (i + off, 0)),
            pl.BlockSpec((tm, LANES), lambda i: (i + off, 0)),
            pl.BlockSpec((1, D_MODEL), lambda i: (0, 0)),
            pl.BlockSpec((1, D_MODEL), lambda i: (0, 0)),
            pl.BlockSpec(memory_space=pl.ANY),
        ],
        out_specs=pl.BlockSpec((tm, D_MODEL), lambda i: (i, 0)),
        scratch_shapes=[pltpu.VMEM((2, TOP_K, tm, PACKED), U32), pltpu.SemaphoreType.DMA((2,))],
        out_shape=jax.ShapeDtypeStruct((nrows, D_MODEL), F32),
        compiler_params=_cparams(("arbitrary",)),
        name="moe_combine",
    )(slot_ids[0:1], slot_ids[1:2], slot_ids[0:1], slot_ids[1:2], h1, wcol, ln2_g.reshape(1, -1),
      ln2_b.reshape(1, -1), ys)


PROJ_ROWS = 512
POST_ROWS = 512
DISPATCH_ROWS = 384
COMBINE_ROWS = 256


def kernel(x_prompt, x_sample, state_gla, cache_swa_k, cache_swa_v, meta_tokens, ln_emb_g, ln_emb_b, rel_bias, w_in,
           gk_up, gk_bias, gla_norm_g, sinks, w_out, ln1_g, ln1_b, w_router_group, b_router_group, w_router_expert,
           b_router_expert, w_gate, w_up, w_down, ln2_g, ln2_b):
    nbatch, seq, d = x_prompt.shape
    nseq = x_sample.shape[0]
    assert w_in.shape[0] == DEPTH == 1 and d == D_MODEL and x_sample.shape[1] == 1
    assert seq % BLK == 0 and nseq == BLK and meta_tokens.shape[0] == N_META
    nblk = seq // BLK
    n_prompt = nbatch * seq
    total = n_prompt + nseq
    kvw = ATT_KV_HEADS * HEAD_DIM

    xp = x_prompt.reshape(n_prompt, d)
    xs = x_sample.reshape(nseq, d)
    extra = jnp.concatenate([xs, jnp.zeros((BLK - N_META, d), xs.dtype), meta_tokens.astype(xs.dtype)], axis=0)
    w_cat = _prep_w_in(w_in[0])
    pp = _ln_proj(xp, ln_emb_g, ln_emb_b, w_cat, PROJ_ROWS)
    pe = _ln_proj(extra, ln_emb_g, ln_emb_b, w_cat, 2 * BLK)

    yg, gla_p = _gla_prompt(pp, pe, gk_up[0], gk_bias[0], gla_norm_g[0], nbatch, nblk)
    yg_s, gla_s = _gla_step(pe, gk_up[0], gk_bias[0], gla_norm_g[0], state_gla[0])
    mg = _swa_prompt(pp, pe, yg, rel_bias, sinks[0], nbatch, nblk)
    mg_s, k_s, v_s = _swa_step(pe, yg_s, cache_swa_k[0].reshape(nseq, WINDOW, kvw),
                               cache_swa_v[0].reshape(nseq, WINDOW, kvw), rel_bias, sinks[0])

    prm = _prep_post_params(ln_emb_g, ln_emb_b, w_out[0], ln1_g[0], ln1_b[0], w_router_group[0], b_router_group[0],
                            w_router_expert[0], b_router_expert[0])
    carry0 = jnp.zeros((N_EXPERTS, LANES), F32)
    rows_alloc = n_prompt + POST_ROWS
    h1, meta, wcol, carry1 = _post(mg, xp, prm, POST_ROWS, 0, rows_alloc, carry0, zero_tail=True)
    h1, meta, wcol, carry2 = _post(mg_s, xs, prm, nseq, n_prompt, rows_alloc, carry1, prev=(h1, meta, wcol))

    counts = carry2[:, 0].astype(jnp.int32)
    pstart, block_e, n_used, nb_max = _moe_plan(counts, TOP_K * total)
    slot_ids = _slot_ids(meta, pstart)
    xs_sorted = _dispatch(h1, slot_ids, nb_max * MOE_ROWS, DISPATCH_ROWS, total)
    ys = _experts(xs_sorted, block_e, n_used, w_gate[0], w_up[0], w_down[0], nb_max)
    y_p = _combine(h1, wcol, slot_ids, ys, ln2_g[0], ln2_b[0], COMBINE_ROWS, 0, n_prompt)
    y_s = _combine(h1, wcol, slot_ids, ys, ln2_g[0], ln2_b[0], nseq, n_prompt, nseq)

    kv_shape = (1, nbatch, WINDOW, ATT_KV_HEADS, HEAD_DIM)
    k_p = pp["ka"].reshape(nbatch, seq, kvw)[:, seq - WINDOW:].reshape(kv_shape)
    v_p = pp["va"].reshape(nbatch, seq, kvw)[:, seq - WINDOW:].reshape(kv_shape)
    return (y_p.reshape(nbatch, seq, d), y_s.reshape(nseq, 1, d), gla_p[None], k_p, v_p, gla_s[None],
            k_s.reshape(cache_swa_k.shape), v_s.reshape(cache_swa_v.shape))
```

```python
import functools
import math

import jax
import jax.numpy as jnp
import numpy as np
from jax import lax
from jax.experimental import pallas as pl
from jax.experimental.pallas import tpu as pltpu

F32 = jnp.float32
BF16 = jnp.bfloat16

D_MODEL = 1024
N_META = 16
LN_EPS = 1e-5
GLA_HEADS = 4
GLA_DK = 128
GLA_DV = 256
GLA_RANK = 16
GLA_TAU = 16.0
HEAD_DIM = 64
ATT_HEADS = 16
ATT_KV_HEADS = 4
GQA_GROUP = 4
WINDOW = 128
REL_BUCKETS = 32
REL_MAX_DIST = 128
N_GROUPS = 4
EXPERTS_PER_GROUP = 8
N_EXPERTS = 32
TOP_K = 2
D_EXPERT = 512
DEPTH = 1
ALPHA = (2.0 * DEPTH) ** 0.25

LANES = 128
BLK = 128
VMEM_LIMIT = 56 * 1024 * 1024


def _cparams(sem):
    return pltpu.CompilerParams(dimension_semantics=sem, vmem_limit_bytes=VMEM_LIMIT)


def _layer_norm_rows(x, g, b):
    mu = jnp.mean(x, axis=-1, keepdims=True)
    xc = x - mu
    var = jnp.mean(xc * xc, axis=-1, keepdims=True)
    return xc * lax.rsqrt(var + LN_EPS) * g + b


_PROJ_OUTS = (
    ("qg", GLA_HEADS * GLA_DK, BF16, GLA_DK ** -0.5),
    ("kg", GLA_HEADS * GLA_DK, BF16, None),
    ("vg", GLA_HEADS * GLA_DV, BF16, None),
    ("rg", GLA_HEADS * GLA_DV, BF16, None),
    ("qa", ATT_HEADS * HEAD_DIM, BF16, HEAD_DIM ** -0.5),
    ("ka", ATT_KV_HEADS * HEAD_DIM, F32, None),
    ("va", ATT_KV_HEADS * HEAD_DIM, F32, None),
    ("ga", D_MODEL, BF16, None),
    ("gb", D_MODEL, BF16, None),
    ("lr", LANES, F32, None),
)
_PROJ_W = sum(w for _, w, _, _ in _PROJ_OUTS)


def _prep_w_in(w_in):
    sizes = (512, 512, 1024, 1024, GLA_RANK, 1024, 256, 256, 1024, 1024)
    offs = np.cumsum((0,) + sizes)
    a = w_in[:, : offs[4]]
    lr = w_in[:, offs[4]: offs[5]]
    b = w_in[:, offs[5]:]
    pad = jnp.zeros((w_in.shape[0], LANES - GLA_RANK), w_in.dtype)
    return jnp.concatenate([a, b, lr, pad], axis=1).astype(BF16)


def _ln_proj_body(x_ref, g_ref, b_ref, w_ref, *out_refs):
    xn = _layer_norm_rows(x_ref[...], g_ref[...], b_ref[...]).astype(BF16)
    c0 = 0
    for (_, width, dtype, scale), o_ref in zip(_PROJ_OUTS, out_refs):
        acc = jnp.dot(xn, w_ref[:, c0:c0 + width], preferred_element_type=F32)
        if scale is not None:
            acc = acc * scale
        o_ref[...] = acc.astype(dtype)
        c0 += width


def _ln_proj(x2d, ln_g, ln_b, w_cat, tm):
    m = x2d.shape[0]
    assert m % tm == 0
    out_shape = [jax.ShapeDtypeStruct((m, w), dt) for _, w, dt, _ in _PROJ_OUTS]
    out_specs = [pl.BlockSpec((tm, w), lambda i: (i, 0)) for _, w, _, _ in _PROJ_OUTS]
    outs = pl.pallas_call(
        _ln_proj_body,
        grid=(m // tm,),
        in_specs=[
            pl.BlockSpec((tm, D_MODEL), lambda i: (i, 0)),
            pl.BlockSpec((1, D_MODEL), lambda i: (0, 0)),
            pl.BlockSpec((1, D_MODEL), lambda i: (0, 0)),
            pl.BlockSpec((D_MODEL, _PROJ_W), lambda i: (0, 0), pipeline_mode=pl.Buffered(1)),
        ],
        out_specs=out_specs,
        out_shape=out_shape,
        compiler_params=_cparams(("arbitrary",)),
        name="ln_proj",
    )(x2d, ln_g.reshape(1, -1), ln_b.reshape(1, -1), w_cat)
    return dict(zip([n for n, _, _, _ in _PROJ_OUTS], outs))


_GLA_LEVELS = tuple(2 ** i for i in range(int(math.log2(BLK))))
GLA_SAFE_EXPONENT = 60.0


def _log_sigmoid(x):
    return jnp.minimum(x, 0.0) - jnp.log(1.0 + jnp.exp(-jnp.abs(x)))


def _sigmoid(x):
    return 1.0 / (1.0 + jnp.exp(-x))


def _split_dot(a01, x):
    hi = x.astype(BF16)
    lo = (x - hi.astype(F32)).astype(BF16)
    n = x.shape[1]
    both = jnp.dot(a01, jnp.concatenate([hi, lo], axis=1), preferred_element_type=F32)
    return both[:, :n] + both[:, n:]


def _gla_anchor_exponent(b, la, s, row):
    if s == 1:
        return jnp.where(row % 2 == 1, la, 0.0)
    if s == 2:
        la_dn = pltpu.roll(la, 1, axis=0)
        la_up = pltpu.roll(la, BLK - 1, axis=0)
        r = row % 4
        return jnp.where(r == 0, la_up, jnp.where(r == 1, 0.0, jnp.where(r == 2, la, la + la_dn)))
    nb = BLK // (2 * s)
    b3 = b.reshape(nb, 2 * s, b.shape[-1])
    anchor = jnp.broadcast_to(b3[:, s - 1:s, :], b3.shape).reshape(b.shape)
    return -jnp.abs(b - anchor)


def _gla_body(nblk, qm, km, vm, rm, lrm, gam, qp, kp, vp, rp, lrp, gap, gkup_ref, gkb_ref, gn_ref, tri_ref,
              y_ref, s_out_ref, s_ref):
    c = pl.program_id(1)
    is_meta = c == 0

    @pl.when(is_meta)
    def _():
        s_ref[...] = jnp.zeros_like(s_ref)

    def pick(m_ref, p_ref):
        return jnp.where(is_meta, m_ref[...], p_ref[...])

    row = lax.broadcasted_iota(jnp.int32, (BLK, GLA_DK), 0)
    col_t = lax.broadcasted_iota(jnp.int32, (BLK, BLK), 1)
    row_t = lax.broadcasted_iota(jnp.int32, (BLK, BLK), 0)
    live = jnp.logical_or(jnp.logical_not(is_meta), row >= BLK - N_META)
    tri = tri_ref[...]
    q_all, k_all, v_all, r_all = pick(qm, qp), pick(km, kp), pick(vm, vp), pick(rm, rp)
    ga_all = pick(gam, gap)
    lr = pick(lrm, lrp).astype(BF16)
    x_all = jnp.dot(lr, gkup_ref[...], preferred_element_type=F32) + gkb_ref[...]
    la_all = _log_sigmoid(x_all) * (1.0 / GLA_TAU)
    nt = (((1,), (1,)), ((), ()))
    mid = BLK // 2 - 1

    def head(h, single_anchor):
        dk = slice(h * GLA_DK, (h + 1) * GLA_DK)
        dv = slice(h * GLA_DV, (h + 1) * GLA_DV)
        la = jnp.where(live, la_all[:, dk], 0.0)
        q = q_all[:, dk].astype(F32)
        k = jnp.where(live, k_all[:, dk].astype(F32), 0.0)
        v = v_all[:, dv]
        b = _split_dot(tri, la)
        b_last = b[BLK - 1:BLK, :]
        s_old = s_ref[h]
        if single_anchor:
            b_mid = b[mid:mid + 1, :]
            qe = q * jnp.exp(b - b_mid)
            ke = k * jnp.exp(b_mid - b)
            a = jnp.where(row_t >= col_t,
                          lax.dot_general(qe.astype(BF16), ke.astype(BF16), nt, preferred_element_type=F32), 0.0)
            qg = qe * jnp.exp(b_mid)
            kd = ke * jnp.exp(b_last - b_mid)
        else:
            a = jnp.where(row_t == col_t,
                          lax.dot_general(q.astype(BF16), k.astype(BF16), nt, preferred_element_type=F32), 0.0)
            for s in _GLA_LEVELS:
                e = jnp.exp(_gla_anchor_exponent(b, la, s, row))
                upper = (row // s) % 2 == 1
                q_s = jnp.where(upper, q * e, 0.0).astype(BF16)
                k_s = jnp.where(upper, 0.0, k * e).astype(BF16)
                p = lax.dot_general(q_s, k_s, nt, preferred_element_type=F32)
                a = a + jnp.where(row_t // (2 * s) == col_t // (2 * s), p, 0.0)
            qg = q * jnp.exp(b)
            kd = k * jnp.exp(b_last - b)
        o = jnp.dot(qg.astype(BF16), s_old.astype(BF16), preferred_element_type=F32)
        lhs = jnp.concatenate([jnp.transpose(kd).astype(BF16), a.astype(BF16)], axis=0)
        both = jnp.dot(lhs, v, preferred_element_type=F32)
        decay_col = jnp.transpose(jnp.broadcast_to(jnp.exp(b_last), (BLK, GLA_DK)))[:, :1]
        s_ref[h] = decay_col * s_old + both[:GLA_DK]
        o = o + both[GLA_DK:]
        o = o * lax.rsqrt(jnp.mean(o * o, axis=-1, keepdims=True) + LN_EPS) * gn_ref[...]
        r = r_all[:, dv].astype(F32)
        y = o * (r * _sigmoid(r)) * _sigmoid(ga_all[:, dv].astype(F32))
        y_ref[:, dv] = y.astype(y_ref.dtype)

    mild = jnp.max(jnp.abs(la_all)) * (BLK // 2) <= GLA_SAFE_EXPONENT

    @pl.when(mild)
    def _():
        for h in range(GLA_HEADS):
            head(h, True)

    @pl.when(jnp.logical_not(mild))
    def _():
        for h in range(GLA_HEADS):
            head(h, False)

    @pl.when(c == nblk)
    def _():
        s_out_ref[...] = s_ref[...]


def _tri_incl():
    i = np.arange(BLK)
    return jnp.asarray((i[None, :] <= i[:, None]).astype(np.float32), dtype=BF16)


def _gla_prompt(pp, pe, gk_up, gk_bias, gnorm, nbatch, nblk):
    names = ("qg", "kg", "vg", "rg", "lr", "ga")
    gkup = jnp.concatenate([gk_up, jnp.zeros((LANES - GLA_RANK, gk_up.shape[1]), gk_up.dtype)], axis=0).astype(BF16)
    m_specs = [pl.BlockSpec((BLK, pe[n].shape[1]), lambda b, c: (1, 0)) for n in names]
    p_specs = [pl.BlockSpec((BLK, pp[n].shape[1]), lambda b, c: (b * nblk + jnp.maximum(c - 1, 0), 0)) for n in names]
    w_specs = [
        pl.BlockSpec((LANES, GLA_HEADS * GLA_DK), lambda b, c: (0, 0)),
        pl.BlockSpec((1, GLA_HEADS * GLA_DK), lambda b, c: (0, 0)),
        pl.BlockSpec((1, GLA_DV), lambda b, c: (0, 0)),
        pl.BlockSpec((BLK, BLK), lambda b, c: (0, 0)),
    ]
    y, s_fin = pl.pallas_call(
        functools.partial(_gla_body, nblk),
        grid=(nbatch, nblk + 1),
        in_specs=m_specs + p_specs + w_specs,
        out_specs=[
            pl.BlockSpec((BLK, D_MODEL), lambda b, c: (b * nblk + jnp.maximum(c - 1, 0), 0)),
            pl.BlockSpec((None, GLA_HEADS, GLA_DK, GLA_DV), lambda b, c: (b, 0, 0, 0)),
        ],
        out_shape=[
            jax.ShapeDtypeStruct((nbatch * nblk * BLK, D_MODEL), BF16),
            jax.ShapeDtypeStruct((nbatch, GLA_HEADS, GLA_DK, GLA_DV), F32),
        ],
        scratch_shapes=[pltpu.VMEM((GLA_HEADS, GLA_DK, GLA_DV), F32)],
        compiler_params=_cparams(("arbitrary", "arbitrary")),
        name="gla_prompt",
    )(*[pe[n] for n in names], *[pp[n] for n in names], gkup, gk_bias.reshape(1, -1), gnorm.reshape(1, -1), _tri_incl())
    return y, s_fin


GLA_STEP_SEQS = 16


def _gla_step_body(q_ref, k_ref, v_ref, r_ref, lr_ref, ga_ref, gkup_ref, gkb_ref, gn_ref, s_in_ref,
                   y_ref, s_out_ref, at_ref, kt_ref, qt_ref):
    g = pl.program_id(0)
    nseq = q_ref.shape[0]

    @pl.when(g == 0)
    def _():
        x = jnp.dot(lr_ref[...].astype(BF16), gkup_ref[...], preferred_element_type=F32) + gkb_ref[...]
        a = jnp.exp(_log_sigmoid(x) * (1.0 / GLA_TAU))
        for h in range(GLA_HEADS):
            dk = slice(h * GLA_DK, (h + 1) * GLA_DK)
            at_ref[h] = jnp.transpose(a[:, dk])
            kt_ref[h] = jnp.transpose(k_ref[:, dk].astype(F32))
            qt_ref[h] = jnp.transpose(q_ref[:, dk].astype(F32))

    lane = lax.broadcasted_iota(jnp.int32, (GLA_DK, nseq), 1)
    ones = jnp.ones((nseq, GLA_DV), BF16)
    grp = pl.ds(pl.multiple_of(g * GLA_STEP_SEQS, GLA_STEP_SEQS), GLA_STEP_SEQS)
    r_grp = r_ref[grp, :].astype(F32)
    ga_grp = ga_ref[grp, :].astype(F32)
    for i in range(GLA_STEP_SEQS):
        n = g * GLA_STEP_SEQS + i
        sel = lane == n
        for h in range(GLA_HEADS):
            dv = slice(h * GLA_DV, (h + 1) * GLA_DV)
            a_sel = jnp.where(sel, at_ref[h], 0.0)
            k_sel = jnp.where(sel, kt_ref[h], 0.0).astype(BF16)
            q_sel = jnp.where(sel, qt_ref[h], 0.0).astype(BF16)
            decay = _split_dot_rhs(a_sel, ones)
            kv = jnp.dot(k_sel, v_ref[:, dv], preferred_element_type=F32)
            q_b = jnp.dot(q_sel, ones, preferred_element_type=F32)
            s_new = decay * s_in_ref[i, h] + kv
            s_out_ref[i, h] = s_new
            o = jnp.sum(q_b * s_new, axis=0, keepdims=True)
            o = o * lax.rsqrt(jnp.mean(o * o, axis=-1, keepdims=True) + LN_EPS) * gn_ref[...]
            r = r_grp[i:i + 1, dv]
            ga = ga_grp[i:i + 1, dv]
            y_ref[i:i + 1, dv] = (o * (r * _sigmoid(r)) * _sigmoid(ga)).astype(y_ref.dtype)


def _split_dot_rhs(x, b01):
    hi = x.astype(BF16)
    lo = (x - hi.astype(F32)).astype(BF16)
    return jnp.dot(hi, b01, preferred_element_type=F32) + jnp.dot(lo, b01, preferred_element_type=F32)


def _gla_step(pe, gk_up, gk_bias, gnorm, state):
    nseq = state.shape[0]
    assert nseq == BLK and nseq % GLA_STEP_SEQS == 0
    names = ("qg", "kg", "vg", "rg", "lr", "ga")
    gkup = jnp.concatenate([gk_up, jnp.zeros((LANES - GLA_RANK, gk_up.shape[1]), gk_up.dtype)], axis=0).astype(BF16)
    t_specs = [pl.BlockSpec((nseq, pe[n].shape[1]), lambda g: (0, 0)) for n in names]
    st_spec = pl.BlockSpec((GLA_STEP_SEQS, GLA_HEADS, GLA_DK, GLA_DV), lambda g: (g, 0, 0, 0))
    return pl.pallas_call(
        _gla_step_body,
        grid=(nseq // GLA_STEP_SEQS,),
        in_specs=t_specs + [
            pl.BlockSpec((LANES, GLA_HEADS * GLA_DK), lambda g: (0, 0)),
            pl.BlockSpec((1, GLA_HEADS * GLA_DK), lambda g: (0, 0)),
            pl.BlockSpec((1, GLA_DV), lambda g: (0, 0)),
            st_spec,
        ],
        out_specs=[pl.BlockSpec((GLA_STEP_SEQS, D_MODEL), lambda g: (g, 0)), st_spec],
        out_shape=[jax.ShapeDtypeStruct((nseq, D_MODEL), F32), jax.ShapeDtypeStruct(state.shape, F32)],
        scratch_shapes=[pltpu.VMEM((GLA_HEADS, GLA_DK, nseq), F32) for _ in range(3)],
        compiler_params=_cparams(("arbitrary",)),
        name="gla_step",
    )(*[pe[n] for n in names], gkup, gk_bias.reshape(1, -1), gnorm.reshape(1, -1), state)


HALF = LANES // 2


def _rel_bucket(dist):
    max_exact = REL_BUCKETS // 2
    d = jnp.maximum(dist, 0)
    large = max_exact + (jnp.log(jnp.maximum(d, 1).astype(F32) / max_exact)
                         / math.log(REL_MAX_DIST / max_exact) * (REL_BUCKETS - max_exact)).astype(jnp.int32)
    large = jnp.minimum(large, REL_BUCKETS - 1)
    return jnp.where(d < max_exact, d, large)


def _bias_lookup(rel_bias, dist):
    onehot = (_rel_bucket(dist)[..., None] == jnp.arange(REL_BUCKETS)).astype(F32)
    return jnp.einsum("...b,bh->h...", onehot, rel_bias.astype(F32), precision=lax.Precision.HIGHEST)


def _swa_bias_tables(rel_bias):
    q = jnp.arange(BLK)[:, None]
    c = jnp.arange(2 * BLK)[None, :]
    dist = BLK + q - c
    bias = _bias_lookup(rel_bias, dist)
    inside = (dist >= 0) & (dist < WINDOW)
    first = inside & (c >= BLK - N_META)
    neg = jnp.float32(-jnp.inf)
    return jnp.stack([jnp.where(first[None], bias, neg), jnp.where(inside[None], bias, neg)])


def _half_tiles(x):
    lane = lax.broadcasted_iota(jnp.int32, (x.shape[0], LANES), 1)
    low = lane < HALF
    out = []
    for t in range(2):
        tile = x[:, t * LANES:(t + 1) * LANES]
        swapped = pltpu.roll(tile, HALF, axis=1)
        zero = jnp.zeros_like(tile)
        even = (jnp.where(low, tile, zero).astype(BF16), jnp.where(low, zero, swapped).astype(BF16))
        odd = (jnp.where(low, swapped, zero).astype(BF16), jnp.where(low, zero, tile).astype(BF16))
        out += [even, odd]
    return out


def _dup_tiles(x):
    lane = lax.broadcasted_iota(jnp.int32, (x.shape[0], LANES), 1)
    low = lane < HALF
    out = []
    for t in range(2):
        tile = x[:, t * LANES:(t + 1) * LANES]
        swapped = pltpu.roll(tile, HALF, axis=1)
        out += [jnp.where(low, tile, swapped).astype(BF16), jnp.where(low, swapped, tile).astype(BF16)]
    return out


def _swa_body(nblk, q_ref, kc_ref, vc_ref, kp_ref, vp_ref, km_ref, vm_ref, gb_ref, yg_ref, tb_ref, sink_ref,
              o_ref, kw_ref, vw_ref):
    blk = pl.program_id(1)
    first = blk == 0

    @pl.when(blk == nblk - 1)
    def _():
        kw_ref[...] = kc_ref[...]
        vw_ref[...] = vc_ref[...]

    k_prev = jnp.where(first, km_ref[...], kp_ref[...])
    v_prev = jnp.where(first, vm_ref[...], vp_ref[...])
    k_tiles = _dup_tiles(jnp.concatenate([k_prev, kc_ref[...]], axis=0))
    v_tiles = _dup_tiles(jnp.concatenate([v_prev, vc_ref[...]], axis=0))
    variant = jnp.minimum(blk, 1)
    npair = GQA_GROUP // 2
    low = lax.broadcasted_iota(jnp.int32, (BLK, LANES), 1) < HALF
    seg = lax.broadcasted_iota(jnp.int32, (GQA_GROUP * BLK, 1), 0) // BLK
    nt = (((1,), (1,)), ((), ()))
    for j in range(ATT_KV_HEADS):
        tiles = [j * npair + pair for pair in range(npair)]
        q_t = [q_ref[:, t * LANES:(t + 1) * LANES] for t in tiles]
        zero = jnp.zeros_like(q_t[0])
        q_st = jnp.concatenate([jnp.where(low, q, zero) for q in q_t] + [jnp.where(low, zero, q) for q in q_t], axis=0)
        heads = [2 * t for t in tiles] + [2 * t + 1 for t in tiles]
        s = lax.dot_general(q_st, k_tiles[j], nt, preferred_element_type=F32)
        s = s + jnp.concatenate([tb_ref[variant, h] for h in heads], axis=0)
        sink = jnp.full((GQA_GROUP * BLK, 1), sink_ref[heads[0]], F32)
        for i in range(1, GQA_GROUP):
            sink = jnp.where(seg == i, sink_ref[heads[i]], sink)
        m = jnp.maximum(jnp.max(s, axis=-1, keepdims=True), sink)
        p = jnp.exp(s - m)
        inv = 1.0 / (jnp.sum(p, axis=-1, keepdims=True) + jnp.exp(sink - m))
        o = jnp.dot(p.astype(BF16), v_tiles[j], preferred_element_type=F32) * inv
        for pair, t in enumerate(tiles):
            cols = slice(t * LANES, (t + 1) * LANES)
            gate = _sigmoid(gb_ref[:, cols].astype(F32))
            even = o[pair * BLK:(pair + 1) * BLK]
            odd = o[(npair + pair) * BLK:(npair + pair + 1) * BLK]
            o_ref[:, cols] = (gate * jnp.where(low, even, odd) + yg_ref[:, cols].astype(F32)).astype(o_ref.dtype)


def _swa_prompt(pp, pe, yg, rel_bias, sinks, nbatch, nblk):
    tb = _swa_bias_tables(rel_bias)
    kvw = ATT_KV_HEADS * HEAD_DIM
    cur = lambda b, c: (b * nblk + c, 0)
    prev = lambda b, c: (b * nblk + jnp.maximum(c - 1, 0), 0)
    return pl.pallas_call(
        functools.partial(_swa_body, nblk),
        grid=(nbatch, nblk),
        in_specs=[
            pl.BlockSpec((BLK, D_MODEL), cur),
            pl.BlockSpec((BLK, kvw), cur), pl.BlockSpec((BLK, kvw), cur),
            pl.BlockSpec((BLK, kvw), prev), pl.BlockSpec((BLK, kvw), prev),
            pl.BlockSpec((BLK, kvw), lambda b, c: (1, 0)), pl.BlockSpec((BLK, kvw), lambda b, c: (1, 0)),
            pl.BlockSpec((BLK, D_MODEL), cur),
            pl.BlockSpec((BLK, D_MODEL), cur),
            pl.BlockSpec(tb.shape, lambda b, c: (0, 0, 0, 0)),
            pl.BlockSpec(memory_space=pltpu.SMEM),
        ],
        out_specs=[pl.BlockSpec((BLK, D_MODEL), cur),
                   pl.BlockSpec((None, BLK, kvw), lambda b, c: (b, 0, 0)),
                   pl.BlockSpec((None, BLK, kvw), lambda b, c: (b, 0, 0))],
        out_shape=[jax.ShapeDtypeStruct((nbatch * nblk * BLK, D_MODEL), BF16),
                   jax.ShapeDtypeStruct((nbatch, BLK, kvw), F32), jax.ShapeDtypeStruct((nbatch, BLK, kvw), F32)],
        compiler_params=_cparams(("arbitrary", "arbitrary")),
        name="swa_prompt",
    )(pp["qa"], pp["ka"], pp["va"], pp["ka"], pp["va"], pe["ka"], pe["va"], pp["gb"], yg, tb, sinks)


SWA_STEP_SEQS = 8
Q_TILES = ATT_HEADS // 2


def _swa_step_body(q_ref, kn_ref, vn_ref, ck_ref, cv_ref, gb_ref, yg_ref, tb_ref, sink_ref,
                   o_ref, ko_ref, vo_ref):
    row = lax.broadcasted_iota(jnp.int32, (WINDOW, ATT_KV_HEADS * HEAD_DIM), 0)
    trow = lax.broadcasted_iota(jnp.int32, (Q_TILES, LANES), 0)
    low = lax.broadcasted_iota(jnp.int32, (Q_TILES, LANES), 1) < HALF
    nt = (((1,), (1,)), ((), ()))
    for i in range(SWA_STEP_SEQS):
        k_win = jnp.where(row == WINDOW - 1, kn_ref[i:i + 1, :], pltpu.roll(ck_ref[i], WINDOW - 1, axis=0))
        v_win = jnp.where(row == WINDOW - 1, vn_ref[i:i + 1, :], pltpu.roll(cv_ref[i], WINDOW - 1, axis=0))
        ko_ref[i] = k_win
        vo_ref[i] = v_win
        k_tiles = _half_tiles(k_win)
        v_tiles = _half_tiles(v_win)
        q8 = q_ref[i].astype(BF16)
        s = [jnp.zeros((Q_TILES, WINDOW), F32), jnp.zeros((Q_TILES, WINDOW), F32)]
        for j in range(ATT_KV_HEADS):
            mine = trow // (GQA_GROUP // 2) == j
            for half in range(2):
                sj = lax.dot_general(q8, k_tiles[j][half], nt, preferred_element_type=F32)
                s[half] = jnp.where(mine, sj, s[half])
        p, inv = [], []
        for half in range(2):
            sh = s[half] + tb_ref[half]
            sink = sink_ref[half]
            m = jnp.maximum(jnp.max(sh, axis=-1, keepdims=True), sink)
            ph = jnp.exp(sh - m)
            inv.append(1.0 / (jnp.sum(ph, axis=-1, keepdims=True) + jnp.exp(sink - m)))
            p.append(ph.astype(BF16))
        o = jnp.zeros((Q_TILES, LANES), F32)
        for j in range(ATT_KV_HEADS):
            mine = trow // (GQA_GROUP // 2) == j
            pv = (jnp.dot(p[0], v_tiles[j][0], preferred_element_type=F32)
                  + jnp.dot(p[1], v_tiles[j][1], preferred_element_type=F32))
            o = jnp.where(mine, pv, o)
        o = o * jnp.where(low, inv[0], inv[1])
        o_ref[i] = _sigmoid(gb_ref[i]) * o + yg_ref[i]


def _swa_step(pe, yg_s, cache_k, cache_v, rel_bias, sinks):
    nseq = cache_k.shape[0]
    kvw = ATT_KV_HEADS * HEAD_DIM
    as_tiles = lambda x: x[:nseq].astype(F32).reshape(nseq, Q_TILES, LANES)
    dist = (WINDOW - 1) - jnp.arange(WINDOW)
    bias = _bias_lookup(rel_bias, dist)
    tb = jnp.stack([bias[0::2], bias[1::2]])
    sk = jnp.broadcast_to(jnp.stack([sinks[0::2], sinks[1::2]])[:, :, None], (2, Q_TILES, 1)).astype(F32)
    g = SWA_STEP_SEQS
    tile_spec = pl.BlockSpec((g, Q_TILES, LANES), lambda s: (s, 0, 0))
    win_spec = pl.BlockSpec((g, WINDOW, kvw), lambda s: (s, 0, 0))
    new_spec = pl.BlockSpec((g, kvw), lambda s: (s, 0))
    o, ko, vo = pl.pallas_call(
        _swa_step_body,
        grid=(nseq // g,),
        in_specs=[tile_spec, new_spec, new_spec, win_spec, win_spec, tile_spec, tile_spec,
                  pl.BlockSpec(tb.shape, lambda s: (0, 0, 0)), pl.BlockSpec(sk.shape, lambda s: (0, 0, 0))],
        out_specs=[tile_spec, win_spec, win_spec],
        out_shape=[jax.ShapeDtypeStruct((nseq, Q_TILES, LANES), F32),
                   jax.ShapeDtypeStruct(cache_k.shape, F32), jax.ShapeDtypeStruct(cache_v.shape, F32)],
        compiler_params=_cparams(("arbitrary",)),
        name="swa_step",
    )(as_tiles(pe["qa"]), pe["ka"], pe["va"], cache_k, cache_v, as_tiles(pe["gb"]),
      yg_s.reshape(nseq, Q_TILES, LANES), tb, sk)
    return o.reshape(nseq, D_MODEL), ko, vo


ROUTER_ROWS = 40
META_ROWS = 8


def _split3_nt(a_hi, a_lo, x):
    nt = (((1,), (1,)), ((), ()))
    x_hi = x.astype(BF16)
    x_lo = (x - x_hi.astype(F32)).astype(BF16)
    return (lax.dot_general(a_hi, x_hi, nt, preferred_element_type=F32)
            + lax.dot_general(a_hi, x_lo, nt, preferred_element_type=F32)
            + lax.dot_general(a_lo, x_hi, nt, preferred_element_type=F32))


def _first_argmax_rows(v, ridx, nrows):
    vmax = jnp.max(v, axis=0, keepdims=True)
    idx = jnp.min(jnp.where(v == vmax, ridx, nrows), axis=0, keepdims=True)
    return vmax, idx


def _post_body(nsteps, *refs):
    h1_ref, meta_ref, wcol_ref = refs[-5:-2]
    i = pl.program_id(0)

    @pl.when(i < nsteps)
    def _():
        _post_tile(i, *refs)

    @pl.when(i >= nsteps)
    def _():
        h1_ref[...] = jnp.zeros_like(h1_ref)
        meta_ref[...] = jnp.zeros_like(meta_ref)
        wcol_ref[...] = jnp.zeros_like(wcol_ref)


def _post_tile(i, mg_ref, x_ref, lng_ref, lnb_ref, wo_ref, g1_ref, b1_ref, wrh_ref, wrl_ref, rb_ref, ut_ref,
               cin_ref, *rest):
    h1_ref, meta_ref, wcol_ref, cout_ref, carry_ref = rest[-5:]

    @pl.when(i == 0)
    def _():
        carry_ref[...] = cin_ref[...]

    tm = x_ref.shape[0]
    h = _layer_norm_rows(x_ref[...], lng_ref[...], lnb_ref[...])
    acc = jnp.dot(mg_ref[...].astype(BF16), wo_ref[...], preferred_element_type=F32)
    h1 = _layer_norm_rows(ALPHA * h + acc, g1_ref[...], b1_ref[...])
    h1_ref[...] = h1

    lt = _split3_nt(wrh_ref[...], wrl_ref[...], h1) + rb_ref[:, :1]
    ridx = lax.broadcasted_iota(jnp.int32, (EXPERTS_PER_GROUP, tm), 0)
    neg = jnp.float32(-jnp.inf)
    g_log = jnp.where(ridx < N_GROUPS, lt[N_EXPERTS:N_EXPERTS + EXPERTS_PER_GROUP], neg)
    g_max, grp = _first_argmax_rows(g_log, ridx, EXPERTS_PER_GROUP)
    p_grp = 1.0 / jnp.sum(jnp.exp(g_log - g_max), axis=0, keepdims=True)
    e_in = lt[0:EXPERTS_PER_GROUP]
    for gi in range(1, N_GROUPS):
        e_in = jnp.where(grp == gi, lt[gi * EXPERTS_PER_GROUP:(gi + 1) * EXPERTS_PER_GROUP], e_in)
    v0, i0 = _first_argmax_rows(e_in, ridx, EXPERTS_PER_GROUP)
    v1, i1 = _first_argmax_rows(jnp.where(ridx == i0, neg, e_in), ridx, EXPERTS_PER_GROUP)
    t = jnp.exp(v1 - v0)
    w0 = p_grp / (1.0 + t)
    w1 = p_grp * t / (1.0 + t)
    e0 = grp * EXPERTS_PER_GROUP + i0
    e1 = grp * EXPERTS_PER_GROUP + i1

    eidx = lax.broadcasted_iota(jnp.int32, (N_EXPERTS, tm), 0)
    hit0 = eidx == e0
    hit1 = eidx == e1
    oh = jnp.where(jnp.logical_or(hit0, hit1), 1.0, 0.0)
    before = jnp.dot(oh.astype(BF16), ut_ref[...], preferred_element_type=F32) + carry_ref[:, :1]
    r0 = jnp.sum(jnp.where(hit0, before, 0.0), axis=0, keepdims=True).astype(jnp.int32)
    r1 = jnp.sum(jnp.where(hit1, before, 0.0), axis=0, keepdims=True).astype(jnp.int32)
    carry_ref[...] = carry_ref[...] + jnp.sum(oh, axis=1, keepdims=True)
    cout_ref[...] = carry_ref[...]

    zi = jnp.zeros((META_ROWS - 4, tm), jnp.int32)
    meta_ref[...] = jnp.concatenate([e0, e1, r0, r1, zi], axis=0)
    wt = jnp.concatenate([w0, w1, jnp.zeros((LANES - 2, tm), F32)], axis=0)
    wcol_ref[...] = jnp.transpose(wt)


def _post(mg, x2d, prm, tm, row0, total_rows, carry_in, prev=None, zero_tail=False):
    m = x2d.shape[0]
    assert m % tm == 0 and row0 % tm == 0 and total_rows % tm == 0
    off = row0 // tm
    nsteps = m // tm
    last = nsteps - 1
    ut = jnp.asarray(np.triu(np.ones((tm, tm), np.float32), 1), dtype=BF16)
    full = lambda shape: pl.BlockSpec(shape, lambda i: (0,) * len(shape))
    in_specs = [
        pl.BlockSpec((tm, D_MODEL), lambda i: (jnp.minimum(i, last), 0)),
        pl.BlockSpec((tm, D_MODEL), lambda i: (jnp.minimum(i, last), 0)),
        full((1, D_MODEL)), full((1, D_MODEL)),
        full((D_MODEL, D_MODEL)),
        full((1, D_MODEL)), full((1, D_MODEL)),
        full((ROUTER_ROWS, D_MODEL)), full((ROUTER_ROWS, D_MODEL)), full((ROUTER_ROWS, LANES)),
        full((tm, tm)),
        full((N_EXPERTS, LANES)),
    ]
    args = [mg, x2d, prm["ln_emb_g"], prm["ln_emb_b"], prm["w_out"], prm["ln1_g"], prm["ln1_b"],
            prm["wr_hi"], prm["wr_lo"], prm["r_bias"], ut, carry_in]
    aliases = {}
    if prev is not None:
        for k, buf in enumerate(prev):
            in_specs.append(pl.BlockSpec(memory_space=pl.ANY))
            aliases[len(args)] = k
            args.append(buf)
    out_shape = [
        jax.ShapeDtypeStruct((total_rows, D_MODEL), F32),
        jax.ShapeDtypeStruct((META_ROWS, total_rows), jnp.int32),
        jax.ShapeDtypeStruct((total_rows, LANES), F32),
        jax.ShapeDtypeStruct((N_EXPERTS, LANES), F32),
    ]
    out_specs = [
        pl.BlockSpec((tm, D_MODEL), lambda i: (i + off, 0)),
        pl.BlockSpec((META_ROWS, tm), lambda i: (0, i + off)),
        pl.BlockSpec((tm, LANES), lambda i: (i + off, 0)),
        full((N_EXPERTS, LANES)),
    ]
    if prev is not None:
        assert len(prev) == 3
    return pl.pallas_call(
        functools.partial(_post_body, nsteps),
        grid=(nsteps + int(zero_tail),),
        in_specs=in_specs,
        out_specs=out_specs,
        out_shape=out_shape,
        input_output_aliases=aliases,
        scratch_shapes=[pltpu.VMEM((N_EXPERTS, LANES), F32)],
        compiler_params=_cparams(("arbitrary",)),
        name="post_attn",
    )(*args)


def _prep_post_params(ln_emb_g, ln_emb_b, w_out, ln1_g, ln1_b, w_rg, b_rg, w_re, b_re):
    row = lambda v: v.reshape(1, -1)
    wr = jnp.concatenate([w_re.T, w_rg.T, jnp.zeros((ROUTER_ROWS - N_EXPERTS - N_GROUPS, D_MODEL), F32)], axis=0)
    wr_hi = wr.astype(BF16)
    wr_lo = (wr - wr_hi.astype(F32)).astype(BF16)
    rb = jnp.concatenate([b_re, b_rg, jnp.zeros((ROUTER_ROWS - N_EXPERTS - N_GROUPS,), F32)])
    return dict(ln_emb_g=row(ln_emb_g), ln_emb_b=row(ln_emb_b), w_out=w_out.astype(BF16), ln1_g=row(ln1_g),
                ln1_b=row(ln1_b), wr_hi=wr_hi, wr_lo=wr_lo,
                r_bias=jnp.broadcast_to(rb[:, None], (ROUTER_ROWS, LANES)))


MOE_ROWS = 256
PACKED = D_MODEL // 2
U32 = jnp.uint32


def _pack_bf16_pairs(x):
    half = x.shape[1] // 2
    hi = pltpu.bitcast(x[:, :half].astype(BF16).astype(F32), U32)
    lo = pltpu.bitcast(x[:, half:].astype(BF16).astype(F32), U32)
    return hi | (lo >> 16)


def _unpack_bf16_pairs(u):
    hi = pltpu.bitcast(u & jnp.uint32(0xFFFF0000), F32)
    lo = pltpu.bitcast(u << 16, F32)
    return hi, lo


def _moe_plan(counts, total_assign):
    nb_max = -(-total_assign // MOE_ROWS) + N_EXPERTS
    padded = (counts + MOE_ROWS - 1) // MOE_ROWS * MOE_ROWS
    pend = jnp.cumsum(padded)
    pstart = (pend - padded).astype(jnp.int32)
    block_start = jnp.arange(nb_max, dtype=jnp.int32) * MOE_ROWS
    n_ended = jnp.sum((pend[None, :] <= block_start[:, None]).astype(jnp.int32), axis=1)
    block_e = jnp.minimum(n_ended, N_EXPERTS - 1).astype(jnp.int32)
    n_used = (pend[-1] // MOE_ROWS).astype(jnp.int32).reshape(1)
    tail_start = jnp.where(padded > 0, pend - MOE_ROWS, -1)
    spare = pend[-1] + jnp.arange(N_EXPERTS, dtype=pend.dtype) * MOE_ROWS
    spare = jnp.where(spare < nb_max * MOE_ROWS, spare, -1)
    zero_blocks = jnp.concatenate([tail_start, spare]).astype(jnp.int32)
    return pstart, block_e, n_used, nb_max, zero_blocks


ROW_UNROLL = 8


def _slot_ids(meta, pstart):
    experts = meta[0:TOP_K]
    ranks = meta[TOP_K:2 * TOP_K]
    onehot = experts[..., None] == jnp.arange(N_EXPERTS, dtype=jnp.int32)
    return ranks + jnp.sum(jnp.where(onehot, pstart, 0), axis=-1)


def _for_row_groups(tm, fn):
    def group(g, c):
        t0 = pl.multiple_of(g * ROW_UNROLL, ROW_UNROLL)
        for r in range(ROW_UNROLL):
            for k in range(TOP_K):
                fn(t0 + r, k)
        return c

    lax.fori_loop(0, tm // ROW_UNROLL, group, 0)


def _dispatch_body(nsteps, tail_ref, s0_ref, s1_ref, h_ref, xs_ref, pk_ref, zero_ref, sems, zsem):
    i = pl.program_id(0)
    tm = h_ref.shape[0]
    slot_refs = (s0_ref, s1_ref)
    cur = i % 2

    @pl.when(i == 0)
    def _():
        zero_ref[...] = jnp.zeros_like(zero_ref)
        for e in range(2 * N_EXPERTS):
            @pl.when(tail_ref[e] >= 0)
            def _():
                pltpu.make_async_copy(zero_ref, xs_ref.at[pl.ds(pl.multiple_of(tail_ref[e], MOE_ROWS), MOE_ROWS)],
                                      zsem).start()
        for e in range(2 * N_EXPERTS):
            @pl.when(tail_ref[e] >= 0)
            def _():
                pltpu.make_async_copy(zero_ref, xs_ref.at[pl.ds(0, MOE_ROWS)], zsem).wait()

    pk_ref[cur] = _pack_bf16_pairs(h_ref[...])

    def send(t, k):
        pltpu.make_async_copy(pk_ref.at[cur, pl.ds(t, 1)], xs_ref.at[pl.ds(slot_refs[k][0, t], 1)],
                              sems.at[cur]).start(priority=k)

    def wait_buffer(buf):
        _for_row_groups(tm, lambda t, k: pltpu.make_async_copy(
            pk_ref.at[buf, pl.ds(t, 1)], xs_ref.at[pl.ds(0, 1)], sems.at[buf]).wait())

    _for_row_groups(tm, send)

    @pl.when(i > 0)
    def _():
        wait_buffer(1 - cur)

    @pl.when(i == nsteps - 1)
    def _():
        wait_buffer(cur)


def _dispatch(h1, slot_ids, tail_start, nslots, tm, total):
    assert total % tm == 0 and total <= h1.shape[0] and tm % ROW_UNROLL == 0
    slot_spec = pl.BlockSpec((1, tm), lambda i, tl: (0, i), memory_space=pltpu.SMEM)
    return pl.pallas_call(
        functools.partial(_dispatch_body, total // tm),
        grid_spec=pltpu.PrefetchScalarGridSpec(
            num_scalar_prefetch=1,
            grid=(total // tm,),
            in_specs=[slot_spec, slot_spec, pl.BlockSpec((tm, D_MODEL), lambda i, tl: (i, 0))],
            out_specs=pl.BlockSpec(memory_space=pl.ANY),
            scratch_shapes=[pltpu.VMEM((2, tm, PACKED), U32), pltpu.VMEM((MOE_ROWS, PACKED), U32),
                            pltpu.SemaphoreType.DMA((2,)), pltpu.SemaphoreType.DMA(())],
        ),
        out_shape=jax.ShapeDtypeStruct((nslots, PACKED), U32),
        compiler_params=_cparams(("arbitrary",)),
        name="moe_dispatch",
    )(tail_start, slot_ids[0:1], slot_ids[1:2], h1)


def _expert_schedule(block_e, n_used):
    nb = block_e.shape[0]
    idx = jnp.arange(nb, dtype=jnp.int32)
    first = (idx < n_used[0]) & ((idx == 0) | (block_e != jnp.roll(block_e, 1)))
    parity = (jnp.cumsum(first.astype(jnp.int32)) - 1) % 2
    pos = jnp.where(first, idx, nb)
    at_or_after = jnp.flip(lax.cummin(jnp.flip(pos)))
    nxt = jnp.concatenate([at_or_after[1:], jnp.full((1,), nb, jnp.int32)])
    nexte = jnp.where(nxt < nb, block_e[jnp.minimum(nxt, nb - 1)], -1)
    return first.astype(jnp.int32), nexte.astype(jnp.int32), parity.astype(jnp.int32)


def _expert_body(be_ref, nu_ref, first_ref, nexte_ref, par_ref, xs_ref, wg_hbm, wu_hbm, wd_hbm, ys_ref,
                 wgf_ref, wuf_ref, wdf_ref, wgb_ref, wub_ref, wdb_ref, sems):
    i = pl.program_id(0)
    hbm = (wg_hbm, wu_hbm, wd_hbm)
    stage = (wgf_ref, wuf_ref, wdf_ref)

    def weight_copies(e, buf):
        return [pltpu.make_async_copy(hbm[w].at[e], stage[w].at[buf], sems.at[buf, w]) for w in range(3)]

    @pl.when(first_ref[i] == 1)
    def _():
        buf = par_ref[i]

        @pl.when(i == 0)
        def _():
            for c in weight_copies(be_ref[0], buf):
                c.start()

        for c in weight_copies(be_ref[i], buf):
            c.wait()

        @pl.when(nexte_ref[i] >= 0)
        def _():
            for c in weight_copies(nexte_ref[i], 1 - buf):
                c.start()

        wgb_ref[...] = wgf_ref[buf].astype(BF16)
        wub_ref[...] = wuf_ref[buf].astype(BF16)
        wdb_ref[...] = wdf_ref[buf].astype(BF16)

    @pl.when(i < nu_ref[0])
    def _():
        hi, lo = _unpack_bf16_pairs(xs_ref[...])
        hi = hi.astype(BF16)
        lo = lo.astype(BF16)
        half = PACKED

        def proj(w_ref):
            return (jnp.dot(hi, w_ref[:half, :], preferred_element_type=F32)
                    + jnp.dot(lo, w_ref[half:, :], preferred_element_type=F32))

        g = proj(wgb_ref)
        hb = (g * _sigmoid(g)) * proj(wub_ref)
        y = jnp.dot(hb.astype(BF16), wdb_ref[...], preferred_element_type=F32)
        ys_ref[...] = _pack_bf16_pairs(y)

    @pl.when(i >= nu_ref[0])
    def _():
        ys_ref[...] = jnp.zeros_like(ys_ref)


def _experts(xs, block_e, n_used, w_gate, w_up, w_down, nb_max):
    first, nexte, parity = _expert_schedule(block_e, n_used)
    rows = lambda i, *_: (i, 0)
    used_rows = lambda i, be, nu, *_: (jnp.minimum(i, nu[0] - 1), 0)
    any_spec = pl.BlockSpec(memory_space=pl.ANY)
    return pl.pallas_call(
        _expert_body,
        grid_spec=pltpu.PrefetchScalarGridSpec(
            num_scalar_prefetch=5,
            grid=(nb_max,),
            in_specs=[pl.BlockSpec((MOE_ROWS, PACKED), used_rows), any_spec, any_spec, any_spec],
            out_specs=pl.BlockSpec((MOE_ROWS, PACKED), rows),
            scratch_shapes=[
                pltpu.VMEM((2, D_MODEL, D_EXPERT), F32), pltpu.VMEM((2, D_MODEL, D_EXPERT), F32),
                pltpu.VMEM((2, D_EXPERT, D_MODEL), F32),
                pltpu.VMEM((D_MODEL, D_EXPERT), BF16), pltpu.VMEM((D_MODEL, D_EXPERT), BF16),
                pltpu.VMEM((D_EXPERT, D_MODEL), BF16),
                pltpu.SemaphoreType.DMA((2, 3)),
            ],
        ),
        out_shape=jax.ShapeDtypeStruct(xs.shape, U32),
        compiler_params=_cparams(("arbitrary",)),
        name="moe_experts",
    )(block_e, n_used, first, nexte, parity, xs, w_gate, w_up, w_down)


def _combine_body(nsteps, s0_ref, s1_ref, n0_ref, n1_ref, h_ref, w_ref, g2_ref, b2_ref, ys_ref, o_ref, buf_ref, sems):
    i = pl.program_id(0)
    tm = h_ref.shape[0]
    cur = i % 2

    def fetch(slot_refs, buf):
        _for_row_groups(tm, lambda t, k: pltpu.make_async_copy(
            ys_ref.at[pl.ds(slot_refs[k][0, t], 1)], buf_ref.at[buf, k, pl.ds(t, 1)], sems.at[buf]).start(priority=k))

    @pl.when(i == 0)
    def _():
        fetch((s0_ref, s1_ref), cur)

    @pl.when(i + 1 < nsteps)
    def _():
        fetch((n0_ref, n1_ref), 1 - cur)

    _for_row_groups(tm, lambda t, k: pltpu.make_async_copy(
        ys_ref.at[pl.ds(0, 1)], buf_ref.at[cur, k, pl.ds(t, 1)], sems.at[cur]).wait())
    w = w_ref[...]
    hi0, lo0 = _unpack_bf16_pairs(buf_ref[cur, 0])
    hi1, lo1 = _unpack_bf16_pairs(buf_ref[cur, 1])
    w0 = w[:, 0:1]
    w1 = w[:, 1:2]
    f = jnp.concatenate([w0 * hi0 + w1 * hi1, w0 * lo0 + w1 * lo1], axis=1)
    o_ref[...] = _layer_norm_rows(ALPHA * h_ref[...] + f, g2_ref[...], b2_ref[...])


def _combine(h1, wcol, slot_ids, ys, ln2_g, ln2_b, tm, row0, nrows):
    assert nrows % tm == 0 and row0 % tm == 0 and tm % ROW_UNROLL == 0
    off = row0 // tm
    nsteps = nrows // tm
    slot_spec = pl.BlockSpec((1, tm), lambda i: (0, i + off), memory_space=pltpu.SMEM)
    next_spec = pl.BlockSpec((1, tm), lambda i: (0, jnp.minimum(i + 1, nsteps - 1) + off), memory_space=pltpu.SMEM)
    return pl.pallas_call(
        functools.partial(_combine_body, nsteps),
        grid=(nsteps,),
        in_specs=[
            slot_spec, slot_spec, next_spec, next_spec,
            pl.BlockSpec((tm, D_MODEL), lambda i: (i + off, 0)),
            pl.BlockSpec((tm, LANES), lambda i: (i + off, 0)),
            pl.BlockSpec((1, D_MODEL), lambda i: (0, 0)),
            pl.BlockSpec((1, D_MODEL), lambda i: (0, 0)),
            pl.BlockSpec(memory_space=pl.ANY),
        ],
        out_specs=pl.BlockSpec((tm, D_MODEL), lambda i: (i, 0)),
        scratch_shapes=[pltpu.VMEM((2, TOP_K, tm, PACKED), U32), pltpu.SemaphoreType.DMA((2,))],
        out_shape=jax.ShapeDtypeStruct((nrows, D_MODEL), F32),
        compiler_params=_cparams(("arbitrary",)),
        name="moe_combine",
    )(slot_ids[0:1], slot_ids[1:2], slot_ids[0:1], slot_ids[1:2], h1, wcol, ln2_g.reshape(1, -1),
      ln2_b.reshape(1, -1), ys)


PROJ_ROWS = 512
POST_ROWS = 512
DISPATCH_ROWS = 384
COMBINE_ROWS = 256


def kernel(x_prompt, x_sample, state_gla, cache_swa_k, cache_swa_v, meta_tokens, ln_emb_g, ln_emb_b, rel_bias, w_in,
           gk_up, gk_bias, gla_norm_g, sinks, w_out, ln1_g, ln1_b, w_router_group, b_router_group, w_router_expert,
           b_router_expert, w_gate, w_up, w_down, ln2_g, ln2_b):
    nbatch, seq, d = x_prompt.shape
    nseq = x_sample.shape[0]
    assert w_in.shape[0] == DEPTH == 1 and d == D_MODEL and x_sample.shape[1] == 1
    assert seq % BLK == 0 and nseq == BLK and meta_tokens.shape[0] == N_META
    nblk = seq // BLK
    n_prompt = nbatch * seq
    total = n_prompt + nseq
    kvw = ATT_KV_HEADS * HEAD_DIM

    xp = x_prompt.reshape(n_prompt, d)
    xs = x_sample.reshape(nseq, d)
    extra = jnp.concatenate([xs, jnp.zeros((BLK - N_META, d), xs.dtype), meta_tokens.astype(xs.dtype)], axis=0)
    w_cat = _prep_w_in(w_in[0])
    pp = _ln_proj(xp, ln_emb_g, ln_emb_b, w_cat, PROJ_ROWS)
    pe = _ln_proj(extra, ln_emb_g, ln_emb_b, w_cat, 2 * BLK)

    yg, gla_p = _gla_prompt(pp, pe, gk_up[0], gk_bias[0], gla_norm_g[0], nbatch, nblk)
    yg_s, gla_s = _gla_step(pe, gk_up[0], gk_bias[0], gla_norm_g[0], state_gla[0])
    mg, k_win, v_win = _swa_prompt(pp, pe, yg, rel_bias, sinks[0], nbatch, nblk)
    mg_s, k_s, v_s = _swa_step(pe, yg_s, cache_swa_k[0].reshape(nseq, WINDOW, kvw),
                               cache_swa_v[0].reshape(nseq, WINDOW, kvw), rel_bias, sinks[0])

    prm = _prep_post_params(ln_emb_g, ln_emb_b, w_out[0], ln1_g[0], ln1_b[0], w_router_group[0], b_router_group[0],
                            w_router_expert[0], b_router_expert[0])
    carry0 = jnp.zeros((N_EXPERTS, LANES), F32)
    rows_alloc = n_prompt + POST_ROWS
    h1, meta, wcol, carry1 = _post(mg, xp, prm, POST_ROWS, 0, rows_alloc, carry0, zero_tail=True)
    h1, meta, wcol, carry2 = _post(mg_s, xs, prm, nseq, n_prompt, rows_alloc, carry1, prev=(h1, meta, wcol))

    counts = carry2[:, 0].astype(jnp.int32)
    pstart, block_e, n_used, nb_max, tail_start = _moe_plan(counts, TOP_K * total)
    slot_ids = _slot_ids(meta, pstart)
    xs_sorted = _dispatch(h1, slot_ids, tail_start, nb_max * MOE_ROWS, DISPATCH_ROWS, total)
    ys = _experts(xs_sorted, block_e, n_used, w_gate[0], w_up[0], w_down[0], nb_max)
    y_p = _combine(h1, wcol, slot_ids, ys, ln2_g[0], ln2_b[0], COMBINE_ROWS, 0, n_prompt)
    y_s = _combine(h1, wcol, slot_ids, ys, ln2_g[0], ln2_b[0], nseq, n_prompt, nseq)

    kv_shape = (1, nbatch, WINDOW, ATT_KV_HEADS, HEAD_DIM)
    k_p = k_win.reshape(kv_shape)
    v_p = v_win.reshape(kv_shape)
    return (y_p.reshape(nbatch, seq, d), y_s.reshape(nseq, 1, d), gla_p[None], k_p, v_p, gla_s[None],
            k_s.reshape(cache_swa_k.shape), v_s.reshape(cache_swa_v.shape))
```

```python
import functools
import math

import jax
import jax.numpy as jnp
import numpy as np
from jax import lax
from jax.experimental import pallas as pl
from jax.experimental.pallas import tpu as pltpu

F32 = jnp.float32
BF16 = jnp.bfloat16

D_MODEL = 1024
N_META = 16
LN_EPS = 1e-5
GLA_HEADS = 4
GLA_DK = 128
GLA_DV = 256
GLA_RANK = 16
GLA_TAU = 16.0
HEAD_DIM = 64
ATT_HEADS = 16
ATT_KV_HEADS = 4
GQA_GROUP = 4
WINDOW = 128
REL_BUCKETS = 32
REL_MAX_DIST = 128
N_GROUPS = 4
EXPERTS_PER_GROUP = 8
N_EXPERTS = 32
TOP_K = 2
D_EXPERT = 512
DEPTH = 1
ALPHA = (2.0 * DEPTH) ** 0.25

LANES = 128
BLK = 128
VMEM_LIMIT = 56 * 1024 * 1024


def _cparams(sem):
    return pltpu.CompilerParams(dimension_semantics=sem, vmem_limit_bytes=VMEM_LIMIT)


def _layer_norm_rows(x, g, b):
    mu = jnp.mean(x, axis=-1, keepdims=True)
    xc = x - mu
    var = jnp.mean(xc * xc, axis=-1, keepdims=True)
    return xc * lax.rsqrt(var + LN_EPS) * g + b


_PROJ_OUTS = (
    ("qg", GLA_HEADS * GLA_DK, BF16, GLA_DK ** -0.5),
    ("kg", GLA_HEADS * GLA_DK, BF16, None),
    ("vg", GLA_HEADS * GLA_DV, BF16, None),
    ("rg", GLA_HEADS * GLA_DV, BF16, None),
    ("qa", ATT_HEADS * HEAD_DIM, BF16, HEAD_DIM ** -0.5),
    ("ka", ATT_KV_HEADS * HEAD_DIM, F32, None),
    ("va", ATT_KV_HEADS * HEAD_DIM, F32, None),
    ("ga", D_MODEL, BF16, None),
    ("gb", D_MODEL, BF16, None),
    ("lr", LANES, F32, None),
)
_PROJ_W = sum(w for _, w, _, _ in _PROJ_OUTS)


def _prep_w_in(w_in):
    sizes = (512, 512, 1024, 1024, GLA_RANK, 1024, 256, 256, 1024, 1024)
    offs = np.cumsum((0,) + sizes)
    a = w_in[:, : offs[4]]
    lr = w_in[:, offs[4]: offs[5]]
    b = w_in[:, offs[5]:]
    pad = jnp.zeros((w_in.shape[0], LANES - GLA_RANK), w_in.dtype)
    return jnp.concatenate([a, b, lr, pad], axis=1).astype(BF16)


def _ln_proj_body(x_ref, g_ref, b_ref, w_ref, *out_refs):
    xn = _layer_norm_rows(x_ref[...], g_ref[...], b_ref[...]).astype(BF16)
    c0 = 0
    for (_, width, dtype, scale), o_ref in zip(_PROJ_OUTS, out_refs):
        acc = jnp.dot(xn, w_ref[:, c0:c0 + width], preferred_element_type=F32)
        if scale is not None:
            acc = acc * scale
        o_ref[...] = acc.astype(dtype)
        c0 += width


def _ln_proj(x2d, ln_g, ln_b, w_cat, tm):
    m = x2d.shape[0]
    assert m % tm == 0
    out_shape = [jax.ShapeDtypeStruct((m, w), dt) for _, w, dt, _ in _PROJ_OUTS]
    out_specs = [pl.BlockSpec((tm, w), lambda i: (i, 0)) for _, w, _, _ in _PROJ_OUTS]
    outs = pl.pallas_call(
        _ln_proj_body,
        grid=(m // tm,),
        in_specs=[
            pl.BlockSpec((tm, D_MODEL), lambda i: (i, 0)),
            pl.BlockSpec((1, D_MODEL), lambda i: (0, 0)),
            pl.BlockSpec((1, D_MODEL), lambda i: (0, 0)),
            pl.BlockSpec((D_MODEL, _PROJ_W), lambda i: (0, 0), pipeline_mode=pl.Buffered(1)),
        ],
        out_specs=out_specs,
        out_shape=out_shape,
        compiler_params=_cparams(("arbitrary",)),
        name="ln_proj",
    )(x2d, ln_g.reshape(1, -1), ln_b.reshape(1, -1), w_cat)
    return dict(zip([n for n, _, _, _ in _PROJ_OUTS], outs))


_GLA_LEVELS = tuple(2 ** i for i in range(int(math.log2(BLK))))
GLA_SAFE_EXPONENT = 60.0


def _log_sigmoid(x):
    return jnp.minimum(x, 0.0) - jnp.log(1.0 + jnp.exp(-jnp.abs(x)))


def _sigmoid(x):
    return 1.0 / (1.0 + jnp.exp(-x))


def _split_dot(a01, x):
    hi = x.astype(BF16)
    lo = (x - hi.astype(F32)).astype(BF16)
    n = x.shape[1]
    both = jnp.dot(a01, jnp.concatenate([hi, lo], axis=1), preferred_element_type=F32)
    return both[:, :n] + both[:, n:]


def _gla_anchor_exponent(b, la, s, row):
    if s == 1:
        return jnp.where(row % 2 == 1, la, 0.0)
    if s == 2:
        la_dn = pltpu.roll(la, 1, axis=0)
        la_up = pltpu.roll(la, BLK - 1, axis=0)
        r = row % 4
        return jnp.where(r == 0, la_up, jnp.where(r == 1, 0.0, jnp.where(r == 2, la, la + la_dn)))
    nb = BLK // (2 * s)
    b3 = b.reshape(nb, 2 * s, b.shape[-1])
    anchor = jnp.broadcast_to(b3[:, s - 1:s, :], b3.shape).reshape(b.shape)
    return -jnp.abs(b - anchor)


def _gla_body(nblk, qm, km, vm, rm, lrm, gam, qp, kp, vp, rp, lrp, gap, gkup_ref, gkb_ref, gn_ref, tri_ref,
              y_ref, s_out_ref, s_ref):
    c = pl.program_id(1)
    is_meta = c == 0

    @pl.when(is_meta)
    def _():
        s_ref[...] = jnp.zeros_like(s_ref)

    def pick(m_ref, p_ref):
        return jnp.where(is_meta, m_ref[...], p_ref[...])

    row = lax.broadcasted_iota(jnp.int32, (BLK, GLA_DK), 0)
    col_t = lax.broadcasted_iota(jnp.int32, (BLK, BLK), 1)
    row_t = lax.broadcasted_iota(jnp.int32, (BLK, BLK), 0)
    live = jnp.logical_or(jnp.logical_not(is_meta), row >= BLK - N_META)
    tri = tri_ref[...]
    q_all, k_all, v_all, r_all = pick(qm, qp), pick(km, kp), pick(vm, vp), pick(rm, rp)
    ga_all = pick(gam, gap)
    lr = pick(lrm, lrp).astype(BF16)
    x_all = jnp.dot(lr, gkup_ref[...], preferred_element_type=F32) + gkb_ref[...]
    la_all = _log_sigmoid(x_all) * (1.0 / GLA_TAU)
    nt = (((1,), (1,)), ((), ()))
    mid = BLK // 2 - 1

    def head(h, single_anchor):
        dk = slice(h * GLA_DK, (h + 1) * GLA_DK)
        dv = slice(h * GLA_DV, (h + 1) * GLA_DV)
        la = jnp.where(live, la_all[:, dk], 0.0)
        q = q_all[:, dk].astype(F32)
        k = jnp.where(live, k_all[:, dk].astype(F32), 0.0)
        v = v_all[:, dv]
        b = _split_dot(tri, la)
        b_last = b[BLK - 1:BLK, :]
        s_old = s_ref[h]
        if single_anchor:
            b_mid = b[mid:mid + 1, :]
            qe = q * jnp.exp(b - b_mid)
            ke = k * jnp.exp(b_mid - b)
            a = jnp.where(row_t >= col_t,
                          lax.dot_general(qe.astype(BF16), ke.astype(BF16), nt, preferred_element_type=F32), 0.0)
            qg = qe * jnp.exp(b_mid)
            kd = ke * jnp.exp(b_last - b_mid)
        else:
            a = jnp.where(row_t == col_t,
                          lax.dot_general(q.astype(BF16), k.astype(BF16), nt, preferred_element_type=F32), 0.0)
            for s in _GLA_LEVELS:
                e = jnp.exp(_gla_anchor_exponent(b, la, s, row))
                upper = (row // s) % 2 == 1
                q_s = jnp.where(upper, q * e, 0.0).astype(BF16)
                k_s = jnp.where(upper, 0.0, k * e).astype(BF16)
                p = lax.dot_general(q_s, k_s, nt, preferred_element_type=F32)
                a = a + jnp.where(row_t // (2 * s) == col_t // (2 * s), p, 0.0)
            qg = q * jnp.exp(b)
            kd = k * jnp.exp(b_last - b)
        o = jnp.dot(qg.astype(BF16), s_old.astype(BF16), preferred_element_type=F32)
        lhs = jnp.concatenate([jnp.transpose(kd).astype(BF16), a.astype(BF16)], axis=0)
        both = jnp.dot(lhs, v, preferred_element_type=F32)
        decay_col = jnp.transpose(jnp.broadcast_to(jnp.exp(b_last), (BLK, GLA_DK)))[:, :1]
        s_ref[h] = decay_col * s_old + both[:GLA_DK]
        o = o + both[GLA_DK:]
        o = o * lax.rsqrt(jnp.mean(o * o, axis=-1, keepdims=True) + LN_EPS) * gn_ref[...]
        r = r_all[:, dv].astype(F32)
        y = o * (r * _sigmoid(r)) * _sigmoid(ga_all[:, dv].astype(F32))
        y_ref[:, dv] = y.astype(y_ref.dtype)

    mild = jnp.max(jnp.abs(la_all)) * (BLK // 2) <= GLA_SAFE_EXPONENT

    @pl.when(mild)
    def _():
        for h in range(GLA_HEADS):
            head(h, True)

    @pl.when(jnp.logical_not(mild))
    def _():
        for h in range(GLA_HEADS):
            head(h, False)

    @pl.when(c == nblk)
    def _():
        s_out_ref[...] = s_ref[...]


def _tri_incl():
    i = np.arange(BLK)
    return jnp.asarray((i[None, :] <= i[:, None]).astype(np.float32), dtype=BF16)


def _gla_prompt(pp, pe, gk_up, gk_bias, gnorm, nbatch, nblk):
    names = ("qg", "kg", "vg", "rg", "lr", "ga")
    gkup = jnp.concatenate([gk_up, jnp.zeros((LANES - GLA_RANK, gk_up.shape[1]), gk_up.dtype)], axis=0).astype(BF16)
    m_specs = [pl.BlockSpec((BLK, pe[n].shape[1]), lambda b, c: (1, 0)) for n in names]
    p_specs = [pl.BlockSpec((BLK, pp[n].shape[1]), lambda b, c: (b * nblk + jnp.maximum(c - 1, 0), 0)) for n in names]
    w_specs = [
        pl.BlockSpec((LANES, GLA_HEADS * GLA_DK), lambda b, c: (0, 0)),
        pl.BlockSpec((1, GLA_HEADS * GLA_DK), lambda b, c: (0, 0)),
        pl.BlockSpec((1, GLA_DV), lambda b, c: (0, 0)),
        pl.BlockSpec((BLK, BLK), lambda b, c: (0, 0)),
    ]
    y, s_fin = pl.pallas_call(
        functools.partial(_gla_body, nblk),
        grid=(nbatch, nblk + 1),
        in_specs=m_specs + p_specs + w_specs,
        out_specs=[
            pl.BlockSpec((BLK, D_MODEL), lambda b, c: (b * nblk + jnp.maximum(c - 1, 0), 0)),
            pl.BlockSpec((None, GLA_HEADS, GLA_DK, GLA_DV), lambda b, c: (b, 0, 0, 0)),
        ],
        out_shape=[
            jax.ShapeDtypeStruct((nbatch * nblk * BLK, D_MODEL), BF16),
            jax.ShapeDtypeStruct((nbatch, GLA_HEADS, GLA_DK, GLA_DV), F32),
        ],
        scratch_shapes=[pltpu.VMEM((GLA_HEADS, GLA_DK, GLA_DV), F32)],
        compiler_params=_cparams(("arbitrary", "arbitrary")),
        name="gla_prompt",
    )(*[pe[n] for n in names], *[pp[n] for n in names], gkup, gk_bias.reshape(1, -1), gnorm.reshape(1, -1), _tri_incl())
    return y, s_fin


GLA_STEP_SEQS = 16


def _gla_step_body(q_ref, k_ref, v_ref, r_ref, lr_ref, ga_ref, gkup_ref, gkb_ref, gn_ref, s_in_ref,
                   y_ref, s_out_ref, at_ref, kt_ref, qt_ref):
    g = pl.program_id(0)
    nseq = q_ref.shape[0]

    @pl.when(g == 0)
    def _():
        x = jnp.dot(lr_ref[...].astype(BF16), gkup_ref[...], preferred_element_type=F32) + gkb_ref[...]
        a = jnp.exp(_log_sigmoid(x) * (1.0 / GLA_TAU))
        for h in range(GLA_HEADS):
            dk = slice(h * GLA_DK, (h + 1) * GLA_DK)
            at_ref[h] = jnp.transpose(a[:, dk])
            kt_ref[h] = jnp.transpose(k_ref[:, dk].astype(F32))
            qt_ref[h] = jnp.transpose(q_ref[:, dk].astype(F32))

    lane = lax.broadcasted_iota(jnp.int32, (GLA_DK, nseq), 1)
    ones = jnp.ones((nseq, GLA_DV), BF16)
    grp = pl.ds(pl.multiple_of(g * GLA_STEP_SEQS, GLA_STEP_SEQS), GLA_STEP_SEQS)
    r_grp = r_ref[grp, :].astype(F32)
    ga_grp = ga_ref[grp, :].astype(F32)
    for i in range(GLA_STEP_SEQS):
        n = g * GLA_STEP_SEQS + i
        sel = lane == n
        for h in range(GLA_HEADS):
            dv = slice(h * GLA_DV, (h + 1) * GLA_DV)
            a_sel = jnp.where(sel, at_ref[h], 0.0)
            k_sel = jnp.where(sel, kt_ref[h], 0.0).astype(BF16)
            q_sel = jnp.where(sel, qt_ref[h], 0.0).astype(BF16)
            decay = _split_dot_rhs(a_sel, ones)
            kv = jnp.dot(k_sel, v_ref[:, dv], preferred_element_type=F32)
            q_b = jnp.dot(q_sel, ones, preferred_element_type=F32)
            s_new = decay * s_in_ref[i, h] + kv
            s_out_ref[i, h] = s_new
            o = jnp.sum(q_b * s_new, axis=0, keepdims=True)
            o = o * lax.rsqrt(jnp.mean(o * o, axis=-1, keepdims=True) + LN_EPS) * gn_ref[...]
            r = r_grp[i:i + 1, dv]
            ga = ga_grp[i:i + 1, dv]
            y_ref[i:i + 1, dv] = (o * (r * _sigmoid(r)) * _sigmoid(ga)).astype(y_ref.dtype)


def _split_dot_rhs(x, b01):
    hi = x.astype(BF16)
    lo = (x - hi.astype(F32)).astype(BF16)
    return jnp.dot(hi, b01, preferred_element_type=F32) + jnp.dot(lo, b01, preferred_element_type=F32)


def _gla_step(pe, gk_up, gk_bias, gnorm, state):
    nseq = state.shape[0]
    assert nseq == BLK and nseq % GLA_STEP_SEQS == 0
    names = ("qg", "kg", "vg", "rg", "lr", "ga")
    gkup = jnp.concatenate([gk_up, jnp.zeros((LANES - GLA_RANK, gk_up.shape[1]), gk_up.dtype)], axis=0).astype(BF16)
    t_specs = [pl.BlockSpec((nseq, pe[n].shape[1]), lambda g: (0, 0)) for n in names]
    st_spec = pl.BlockSpec((GLA_STEP_SEQS, GLA_HEADS, GLA_DK, GLA_DV), lambda g: (g, 0, 0, 0))
    return pl.pallas_call(
        _gla_step_body,
        grid=(nseq // GLA_STEP_SEQS,),
        in_specs=t_specs + [
            pl.BlockSpec((LANES, GLA_HEADS * GLA_DK), lambda g: (0, 0)),
            pl.BlockSpec((1, GLA_HEADS * GLA_DK), lambda g: (0, 0)),
            pl.BlockSpec((1, GLA_DV), lambda g: (0, 0)),
            st_spec,
        ],
        out_specs=[pl.BlockSpec((GLA_STEP_SEQS, D_MODEL), lambda g: (g, 0)), st_spec],
        out_shape=[jax.ShapeDtypeStruct((nseq, D_MODEL), F32), jax.ShapeDtypeStruct(state.shape, F32)],
        scratch_shapes=[pltpu.VMEM((GLA_HEADS, GLA_DK, nseq), F32) for _ in range(3)],
        compiler_params=_cparams(("arbitrary",)),
        name="gla_step",
    )(*[pe[n] for n in names], gkup, gk_bias.reshape(1, -1), gnorm.reshape(1, -1), state)


HALF = LANES // 2


def _rel_bucket(dist):
    max_exact = REL_BUCKETS // 2
    d = jnp.maximum(dist, 0)
    large = max_exact + (jnp.log(jnp.maximum(d, 1).astype(F32) / max_exact)
                         / math.log(REL_MAX_DIST / max_exact) * (REL_BUCKETS - max_exact)).astype(jnp.int32)
    large = jnp.minimum(large, REL_BUCKETS - 1)
    return jnp.where(d < max_exact, d, large)


def _bias_lookup(rel_bias, dist):
    onehot = (_rel_bucket(dist)[..., None] == jnp.arange(REL_BUCKETS)).astype(F32)
    return jnp.einsum("...b,bh->h...", onehot, rel_bias.astype(F32), precision=lax.Precision.HIGHEST)


def _swa_bias_tables(rel_bias):
    q = jnp.arange(BLK)[:, None]
    c = jnp.arange(2 * BLK)[None, :]
    dist = BLK + q - c
    bias = _bias_lookup(rel_bias, dist)
    inside = (dist >= 0) & (dist < WINDOW)
    first = inside & (c >= BLK - N_META)
    neg = jnp.float32(-jnp.inf)
    return jnp.stack([jnp.where(first[None], bias, neg), jnp.where(inside[None], bias, neg)])


def _half_tiles(x):
    lane = lax.broadcasted_iota(jnp.int32, (x.shape[0], LANES), 1)
    low = lane < HALF
    out = []
    for t in range(2):
        tile = x[:, t * LANES:(t + 1) * LANES]
        swapped = pltpu.roll(tile, HALF, axis=1)
        zero = jnp.zeros_like(tile)
        even = (jnp.where(low, tile, zero).astype(BF16), jnp.where(low, zero, swapped).astype(BF16))
        odd = (jnp.where(low, swapped, zero).astype(BF16), jnp.where(low, zero, tile).astype(BF16))
        out += [even, odd]
    return out


def _dup_tiles(x):
    lane = lax.broadcasted_iota(jnp.int32, (x.shape[0], LANES), 1)
    low = lane < HALF
    out = []
    for t in range(2):
        tile = x[:, t * LANES:(t + 1) * LANES]
        swapped = pltpu.roll(tile, HALF, axis=1)
        out += [jnp.where(low, tile, swapped).astype(BF16), jnp.where(low, swapped, tile).astype(BF16)]
    return out


def _swa_body(nblk, q_ref, kc_ref, vc_ref, kp_ref, vp_ref, km_ref, vm_ref, gb_ref, yg_ref, tb_ref, sink_ref,
              o_ref, kw_ref, vw_ref):
    blk = pl.program_id(1)
    first = blk == 0

    @pl.when(blk == nblk - 1)
    def _():
        kw_ref[...] = kc_ref[...]
        vw_ref[...] = vc_ref[...]

    k_prev = jnp.where(first, km_ref[...], kp_ref[...])
    v_prev = jnp.where(first, vm_ref[...], vp_ref[...])
    k_tiles = _dup_tiles(jnp.concatenate([k_prev, kc_ref[...]], axis=0))
    v_tiles = _dup_tiles(jnp.concatenate([v_prev, vc_ref[...]], axis=0))
    variant = jnp.minimum(blk, 1)
    npair = GQA_GROUP // 2
    low = lax.broadcasted_iota(jnp.int32, (BLK, LANES), 1) < HALF
    seg = lax.broadcasted_iota(jnp.int32, (GQA_GROUP * BLK, 1), 0) // BLK
    nt = (((1,), (1,)), ((), ()))
    for j in range(ATT_KV_HEADS):
        tiles = [j * npair + pair for pair in range(npair)]
        q_t = [q_ref[:, t * LANES:(t + 1) * LANES] for t in tiles]
        zero = jnp.zeros_like(q_t[0])
        q_st = jnp.concatenate([jnp.where(low, q, zero) for q in q_t] + [jnp.where(low, zero, q) for q in q_t], axis=0)
        heads = [2 * t for t in tiles] + [2 * t + 1 for t in tiles]
        s = lax.dot_general(q_st, k_tiles[j], nt, preferred_element_type=F32)
        s = s + jnp.concatenate([tb_ref[variant, h] for h in heads], axis=0)
        sink = jnp.full((GQA_GROUP * BLK, 1), sink_ref[heads[0]], F32)
        for i in range(1, GQA_GROUP):
            sink = jnp.where(seg == i, sink_ref[heads[i]], sink)
        m = jnp.maximum(jnp.max(s, axis=-1, keepdims=True), sink)
        p = jnp.exp(s - m)
        inv = 1.0 / (jnp.sum(p, axis=-1, keepdims=True) + jnp.exp(sink - m))
        o = jnp.dot(p.astype(BF16), v_tiles[j], preferred_element_type=F32) * inv
        for pair, t in enumerate(tiles):
            cols = slice(t * LANES, (t + 1) * LANES)
            gate = _sigmoid(gb_ref[:, cols].astype(F32))
            even = o[pair * BLK:(pair + 1) * BLK]
            odd = o[(npair + pair) * BLK:(npair + pair + 1) * BLK]
            o_ref[:, cols] = (gate * jnp.where(low, even, odd) + yg_ref[:, cols].astype(F32)).astype(o_ref.dtype)


def _swa_prompt(pp, pe, yg, rel_bias, sinks, nbatch, nblk):
    tb = _swa_bias_tables(rel_bias)
    kvw = ATT_KV_HEADS * HEAD_DIM
    cur = lambda b, c: (b * nblk + c, 0)
    prev = lambda b, c: (b * nblk + jnp.maximum(c - 1, 0), 0)
    return pl.pallas_call(
        functools.partial(_swa_body, nblk),
        grid=(nbatch, nblk),
        in_specs=[
            pl.BlockSpec((BLK, D_MODEL), cur),
            pl.BlockSpec((BLK, kvw), cur), pl.BlockSpec((BLK, kvw), cur),
            pl.BlockSpec((BLK, kvw), prev), pl.BlockSpec((BLK, kvw), prev),
            pl.BlockSpec((BLK, kvw), lambda b, c: (1, 0)), pl.BlockSpec((BLK, kvw), lambda b, c: (1, 0)),
            pl.BlockSpec((BLK, D_MODEL), cur),
            pl.BlockSpec((BLK, D_MODEL), cur),
            pl.BlockSpec(tb.shape, lambda b, c: (0, 0, 0, 0)),
            pl.BlockSpec(memory_space=pltpu.SMEM),
        ],
        out_specs=[pl.BlockSpec((BLK, D_MODEL), cur),
                   pl.BlockSpec((None, BLK, kvw), lambda b, c: (b, 0, 0)),
                   pl.BlockSpec((None, BLK, kvw), lambda b, c: (b, 0, 0))],
        out_shape=[jax.ShapeDtypeStruct((nbatch * nblk * BLK, D_MODEL), BF16),
                   jax.ShapeDtypeStruct((nbatch, BLK, kvw), F32), jax.ShapeDtypeStruct((nbatch, BLK, kvw), F32)],
        compiler_params=_cparams(("arbitrary", "arbitrary")),
        name="swa_prompt",
    )(pp["qa"], pp["ka"], pp["va"], pp["ka"], pp["va"], pe["ka"], pe["va"], pp["gb"], yg, tb, sinks)


SWA_STEP_SEQS = 8
Q_TILES = ATT_HEADS // 2


def _swa_step_body(q_ref, kn_ref, vn_ref, ck_ref, cv_ref, gb_ref, yg_ref, tb_ref, sink_ref,
                   o_ref, ko_ref, vo_ref):
    row = lax.broadcasted_iota(jnp.int32, (WINDOW, ATT_KV_HEADS * HEAD_DIM), 0)
    trow = lax.broadcasted_iota(jnp.int32, (Q_TILES, LANES), 0)
    low = lax.broadcasted_iota(jnp.int32, (Q_TILES, LANES), 1) < HALF
    nt = (((1,), (1,)), ((), ()))
    for i in range(SWA_STEP_SEQS):
        k_win = jnp.where(row == WINDOW - 1, kn_ref[i:i + 1, :], pltpu.roll(ck_ref[i], WINDOW - 1, axis=0))
        v_win = jnp.where(row == WINDOW - 1, vn_ref[i:i + 1, :], pltpu.roll(cv_ref[i], WINDOW - 1, axis=0))
        ko_ref[i] = k_win
        vo_ref[i] = v_win
        k_tiles = _half_tiles(k_win)
        v_tiles = _half_tiles(v_win)
        q8 = q_ref[i].astype(BF16)
        s = [jnp.zeros((Q_TILES, WINDOW), F32), jnp.zeros((Q_TILES, WINDOW), F32)]
        for j in range(ATT_KV_HEADS):
            mine = trow // (GQA_GROUP // 2) == j
            for half in range(2):
                sj = lax.dot_general(q8, k_tiles[j][half], nt, preferred_element_type=F32)
                s[half] = jnp.where(mine, sj, s[half])
        p, inv = [], []
        for half in range(2):
            sh = s[half] + tb_ref[half]
            sink = sink_ref[half]
            m = jnp.maximum(jnp.max(sh, axis=-1, keepdims=True), sink)
            ph = jnp.exp(sh - m)
            inv.append(1.0 / (jnp.sum(ph, axis=-1, keepdims=True) + jnp.exp(sink - m)))
            p.append(ph.astype(BF16))
        o = jnp.zeros((Q_TILES, LANES), F32)
        for j in range(ATT_KV_HEADS):
            mine = trow // (GQA_GROUP // 2) == j
            pv = (jnp.dot(p[0], v_tiles[j][0], preferred_element_type=F32)
                  + jnp.dot(p[1], v_tiles[j][1], preferred_element_type=F32))
            o = jnp.where(mine, pv, o)
        o = o * jnp.where(low, inv[0], inv[1])
        o_ref[i] = _sigmoid(gb_ref[i]) * o + yg_ref[i]


def _swa_step(pe, yg_s, cache_k, cache_v, rel_bias, sinks):
    nseq = cache_k.shape[0]
    kvw = ATT_KV_HEADS * HEAD_DIM
    as_tiles = lambda x: x[:nseq].astype(F32).reshape(nseq, Q_TILES, LANES)
    dist = (WINDOW - 1) - jnp.arange(WINDOW)
    bias = _bias_lookup(rel_bias, dist)
    tb = jnp.stack([bias[0::2], bias[1::2]])
    sk = jnp.broadcast_to(jnp.stack([sinks[0::2], sinks[1::2]])[:, :, None], (2, Q_TILES, 1)).astype(F32)
    g = SWA_STEP_SEQS
    tile_spec = pl.BlockSpec((g, Q_TILES, LANES), lambda s: (s, 0, 0))
    win_spec = pl.BlockSpec((g, WINDOW, kvw), lambda s: (s, 0, 0))
    new_spec = pl.BlockSpec((g, kvw), lambda s: (s, 0))
    o, ko, vo = pl.pallas_call(
        _swa_step_body,
        grid=(nseq // g,),
        in_specs=[tile_spec, new_spec, new_spec, win_spec, win_spec, tile_spec, tile_spec,
                  pl.BlockSpec(tb.shape, lambda s: (0, 0, 0)), pl.BlockSpec(sk.shape, lambda s: (0, 0, 0))],
        out_specs=[tile_spec, win_spec, win_spec],
        out_shape=[jax.ShapeDtypeStruct((nseq, Q_TILES, LANES), F32),
                   jax.ShapeDtypeStruct(cache_k.shape, F32), jax.ShapeDtypeStruct(cache_v.shape, F32)],
        compiler_params=_cparams(("arbitrary",)),
        name="swa_step",
    )(as_tiles(pe["qa"]), pe["ka"], pe["va"], cache_k, cache_v, as_tiles(pe["gb"]),
      yg_s.reshape(nseq, Q_TILES, LANES), tb, sk)
    return o.reshape(nseq, D_MODEL), ko, vo


ROUTER_ROWS = 40
META_ROWS = 8


def _split3_nt(a_hi, a_lo, x):
    nt = (((1,), (1,)), ((), ()))
    x_hi = x.astype(BF16)
    x_lo = (x - x_hi.astype(F32)).astype(BF16)
    return (lax.dot_general(a_hi, x_hi, nt, preferred_element_type=F32)
            + lax.dot_general(a_hi, x_lo, nt, preferred_element_type=F32)
            + lax.dot_general(a_lo, x_hi, nt, preferred_element_type=F32))


def _first_argmax_rows(v, ridx, nrows):
    vmax = jnp.max(v, axis=0, keepdims=True)
    idx = jnp.min(jnp.where(v == vmax, ridx, nrows), axis=0, keepdims=True)
    return vmax, idx


def _post_body(nsteps, *refs):
    h1_ref, meta_ref, wcol_ref = refs[-5:-2]
    i = pl.program_id(0)

    @pl.when(i < nsteps)
    def _():
        _post_tile(i, *refs)

    @pl.when(i >= nsteps)
    def _():
        h1_ref[...] = jnp.zeros_like(h1_ref)
        meta_ref[...] = jnp.zeros_like(meta_ref)
        wcol_ref[...] = jnp.zeros_like(wcol_ref)


def _post_tile(i, mg_ref, x_ref, lng_ref, lnb_ref, wo_ref, g1_ref, b1_ref, wrh_ref, wrl_ref, rb_ref, ut_ref,
               cin_ref, *rest):
    h1_ref, meta_ref, wcol_ref, cout_ref, carry_ref = rest[-5:]

    @pl.when(i == 0)
    def _():
        carry_ref[...] = cin_ref[...]

    tm = x_ref.shape[0]
    h = _layer_norm_rows(x_ref[...], lng_ref[...], lnb_ref[...])
    acc = jnp.dot(mg_ref[...].astype(BF16), wo_ref[...], preferred_element_type=F32)
    h1 = _layer_norm_rows(ALPHA * h + acc, g1_ref[...], b1_ref[...])
    h1_ref[...] = h1

    lt = _split3_nt(wrh_ref[...], wrl_ref[...], h1) + rb_ref[:, :1]
    ridx = lax.broadcasted_iota(jnp.int32, (EXPERTS_PER_GROUP, tm), 0)
    neg = jnp.float32(-jnp.inf)
    g_log = jnp.where(ridx < N_GROUPS, lt[N_EXPERTS:N_EXPERTS + EXPERTS_PER_GROUP], neg)
    g_max, grp = _first_argmax_rows(g_log, ridx, EXPERTS_PER_GROUP)
    p_grp = 1.0 / jnp.sum(jnp.exp(g_log - g_max), axis=0, keepdims=True)
    e_in = lt[0:EXPERTS_PER_GROUP]
    for gi in range(1, N_GROUPS):
        e_in = jnp.where(grp == gi, lt[gi * EXPERTS_PER_GROUP:(gi + 1) * EXPERTS_PER_GROUP], e_in)
    v0, i0 = _first_argmax_rows(e_in, ridx, EXPERTS_PER_GROUP)
    v1, i1 = _first_argmax_rows(jnp.where(ridx == i0, neg, e_in), ridx, EXPERTS_PER_GROUP)
    t = jnp.exp(v1 - v0)
    w0 = p_grp / (1.0 + t)
    w1 = p_grp * t / (1.0 + t)
    e0 = grp * EXPERTS_PER_GROUP + i0
    e1 = grp * EXPERTS_PER_GROUP + i1

    eidx = lax.broadcasted_iota(jnp.int32, (N_EXPERTS, tm), 0)
    hit0 = eidx == e0
    hit1 = eidx == e1
    oh = jnp.where(jnp.logical_or(hit0, hit1), 1.0, 0.0)
    before = jnp.dot(oh.astype(BF16), ut_ref[...], preferred_element_type=F32) + carry_ref[:, :1]
    r0 = jnp.sum(jnp.where(hit0, before, 0.0), axis=0, keepdims=True).astype(jnp.int32)
    r1 = jnp.sum(jnp.where(hit1, before, 0.0), axis=0, keepdims=True).astype(jnp.int32)
    carry_ref[...] = carry_ref[...] + jnp.sum(oh, axis=1, keepdims=True)
    cout_ref[...] = carry_ref[...]

    zi = jnp.zeros((META_ROWS - 4, tm), jnp.int32)
    meta_ref[...] = jnp.concatenate([e0, e1, r0, r1, zi], axis=0)
    wt = jnp.concatenate([w0, w1, jnp.zeros((LANES - 2, tm), F32)], axis=0)
    wcol_ref[...] = jnp.transpose(wt)


def _post(mg, x2d, prm, tm, row0, total_rows, carry_in, prev=None, zero_tail=False):
    m = x2d.shape[0]
    assert m % tm == 0 and row0 % tm == 0 and total_rows % tm == 0
    off = row0 // tm
    nsteps = m // tm
    last = nsteps - 1
    ut = jnp.asarray(np.triu(np.ones((tm, tm), np.float32), 1), dtype=BF16)
    full = lambda shape: pl.BlockSpec(shape, lambda i: (0,) * len(shape))
    in_specs = [
        pl.BlockSpec((tm, D_MODEL), lambda i: (jnp.minimum(i, last), 0)),
        pl.BlockSpec((tm, D_MODEL), lambda i: (jnp.minimum(i, last), 0)),
        full((1, D_MODEL)), full((1, D_MODEL)),
        full((D_MODEL, D_MODEL)),
        full((1, D_MODEL)), full((1, D_MODEL)),
        full((ROUTER_ROWS, D_MODEL)), full((ROUTER_ROWS, D_MODEL)), full((ROUTER_ROWS, LANES)),
        full((tm, tm)),
        full((N_EXPERTS, LANES)),
    ]
    args = [mg, x2d, prm["ln_emb_g"], prm["ln_emb_b"], prm["w_out"], prm["ln1_g"], prm["ln1_b"],
            prm["wr_hi"], prm["wr_lo"], prm["r_bias"], ut, carry_in]
    aliases = {}
    if prev is not None:
        for k, buf in enumerate(prev):
            in_specs.append(pl.BlockSpec(memory_space=pl.ANY))
            aliases[len(args)] = k
            args.append(buf)
    out_shape = [
        jax.ShapeDtypeStruct((total_rows, D_MODEL), F32),
        jax.ShapeDtypeStruct((META_ROWS, total_rows), jnp.int32),
        jax.ShapeDtypeStruct((total_rows, LANES), F32),
        jax.ShapeDtypeStruct((N_EXPERTS, LANES), F32),
    ]
    out_specs = [
        pl.BlockSpec((tm, D_MODEL), lambda i: (i + off, 0)),
        pl.BlockSpec((META_ROWS, tm), lambda i: (0, i + off)),
        pl.BlockSpec((tm, LANES), lambda i: (i + off, 0)),
        full((N_EXPERTS, LANES)),
    ]
    if prev is not None:
        assert len(prev) == 3
    return pl.pallas_call(
        functools.partial(_post_body, nsteps),
        grid=(nsteps + int(zero_tail),),
        in_specs=in_specs,
        out_specs=out_specs,
        out_shape=out_shape,
        input_output_aliases=aliases,
        scratch_shapes=[pltpu.VMEM((N_EXPERTS, LANES), F32)],
        compiler_params=_cparams(("arbitrary",)),
        name="post_attn",
    )(*args)


def _prep_post_params(ln_emb_g, ln_emb_b, w_out, ln1_g, ln1_b, w_rg, b_rg, w_re, b_re):
    row = lambda v: v.reshape(1, -1)
    wr = jnp.concatenate([w_re.T, w_rg.T, jnp.zeros((ROUTER_ROWS - N_EXPERTS - N_GROUPS, D_MODEL), F32)], axis=0)
    wr_hi = wr.astype(BF16)
    wr_lo = (wr - wr_hi.astype(F32)).astype(BF16)
    rb = jnp.concatenate([b_re, b_rg, jnp.zeros((ROUTER_ROWS - N_EXPERTS - N_GROUPS,), F32)])
    return dict(ln_emb_g=row(ln_emb_g), ln_emb_b=row(ln_emb_b), w_out=w_out.astype(BF16), ln1_g=row(ln1_g),
                ln1_b=row(ln1_b), wr_hi=wr_hi, wr_lo=wr_lo,
                r_bias=jnp.broadcast_to(rb[:, None], (ROUTER_ROWS, LANES)))


MOE_ROWS = 256
SUBLANES = 8
assert D_MODEL == SUBLANES * LANES


def _store_rows_as_tiles(ref, x):
    n = x.shape[0]
    for c in range(SUBLANES):
        ref[pl.ds(c, n, stride=SUBLANES), :] = x[:, c * LANES:(c + 1) * LANES]


def _load_rows_from_tiles(ref, n):
    return jnp.concatenate([ref[pl.ds(c, n, stride=SUBLANES), :] for c in range(SUBLANES)], axis=1)


def _tile_of_row(ref, r):
    return ref.at[pl.ds(pl.multiple_of(r * SUBLANES, SUBLANES), SUBLANES)]


def _moe_plan(counts, total_assign):
    nb_max = -(-total_assign // MOE_ROWS) + N_EXPERTS
    padded = (counts + MOE_ROWS - 1) // MOE_ROWS * MOE_ROWS
    pend = jnp.cumsum(padded)
    pstart = (pend - padded).astype(jnp.int32)
    block_start = jnp.arange(nb_max, dtype=jnp.int32) * MOE_ROWS
    n_ended = jnp.sum((pend[None, :] <= block_start[:, None]).astype(jnp.int32), axis=1)
    block_e = jnp.minimum(n_ended, N_EXPERTS - 1).astype(jnp.int32)
    n_used = (pend[-1] // MOE_ROWS).astype(jnp.int32).reshape(1)
    tail_start = jnp.where(padded > 0, pend - MOE_ROWS, -1)
    spare = pend[-1] + jnp.arange(N_EXPERTS, dtype=pend.dtype) * MOE_ROWS
    spare = jnp.where(spare < nb_max * MOE_ROWS, spare, -1)
    zero_blocks = jnp.concatenate([tail_start, spare]).astype(jnp.int32)
    return pstart, block_e, n_used, nb_max, zero_blocks


ROW_UNROLL = 8


def _slot_ids(meta, pstart):
    experts = meta[0:TOP_K]
    ranks = meta[TOP_K:2 * TOP_K]
    onehot = experts[..., None] == jnp.arange(N_EXPERTS, dtype=jnp.int32)
    return ranks + jnp.sum(jnp.where(onehot, pstart, 0), axis=-1)


def _for_row_groups(tm, fn):
    def group(g, c):
        t0 = pl.multiple_of(g * ROW_UNROLL, ROW_UNROLL)
        for r in range(ROW_UNROLL):
            for k in range(TOP_K):
                fn(t0 + r, k)
        return c

    lax.fori_loop(0, tm // ROW_UNROLL, group, 0)


def _dispatch_body(nsteps, tail_ref, s0_ref, s1_ref, h_ref, xs_ref, pk_ref, zero_ref, sems, zsem):
    i = pl.program_id(0)
    tm = h_ref.shape[0]
    slot_refs = (s0_ref, s1_ref)
    cur = i % 2

    blk_tiles = MOE_ROWS * SUBLANES

    @pl.when(i == 0)
    def _():
        zero_ref[...] = jnp.zeros_like(zero_ref)
        for e in range(2 * N_EXPERTS):
            @pl.when(tail_ref[e] >= 0)
            def _():
                start = pl.multiple_of(tail_ref[e] * SUBLANES, blk_tiles)
                pltpu.make_async_copy(zero_ref, xs_ref.at[pl.ds(start, blk_tiles)], zsem).start()
        for e in range(2 * N_EXPERTS):
            @pl.when(tail_ref[e] >= 0)
            def _():
                pltpu.make_async_copy(zero_ref, xs_ref.at[pl.ds(0, blk_tiles)], zsem).wait()

    _store_rows_as_tiles(pk_ref.at[cur], h_ref[...])

    def send(t, k):
        pltpu.make_async_copy(_tile_of_row(pk_ref.at[cur], t), _tile_of_row(xs_ref, slot_refs[k][0, t]),
                              sems.at[cur]).start(priority=k)

    def wait_buffer(buf):
        _for_row_groups(tm, lambda t, k: pltpu.make_async_copy(
            _tile_of_row(pk_ref.at[buf], t), _tile_of_row(xs_ref, 0), sems.at[buf]).wait())

    _for_row_groups(tm, send)

    @pl.when(i > 0)
    def _():
        wait_buffer(1 - cur)

    @pl.when(i == nsteps - 1)
    def _():
        wait_buffer(cur)


def _dispatch(h1, slot_ids, tail_start, nslots, tm, total):
    assert total % tm == 0 and total <= h1.shape[0] and tm % ROW_UNROLL == 0
    slot_spec = pl.BlockSpec((1, tm), lambda i, tl: (0, i), memory_space=pltpu.SMEM)
    return pl.pallas_call(
        functools.partial(_dispatch_body, total // tm),
        grid_spec=pltpu.PrefetchScalarGridSpec(
            num_scalar_prefetch=1,
            grid=(total // tm,),
            in_specs=[slot_spec, slot_spec, pl.BlockSpec((tm, D_MODEL), lambda i, tl: (i, 0))],
            out_specs=pl.BlockSpec(memory_space=pl.ANY),
            scratch_shapes=[pltpu.VMEM((2, tm * SUBLANES, LANES), F32), pltpu.VMEM((MOE_ROWS * SUBLANES, LANES), F32),
                            pltpu.SemaphoreType.DMA((2,)), pltpu.SemaphoreType.DMA(())],
        ),
        out_shape=jax.ShapeDtypeStruct((nslots * SUBLANES, LANES), F32),
        compiler_params=_cparams(("arbitrary",)),
        name="moe_dispatch",
    )(tail_start, slot_ids[0:1], slot_ids[1:2], h1)


def _expert_schedule(block_e, n_used):
    nb = block_e.shape[0]
    idx = jnp.arange(nb, dtype=jnp.int32)
    first = (idx < n_used[0]) & ((idx == 0) | (block_e != jnp.roll(block_e, 1)))
    parity = (jnp.cumsum(first.astype(jnp.int32)) - 1) % 2
    pos = jnp.where(first, idx, nb)
    at_or_after = jnp.flip(lax.cummin(jnp.flip(pos)))
    nxt = jnp.concatenate([at_or_after[1:], jnp.full((1,), nb, jnp.int32)])
    nexte = jnp.where(nxt < nb, block_e[jnp.minimum(nxt, nb - 1)], -1)
    return first.astype(jnp.int32), nexte.astype(jnp.int32), parity.astype(jnp.int32)


def _expert_body(be_ref, nu_ref, first_ref, nexte_ref, par_ref, xs_ref, wg_hbm, wu_hbm, wd_hbm, ys_ref,
                 wgf_ref, wuf_ref, wdf_ref, wgb_ref, wub_ref, wdb_ref, sems):
    i = pl.program_id(0)
    hbm = (wg_hbm, wu_hbm, wd_hbm)
    stage = (wgf_ref, wuf_ref, wdf_ref)

    def weight_copies(e, buf):
        return [pltpu.make_async_copy(hbm[w].at[e], stage[w].at[buf], sems.at[buf, w]) for w in range(3)]

    @pl.when(first_ref[i] == 1)
    def _():
        buf = par_ref[i]

        @pl.when(i == 0)
        def _():
            for c in weight_copies(be_ref[0], buf):
                c.start()

        for c in weight_copies(be_ref[i], buf):
            c.wait()

        @pl.when(nexte_ref[i] >= 0)
        def _():
            for c in weight_copies(nexte_ref[i], 1 - buf):
                c.start()

        wgb_ref[...] = wgf_ref[buf].astype(BF16)
        wub_ref[...] = wuf_ref[buf].astype(BF16)
        wdb_ref[...] = wdf_ref[buf].astype(BF16)

    @pl.when(i < nu_ref[0])
    def _():
        x = _load_rows_from_tiles(xs_ref, MOE_ROWS).astype(BF16)
        g = jnp.dot(x, wgb_ref[...], preferred_element_type=F32)
        hb = (g * _sigmoid(g)) * jnp.dot(x, wub_ref[...], preferred_element_type=F32)
        y = jnp.dot(hb.astype(BF16), wdb_ref[...], preferred_element_type=F32)
        _store_rows_as_tiles(ys_ref, y)

    @pl.when(i >= nu_ref[0])
    def _():
        ys_ref[...] = jnp.zeros_like(ys_ref)


def _experts(xs, block_e, n_used, w_gate, w_up, w_down, nb_max):
    first, nexte, parity = _expert_schedule(block_e, n_used)
    rows = lambda i, *_: (i, 0)
    used_rows = lambda i, be, nu, *_: (jnp.maximum(jnp.minimum(i, nu[0] - 1), 0), 0)
    any_spec = pl.BlockSpec(memory_space=pl.ANY)
    blk = (MOE_ROWS * SUBLANES, LANES)
    return pl.pallas_call(
        _expert_body,
        grid_spec=pltpu.PrefetchScalarGridSpec(
            num_scalar_prefetch=5,
            grid=(nb_max,),
            in_specs=[pl.BlockSpec(blk, used_rows), any_spec, any_spec, any_spec],
            out_specs=pl.BlockSpec(blk, rows),
            scratch_shapes=[
                pltpu.VMEM((2, D_MODEL, D_EXPERT), F32), pltpu.VMEM((2, D_MODEL, D_EXPERT), F32),
                pltpu.VMEM((2, D_EXPERT, D_MODEL), F32),
                pltpu.VMEM((D_MODEL, D_EXPERT), BF16), pltpu.VMEM((D_MODEL, D_EXPERT), BF16),
                pltpu.VMEM((D_EXPERT, D_MODEL), BF16),
                pltpu.SemaphoreType.DMA((2, 3)),
            ],
        ),
        out_shape=jax.ShapeDtypeStruct(xs.shape, F32),
        compiler_params=_cparams(("arbitrary",)),
        name="moe_experts",
    )(block_e, n_used, first, nexte, parity, xs, w_gate, w_up, w_down)


def _combine_body(nsteps, s0_ref, s1_ref, n0_ref, n1_ref, h_ref, w_ref, g2_ref, b2_ref, ys_ref, o_ref, buf_ref, sems):
    i = pl.program_id(0)
    tm = h_ref.shape[0]
    cur = i % 2

    def fetch(slot_refs, buf):
        _for_row_groups(tm, lambda t, k: pltpu.make_async_copy(
            _tile_of_row(ys_ref, slot_refs[k][0, t]), _tile_of_row(buf_ref.at[buf, k], t),
            sems.at[buf]).start(priority=k))

    @pl.when(i == 0)
    def _():
        fetch((s0_ref, s1_ref), cur)

    @pl.when(i + 1 < nsteps)
    def _():
        fetch((n0_ref, n1_ref), 1 - cur)

    _for_row_groups(tm, lambda t, k: pltpu.make_async_copy(
        _tile_of_row(ys_ref, 0), _tile_of_row(buf_ref.at[cur, k], t), sems.at[cur]).wait())
    w = w_ref[...]
    f = (w[:, 0:1] * _load_rows_from_tiles(buf_ref.at[cur, 0], tm)
         + w[:, 1:2] * _load_rows_from_tiles(buf_ref.at[cur, 1], tm))
    o_ref[...] = _layer_norm_rows(ALPHA * h_ref[...] + f, g2_ref[...], b2_ref[...])


def _combine(h1, wcol, slot_ids, ys, ln2_g, ln2_b, tm, row0, nrows):
    assert nrows % tm == 0 and row0 % tm == 0 and tm % ROW_UNROLL == 0
    off = row0 // tm
    nsteps = nrows // tm
    slot_spec = pl.BlockSpec((1, tm), lambda i: (0, i + off), memory_space=pltpu.SMEM)
    next_spec = pl.BlockSpec((1, tm), lambda i: (0, jnp.minimum(i + 1, nsteps - 1) + off), memory_space=pltpu.SMEM)
    return pl.pallas_call(
        functools.partial(_combine_body, nsteps),
        grid=(nsteps,),
        in_specs=[
            slot_spec, slot_spec, next_spec, next_spec,
            pl.BlockSpec((tm, D_MODEL), lambda i: (i + off, 0)),
            pl.BlockSpec((tm, LANES), lambda i: (i + off, 0)),
            pl.BlockSpec((1, D_MODEL), lambda i: (0, 0)),
            pl.BlockSpec((1, D_MODEL), lambda i: (0, 0)),
            pl.BlockSpec(memory_space=pl.ANY),
        ],
        out_specs=pl.BlockSpec((tm, D_MODEL), lambda i: (i, 0)),
        scratch_shapes=[pltpu.VMEM((2, TOP_K, tm * SUBLANES, LANES), F32), pltpu.SemaphoreType.DMA((2,))],
        out_shape=jax.ShapeDtypeStruct((nrows, D_MODEL), F32),
        compiler_params=_cparams(("arbitrary",)),
        name="moe_combine",
    )(slot_ids[0:1], slot_ids[1:2], slot_ids[0:1], slot_ids[1:2], h1, wcol, ln2_g.reshape(1, -1),
      ln2_b.reshape(1, -1), ys)


PROJ_ROWS = 512
POST_ROWS = 512
DISPATCH_ROWS = 384
COMBINE_ROWS = 256


def kernel(x_prompt, x_sample, state_gla, cache_swa_k, cache_swa_v, meta_tokens, ln_emb_g, ln_emb_b, rel_bias, w_in,
           gk_up, gk_bias, gla_norm_g, sinks, w_out, ln1_g, ln1_b, w_router_group, b_router_group, w_router_expert,
           b_router_expert, w_gate, w_up, w_down, ln2_g, ln2_b):
    nbatch, seq, d = x_prompt.shape
    nseq = x_sample.shape[0]
    assert w_in.shape[0] == DEPTH == 1 and d == D_MODEL and x_sample.shape[1] == 1
    assert seq % BLK == 0 and nseq == BLK and meta_tokens.shape[0] == N_META
    nblk = seq // BLK
    n_prompt = nbatch * seq
    total = n_prompt + nseq
    kvw = ATT_KV_HEADS * HEAD_DIM

    xp = x_prompt.reshape(n_prompt, d)
    xs = x_sample.reshape(nseq, d)
    extra = jnp.concatenate([xs, jnp.zeros((BLK - N_META, d), xs.dtype), meta_tokens.astype(xs.dtype)], axis=0)
    w_cat = _prep_w_in(w_in[0])
    pp = _ln_proj(xp, ln_emb_g, ln_emb_b, w_cat, PROJ_ROWS)
    pe = _ln_proj(extra, ln_emb_g, ln_emb_b, w_cat, 2 * BLK)

    yg, gla_p = _gla_prompt(pp, pe, gk_up[0], gk_bias[0], gla_norm_g[0], nbatch, nblk)
    yg_s, gla_s = _gla_step(pe, gk_up[0], gk_bias[0], gla_norm_g[0], state_gla[0])
    mg, k_win, v_win = _swa_prompt(pp, pe, yg, rel_bias, sinks[0], nbatch, nblk)
    mg_s, k_s, v_s = _swa_step(pe, yg_s, cache_swa_k[0].reshape(nseq, WINDOW, kvw),
                               cache_swa_v[0].reshape(nseq, WINDOW, kvw), rel_bias, sinks[0])

    prm = _prep_post_params(ln_emb_g, ln_emb_b, w_out[0], ln1_g[0], ln1_b[0], w_router_group[0], b_router_group[0],
                            w_router_expert[0], b_router_expert[0])
    carry0 = jnp.zeros((N_EXPERTS, LANES), F32)
    rows_alloc = n_prompt + POST_ROWS
    h1, meta, wcol, carry1 = _post(mg, xp, prm, POST_ROWS, 0, rows_alloc, carry0, zero_tail=True)
    h1, meta, wcol, carry2 = _post(mg_s, xs, prm, nseq, n_prompt, rows_alloc, carry1, prev=(h1, meta, wcol))

    counts = carry2[:, 0].astype(jnp.int32)
    pstart, block_e, n_used, nb_max, tail_start = _moe_plan(counts, TOP_K * total)
    slot_ids = _slot_ids(meta, pstart)
    xs_sorted = _dispatch(h1, slot_ids, tail_start, nb_max * MOE_ROWS, DISPATCH_ROWS, total)
    ys = _experts(xs_sorted, block_e, n_used, w_gate[0], w_up[0], w_down[0], nb_max)
    y_p = _combine(h1, wcol, slot_ids, ys, ln2_g[0], ln2_b[0], COMBINE_ROWS, 0, n_prompt)
    y_s = _combine(h1, wcol, slot_ids, ys, ln2_g[0], ln2_b[0], nseq, n_prompt, nseq)

    kv_shape = (1, nbatch, WINDOW, ATT_KV_HEADS, HEAD_DIM)
    k_p = k_win.reshape(kv_shape)
    v_p = v_win.reshape(kv_shape)
    return (y_p.reshape(nbatch, seq, d), y_s.reshape(nseq, 1, d), gla_p[None], k_p, v_p, gla_s[None],
            k_s.reshape(cache_swa_k.shape), v_s.reshape(cache_swa_v.shape))
```

```python
import functools
import math

import jax
import jax.numpy as jnp
import numpy as np
from jax import lax
from jax.experimental import pallas as pl
from jax.experimental.pallas import tpu as pltpu

F32 = jnp.float32
BF16 = jnp.bfloat16

D_MODEL = 1024
N_META = 16
LN_EPS = 1e-5
GLA_HEADS = 4
GLA_DK = 128
GLA_DV = 256
GLA_RANK = 16
GLA_TAU = 16.0
HEAD_DIM = 64
ATT_HEADS = 16
ATT_KV_HEADS = 4
GQA_GROUP = 4
WINDOW = 128
REL_BUCKETS = 32
REL_MAX_DIST = 128
N_GROUPS = 4
EXPERTS_PER_GROUP = 8
N_EXPERTS = 32
TOP_K = 2
D_EXPERT = 512
DEPTH = 1
ALPHA = (2.0 * DEPTH) ** 0.25

LANES = 128
BLK = 128
VMEM_LIMIT = 56 * 1024 * 1024


def _cparams(sem):
    return pltpu.CompilerParams(dimension_semantics=sem, vmem_limit_bytes=VMEM_LIMIT)


def _layer_norm_rows(x, g, b):
    mu = jnp.mean(x, axis=-1, keepdims=True)
    xc = x - mu
    var = jnp.mean(xc * xc, axis=-1, keepdims=True)
    return xc * lax.rsqrt(var + LN_EPS) * g + b


_PROJ_OUTS = (
    ("qg", GLA_HEADS * GLA_DK, BF16, GLA_DK ** -0.5),
    ("kg", GLA_HEADS * GLA_DK, BF16, None),
    ("vg", GLA_HEADS * GLA_DV, BF16, None),
    ("rg", GLA_HEADS * GLA_DV, BF16, None),
    ("qa", ATT_HEADS * HEAD_DIM, BF16, HEAD_DIM ** -0.5),
    ("ka", ATT_KV_HEADS * HEAD_DIM, F32, None),
    ("va", ATT_KV_HEADS * HEAD_DIM, F32, None),
    ("ga", D_MODEL, BF16, None),
    ("gb", D_MODEL, BF16, None),
    ("lr", LANES, F32, None),
)
_PROJ_W = sum(w for _, w, _, _ in _PROJ_OUTS)


def _prep_w_in(w_in):
    sizes = (512, 512, 1024, 1024, GLA_RANK, 1024, 256, 256, 1024, 1024)
    offs = np.cumsum((0,) + sizes)
    a = w_in[:, : offs[4]]
    lr = w_in[:, offs[4]: offs[5]]
    b = w_in[:, offs[5]:]
    pad = jnp.zeros((w_in.shape[0], LANES - GLA_RANK), w_in.dtype)
    return jnp.concatenate([a, b, lr, pad], axis=1).astype(BF16)


def _ln_proj_body(x_ref, g_ref, b_ref, w_ref, *out_refs):
    xn = _layer_norm_rows(x_ref[...], g_ref[...], b_ref[...]).astype(BF16)
    c0 = 0
    for (_, width, dtype, scale), o_ref in zip(_PROJ_OUTS, out_refs):
        acc = jnp.dot(xn, w_ref[:, c0:c0 + width], preferred_element_type=F32)
        if scale is not None:
            acc = acc * scale
        o_ref[...] = acc.astype(dtype)
        c0 += width


def _ln_proj(x2d, ln_g, ln_b, w_cat, tm):
    m = x2d.shape[0]
    assert m % tm == 0
    out_shape = [jax.ShapeDtypeStruct((m, w), dt) for _, w, dt, _ in _PROJ_OUTS]
    out_specs = [pl.BlockSpec((tm, w), lambda i: (i, 0)) for _, w, _, _ in _PROJ_OUTS]
    outs = pl.pallas_call(
        _ln_proj_body,
        grid=(m // tm,),
        in_specs=[
            pl.BlockSpec((tm, D_MODEL), lambda i: (i, 0)),
            pl.BlockSpec((1, D_MODEL), lambda i: (0, 0)),
            pl.BlockSpec((1, D_MODEL), lambda i: (0, 0)),
            pl.BlockSpec((D_MODEL, _PROJ_W), lambda i: (0, 0), pipeline_mode=pl.Buffered(1)),
        ],
        out_specs=out_specs,
        out_shape=out_shape,
        compiler_params=_cparams(("arbitrary",)),
        name="ln_proj",
    )(x2d, ln_g.reshape(1, -1), ln_b.reshape(1, -1), w_cat)
    return dict(zip([n for n, _, _, _ in _PROJ_OUTS], outs))


_GLA_LEVELS = tuple(2 ** i for i in range(int(math.log2(BLK))))
GLA_SAFE_EXPONENT = 60.0


def _log_sigmoid(x):
    return jnp.minimum(x, 0.0) - jnp.log(1.0 + jnp.exp(-jnp.abs(x)))


def _sigmoid(x):
    return 0.5 * jnp.tanh(0.5 * x) + 0.5


def _split_dot(a01, x):
    hi = x.astype(BF16)
    lo = (x - hi.astype(F32)).astype(BF16)
    n = x.shape[1]
    both = jnp.dot(a01, jnp.concatenate([hi, lo], axis=1), preferred_element_type=F32)
    return both[:, :n] + both[:, n:]


def _gla_anchor_exponent(b, la, s, row):
    if s == 1:
        return jnp.where(row % 2 == 1, la, 0.0)
    if s == 2:
        la_dn = pltpu.roll(la, 1, axis=0)
        la_up = pltpu.roll(la, BLK - 1, axis=0)
        r = row % 4
        return jnp.where(r == 0, la_up, jnp.where(r == 1, 0.0, jnp.where(r == 2, la, la + la_dn)))
    nb = BLK // (2 * s)
    b3 = b.reshape(nb, 2 * s, b.shape[-1])
    anchor = jnp.broadcast_to(b3[:, s - 1:s, :], b3.shape).reshape(b.shape)
    return -jnp.abs(b - anchor)


def _gla_body(nblk, qm, km, vm, rm, lrm, gam, qp, kp, vp, rp, lrp, gap, gkup_ref, gkb_ref, gn_ref, tri_ref,
              y_ref, s_out_ref, s_ref):
    c = pl.program_id(1)
    is_meta = c == 0

    @pl.when(is_meta)
    def _():
        s_ref[...] = jnp.zeros_like(s_ref)

    def pick(m_ref, p_ref):
        return jnp.where(is_meta, m_ref[...], p_ref[...])

    row = lax.broadcasted_iota(jnp.int32, (BLK, GLA_DK), 0)
    col_t = lax.broadcasted_iota(jnp.int32, (BLK, BLK), 1)
    row_t = lax.broadcasted_iota(jnp.int32, (BLK, BLK), 0)
    live = jnp.logical_or(jnp.logical_not(is_meta), row >= BLK - N_META)
    tri = tri_ref[...]
    q_all, k_all, v_all, r_all = pick(qm, qp), pick(km, kp), pick(vm, vp), pick(rm, rp)
    ga_all = pick(gam, gap)
    lr = pick(lrm, lrp).astype(BF16)
    x_all = jnp.dot(lr, gkup_ref[...], preferred_element_type=F32) + gkb_ref[...]
    la_all = _log_sigmoid(x_all) * (1.0 / GLA_TAU)
    nt = (((1,), (1,)), ((), ()))
    mid = BLK // 2 - 1

    def head(h, single_anchor):
        dk = slice(h * GLA_DK, (h + 1) * GLA_DK)
        dv = slice(h * GLA_DV, (h + 1) * GLA_DV)
        la = jnp.where(live, la_all[:, dk], 0.0)
        q = q_all[:, dk].astype(F32)
        k = jnp.where(live, k_all[:, dk].astype(F32), 0.0)
        v = v_all[:, dv]
        b = _split_dot(tri, la)
        b_last = b[BLK - 1:BLK, :]
        s_old = s_ref[h]
        if single_anchor:
            b_mid = b[mid:mid + 1, :]
            qe = q * jnp.exp(b - b_mid)
            ke = k * jnp.exp(b_mid - b)
            a = jnp.where(row_t >= col_t,
                          lax.dot_general(qe.astype(BF16), ke.astype(BF16), nt, preferred_element_type=F32), 0.0)
            qg = qe * jnp.exp(b_mid)
            kd = ke * jnp.exp(b_last - b_mid)
        else:
            a = jnp.where(row_t == col_t,
                          lax.dot_general(q.astype(BF16), k.astype(BF16), nt, preferred_element_type=F32), 0.0)
            for s in _GLA_LEVELS:
                e = jnp.exp(_gla_anchor_exponent(b, la, s, row))
                upper = (row // s) % 2 == 1
                q_s = jnp.where(upper, q * e, 0.0).astype(BF16)
                k_s = jnp.where(upper, 0.0, k * e).astype(BF16)
                p = lax.dot_general(q_s, k_s, nt, preferred_element_type=F32)
                a = a + jnp.where(row_t // (2 * s) == col_t // (2 * s), p, 0.0)
            qg = q * jnp.exp(b)
            kd = k * jnp.exp(b_last - b)
        o = jnp.dot(qg.astype(BF16), s_old.astype(BF16), preferred_element_type=F32)
        lhs = jnp.concatenate([jnp.transpose(kd).astype(BF16), a.astype(BF16)], axis=0)
        both = jnp.dot(lhs, v, preferred_element_type=F32)
        decay_col = jnp.transpose(jnp.broadcast_to(jnp.exp(b_last), (BLK, GLA_DK)))[:, :1]
        s_ref[h] = decay_col * s_old + both[:GLA_DK]
        o = o + both[GLA_DK:]
        o = o * lax.rsqrt(jnp.mean(o * o, axis=-1, keepdims=True) + LN_EPS) * gn_ref[...]
        r = r_all[:, dv].astype(F32)
        y = o * (r * _sigmoid(r)) * _sigmoid(ga_all[:, dv].astype(F32))
        y_ref[:, dv] = y.astype(y_ref.dtype)

    mild = jnp.max(jnp.abs(la_all)) * (BLK // 2) <= GLA_SAFE_EXPONENT

    @pl.when(mild)
    def _():
        for h in range(GLA_HEADS):
            head(h, True)

    @pl.when(jnp.logical_not(mild))
    def _():
        for h in range(GLA_HEADS):
            head(h, False)

    @pl.when(c == nblk)
    def _():
        s_out_ref[...] = s_ref[...]


def _tri_incl():
    i = np.arange(BLK)
    return jnp.asarray((i[None, :] <= i[:, None]).astype(np.float32), dtype=BF16)


def _gla_prompt(pp, pe, gk_up, gk_bias, gnorm, nbatch, nblk):
    names = ("qg", "kg", "vg", "rg", "lr", "ga")
    gkup = jnp.concatenate([gk_up, jnp.zeros((LANES - GLA_RANK, gk_up.shape[1]), gk_up.dtype)], axis=0).astype(BF16)
    m_specs = [pl.BlockSpec((BLK, pe[n].shape[1]), lambda b, c: (1, 0)) for n in names]
    p_specs = [pl.BlockSpec((BLK, pp[n].shape[1]), lambda b, c: (b * nblk + jnp.maximum(c - 1, 0), 0)) for n in names]
    w_specs = [
        pl.BlockSpec((LANES, GLA_HEADS * GLA_DK), lambda b, c: (0, 0)),
        pl.BlockSpec((1, GLA_HEADS * GLA_DK), lambda b, c: (0, 0)),
        pl.BlockSpec((1, GLA_DV), lambda b, c: (0, 0)),
        pl.BlockSpec((BLK, BLK), lambda b, c: (0, 0)),
    ]
    y, s_fin = pl.pallas_call(
        functools.partial(_gla_body, nblk),
        grid=(nbatch, nblk + 1),
        in_specs=m_specs + p_specs + w_specs,
        out_specs=[
            pl.BlockSpec((BLK, D_MODEL), lambda b, c: (b * nblk + jnp.maximum(c - 1, 0), 0)),
            pl.BlockSpec((None, GLA_HEADS, GLA_DK, GLA_DV), lambda b, c: (b, 0, 0, 0)),
        ],
        out_shape=[
            jax.ShapeDtypeStruct((nbatch * nblk * BLK, D_MODEL), BF16),
            jax.ShapeDtypeStruct((nbatch, GLA_HEADS, GLA_DK, GLA_DV), F32),
        ],
        scratch_shapes=[pltpu.VMEM((GLA_HEADS, GLA_DK, GLA_DV), F32)],
        compiler_params=_cparams(("arbitrary", "arbitrary")),
        name="gla_prompt",
    )(*[pe[n] for n in names], *[pp[n] for n in names], gkup, gk_bias.reshape(1, -1), gnorm.reshape(1, -1), _tri_incl())
    return y, s_fin


GLA_STEP_SEQS = 16


def _gla_step_body(q_ref, k_ref, v_ref, r_ref, lr_ref, ga_ref, gkup_ref, gkb_ref, gn_ref, s_in_ref,
                   y_ref, s_out_ref, at_ref, kt_ref, qt_ref):
    g = pl.program_id(0)
    nseq = q_ref.shape[0]

    @pl.when(g == 0)
    def _():
        x = jnp.dot(lr_ref[...].astype(BF16), gkup_ref[...], preferred_element_type=F32) + gkb_ref[...]
        a = jnp.exp(_log_sigmoid(x) * (1.0 / GLA_TAU))
        for h in range(GLA_HEADS):
            dk = slice(h * GLA_DK, (h + 1) * GLA_DK)
            at_ref[h] = jnp.transpose(a[:, dk])
            kt_ref[h] = jnp.transpose(k_ref[:, dk].astype(F32))
            qt_ref[h] = jnp.transpose(q_ref[:, dk].astype(F32))

    lane = lax.broadcasted_iota(jnp.int32, (GLA_DK, nseq), 1)
    ones = jnp.ones((nseq, GLA_DV), BF16)
    grp = pl.ds(pl.multiple_of(g * GLA_STEP_SEQS, GLA_STEP_SEQS), GLA_STEP_SEQS)
    r_grp = r_ref[grp, :].astype(F32)
    ga_grp = ga_ref[grp, :].astype(F32)
    for i in range(GLA_STEP_SEQS):
        n = g * GLA_STEP_SEQS + i
        sel = lane == n
        for h in range(GLA_HEADS):
            dv = slice(h * GLA_DV, (h + 1) * GLA_DV)
            a_sel = jnp.where(sel, at_ref[h], 0.0)
            k_sel = jnp.where(sel, kt_ref[h], 0.0).astype(BF16)
            q_sel = jnp.where(sel, qt_ref[h], 0.0).astype(BF16)
            decay = _split_dot_rhs(a_sel, ones)
            kv = jnp.dot(k_sel, v_ref[:, dv], preferred_element_type=F32)
            q_b = jnp.dot(q_sel, ones, preferred_element_type=F32)
            s_new = decay * s_in_ref[i, h] + kv
            s_out_ref[i, h] = s_new
            o = jnp.sum(q_b * s_new, axis=0, keepdims=True)
            o = o * lax.rsqrt(jnp.mean(o * o, axis=-1, keepdims=True) + LN_EPS) * gn_ref[...]
            r = r_grp[i:i + 1, dv]
            ga = ga_grp[i:i + 1, dv]
            y_ref[i:i + 1, dv] = (o * (r * _sigmoid(r)) * _sigmoid(ga)).astype(y_ref.dtype)


def _split_dot_rhs(x, b01):
    hi = x.astype(BF16)
    lo = (x - hi.astype(F32)).astype(BF16)
    return jnp.dot(hi, b01, preferred_element_type=F32) + jnp.dot(lo, b01, preferred_element_type=F32)


def _gla_step(pe, gk_up, gk_bias, gnorm, state):
    nseq = state.shape[0]
    assert nseq == BLK and nseq % GLA_STEP_SEQS == 0
    names = ("qg", "kg", "vg", "rg", "lr", "ga")
    gkup = jnp.concatenate([gk_up, jnp.zeros((LANES - GLA_RANK, gk_up.shape[1]), gk_up.dtype)], axis=0).astype(BF16)
    t_specs = [pl.BlockSpec((nseq, pe[n].shape[1]), lambda g: (0, 0)) for n in names]
    st_spec = pl.BlockSpec((GLA_STEP_SEQS, GLA_HEADS, GLA_DK, GLA_DV), lambda g: (g, 0, 0, 0))
    return pl.pallas_call(
        _gla_step_body,
        grid=(nseq // GLA_STEP_SEQS,),
        in_specs=t_specs + [
            pl.BlockSpec((LANES, GLA_HEADS * GLA_DK), lambda g: (0, 0)),
            pl.BlockSpec((1, GLA_HEADS * GLA_DK), lambda g: (0, 0)),
            pl.BlockSpec((1, GLA_DV), lambda g: (0, 0)),
            st_spec,
        ],
        out_specs=[pl.BlockSpec((GLA_STEP_SEQS, D_MODEL), lambda g: (g, 0)), st_spec],
        out_shape=[jax.ShapeDtypeStruct((nseq, D_MODEL), F32), jax.ShapeDtypeStruct(state.shape, F32)],
        scratch_shapes=[pltpu.VMEM((GLA_HEADS, GLA_DK, nseq), F32) for _ in range(3)],
        compiler_params=_cparams(("arbitrary",)),
        name="gla_step",
    )(*[pe[n] for n in names], gkup, gk_bias.reshape(1, -1), gnorm.reshape(1, -1), state)


HALF = LANES // 2


def _rel_bucket(dist):
    max_exact = REL_BUCKETS // 2
    d = jnp.maximum(dist, 0)
    large = max_exact + (jnp.log(jnp.maximum(d, 1).astype(F32) / max_exact)
                         / math.log(REL_MAX_DIST / max_exact) * (REL_BUCKETS - max_exact)).astype(jnp.int32)
    large = jnp.minimum(large, REL_BUCKETS - 1)
    return jnp.where(d < max_exact, d, large)


def _bias_lookup(rel_bias, dist):
    onehot = (_rel_bucket(dist)[..., None] == jnp.arange(REL_BUCKETS)).astype(F32)
    return jnp.einsum("...b,bh->h...", onehot, rel_bias.astype(F32), precision=lax.Precision.HIGHEST)


def _swa_bias_tables(rel_bias):
    q = jnp.arange(BLK)[:, None]
    c = jnp.arange(2 * BLK)[None, :]
    dist = BLK + q - c
    bias = _bias_lookup(rel_bias, dist)
    inside = (dist >= 0) & (dist < WINDOW)
    first = inside & (c >= BLK - N_META)
    neg = jnp.float32(-jnp.inf)
    return jnp.stack([jnp.where(first[None], bias, neg), jnp.where(inside[None], bias, neg)])


def _half_tiles(x):
    lane = lax.broadcasted_iota(jnp.int32, (x.shape[0], LANES), 1)
    low = lane < HALF
    out = []
    for t in range(2):
        tile = x[:, t * LANES:(t + 1) * LANES]
        swapped = pltpu.roll(tile, HALF, axis=1)
        zero = jnp.zeros_like(tile)
        even = (jnp.where(low, tile, zero).astype(BF16), jnp.where(low, zero, swapped).astype(BF16))
        odd = (jnp.where(low, swapped, zero).astype(BF16), jnp.where(low, zero, tile).astype(BF16))
        out += [even, odd]
    return out


def _dup_tiles(x):
    lane = lax.broadcasted_iota(jnp.int32, (x.shape[0], LANES), 1)
    low = lane < HALF
    out = []
    for t in range(2):
        tile = x[:, t * LANES:(t + 1) * LANES]
        swapped = pltpu.roll(tile, HALF, axis=1)
        out += [jnp.where(low, tile, swapped).astype(BF16), jnp.where(low, swapped, tile).astype(BF16)]
    return out


def _swa_body(nblk, q_ref, kc_ref, vc_ref, kp_ref, vp_ref, km_ref, vm_ref, gb_ref, yg_ref, tb_ref, sink_ref,
              o_ref, kw_ref, vw_ref):
    blk = pl.program_id(1)
    first = blk == 0

    @pl.when(blk == nblk - 1)
    def _():
        kw_ref[...] = kc_ref[...]
        vw_ref[...] = vc_ref[...]

    k_prev = jnp.where(first, km_ref[...], kp_ref[...])
    v_prev = jnp.where(first, vm_ref[...], vp_ref[...])
    k_tiles = _dup_tiles(jnp.concatenate([k_prev, kc_ref[...]], axis=0))
    v_tiles = _dup_tiles(jnp.concatenate([v_prev, vc_ref[...]], axis=0))
    variant = jnp.minimum(blk, 1)
    npair = GQA_GROUP // 2
    low = lax.broadcasted_iota(jnp.int32, (BLK, LANES), 1) < HALF
    seg = lax.broadcasted_iota(jnp.int32, (GQA_GROUP * BLK, 1), 0) // BLK
    nt = (((1,), (1,)), ((), ()))
    for j in range(ATT_KV_HEADS):
        tiles = [j * npair + pair for pair in range(npair)]
        q_t = [q_ref[:, t * LANES:(t + 1) * LANES] for t in tiles]
        zero = jnp.zeros_like(q_t[0])
        q_st = jnp.concatenate([jnp.where(low, q, zero) for q in q_t] + [jnp.where(low, zero, q) for q in q_t], axis=0)
        heads = [2 * t for t in tiles] + [2 * t + 1 for t in tiles]
        s = lax.dot_general(q_st, k_tiles[j], nt, preferred_element_type=F32)
        s = s + jnp.concatenate([tb_ref[variant, h] for h in heads], axis=0)
        sink = jnp.full((GQA_GROUP * BLK, 1), sink_ref[heads[0]], F32)
        for i in range(1, GQA_GROUP):
            sink = jnp.where(seg == i, sink_ref[heads[i]], sink)
        m = jnp.maximum(jnp.max(s, axis=-1, keepdims=True), sink)
        p = jnp.exp(s - m)
        inv = 1.0 / (jnp.sum(p, axis=-1, keepdims=True) + jnp.exp(sink - m))
        o = jnp.dot(p.astype(BF16), v_tiles[j], preferred_element_type=F32) * inv
        for pair, t in enumerate(tiles):
            cols = slice(t * LANES, (t + 1) * LANES)
            gate = _sigmoid(gb_ref[:, cols].astype(F32))
            even = o[pair * BLK:(pair + 1) * BLK]
            odd = o[(npair + pair) * BLK:(npair + pair + 1) * BLK]
            o_ref[:, cols] = (gate * jnp.where(low, even, odd) + yg_ref[:, cols].astype(F32)).astype(o_ref.dtype)


def _swa_prompt(pp, pe, yg, rel_bias, sinks, nbatch, nblk):
    tb = _swa_bias_tables(rel_bias)
    kvw = ATT_KV_HEADS * HEAD_DIM
    cur = lambda b, c: (b * nblk + c, 0)
    prev = lambda b, c: (b * nblk + jnp.maximum(c - 1, 0), 0)
    return pl.pallas_call(
        functools.partial(_swa_body, nblk),
        grid=(nbatch, nblk),
        in_specs=[
            pl.BlockSpec((BLK, D_MODEL), cur),
            pl.BlockSpec((BLK, kvw), cur), pl.BlockSpec((BLK, kvw), cur),
            pl.BlockSpec((BLK, kvw), prev), pl.BlockSpec((BLK, kvw), prev),
            pl.BlockSpec((BLK, kvw), lambda b, c: (1, 0)), pl.BlockSpec((BLK, kvw), lambda b, c: (1, 0)),
            pl.BlockSpec((BLK, D_MODEL), cur),
            pl.BlockSpec((BLK, D_MODEL), cur),
            pl.BlockSpec(tb.shape, lambda b, c: (0, 0, 0, 0)),
            pl.BlockSpec(memory_space=pltpu.SMEM),
        ],
        out_specs=[pl.BlockSpec((BLK, D_MODEL), cur),
                   pl.BlockSpec((None, BLK, kvw), lambda b, c: (b, 0, 0)),
                   pl.BlockSpec((None, BLK, kvw), lambda b, c: (b, 0, 0))],
        out_shape=[jax.ShapeDtypeStruct((nbatch * nblk * BLK, D_MODEL), BF16),
                   jax.ShapeDtypeStruct((nbatch, BLK, kvw), F32), jax.ShapeDtypeStruct((nbatch, BLK, kvw), F32)],
        compiler_params=_cparams(("arbitrary", "arbitrary")),
        name="swa_prompt",
    )(pp["qa"], pp["ka"], pp["va"], pp["ka"], pp["va"], pe["ka"], pe["va"], pp["gb"], yg, tb, sinks)


SWA_STEP_SEQS = 8
Q_TILES = ATT_HEADS // 2


def _swa_step_body(q_ref, kn_ref, vn_ref, ck_ref, cv_ref, gb_ref, yg_ref, tb_ref, sink_ref,
                   o_ref, ko_ref, vo_ref):
    row = lax.broadcasted_iota(jnp.int32, (WINDOW, ATT_KV_HEADS * HEAD_DIM), 0)
    low = lax.broadcasted_iota(jnp.int32, (Q_TILES, LANES), 1) < HALF
    mine16 = (lax.broadcasted_iota(jnp.int32, (2 * Q_TILES, LANES), 0) % Q_TILES) // (GQA_GROUP // 2)
    nt = (((1,), (1,)), ((), ()))
    for i in range(SWA_STEP_SEQS):
        k_win = jnp.where(row == WINDOW - 1, kn_ref[i:i + 1, :], pltpu.roll(ck_ref[i], WINDOW - 1, axis=0))
        v_win = jnp.where(row == WINDOW - 1, vn_ref[i:i + 1, :], pltpu.roll(cv_ref[i], WINDOW - 1, axis=0))
        ko_ref[i] = k_win
        vo_ref[i] = v_win
        k_tiles = _dup_tiles(k_win)
        v_tiles = _dup_tiles(v_win)
        q8 = q_ref[i]
        q16 = jnp.concatenate([jnp.where(low, q8, 0.0), jnp.where(low, 0.0, q8)], axis=0).astype(BF16)
        s = jnp.zeros((2 * Q_TILES, WINDOW), F32)
        for j in range(ATT_KV_HEADS):
            sj = lax.dot_general(q16, k_tiles[j], nt, preferred_element_type=F32)
            s = jnp.where(mine16 == j, sj, s)
        s = s + tb_ref[...]
        sink = sink_ref[...]
        m = jnp.maximum(jnp.max(s, axis=-1, keepdims=True), sink)
        p = jnp.exp(s - m)
        inv = 1.0 / (jnp.sum(p, axis=-1, keepdims=True) + jnp.exp(sink - m))
        p = p.astype(BF16)
        o = jnp.zeros((2 * Q_TILES, LANES), F32)
        for j in range(ATT_KV_HEADS):
            o = jnp.where(mine16 == j, jnp.dot(p, v_tiles[j], preferred_element_type=F32), o)
        o = o * inv
        o = jnp.where(low, o[:Q_TILES], o[Q_TILES:])
        o_ref[i] = _sigmoid(gb_ref[i]) * o + yg_ref[i]


def _swa_step(pe, yg_s, cache_k, cache_v, rel_bias, sinks):
    nseq = cache_k.shape[0]
    kvw = ATT_KV_HEADS * HEAD_DIM
    as_tiles = lambda x: x[:nseq].astype(F32).reshape(nseq, Q_TILES, LANES)
    dist = (WINDOW - 1) - jnp.arange(WINDOW)
    bias = _bias_lookup(rel_bias, dist)
    tb = jnp.concatenate([bias[0::2], bias[1::2]], axis=0)
    sk = jnp.concatenate([sinks[0::2], sinks[1::2]])[:, None].astype(F32)
    g = SWA_STEP_SEQS
    tile_spec = pl.BlockSpec((g, Q_TILES, LANES), lambda s: (s, 0, 0))
    win_spec = pl.BlockSpec((g, WINDOW, kvw), lambda s: (s, 0, 0))
    new_spec = pl.BlockSpec((g, kvw), lambda s: (s, 0))
    o, ko, vo = pl.pallas_call(
        _swa_step_body,
        grid=(nseq // g,),
        in_specs=[tile_spec, new_spec, new_spec, win_spec, win_spec, tile_spec, tile_spec,
                  pl.BlockSpec(tb.shape, lambda s: (0, 0)), pl.BlockSpec(sk.shape, lambda s: (0, 0))],
        out_specs=[tile_spec, win_spec, win_spec],
        out_shape=[jax.ShapeDtypeStruct((nseq, Q_TILES, LANES), F32),
                   jax.ShapeDtypeStruct(cache_k.shape, F32), jax.ShapeDtypeStruct(cache_v.shape, F32)],
        compiler_params=_cparams(("arbitrary",)),
        name="swa_step",
    )(as_tiles(pe["qa"]), pe["ka"], pe["va"], cache_k, cache_v, as_tiles(pe["gb"]),
      yg_s.reshape(nseq, Q_TILES, LANES), tb, sk)
    return o.reshape(nseq, D_MODEL), ko, vo


ROUTER_ROWS = 40
META_ROWS = 8


def _split3_nt(a_hi, a_lo, x):
    nt = (((1,), (1,)), ((), ()))
    x_hi = x.astype(BF16)
    x_lo = (x - x_hi.astype(F32)).astype(BF16)
    return (lax.dot_general(a_hi, x_hi, nt, preferred_element_type=F32)
            + lax.dot_general(a_hi, x_lo, nt, preferred_element_type=F32)
            + lax.dot_general(a_lo, x_hi, nt, preferred_element_type=F32))


def _first_argmax_rows(v, ridx, nrows):
    vmax = jnp.max(v, axis=0, keepdims=True)
    idx = jnp.min(jnp.where(v == vmax, ridx, nrows), axis=0, keepdims=True)
    return vmax, idx


def _post_body(nsteps, *refs):
    h1_ref, meta_ref, wcol_ref = refs[-5:-2]
    i = pl.program_id(0)

    @pl.when(i < nsteps)
    def _():
        _post_tile(i, *refs)

    @pl.when(i >= nsteps)
    def _():
        h1_ref[...] = jnp.zeros_like(h1_ref)
        meta_ref[...] = jnp.zeros_like(meta_ref)
        wcol_ref[...] = jnp.zeros_like(wcol_ref)


def _post_tile(i, mg_ref, x_ref, lng_ref, lnb_ref, wo_ref, g1_ref, b1_ref, wrh_ref, wrl_ref, rb_ref, ut_ref,
               cin_ref, *rest):
    h1_ref, meta_ref, wcol_ref, cout_ref, carry_ref = rest[-5:]

    @pl.when(i == 0)
    def _():
        carry_ref[...] = cin_ref[...]

    tm = x_ref.shape[0]
    h = _layer_norm_rows(x_ref[...], lng_ref[...], lnb_ref[...])
    acc = jnp.dot(mg_ref[...].astype(BF16), wo_ref[...], preferred_element_type=F32)
    h1 = _layer_norm_rows(ALPHA * h + acc, g1_ref[...], b1_ref[...])
    h1_ref[...] = h1

    lt = _split3_nt(wrh_ref[...], wrl_ref[...], h1) + rb_ref[:, :1]
    ridx = lax.broadcasted_iota(jnp.int32, (EXPERTS_PER_GROUP, tm), 0)
    neg = jnp.float32(-jnp.inf)
    g_log = jnp.where(ridx < N_GROUPS, lt[N_EXPERTS:N_EXPERTS + EXPERTS_PER_GROUP], neg)
    g_max, grp = _first_argmax_rows(g_log, ridx, EXPERTS_PER_GROUP)
    p_grp = 1.0 / jnp.sum(jnp.exp(g_log - g_max), axis=0, keepdims=True)
    e_in = lt[0:EXPERTS_PER_GROUP]
    for gi in range(1, N_GROUPS):
        e_in = jnp.where(grp == gi, lt[gi * EXPERTS_PER_GROUP:(gi + 1) * EXPERTS_PER_GROUP], e_in)
    v0, i0 = _first_argmax_rows(e_in, ridx, EXPERTS_PER_GROUP)
    v1, i1 = _first_argmax_rows(jnp.where(ridx == i0, neg, e_in), ridx, EXPERTS_PER_GROUP)
    t = jnp.exp(v1 - v0)
    w0 = p_grp / (1.0 + t)
    w1 = p_grp * t / (1.0 + t)
    e0 = grp * EXPERTS_PER_GROUP + i0
    e1 = grp * EXPERTS_PER_GROUP + i1

    eidx = lax.broadcasted_iota(jnp.int32, (N_EXPERTS, tm), 0)
    hit0 = eidx == e0
    hit1 = eidx == e1
    oh = jnp.where(jnp.logical_or(hit0, hit1), 1.0, 0.0)
    before = jnp.dot(oh.astype(BF16), ut_ref[...], preferred_element_type=F32) + carry_ref[:, :1]
    r0 = jnp.sum(jnp.where(hit0, before, 0.0), axis=0, keepdims=True).astype(jnp.int32)
    r1 = jnp.sum(jnp.where(hit1, before, 0.0), axis=0, keepdims=True).astype(jnp.int32)
    carry_ref[...] = carry_ref[...] + jnp.sum(oh, axis=1, keepdims=True)
    cout_ref[...] = carry_ref[...]

    zi = jnp.zeros((META_ROWS - 4, tm), jnp.int32)
    meta_ref[...] = jnp.concatenate([e0, e1, r0, r1, zi], axis=0)
    wt = jnp.concatenate([w0, w1, jnp.zeros((LANES - 2, tm), F32)], axis=0)
    wcol_ref[...] = jnp.transpose(wt)


def _post(mg, x2d, prm, tm, row0, total_rows, carry_in, prev=None, zero_tail=False):
    m = x2d.shape[0]
    assert m % tm == 0 and row0 % tm == 0 and total_rows % tm == 0
    off = row0 // tm
    nsteps = m // tm
    last = nsteps - 1
    ut = jnp.asarray(np.triu(np.ones((tm, tm), np.float32), 1), dtype=BF16)
    full = lambda shape: pl.BlockSpec(shape, lambda i: (0,) * len(shape))
    in_specs = [
        pl.BlockSpec((tm, D_MODEL), lambda i: (jnp.minimum(i, last), 0)),
        pl.BlockSpec((tm, D_MODEL), lambda i: (jnp.minimum(i, last), 0)),
        full((1, D_MODEL)), full((1, D_MODEL)),
        full((D_MODEL, D_MODEL)),
        full((1, D_MODEL)), full((1, D_MODEL)),
        full((ROUTER_ROWS, D_MODEL)), full((ROUTER_ROWS, D_MODEL)), full((ROUTER_ROWS, LANES)),
        full((tm, tm)),
        full((N_EXPERTS, LANES)),
    ]
    args = [mg, x2d, prm["ln_emb_g"], prm["ln_emb_b"], prm["w_out"], prm["ln1_g"], prm["ln1_b"],
            prm["wr_hi"], prm["wr_lo"], prm["r_bias"], ut, carry_in]
    aliases = {}
    if prev is not None:
        for k, buf in enumerate(prev):
            in_specs.append(pl.BlockSpec(memory_space=pl.ANY))
            aliases[len(args)] = k
            args.append(buf)
    out_shape = [
        jax.ShapeDtypeStruct((total_rows, D_MODEL), F32),
        jax.ShapeDtypeStruct((META_ROWS, total_rows), jnp.int32),
        jax.ShapeDtypeStruct((total_rows, LANES), F32),
        jax.ShapeDtypeStruct((N_EXPERTS, LANES), F32),
    ]
    out_specs = [
        pl.BlockSpec((tm, D_MODEL), lambda i: (i + off, 0)),
        pl.BlockSpec((META_ROWS, tm), lambda i: (0, i + off)),
        pl.BlockSpec((tm, LANES), lambda i: (i + off, 0)),
        full((N_EXPERTS, LANES)),
    ]
    if prev is not None:
        assert len(prev) == 3
    return pl.pallas_call(
        functools.partial(_post_body, nsteps),
        grid=(nsteps + int(zero_tail),),
        in_specs=in_specs,
        out_specs=out_specs,
        out_shape=out_shape,
        input_output_aliases=aliases,
        scratch_shapes=[pltpu.VMEM((N_EXPERTS, LANES), F32)],
        compiler_params=_cparams(("arbitrary",)),
        name="post_attn",
    )(*args)


def _prep_post_params(ln_emb_g, ln_emb_b, w_out, ln1_g, ln1_b, w_rg, b_rg, w_re, b_re):
    row = lambda v: v.reshape(1, -1)
    wr = jnp.concatenate([w_re.T, w_rg.T, jnp.zeros((ROUTER_ROWS - N_EXPERTS - N_GROUPS, D_MODEL), F32)], axis=0)
    wr_hi = wr.astype(BF16)
    wr_lo = (wr - wr_hi.astype(F32)).astype(BF16)
    rb = jnp.concatenate([b_re, b_rg, jnp.zeros((ROUTER_ROWS - N_EXPERTS - N_GROUPS,), F32)])
    return dict(ln_emb_g=row(ln_emb_g), ln_emb_b=row(ln_emb_b), w_out=w_out.astype(BF16), ln1_g=row(ln1_g),
                ln1_b=row(ln1_b), wr_hi=wr_hi, wr_lo=wr_lo,
                r_bias=jnp.broadcast_to(rb[:, None], (ROUTER_ROWS, LANES)))


MOE_ROWS = 256
SUBLANES = 8
assert D_MODEL == SUBLANES * LANES


def _store_rows_as_tiles(ref, x):
    n = x.shape[0]
    for c in range(SUBLANES):
        ref[pl.ds(c, n, stride=SUBLANES), :] = x[:, c * LANES:(c + 1) * LANES]


def _load_rows_from_tiles(ref, n):
    return jnp.concatenate([ref[pl.ds(c, n, stride=SUBLANES), :] for c in range(SUBLANES)], axis=1)


def _tile_of_row(ref, r):
    return ref.at[pl.ds(pl.multiple_of(r * SUBLANES, SUBLANES), SUBLANES)]


def _moe_plan(counts, total_assign):
    nb_max = -(-total_assign // MOE_ROWS) + N_EXPERTS
    padded = (counts + MOE_ROWS - 1) // MOE_ROWS * MOE_ROWS
    pend = jnp.cumsum(padded)
    pstart = (pend - padded).astype(jnp.int32)
    block_start = jnp.arange(nb_max, dtype=jnp.int32) * MOE_ROWS
    n_ended = jnp.sum((pend[None, :] <= block_start[:, None]).astype(jnp.int32), axis=1)
    block_e = jnp.minimum(n_ended, N_EXPERTS - 1).astype(jnp.int32)
    n_used = (pend[-1] // MOE_ROWS).astype(jnp.int32).reshape(1)
    tail_start = jnp.where(padded > 0, pend - MOE_ROWS, -1)
    spare = pend[-1] + jnp.arange(N_EXPERTS, dtype=pend.dtype) * MOE_ROWS
    spare = jnp.where(spare < nb_max * MOE_ROWS, spare, -1)
    zero_blocks = jnp.concatenate([tail_start, spare]).astype(jnp.int32)
    return pstart, block_e, n_used, nb_max, zero_blocks


ROW_UNROLL = 8


def _slot_ids(meta, pstart):
    experts = meta[0:TOP_K]
    ranks = meta[TOP_K:2 * TOP_K]
    onehot = experts[..., None] == jnp.arange(N_EXPERTS, dtype=jnp.int32)
    return ranks + jnp.sum(jnp.where(onehot, pstart, 0), axis=-1)


def _for_row_groups(tm, fn):
    def group(g, c):
        t0 = pl.multiple_of(g * ROW_UNROLL, ROW_UNROLL)
        for r in range(ROW_UNROLL):
            for k in range(TOP_K):
                fn(t0 + r, k)
        return c

    lax.fori_loop(0, tm // ROW_UNROLL, group, 0)


def _dispatch_body(nsteps, tail_ref, s0_ref, s1_ref, h_ref, xs_ref, pk_ref, zero_ref, sems, zsem):
    i = pl.program_id(0)
    tm = h_ref.shape[0]
    slot_refs = (s0_ref, s1_ref)
    cur = i % 2

    blk_tiles = MOE_ROWS * SUBLANES

    @pl.when(i == 0)
    def _():
        zero_ref[...] = jnp.zeros_like(zero_ref)
        for e in range(2 * N_EXPERTS):
            @pl.when(tail_ref[e] >= 0)
            def _():
                start = pl.multiple_of(tail_ref[e] * SUBLANES, blk_tiles)
                pltpu.make_async_copy(zero_ref, xs_ref.at[pl.ds(start, blk_tiles)], zsem).start()
        for e in range(2 * N_EXPERTS):
            @pl.when(tail_ref[e] >= 0)
            def _():
                pltpu.make_async_copy(zero_ref, xs_ref.at[pl.ds(0, blk_tiles)], zsem).wait()

    _store_rows_as_tiles(pk_ref.at[cur], h_ref[...])

    def send(t, k):
        pltpu.make_async_copy(_tile_of_row(pk_ref.at[cur], t), _tile_of_row(xs_ref, slot_refs[k][0, t]),
                              sems.at[cur]).start(priority=k)

    def wait_buffer(buf):
        _for_row_groups(tm, lambda t, k: pltpu.make_async_copy(
            _tile_of_row(pk_ref.at[buf], t), _tile_of_row(xs_ref, 0), sems.at[buf]).wait())

    _for_row_groups(tm, send)

    @pl.when(i > 0)
    def _():
        wait_buffer(1 - cur)

    @pl.when(i == nsteps - 1)
    def _():
        wait_buffer(cur)


def _dispatch(h1, slot_ids, tail_start, nslots, tm, total):
    assert total % tm == 0 and total <= h1.shape[0] and tm % ROW_UNROLL == 0
    slot_spec = pl.BlockSpec((1, tm), lambda i, tl: (0, i), memory_space=pltpu.SMEM)
    return pl.pallas_call(
        functools.partial(_dispatch_body, total // tm),
        grid_spec=pltpu.PrefetchScalarGridSpec(
            num_scalar_prefetch=1,
            grid=(total // tm,),
            in_specs=[slot_spec, slot_spec, pl.BlockSpec((tm, D_MODEL), lambda i, tl: (i, 0))],
            out_specs=pl.BlockSpec(memory_space=pl.ANY),
            scratch_shapes=[pltpu.VMEM((2, tm * SUBLANES, LANES), F32), pltpu.VMEM((MOE_ROWS * SUBLANES, LANES), F32),
                            pltpu.SemaphoreType.DMA((2,)), pltpu.SemaphoreType.DMA(())],
        ),
        out_shape=jax.ShapeDtypeStruct((nslots * SUBLANES, LANES), F32),
        compiler_params=_cparams(("arbitrary",)),
        name="moe_dispatch",
    )(tail_start, slot_ids[0:1], slot_ids[1:2], h1)


def _expert_schedule(block_e, n_used):
    nb = block_e.shape[0]
    idx = jnp.arange(nb, dtype=jnp.int32)
    first = (idx < n_used[0]) & ((idx == 0) | (block_e != jnp.roll(block_e, 1)))
    parity = (jnp.cumsum(first.astype(jnp.int32)) - 1) % 2
    pos = jnp.where(first, idx, nb)
    at_or_after = jnp.flip(lax.cummin(jnp.flip(pos)))
    nxt = jnp.concatenate([at_or_after[1:], jnp.full((1,), nb, jnp.int32)])
    nexte = jnp.where(nxt < nb, block_e[jnp.minimum(nxt, nb - 1)], -1)
    return first.astype(jnp.int32), nexte.astype(jnp.int32), parity.astype(jnp.int32)


def _expert_body(be_ref, nu_ref, first_ref, nexte_ref, par_ref, xs_ref, wg_hbm, wu_hbm, wd_hbm, ys_ref,
                 wgf_ref, wuf_ref, wdf_ref, wgb_ref, wub_ref, wdb_ref, sems):
    i = pl.program_id(0)
    hbm = (wg_hbm, wu_hbm, wd_hbm)
    stage = (wgf_ref, wuf_ref, wdf_ref)

    def weight_copies(e, buf):
        return [pltpu.make_async_copy(hbm[w].at[e], stage[w].at[buf], sems.at[buf, w]) for w in range(3)]

    @pl.when(first_ref[i] == 1)
    def _():
        buf = par_ref[i]

        @pl.when(i == 0)
        def _():
            for c in weight_copies(be_ref[0], buf):
                c.start()

        for c in weight_copies(be_ref[i], buf):
            c.wait()

        @pl.when(nexte_ref[i] >= 0)
        def _():
            for c in weight_copies(nexte_ref[i], 1 - buf):
                c.start()

        wgb_ref[...] = wgf_ref[buf].astype(BF16)
        wub_ref[...] = wuf_ref[buf].astype(BF16)
        wdb_ref[...] = wdf_ref[buf].astype(BF16)

    @pl.when(i < nu_ref[0])
    def _():
        x = _load_rows_from_tiles(xs_ref, MOE_ROWS).astype(BF16)
        g = jnp.dot(x, wgb_ref[...], preferred_element_type=F32)
        hb = (g * _sigmoid(g)) * jnp.dot(x, wub_ref[...], preferred_element_type=F32)
        y = jnp.dot(hb.astype(BF16), wdb_ref[...], preferred_element_type=F32)
        _store_rows_as_tiles(ys_ref, y)

    @pl.when(i >= nu_ref[0])
    def _():
        ys_ref[...] = jnp.zeros_like(ys_ref)


def _experts(xs, block_e, n_used, w_gate, w_up, w_down, nb_max):
    first, nexte, parity = _expert_schedule(block_e, n_used)
    rows = lambda i, *_: (i, 0)
    used_rows = lambda i, be, nu, *_: (jnp.maximum(jnp.minimum(i, nu[0] - 1), 0), 0)
    any_spec = pl.BlockSpec(memory_space=pl.ANY)
    blk = (MOE_ROWS * SUBLANES, LANES)
    return pl.pallas_call(
        _expert_body,
        grid_spec=pltpu.PrefetchScalarGridSpec(
            num_scalar_prefetch=5,
            grid=(nb_max,),
            in_specs=[pl.BlockSpec(blk, used_rows), any_spec, any_spec, any_spec],
            out_specs=pl.BlockSpec(blk, rows),
            scratch_shapes=[
                pltpu.VMEM((2, D_MODEL, D_EXPERT), F32), pltpu.VMEM((2, D_MODEL, D_EXPERT), F32),
                pltpu.VMEM((2, D_EXPERT, D_MODEL), F32),
                pltpu.VMEM((D_MODEL, D_EXPERT), BF16), pltpu.VMEM((D_MODEL, D_EXPERT), BF16),
                pltpu.VMEM((D_EXPERT, D_MODEL), BF16),
                pltpu.SemaphoreType.DMA((2, 3)),
            ],
        ),
        out_shape=jax.ShapeDtypeStruct(xs.shape, F32),
        compiler_params=_cparams(("arbitrary",)),
        name="moe_experts",
    )(block_e, n_used, first, nexte, parity, xs, w_gate, w_up, w_down)


def _combine_body(nsteps, s0_ref, s1_ref, n0_ref, n1_ref, h_ref, w_ref, g2_ref, b2_ref, ys_ref, o_ref, buf_ref, sems):
    i = pl.program_id(0)
    tm = h_ref.shape[0]
    cur = i % 2

    def fetch(slot_refs, buf):
        _for_row_groups(tm, lambda t, k: pltpu.make_async_copy(
            _tile_of_row(ys_ref, slot_refs[k][0, t]), _tile_of_row(buf_ref.at[buf, k], t),
            sems.at[buf]).start(priority=k))

    @pl.when(i == 0)
    def _():
        fetch((s0_ref, s1_ref), cur)

    @pl.when(i + 1 < nsteps)
    def _():
        fetch((n0_ref, n1_ref), 1 - cur)

    _for_row_groups(tm, lambda t, k: pltpu.make_async_copy(
        _tile_of_row(ys_ref, 0), _tile_of_row(buf_ref.at[cur, k], t), sems.at[cur]).wait())
    w = w_ref[...]
    f = (w[:, 0:1] * _load_rows_from_tiles(buf_ref.at[cur, 0], tm)
         + w[:, 1:2] * _load_rows_from_tiles(buf_ref.at[cur, 1], tm))
    o_ref[...] = _layer_norm_rows(ALPHA * h_ref[...] + f, g2_ref[...], b2_ref[...])


def _combine(h1, wcol, slot_ids, ys, ln2_g, ln2_b, tm, row0, nrows):
    assert nrows % tm == 0 and row0 % tm == 0 and tm % ROW_UNROLL == 0
    off = row0 // tm
    nsteps = nrows // tm
    slot_spec = pl.BlockSpec((1, tm), lambda i: (0, i + off), memory_space=pltpu.SMEM)
    next_spec = pl.BlockSpec((1, tm), lambda i: (0, jnp.minimum(i + 1, nsteps - 1) + off), memory_space=pltpu.SMEM)
    return pl.pallas_call(
        functools.partial(_combine_body, nsteps),
        grid=(nsteps,),
        in_specs=[
            slot_spec, slot_spec, next_spec, next_spec,
            pl.BlockSpec((tm, D_MODEL), lambda i: (i + off, 0)),
            pl.BlockSpec((tm, LANES), lambda i: (i + off, 0)),
            pl.BlockSpec((1, D_MODEL), lambda i: (0, 0)),
            pl.BlockSpec((1, D_MODEL), lambda i: (0, 0)),
            pl.BlockSpec(memory_space=pl.ANY),
        ],
        out_specs=pl.BlockSpec((tm, D_MODEL), lambda i: (i, 0)),
        scratch_shapes=[pltpu.VMEM((2, TOP_K, tm * SUBLANES, LANES), F32), pltpu.SemaphoreType.DMA((2,))],
        out_shape=jax.ShapeDtypeStruct((nrows, D_MODEL), F32),
        compiler_params=_cparams(("arbitrary",)),
        name="moe_combine",
    )(slot_ids[0:1], slot_ids[1:2], slot_ids[0:1], slot_ids[1:2], h1, wcol, ln2_g.reshape(1, -1),
      ln2_b.reshape(1, -1), ys)


PROJ_ROWS = 512
POST_ROWS = 512
DISPATCH_ROWS = 384
COMBINE_ROWS = 256


def kernel(x_prompt, x_sample, state_gla, cache_swa_k, cache_swa_v, meta_tokens, ln_emb_g, ln_emb_b, rel_bias, w_in,
           gk_up, gk_bias, gla_norm_g, sinks, w_out, ln1_g, ln1_b, w_router_group, b_router_group, w_router_expert,
           b_router_expert, w_gate, w_up, w_down, ln2_g, ln2_b):
    nbatch, seq, d = x_prompt.shape
    nseq = x_sample.shape[0]
    assert w_in.shape[0] == DEPTH == 1 and d == D_MODEL and x_sample.shape[1] == 1
    assert seq % BLK == 0 and nseq == BLK and meta_tokens.shape[0] == N_META
    nblk = seq // BLK
    n_prompt = nbatch * seq
    total = n_prompt + nseq
    kvw = ATT_KV_HEADS * HEAD_DIM

    xp = x_prompt.reshape(n_prompt, d)
    xs = x_sample.reshape(nseq, d)
    extra = jnp.concatenate([xs, jnp.zeros((BLK - N_META, d), xs.dtype), meta_tokens.astype(xs.dtype)], axis=0)
    w_cat = _prep_w_in(w_in[0])
    pp = _ln_proj(xp, ln_emb_g, ln_emb_b, w_cat, PROJ_ROWS)
    pe = _ln_proj(extra, ln_emb_g, ln_emb_b, w_cat, 2 * BLK)

    yg, gla_p = _gla_prompt(pp, pe, gk_up[0], gk_bias[0], gla_norm_g[0], nbatch, nblk)
    yg_s, gla_s = _gla_step(pe, gk_up[0], gk_bias[0], gla_norm_g[0], state_gla[0])
    mg, k_win, v_win = _swa_prompt(pp, pe, yg, rel_bias, sinks[0], nbatch, nblk)
    mg_s, k_s, v_s = _swa_step(pe, yg_s, cache_swa_k[0].reshape(nseq, WINDOW, kvw),
                               cache_swa_v[0].reshape(nseq, WINDOW, kvw), rel_bias, sinks[0])

    prm = _prep_post_params(ln_emb_g, ln_emb_b, w_out[0], ln1_g[0], ln1_b[0], w_router_group[0], b_router_group[0],
                            w_router_expert[0], b_router_expert[0])
    carry0 = jnp.zeros((N_EXPERTS, LANES), F32)
    rows_alloc = n_prompt + POST_ROWS
    h1, meta, wcol, carry1 = _post(mg, xp, prm, POST_ROWS, 0, rows_alloc, carry0, zero_tail=True)
    h1, meta, wcol, carry2 = _post(mg_s, xs, prm, nseq, n_prompt, rows_alloc, carry1, prev=(h1, meta, wcol))

    counts = carry2[:, 0].astype(jnp.int32)
    pstart, block_e, n_used, nb_max, tail_start = _moe_plan(counts, TOP_K * total)
    slot_ids = _slot_ids(meta, pstart)
    xs_sorted = _dispatch(h1, slot_ids, tail_start, nb_max * MOE_ROWS, DISPATCH_ROWS, total)
    ys = _experts(xs_sorted, block_e, n_used, w_gate[0], w_up[0], w_down[0], nb_max)
    y_p = _combine(h1, wcol, slot_ids, ys, ln2_g[0], ln2_b[0], COMBINE_ROWS, 0, n_prompt)
    y_s = _combine(h1, wcol, slot_ids, ys, ln2_g[0], ln2_b[0], nseq, n_prompt, nseq)

    kv_shape = (1, nbatch, WINDOW, ATT_KV_HEADS, HEAD_DIM)
    k_p = k_win.reshape(kv_shape)
    v_p = v_win.reshape(kv_shape)
    return (y_p.reshape(nbatch, seq, d), y_s.reshape(nseq, 1, d), gla_p[None], k_p, v_p, gla_s[None],
            k_s.reshape(cache_swa_k.shape), v_s.reshape(cache_swa_v.shape))
```

```python
import functools
import math

import jax
import jax.numpy as jnp
import numpy as np
from jax import lax
from jax.experimental import pallas as pl
from jax.experimental.pallas import tpu as pltpu

F32 = jnp.float32
BF16 = jnp.bfloat16

D_MODEL = 1024
N_META = 16
LN_EPS = 1e-5
GLA_HEADS = 4
GLA_DK = 128
GLA_DV = 256
GLA_RANK = 16
GLA_TAU = 16.0
HEAD_DIM = 64
ATT_HEADS = 16
ATT_KV_HEADS = 4
GQA_GROUP = 4
WINDOW = 128
REL_BUCKETS = 32
REL_MAX_DIST = 128
N_GROUPS = 4
EXPERTS_PER_GROUP = 8
N_EXPERTS = 32
TOP_K = 2
D_EXPERT = 512
DEPTH = 1
ALPHA = (2.0 * DEPTH) ** 0.25

LANES = 128
BLK = 128
VMEM_LIMIT = 56 * 1024 * 1024


def _cparams(sem):
    return pltpu.CompilerParams(dimension_semantics=sem, vmem_limit_bytes=VMEM_LIMIT)


def _layer_norm_rows(x, g, b):
    mu = jnp.mean(x, axis=-1, keepdims=True)
    xc = x - mu
    var = jnp.mean(xc * xc, axis=-1, keepdims=True)
    return xc * lax.rsqrt(var + LN_EPS) * g + b


_PROJ_OUTS = (
    ("lr", LANES, F32, None),
    ("qg", GLA_HEADS * GLA_DK, BF16, GLA_DK ** -0.5),
    ("kg", GLA_HEADS * GLA_DK, BF16, None),
    ("vg", GLA_HEADS * GLA_DV, BF16, None),
    ("rg", GLA_HEADS * GLA_DV, BF16, None),
    ("qa", ATT_HEADS * HEAD_DIM, BF16, HEAD_DIM ** -0.5),
    ("ka", ATT_KV_HEADS * HEAD_DIM, F32, None),
    ("va", ATT_KV_HEADS * HEAD_DIM, F32, None),
    ("ga", D_MODEL, BF16, None),
    ("gb", D_MODEL, BF16, None),
)
_PROJ_W = sum(w for _, w, _, _ in _PROJ_OUTS)


def _prep_w_in(w_in):
    sizes = (512, 512, 1024, 1024, GLA_RANK, 1024, 256, 256, 1024, 1024)
    offs = np.cumsum((0,) + sizes)
    a = w_in[:, : offs[4]]
    lr = w_in[:, offs[4]: offs[5]]
    b = w_in[:, offs[5]:]
    pad = jnp.zeros((w_in.shape[0], LANES - GLA_RANK), w_in.dtype)
    return jnp.concatenate([lr, pad, a, b], axis=1).astype(BF16)


LAMAX_ROWS = 8


def _log_sigmoid(x):
    return jnp.minimum(x, 0.0) - jnp.log(1.0 + jnp.exp(-jnp.abs(x)))


def _ln_proj_body(x_ref, g_ref, b_ref, w_ref, gkup_ref, gkb_ref, *out_refs):
    la_ref, lamax_ref = out_refs[-2:]
    tm = x_ref.shape[0]
    xn = _layer_norm_rows(x_ref[...], g_ref[...], b_ref[...]).astype(BF16)
    c0 = 0
    plain_refs = iter(out_refs[:-2])
    for name, width, dtype, scale in _PROJ_OUTS:
        acc = jnp.dot(xn, w_ref[:, c0:c0 + width], preferred_element_type=F32)
        c0 += width
        if name == "lr":
            x = jnp.dot(acc.astype(BF16), gkup_ref[...], preferred_element_type=F32) + gkb_ref[...]
            la = _log_sigmoid(x) * (1.0 / GLA_TAU)
            la_ref[...] = la
            rows = []
            for r in range(tm // BLK):
                blk_max = jnp.max(jnp.max(jnp.abs(la[r * BLK:(r + 1) * BLK]), axis=0, keepdims=True), axis=1, keepdims=True)
                rows.append(jnp.broadcast_to(blk_max, (1, LANES)))
            rows.append(jnp.zeros((LAMAX_ROWS - tm // BLK, LANES), F32))
            lamax_ref[...] = jnp.concatenate(rows, axis=0)
            continue
        if scale is not None:
            acc = acc * scale
        next(plain_refs)[...] = acc.astype(dtype)


def _ln_proj(x2d, ln_g, ln_b, w_cat, gk_up, gk_bias, tm):
    m = x2d.shape[0]
    assert m % tm == 0 and tm % BLK == 0 and tm // BLK < LAMAX_ROWS
    names = [n for n, _, _, _ in _PROJ_OUTS if n != "lr"] + ["la", "lamax"]
    ladim = GLA_HEADS * GLA_DK
    gkup = jnp.concatenate([gk_up, jnp.zeros((LANES - GLA_RANK, ladim), gk_up.dtype)], axis=0).astype(BF16)
    out_shape = [jax.ShapeDtypeStruct((m, w), dt) for n, w, dt, _ in _PROJ_OUTS if n != "lr"]
    out_specs = [pl.BlockSpec((tm, w), lambda i: (i, 0)) for n, w, _, _ in _PROJ_OUTS if n != "lr"]
    out_shape += [jax.ShapeDtypeStruct((m, ladim), F32), jax.ShapeDtypeStruct((m // tm * LAMAX_ROWS, LANES), F32)]
    out_specs += [pl.BlockSpec((tm, ladim), lambda i: (i, 0)), pl.BlockSpec((LAMAX_ROWS, LANES), lambda i: (i, 0))]
    outs = pl.pallas_call(
        _ln_proj_body,
        grid=(m // tm,),
        in_specs=[
            pl.BlockSpec((tm, D_MODEL), lambda i: (i, 0)),
            pl.BlockSpec((1, D_MODEL), lambda i: (0, 0)),
            pl.BlockSpec((1, D_MODEL), lambda i: (0, 0)),
            pl.BlockSpec((D_MODEL, _PROJ_W), lambda i: (0, 0), pipeline_mode=pl.Buffered(1)),
            pl.BlockSpec((LANES, ladim), lambda i: (0, 0)),
            pl.BlockSpec((1, ladim), lambda i: (0, 0)),
        ],
        out_specs=out_specs,
        out_shape=out_shape,
        compiler_params=_cparams(("arbitrary",)),
        name="ln_proj",
    )(x2d, ln_g.reshape(1, -1), ln_b.reshape(1, -1), w_cat, gkup, gk_bias.reshape(1, -1))
    res = dict(zip(names, outs))
    res["lamax"] = res["lamax"].reshape(m // tm, LAMAX_ROWS, LANES)[:, :tm // BLK, 0].reshape(m // BLK)
    return res


_GLA_LEVELS = tuple(2 ** i for i in range(int(math.log2(BLK))))
GLA_SAFE_EXPONENT = 60.0


def _sigmoid(x):
    return 0.5 * jnp.tanh(0.5 * x) + 0.5


def _split_dot(a01, x):
    hi = x.astype(BF16)
    lo = (x - hi.astype(F32)).astype(BF16)
    n = x.shape[1]
    both = jnp.dot(a01, jnp.concatenate([hi, lo], axis=1), preferred_element_type=F32)
    return both[:, :n] + both[:, n:]


def _gla_anchor_exponent(b, la, s, row):
    if s == 1:
        return jnp.where(row % 2 == 1, la, 0.0)
    if s == 2:
        la_dn = pltpu.roll(la, 1, axis=0)
        la_up = pltpu.roll(la, BLK - 1, axis=0)
        r = row % 4
        return jnp.where(r == 0, la_up, jnp.where(r == 1, 0.0, jnp.where(r == 2, la, la + la_dn)))
    nb = BLK // (2 * s)
    b3 = b.reshape(nb, 2 * s, b.shape[-1])
    anchor = jnp.broadcast_to(b3[:, s - 1:s, :], b3.shape).reshape(b.shape)
    return -jnp.abs(b - anchor)


def _gla_body(nblk, lamax_ref, qm, km, vm, rm, lam, gam, qp, kp, vp, rp, lap, gap, gn_ref, tri_ref,
              y_ref, s_out_ref, s_ref):
    c = pl.program_id(1)
    is_meta = c == 0
    n_prompt_blocks = pl.num_programs(0) * nblk
    blk_max = lamax_ref[jnp.where(is_meta, n_prompt_blocks, pl.program_id(0) * nblk + c - 1)]

    @pl.when(is_meta)
    def _():
        s_ref[...] = jnp.zeros_like(s_ref)

    def pick(m_ref, p_ref):
        return jnp.where(is_meta, m_ref[...], p_ref[...])

    row = lax.broadcasted_iota(jnp.int32, (BLK, GLA_DK), 0)
    col_t = lax.broadcasted_iota(jnp.int32, (BLK, BLK), 1)
    row_t = lax.broadcasted_iota(jnp.int32, (BLK, BLK), 0)
    live = jnp.logical_or(jnp.logical_not(is_meta), row >= BLK - N_META)
    tri = tri_ref[...]
    q_all, k_all, v_all, r_all = pick(qm, qp), pick(km, kp), pick(vm, vp), pick(rm, rp)
    ga_all = pick(gam, gap)
    la_all = pick(lam, lap)
    nt = (((1,), (1,)), ((), ()))
    mid = BLK // 2 - 1

    def head(h, single_anchor):
        dk = slice(h * GLA_DK, (h + 1) * GLA_DK)
        dv = slice(h * GLA_DV, (h + 1) * GLA_DV)
        la = jnp.where(live, la_all[:, dk], 0.0)
        q = q_all[:, dk].astype(F32)
        k = jnp.where(live, k_all[:, dk].astype(F32), 0.0)
        v = v_all[:, dv]
        b = _split_dot(tri, la)
        b_last = b[BLK - 1:BLK, :]
        s_old = s_ref[h]
        if single_anchor:
            b_mid = b[mid:mid + 1, :]
            qe = q * jnp.exp(b - b_mid)
            ke = k * jnp.exp(b_mid - b)
            a = jnp.where(row_t >= col_t,
                          lax.dot_general(qe.astype(BF16), ke.astype(BF16), nt, preferred_element_type=F32), 0.0)
            qg = qe * jnp.exp(b_mid)
            kd = ke * jnp.exp(b_last - b_mid)
        else:
            a = jnp.where(row_t == col_t,
                          lax.dot_general(q.astype(BF16), k.astype(BF16), nt, preferred_element_type=F32), 0.0)
            for s in _GLA_LEVELS:
                e = jnp.exp(_gla_anchor_exponent(b, la, s, row))
                upper = (row // s) % 2 == 1
                q_s = jnp.where(upper, q * e, 0.0).astype(BF16)
                k_s = jnp.where(upper, 0.0, k * e).astype(BF16)
                p = lax.dot_general(q_s, k_s, nt, preferred_element_type=F32)
                a = a + jnp.where(row_t // (2 * s) == col_t // (2 * s), p, 0.0)
            qg = q * jnp.exp(b)
            kd = k * jnp.exp(b_last - b)
        o = jnp.dot(qg.astype(BF16), s_old.astype(BF16), preferred_element_type=F32)
        lhs = jnp.concatenate([jnp.transpose(kd).astype(BF16), a.astype(BF16)], axis=0)
        both = jnp.dot(lhs, v, preferred_element_type=F32)
        decay_col = jnp.transpose(jnp.broadcast_to(jnp.exp(b_last), (BLK, GLA_DK)))[:, :1]
        s_ref[h] = decay_col * s_old + both[:GLA_DK]
        o = o + both[GLA_DK:]
        o = o * lax.rsqrt(jnp.mean(o * o, axis=-1, keepdims=True) + LN_EPS) * gn_ref[...]
        r = r_all[:, dv].astype(F32)
        y = o * (r * _sigmoid(r)) * _sigmoid(ga_all[:, dv].astype(F32))
        y_ref[:, dv] = y.astype(y_ref.dtype)

    mild = blk_max * (BLK // 2) <= GLA_SAFE_EXPONENT

    @pl.when(mild)
    def _():
        for h in range(GLA_HEADS):
            head(h, True)

    @pl.when(jnp.logical_not(mild))
    def _():
        for h in range(GLA_HEADS):
            head(h, False)

    @pl.when(c == nblk)
    def _():
        s_out_ref[...] = s_ref[...]


def _tri_incl():
    i = np.arange(BLK)
    return jnp.asarray((i[None, :] <= i[:, None]).astype(np.float32), dtype=BF16)


def _gla_prompt(pp, pe, gnorm, nbatch, nblk):
    names = ("qg", "kg", "vg", "rg", "la", "ga")
    lamax = jnp.concatenate([pp["lamax"], pe["lamax"][1:2]])
    m_specs = [pl.BlockSpec((BLK, pe[n].shape[1]), lambda b, c, lm: (1, 0)) for n in names]
    p_specs = [pl.BlockSpec((BLK, pp[n].shape[1]), lambda b, c, lm: (b * nblk + jnp.maximum(c - 1, 0), 0))
               for n in names]
    w_specs = [
        pl.BlockSpec((1, GLA_DV), lambda b, c, lm: (0, 0)),
        pl.BlockSpec((BLK, BLK), lambda b, c, lm: (0, 0)),
    ]
    y, s_fin = pl.pallas_call(
        functools.partial(_gla_body, nblk),
        grid_spec=pltpu.PrefetchScalarGridSpec(
            num_scalar_prefetch=1,
            grid=(nbatch, nblk + 1),
            in_specs=m_specs + p_specs + w_specs,
            out_specs=[
                pl.BlockSpec((BLK, D_MODEL), lambda b, c, lm: (b * nblk + jnp.maximum(c - 1, 0), 0)),
                pl.BlockSpec((None, GLA_HEADS, GLA_DK, GLA_DV), lambda b, c, lm: (b, 0, 0, 0)),
            ],
            scratch_shapes=[pltpu.VMEM((GLA_HEADS, GLA_DK, GLA_DV), F32)],
        ),
        out_shape=[
            jax.ShapeDtypeStruct((nbatch * nblk * BLK, D_MODEL), BF16),
            jax.ShapeDtypeStruct((nbatch, GLA_HEADS, GLA_DK, GLA_DV), F32),
        ],
        compiler_params=_cparams(("arbitrary", "arbitrary")),
        name="gla_prompt",
    )(lamax, *[pe[n] for n in names], *[pp[n] for n in names], gnorm.reshape(1, -1), _tri_incl())
    return y, s_fin


GLA_STEP_SEQS = 16


def _gla_step_body(q_ref, k_ref, v_ref, r_ref, la_ref, ga_ref, gn_ref, s_in_ref,
                   y_ref, s_out_ref, at_ref, kt_ref, qt_ref):
    g = pl.program_id(0)
    nseq = q_ref.shape[0]

    @pl.when(g == 0)
    def _():
        a = jnp.exp(la_ref[...])
        for h in range(GLA_HEADS):
            dk = slice(h * GLA_DK, (h + 1) * GLA_DK)
            at_ref[h] = jnp.transpose(a[:, dk])
            kt_ref[h] = jnp.transpose(k_ref[:, dk].astype(F32))
            qt_ref[h] = jnp.transpose(q_ref[:, dk].astype(F32))

    lane = lax.broadcasted_iota(jnp.int32, (GLA_DK, nseq), 1)
    ones = jnp.ones((nseq, GLA_DV), BF16)
    grp = pl.ds(pl.multiple_of(g * GLA_STEP_SEQS, GLA_STEP_SEQS), GLA_STEP_SEQS)
    r_grp = r_ref[grp, :].astype(F32)
    ga_grp = ga_ref[grp, :].astype(F32)
    for i in range(GLA_STEP_SEQS):
        n = g * GLA_STEP_SEQS + i
        sel = lane == n
        for h in range(GLA_HEADS):
            dv = slice(h * GLA_DV, (h + 1) * GLA_DV)
            a_sel = jnp.where(sel, at_ref[h], 0.0)
            k_sel = jnp.where(sel, kt_ref[h], 0.0).astype(BF16)
            q_sel = jnp.where(sel, qt_ref[h], 0.0).astype(BF16)
            decay = _split_dot_rhs(a_sel, ones)
            kv = jnp.dot(k_sel, v_ref[:, dv], preferred_element_type=F32)
            q_b = jnp.dot(q_sel, ones, preferred_element_type=F32)
            s_new = decay * s_in_ref[i, h] + kv
            s_out_ref[i, h] = s_new
            o = jnp.sum(q_b * s_new, axis=0, keepdims=True)
            o = o * lax.rsqrt(jnp.mean(o * o, axis=-1, keepdims=True) + LN_EPS) * gn_ref[...]
            r = r_grp[i:i + 1, dv]
            ga = ga_grp[i:i + 1, dv]
            y_ref[i:i + 1, dv] = (o * (r * _sigmoid(r)) * _sigmoid(ga)).astype(y_ref.dtype)


def _split_dot_rhs(x, b01):
    hi = x.astype(BF16)
    lo = (x - hi.astype(F32)).astype(BF16)
    return jnp.dot(hi, b01, preferred_element_type=F32) + jnp.dot(lo, b01, preferred_element_type=F32)


def _gla_step(pe, gnorm, state):
    nseq = state.shape[0]
    assert nseq == BLK and nseq % GLA_STEP_SEQS == 0
    names = ("qg", "kg", "vg", "rg", "la", "ga")
    t_specs = [pl.BlockSpec((nseq, pe[n].shape[1]), lambda g: (0, 0)) for n in names]
    st_spec = pl.BlockSpec((GLA_STEP_SEQS, GLA_HEADS, GLA_DK, GLA_DV), lambda g: (g, 0, 0, 0))
    return pl.pallas_call(
        _gla_step_body,
        grid=(nseq // GLA_STEP_SEQS,),
        in_specs=t_specs + [
            pl.BlockSpec((1, GLA_DV), lambda g: (0, 0)),
            st_spec,
        ],
        out_specs=[pl.BlockSpec((GLA_STEP_SEQS, D_MODEL), lambda g: (g, 0)), st_spec],
        out_shape=[jax.ShapeDtypeStruct((nseq, D_MODEL), F32), jax.ShapeDtypeStruct(state.shape, F32)],
        scratch_shapes=[pltpu.VMEM((GLA_HEADS, GLA_DK, nseq), F32) for _ in range(3)],
        compiler_params=_cparams(("arbitrary",)),
        name="gla_step",
    )(*[pe[n] for n in names], gnorm.reshape(1, -1), state)


HALF = LANES // 2


def _rel_bucket(dist):
    max_exact = REL_BUCKETS // 2
    d = jnp.maximum(dist, 0)
    large = max_exact + (jnp.log(jnp.maximum(d, 1).astype(F32) / max_exact)
                         / math.log(REL_MAX_DIST / max_exact) * (REL_BUCKETS - max_exact)).astype(jnp.int32)
    large = jnp.minimum(large, REL_BUCKETS - 1)
    return jnp.where(d < max_exact, d, large)


def _bias_lookup(rel_bias, dist):
    onehot = (_rel_bucket(dist)[..., None] == jnp.arange(REL_BUCKETS)).astype(F32)
    return jnp.einsum("...b,bh->h...", onehot, rel_bias.astype(F32), precision=lax.Precision.HIGHEST)


def _swa_bias_tables(rel_bias):
    q = jnp.arange(BLK)[:, None]
    c = jnp.arange(2 * BLK)[None, :]
    dist = BLK + q - c
    bias = _bias_lookup(rel_bias, dist)
    inside = (dist >= 0) & (dist < WINDOW)
    first = inside & (c >= BLK - N_META)
    neg = jnp.float32(-jnp.inf)
    return jnp.stack([jnp.where(first[None], bias, neg), jnp.where(inside[None], bias, neg)])


def _half_tiles(x):
    lane = lax.broadcasted_iota(jnp.int32, (x.shape[0], LANES), 1)
    low = lane < HALF
    out = []
    for t in range(2):
        tile = x[:, t * LANES:(t + 1) * LANES]
        swapped = pltpu.roll(tile, HALF, axis=1)
        zero = jnp.zeros_like(tile)
        even = (jnp.where(low, tile, zero).astype(BF16), jnp.where(low, zero, swapped).astype(BF16))
        odd = (jnp.where(low, swapped, zero).astype(BF16), jnp.where(low, zero, tile).astype(BF16))
        out += [even, odd]
    return out


def _dup_tiles(x):
    lane = lax.broadcasted_iota(jnp.int32, (x.shape[0], LANES), 1)
    low = lane < HALF
    out = []
    for t in range(2):
        tile = x[:, t * LANES:(t + 1) * LANES]
        swapped = pltpu.roll(tile, HALF, axis=1)
        out += [jnp.where(low, tile, swapped).astype(BF16), jnp.where(low, swapped, tile).astype(BF16)]
    return out


def _swa_body(nblk, q_ref, kc_ref, vc_ref, kp_ref, vp_ref, km_ref, vm_ref, gb_ref, yg_ref, tb_ref, sink_ref,
              o_ref, kw_ref, vw_ref):
    blk = pl.program_id(1)
    first = blk == 0

    @pl.when(blk == nblk - 1)
    def _():
        kw_ref[...] = kc_ref[...]
        vw_ref[...] = vc_ref[...]

    k_prev = jnp.where(first, km_ref[...], kp_ref[...])
    v_prev = jnp.where(first, vm_ref[...], vp_ref[...])
    k_tiles = _dup_tiles(jnp.concatenate([k_prev, kc_ref[...]], axis=0))
    v_tiles = _dup_tiles(jnp.concatenate([v_prev, vc_ref[...]], axis=0))
    variant = jnp.minimum(blk, 1)
    npair = GQA_GROUP // 2
    low = lax.broadcasted_iota(jnp.int32, (BLK, LANES), 1) < HALF
    seg = lax.broadcasted_iota(jnp.int32, (GQA_GROUP * BLK, 1), 0) // BLK
    nt = (((1,), (1,)), ((), ()))
    for j in range(ATT_KV_HEADS):
        tiles = [j * npair + pair for pair in range(npair)]
        q_t = [q_ref[:, t * LANES:(t + 1) * LANES] for t in tiles]
        zero = jnp.zeros_like(q_t[0])
        q_st = jnp.concatenate([jnp.where(low, q, zero) for q in q_t] + [jnp.where(low, zero, q) for q in q_t], axis=0)
        heads = [2 * t for t in tiles] + [2 * t + 1 for t in tiles]
        s = lax.dot_general(q_st, k_tiles[j], nt, preferred_element_type=F32)
        s = s + jnp.concatenate([tb_ref[variant, h] for h in heads], axis=0)
        sink = jnp.full((GQA_GROUP * BLK, 1), sink_ref[heads[0]], F32)
        for i in range(1, GQA_GROUP):
            sink = jnp.where(seg == i, sink_ref[heads[i]], sink)
        m = jnp.maximum(jnp.max(s, axis=-1, keepdims=True), sink)
        p = jnp.exp(s - m)
        inv = 1.0 / (jnp.sum(p, axis=-1, keepdims=True) + jnp.exp(sink - m))
        o = jnp.dot(p.astype(BF16), v_tiles[j], preferred_element_type=F32) * inv
        for pair, t in enumerate(tiles):
            cols = slice(t * LANES, (t + 1) * LANES)
            gate = _sigmoid(gb_ref[:, cols].astype(F32))
            even = o[pair * BLK:(pair + 1) * BLK]
            odd = o[(npair + pair) * BLK:(npair + pair + 1) * BLK]
            o_ref[:, cols] = (gate * jnp.where(low, even, odd) + yg_ref[:, cols].astype(F32)).astype(o_ref.dtype)


def _swa_prompt(pp, pe, yg, rel_bias, sinks, nbatch, nblk):
    tb = _swa_bias_tables(rel_bias)
    kvw = ATT_KV_HEADS * HEAD_DIM
    cur = lambda b, c: (b * nblk + c, 0)
    prev = lambda b, c: (b * nblk + jnp.maximum(c - 1, 0), 0)
    return pl.pallas_call(
        functools.partial(_swa_body, nblk),
        grid=(nbatch, nblk),
        in_specs=[
            pl.BlockSpec((BLK, D_MODEL), cur),
            pl.BlockSpec((BLK, kvw), cur), pl.BlockSpec((BLK, kvw), cur),
            pl.BlockSpec((BLK, kvw), prev), pl.BlockSpec((BLK, kvw), prev),
            pl.BlockSpec((BLK, kvw), lambda b, c: (1, 0)), pl.BlockSpec((BLK, kvw), lambda b, c: (1, 0)),
            pl.BlockSpec((BLK, D_MODEL), cur),
            pl.BlockSpec((BLK, D_MODEL), cur),
            pl.BlockSpec(tb.shape, lambda b, c: (0, 0, 0, 0)),
            pl.BlockSpec(memory_space=pltpu.SMEM),
        ],
        out_specs=[pl.BlockSpec((BLK, D_MODEL), cur),
                   pl.BlockSpec((None, BLK, kvw), lambda b, c: (b, 0, 0)),
                   pl.BlockSpec((None, BLK, kvw), lambda b, c: (b, 0, 0))],
        out_shape=[jax.ShapeDtypeStruct((nbatch * nblk * BLK, D_MODEL), BF16),
                   jax.ShapeDtypeStruct((nbatch, BLK, kvw), F32), jax.ShapeDtypeStruct((nbatch, BLK, kvw), F32)],
        compiler_params=_cparams(("arbitrary", "arbitrary")),
        name="swa_prompt",
    )(pp["qa"], pp["ka"], pp["va"], pp["ka"], pp["va"], pe["ka"], pe["va"], pp["gb"], yg, tb, sinks)


SWA_STEP_SEQS = 8
Q_TILES = ATT_HEADS // 2


def _swa_step_body(q_ref, kn_ref, vn_ref, ck_ref, cv_ref, gb_ref, yg_ref, tb_ref, sink_ref,
                   o_ref, ko_ref, vo_ref):
    row = lax.broadcasted_iota(jnp.int32, (WINDOW, ATT_KV_HEADS * HEAD_DIM), 0)
    low = lax.broadcasted_iota(jnp.int32, (Q_TILES, LANES), 1) < HALF
    mine16 = (lax.broadcasted_iota(jnp.int32, (2 * Q_TILES, LANES), 0) % Q_TILES) // (GQA_GROUP // 2)
    nt = (((1,), (1,)), ((), ()))
    for i in range(SWA_STEP_SEQS):
        k_win = jnp.where(row == WINDOW - 1, kn_ref[i:i + 1, :], pltpu.roll(ck_ref[i], WINDOW - 1, axis=0))
        v_win = jnp.where(row == WINDOW - 1, vn_ref[i:i + 1, :], pltpu.roll(cv_ref[i], WINDOW - 1, axis=0))
        ko_ref[i] = k_win
        vo_ref[i] = v_win
        k_tiles = _dup_tiles(k_win)
        v_tiles = _dup_tiles(v_win)
        q8 = q_ref[i]
        q16 = jnp.concatenate([jnp.where(low, q8, 0.0), jnp.where(low, 0.0, q8)], axis=0).astype(BF16)
        s = jnp.zeros((2 * Q_TILES, WINDOW), F32)
        for j in range(ATT_KV_HEADS):
            sj = lax.dot_general(q16, k_tiles[j], nt, preferred_element_type=F32)
            s = jnp.where(mine16 == j, sj, s)
        s = s + tb_ref[...]
        sink = sink_ref[...]
        m = jnp.maximum(jnp.max(s, axis=-1, keepdims=True), sink)
        p = jnp.exp(s - m)
        inv = 1.0 / (jnp.sum(p, axis=-1, keepdims=True) + jnp.exp(sink - m))
        p = p.astype(BF16)
        o = jnp.zeros((2 * Q_TILES, LANES), F32)
        for j in range(ATT_KV_HEADS):
            o = jnp.where(mine16 == j, jnp.dot(p, v_tiles[j], preferred_element_type=F32), o)
        o = o * inv
        o = jnp.where(low, o[:Q_TILES], o[Q_TILES:])
        o_ref[i] = _sigmoid(gb_ref[i]) * o + yg_ref[i]


def _swa_step(pe, yg_s, cache_k, cache_v, rel_bias, sinks):
    nseq = cache_k.shape[0]
    kvw = ATT_KV_HEADS * HEAD_DIM
    as_tiles = lambda x: x[:nseq].astype(F32).reshape(nseq, Q_TILES, LANES)
    dist = (WINDOW - 1) - jnp.arange(WINDOW)
    bias = _bias_lookup(rel_bias, dist)
    tb = jnp.concatenate([bias[0::2], bias[1::2]], axis=0)
    sk = jnp.concatenate([sinks[0::2], sinks[1::2]])[:, None].astype(F32)
    g = SWA_STEP_SEQS
    tile_spec = pl.BlockSpec((g, Q_TILES, LANES), lambda s: (s, 0, 0))
    win_spec = pl.BlockSpec((g, WINDOW, kvw), lambda s: (s, 0, 0))
    new_spec = pl.BlockSpec((g, kvw), lambda s: (s, 0))
    o, ko, vo = pl.pallas_call(
        _swa_step_body,
        grid=(nseq // g,),
        in_specs=[tile_spec, new_spec, new_spec, win_spec, win_spec, tile_spec, tile_spec,
                  pl.BlockSpec(tb.shape, lambda s: (0, 0)), pl.BlockSpec(sk.shape, lambda s: (0, 0))],
        out_specs=[tile_spec, win_spec, win_spec],
        out_shape=[jax.ShapeDtypeStruct((nseq, Q_TILES, LANES), F32),
                   jax.ShapeDtypeStruct(cache_k.shape, F32), jax.ShapeDtypeStruct(cache_v.shape, F32)],
        compiler_params=_cparams(("arbitrary",)),
        name="swa_step",
    )(as_tiles(pe["qa"]), pe["ka"], pe["va"], cache_k, cache_v, as_tiles(pe["gb"]),
      yg_s.reshape(nseq, Q_TILES, LANES), tb, sk)
    return o.reshape(nseq, D_MODEL), ko, vo


ROUTER_ROWS = 40
META_ROWS = 8


def _split3_nt(a_hi, a_lo, x):
    nt = (((1,), (1,)), ((), ()))
    x_hi = x.astype(BF16)
    x_lo = (x - x_hi.astype(F32)).astype(BF16)
    return (lax.dot_general(a_hi, x_hi, nt, preferred_element_type=F32)
            + lax.dot_general(a_hi, x_lo, nt, preferred_element_type=F32)
            + lax.dot_general(a_lo, x_hi, nt, preferred_element_type=F32))


def _first_argmax_rows(v, ridx, nrows):
    vmax = jnp.max(v, axis=0, keepdims=True)
    idx = jnp.min(jnp.where(v == vmax, ridx, nrows), axis=0, keepdims=True)
    return vmax, idx


def _post_body(nsteps, *refs):
    h1_ref, meta_ref, wcol_ref = refs[-5:-2]
    i = pl.program_id(0)

    @pl.when(i < nsteps)
    def _():
        _post_tile(i, *refs)

    @pl.when(i >= nsteps)
    def _():
        h1_ref[...] = jnp.zeros_like(h1_ref)
        meta_ref[...] = jnp.zeros_like(meta_ref)
        wcol_ref[...] = jnp.zeros_like(wcol_ref)


def _post_tile(i, mg_ref, x_ref, lng_ref, lnb_ref, wo_ref, g1_ref, b1_ref, wrh_ref, wrl_ref, rb_ref, ut_ref,
               cin_ref, *rest):
    h1_ref, meta_ref, wcol_ref, cout_ref, carry_ref = rest[-5:]

    @pl.when(i == 0)
    def _():
        carry_ref[...] = cin_ref[...]

    tm = x_ref.shape[0]
    h = _layer_norm_rows(x_ref[...], lng_ref[...], lnb_ref[...])
    acc = jnp.dot(mg_ref[...].astype(BF16), wo_ref[...], preferred_element_type=F32)
    h1 = _layer_norm_rows(ALPHA * h + acc, g1_ref[...], b1_ref[...])
    h1_ref[...] = h1

    lt = _split3_nt(wrh_ref[...], wrl_ref[...], h1) + rb_ref[:, :1]
    ridx = lax.broadcasted_iota(jnp.int32, (EXPERTS_PER_GROUP, tm), 0)
    neg = jnp.float32(-jnp.inf)
    g_log = jnp.where(ridx < N_GROUPS, lt[N_EXPERTS:N_EXPERTS + EXPERTS_PER_GROUP], neg)
    g_max, grp = _first_argmax_rows(g_log, ridx, EXPERTS_PER_GROUP)
    p_grp = 1.0 / jnp.sum(jnp.exp(g_log - g_max), axis=0, keepdims=True)
    e_in = lt[0:EXPERTS_PER_GROUP]
    for gi in range(1, N_GROUPS):
        e_in = jnp.where(grp == gi, lt[gi * EXPERTS_PER_GROUP:(gi + 1) * EXPERTS_PER_GROUP], e_in)
    v0, i0 = _first_argmax_rows(e_in, ridx, EXPERTS_PER_GROUP)
    v1, i1 = _first_argmax_rows(jnp.where(ridx == i0, neg, e_in), ridx, EXPERTS_PER_GROUP)
    t = jnp.exp(v1 - v0)
    w0 = p_grp / (1.0 + t)
    w1 = p_grp * t / (1.0 + t)
    e0 = grp * EXPERTS_PER_GROUP + i0
    e1 = grp * EXPERTS_PER_GROUP + i1

    eidx = lax.broadcasted_iota(jnp.int32, (N_EXPERTS, tm), 0)
    hit0 = eidx == e0
    hit1 = eidx == e1
    oh = jnp.where(jnp.logical_or(hit0, hit1), 1.0, 0.0)
    before = jnp.dot(oh.astype(BF16), ut_ref[...], preferred_element_type=F32) + carry_ref[:, :1]
    r0 = jnp.sum(jnp.where(hit0, before, 0.0), axis=0, keepdims=True).astype(jnp.int32)
    r1 = jnp.sum(jnp.where(hit1, before, 0.0), axis=0, keepdims=True).astype(jnp.int32)
    carry_ref[...] = carry_ref[...] + jnp.sum(oh, axis=1, keepdims=True)
    cout_ref[...] = carry_ref[...]

    zi = jnp.zeros((META_ROWS - 4, tm), jnp.int32)
    meta_ref[...] = jnp.concatenate([e0, e1, r0, r1, zi], axis=0)
    wt = jnp.concatenate([w0, w1, jnp.zeros((LANES - 2, tm), F32)], axis=0)
    wcol_ref[...] = jnp.transpose(wt)


def _post(mg, x2d, prm, tm, row0, total_rows, carry_in, prev=None, zero_tail=False):
    m = x2d.shape[0]
    assert m % tm == 0 and row0 % tm == 0 and total_rows % tm == 0
    off = row0 // tm
    nsteps = m // tm
    last = nsteps - 1
    ut = jnp.asarray(np.triu(np.ones((tm, tm), np.float32), 1), dtype=BF16)
    full = lambda shape: pl.BlockSpec(shape, lambda i: (0,) * len(shape))
    in_specs = [
        pl.BlockSpec((tm, D_MODEL), lambda i: (jnp.minimum(i, last), 0)),
        pl.BlockSpec((tm, D_MODEL), lambda i: (jnp.minimum(i, last), 0)),
        full((1, D_MODEL)), full((1, D_MODEL)),
        full((D_MODEL, D_MODEL)),
        full((1, D_MODEL)), full((1, D_MODEL)),
        full((ROUTER_ROWS, D_MODEL)), full((ROUTER_ROWS, D_MODEL)), full((ROUTER_ROWS, LANES)),
        full((tm, tm)),
        full((N_EXPERTS, LANES)),
    ]
    args = [mg, x2d, prm["ln_emb_g"], prm["ln_emb_b"], prm["w_out"], prm["ln1_g"], prm["ln1_b"],
            prm["wr_hi"], prm["wr_lo"], prm["r_bias"], ut, carry_in]
    aliases = {}
    if prev is not None:
        for k, buf in enumerate(prev):
            in_specs.append(pl.BlockSpec(memory_space=pl.ANY))
            aliases[len(args)] = k
            args.append(buf)
    out_shape = [
        jax.ShapeDtypeStruct((total_rows, D_MODEL), F32),
        jax.ShapeDtypeStruct((META_ROWS, total_rows), jnp.int32),
        jax.ShapeDtypeStruct((total_rows, LANES), F32),
        jax.ShapeDtypeStruct((N_EXPERTS, LANES), F32),
    ]
    out_specs = [
        pl.BlockSpec((tm, D_MODEL), lambda i: (i + off, 0)),
        pl.BlockSpec((META_ROWS, tm), lambda i: (0, i + off)),
        pl.BlockSpec((tm, LANES), lambda i: (i + off, 0)),
        full((N_EXPERTS, LANES)),
    ]
    if prev is not None:
        assert len(prev) == 3
    return pl.pallas_call(
        functools.partial(_post_body, nsteps),
        grid=(nsteps + int(zero_tail),),
        in_specs=in_specs,
        out_specs=out_specs,
        out_shape=out_shape,
        input_output_aliases=aliases,
        scratch_shapes=[pltpu.VMEM((N_EXPERTS, LANES), F32)],
        compiler_params=_cparams(("arbitrary",)),
        name="post_attn",
    )(*args)


def _prep_post_params(ln_emb_g, ln_emb_b, w_out, ln1_g, ln1_b, w_rg, b_rg, w_re, b_re):
    row = lambda v: v.reshape(1, -1)
    wr = jnp.concatenate([w_re.T, w_rg.T, jnp.zeros((ROUTER_ROWS - N_EXPERTS - N_GROUPS, D_MODEL), F32)], axis=0)
    wr_hi = wr.astype(BF16)
    wr_lo = (wr - wr_hi.astype(F32)).astype(BF16)
    rb = jnp.concatenate([b_re, b_rg, jnp.zeros((ROUTER_ROWS - N_EXPERTS - N_GROUPS,), F32)])
    return dict(ln_emb_g=row(ln_emb_g), ln_emb_b=row(ln_emb_b), w_out=w_out.astype(BF16), ln1_g=row(ln1_g),
                ln1_b=row(ln1_b), wr_hi=wr_hi, wr_lo=wr_lo,
                r_bias=jnp.broadcast_to(rb[:, None], (ROUTER_ROWS, LANES)))


MOE_ROWS = 256
SUBLANES = 8
assert D_MODEL == SUBLANES * LANES


def _store_rows_as_tiles(ref, x):
    n = x.shape[0]
    for c in range(SUBLANES):
        ref[pl.ds(c, n, stride=SUBLANES), :] = x[:, c * LANES:(c + 1) * LANES]


def _load_rows_from_tiles(ref, n):
    return jnp.concatenate([ref[pl.ds(c, n, stride=SUBLANES), :] for c in range(SUBLANES)], axis=1)


def _tile_of_row(ref, r):
    return ref.at[pl.ds(pl.multiple_of(r * SUBLANES, SUBLANES), SUBLANES)]


def _moe_plan(counts, total_assign):
    nb_max = -(-total_assign // MOE_ROWS) + N_EXPERTS
    padded = (counts + MOE_ROWS - 1) // MOE_ROWS * MOE_ROWS
    pend = jnp.cumsum(padded)
    pstart = (pend - padded).astype(jnp.int32)
    block_start = jnp.arange(nb_max, dtype=jnp.int32) * MOE_ROWS
    n_ended = jnp.sum((pend[None, :] <= block_start[:, None]).astype(jnp.int32), axis=1)
    block_e = jnp.minimum(n_ended, N_EXPERTS - 1).astype(jnp.int32)
    n_used = (pend[-1] // MOE_ROWS).astype(jnp.int32).reshape(1)
    tail_start = jnp.where(padded > 0, pend - MOE_ROWS, -1)
    spare = pend[-1] + jnp.arange(N_EXPERTS, dtype=pend.dtype) * MOE_ROWS
    spare = jnp.where(spare < nb_max * MOE_ROWS, spare, -1)
    zero_blocks = jnp.concatenate([tail_start, spare]).astype(jnp.int32)
    return pstart, block_e, n_used, nb_max, zero_blocks


ROW_UNROLL = 8


def _slot_ids(meta, pstart):
    experts = meta[0:TOP_K]
    ranks = meta[TOP_K:2 * TOP_K]
    onehot = experts[..., None] == jnp.arange(N_EXPERTS, dtype=jnp.int32)
    return ranks + jnp.sum(jnp.where(onehot, pstart, 0), axis=-1)


def _for_row_groups(tm, fn):
    def group(g, c):
        t0 = pl.multiple_of(g * ROW_UNROLL, ROW_UNROLL)
        for r in range(ROW_UNROLL):
            for k in range(TOP_K):
                fn(t0 + r, k)
        return c

    lax.fori_loop(0, tm // ROW_UNROLL, group, 0)


def _dispatch_body(nsteps, tail_ref, s0_ref, s1_ref, h_ref, xs_ref, pk_ref, zero_ref, sems, zsem):
    i = pl.program_id(0)
    tm = h_ref.shape[0]
    slot_refs = (s0_ref, s1_ref)
    cur = i % 2

    blk_tiles = MOE_ROWS * SUBLANES

    @pl.when(i == 0)
    def _():
        zero_ref[...] = jnp.zeros_like(zero_ref)
        for e in range(2 * N_EXPERTS):
            @pl.when(tail_ref[e] >= 0)
            def _():
                start = pl.multiple_of(tail_ref[e] * SUBLANES, blk_tiles)
                pltpu.make_async_copy(zero_ref, xs_ref.at[pl.ds(start, blk_tiles)], zsem).start()
        for e in range(2 * N_EXPERTS):
            @pl.when(tail_ref[e] >= 0)
            def _():
                pltpu.make_async_copy(zero_ref, xs_ref.at[pl.ds(0, blk_tiles)], zsem).wait()

    _store_rows_as_tiles(pk_ref.at[cur], h_ref[...])

    def send(t, k):
        pltpu.make_async_copy(_tile_of_row(pk_ref.at[cur], t), _tile_of_row(xs_ref, slot_refs[k][0, t]),
                              sems.at[cur]).start(priority=k)

    def wait_buffer(buf):
        _for_row_groups(tm, lambda t, k: pltpu.make_async_copy(
            _tile_of_row(pk_ref.at[buf], t), _tile_of_row(xs_ref, 0), sems.at[buf]).wait())

    _for_row_groups(tm, send)

    @pl.when(i > 0)
    def _():
        wait_buffer(1 - cur)

    @pl.when(i == nsteps - 1)
    def _():
        wait_buffer(cur)


def _dispatch(h1, slot_ids, tail_start, nslots, tm, total):
    assert total % tm == 0 and total <= h1.shape[0] and tm % ROW_UNROLL == 0
    slot_spec = pl.BlockSpec((1, tm), lambda i, tl: (0, i), memory_space=pltpu.SMEM)
    return pl.pallas_call(
        functools.partial(_dispatch_body, total // tm),
        grid_spec=pltpu.PrefetchScalarGridSpec(
            num_scalar_prefetch=1,
            grid=(total // tm,),
            in_specs=[slot_spec, slot_spec, pl.BlockSpec((tm, D_MODEL), lambda i, tl: (i, 0))],
            out_specs=pl.BlockSpec(memory_space=pl.ANY),
            scratch_shapes=[pltpu.VMEM((2, tm * SUBLANES, LANES), F32), pltpu.VMEM((MOE_ROWS * SUBLANES, LANES), F32),
                            pltpu.SemaphoreType.DMA((2,)), pltpu.SemaphoreType.DMA(())],
        ),
        out_shape=jax.ShapeDtypeStruct((nslots * SUBLANES, LANES), F32),
        compiler_params=_cparams(("arbitrary",)),
        name="moe_dispatch",
    )(tail_start, slot_ids[0:1], slot_ids[1:2], h1)


def _expert_schedule(block_e, n_used):
    nb = block_e.shape[0]
    idx = jnp.arange(nb, dtype=jnp.int32)
    first = (idx < n_used[0]) & ((idx == 0) | (block_e != jnp.roll(block_e, 1)))
    parity = (jnp.cumsum(first.astype(jnp.int32)) - 1) % 2
    pos = jnp.where(first, idx, nb)
    at_or_after = jnp.flip(lax.cummin(jnp.flip(pos)))
    nxt = jnp.concatenate([at_or_after[1:], jnp.full((1,), nb, jnp.int32)])
    nexte = jnp.where(nxt < nb, block_e[jnp.minimum(nxt, nb - 1)], -1)
    return first.astype(jnp.int32), nexte.astype(jnp.int32), parity.astype(jnp.int32)


def _expert_body(be_ref, nu_ref, first_ref, nexte_ref, par_ref, xs_ref, wg_hbm, wu_hbm, wd_hbm, ys_ref,
                 wgf_ref, wuf_ref, wdf_ref, wgb_ref, wub_ref, wdb_ref, sems):
    i = pl.program_id(0)
    hbm = (wg_hbm, wu_hbm, wd_hbm)
    stage = (wgf_ref, wuf_ref, wdf_ref)

    def weight_copies(e, buf):
        return [pltpu.make_async_copy(hbm[w].at[e], stage[w].at[buf], sems.at[buf, w]) for w in range(3)]

    @pl.when(first_ref[i] == 1)
    def _():
        buf = par_ref[i]

        @pl.when(i == 0)
        def _():
            for c in weight_copies(be_ref[0], buf):
                c.start()

        for c in weight_copies(be_ref[i], buf):
            c.wait()

        @pl.when(nexte_ref[i] >= 0)
        def _():
            for c in weight_copies(nexte_ref[i], 1 - buf):
                c.start()

        wgb_ref[...] = wgf_ref[buf].astype(BF16)
        wub_ref[...] = wuf_ref[buf].astype(BF16)
        wdb_ref[...] = wdf_ref[buf].astype(BF16)

    @pl.when(i < nu_ref[0])
    def _():
        x = _load_rows_from_tiles(xs_ref, MOE_ROWS).astype(BF16)
        g = jnp.dot(x, wgb_ref[...], preferred_element_type=F32)
        hb = (g * _sigmoid(g)) * jnp.dot(x, wub_ref[...], preferred_element_type=F32)
        y = jnp.dot(hb.astype(BF16), wdb_ref[...], preferred_element_type=F32)
        _store_rows_as_tiles(ys_ref, y)

    @pl.when(i >= nu_ref[0])
    def _():
        ys_ref[...] = jnp.zeros_like(ys_ref)


def _experts(xs, block_e, n_used, w_gate, w_up, w_down, nb_max):
    first, nexte, parity = _expert_schedule(block_e, n_used)
    rows = lambda i, *_: (i, 0)
    used_rows = lambda i, be, nu, *_: (jnp.maximum(jnp.minimum(i, nu[0] - 1), 0), 0)
    any_spec = pl.BlockSpec(memory_space=pl.ANY)
    blk = (MOE_ROWS * SUBLANES, LANES)
    return pl.pallas_call(
        _expert_body,
        grid_spec=pltpu.PrefetchScalarGridSpec(
            num_scalar_prefetch=5,
            grid=(nb_max,),
            in_specs=[pl.BlockSpec(blk, used_rows), any_spec, any_spec, any_spec],
            out_specs=pl.BlockSpec(blk, rows),
            scratch_shapes=[
                pltpu.VMEM((2, D_MODEL, D_EXPERT), F32), pltpu.VMEM((2, D_MODEL, D_EXPERT), F32),
                pltpu.VMEM((2, D_EXPERT, D_MODEL), F32),
                pltpu.VMEM((D_MODEL, D_EXPERT), BF16), pltpu.VMEM((D_MODEL, D_EXPERT), BF16),
                pltpu.VMEM((D_EXPERT, D_MODEL), BF16),
                pltpu.SemaphoreType.DMA((2, 3)),
            ],
        ),
        out_shape=jax.ShapeDtypeStruct(xs.shape, F32),
        compiler_params=_cparams(("arbitrary",)),
        name="moe_experts",
    )(block_e, n_used, first, nexte, parity, xs, w_gate, w_up, w_down)


def _combine_body(nsteps, s0_ref, s1_ref, n0_ref, n1_ref, h_ref, w_ref, g2_ref, b2_ref, ys_ref, o_ref, buf_ref, sems):
    i = pl.program_id(0)
    tm = h_ref.shape[0]
    cur = i % 2

    def fetch(slot_refs, buf):
        _for_row_groups(tm, lambda t, k: pltpu.make_async_copy(
            _tile_of_row(ys_ref, slot_refs[k][0, t]), _tile_of_row(buf_ref.at[buf, k], t),
            sems.at[buf]).start(priority=k))

    @pl.when(i == 0)
    def _():
        fetch((s0_ref, s1_ref), cur)

    @pl.when(i + 1 < nsteps)
    def _():
        fetch((n0_ref, n1_ref), 1 - cur)

    _for_row_groups(tm, lambda t, k: pltpu.make_async_copy(
        _tile_of_row(ys_ref, 0), _tile_of_row(buf_ref.at[cur, k], t), sems.at[cur]).wait())
    w = w_ref[...]
    f = (w[:, 0:1] * _load_rows_from_tiles(buf_ref.at[cur, 0], tm)
         + w[:, 1:2] * _load_rows_from_tiles(buf_ref.at[cur, 1], tm))
    o_ref[...] = _layer_norm_rows(ALPHA * h_ref[...] + f, g2_ref[...], b2_ref[...])


def _combine(h1, wcol, slot_ids, ys, ln2_g, ln2_b, tm, row0, nrows):
    assert nrows % tm == 0 and row0 % tm == 0 and tm % ROW_UNROLL == 0
    off = row0 // tm
    nsteps = nrows // tm
    slot_spec = pl.BlockSpec((1, tm), lambda i: (0, i + off), memory_space=pltpu.SMEM)
    next_spec = pl.BlockSpec((1, tm), lambda i: (0, jnp.minimum(i + 1, nsteps - 1) + off), memory_space=pltpu.SMEM)
    return pl.pallas_call(
        functools.partial(_combine_body, nsteps),
        grid=(nsteps,),
        in_specs=[
            slot_spec, slot_spec, next_spec, next_spec,
            pl.BlockSpec((tm, D_MODEL), lambda i: (i + off, 0)),
            pl.BlockSpec((tm, LANES), lambda i: (i + off, 0)),
            pl.BlockSpec((1, D_MODEL), lambda i: (0, 0)),
            pl.BlockSpec((1, D_MODEL), lambda i: (0, 0)),
            pl.BlockSpec(memory_space=pl.ANY),
        ],
        out_specs=pl.BlockSpec((tm, D_MODEL), lambda i: (i, 0)),
        scratch_shapes=[pltpu.VMEM((2, TOP_K, tm * SUBLANES, LANES), F32), pltpu.SemaphoreType.DMA((2,))],
        out_shape=jax.ShapeDtypeStruct((nrows, D_MODEL), F32),
        compiler_params=_cparams(("arbitrary",)),
        name="moe_combine",
    )(slot_ids[0:1], slot_ids[1:2], slot_ids[0:1], slot_ids[1:2], h1, wcol, ln2_g.reshape(1, -1),
      ln2_b.reshape(1, -1), ys)


PROJ_ROWS = 512
POST_ROWS = 512
DISPATCH_ROWS = 384
COMBINE_ROWS = 256


def kernel(x_prompt, x_sample, state_gla, cache_swa_k, cache_swa_v, meta_tokens, ln_emb_g, ln_emb_b, rel_bias, w_in,
           gk_up, gk_bias, gla_norm_g, sinks, w_out, ln1_g, ln1_b, w_router_group, b_router_group, w_router_expert,
           b_router_expert, w_gate, w_up, w_down, ln2_g, ln2_b):
    nbatch, seq, d = x_prompt.shape
    nseq = x_sample.shape[0]
    assert w_in.shape[0] == DEPTH == 1 and d == D_MODEL and x_sample.shape[1] == 1
    assert seq % BLK == 0 and nseq == BLK and meta_tokens.shape[0] == N_META
    nblk = seq // BLK
    n_prompt = nbatch * seq
    total = n_prompt + nseq
    kvw = ATT_KV_HEADS * HEAD_DIM

    xp = x_prompt.reshape(n_prompt, d)
    xs = x_sample.reshape(nseq, d)
    extra = jnp.concatenate([xs, jnp.zeros((BLK - N_META, d), xs.dtype), meta_tokens.astype(xs.dtype)], axis=0)
    w_cat = _prep_w_in(w_in[0])
    pp = _ln_proj(xp, ln_emb_g, ln_emb_b, w_cat, gk_up[0], gk_bias[0], PROJ_ROWS)
    pe = _ln_proj(extra, ln_emb_g, ln_emb_b, w_cat, gk_up[0], gk_bias[0], 2 * BLK)

    yg, gla_p = _gla_prompt(pp, pe, gla_norm_g[0], nbatch, nblk)
    yg_s, gla_s = _gla_step(pe, gla_norm_g[0], state_gla[0])
    mg, k_win, v_win = _swa_prompt(pp, pe, yg, rel_bias, sinks[0], nbatch, nblk)
    mg_s, k_s, v_s = _swa_step(pe, yg_s, cache_swa_k[0].reshape(nseq, WINDOW, kvw),
                               cache_swa_v[0].reshape(nseq, WINDOW, kvw), rel_bias, sinks[0])

    prm = _prep_post_params(ln_emb_g, ln_emb_b, w_out[0], ln1_g[0], ln1_b[0], w_router_group[0], b_router_group[0],
                            w_router_expert[0], b_router_expert[0])
    carry0 = jnp.zeros((N_EXPERTS, LANES), F32)
    rows_alloc = n_prompt + POST_ROWS
    h1, meta, wcol, carry1 = _post(mg, xp, prm, POST_ROWS, 0, rows_alloc, carry0, zero_tail=True)
    h1, meta, wcol, carry2 = _post(mg_s, xs, prm, nseq, n_prompt, rows_alloc, carry1, prev=(h1, meta, wcol))

    counts = carry2[:, 0].astype(jnp.int32)
    pstart, block_e, n_used, nb_max, tail_start = _moe_plan(counts, TOP_K * total)
    slot_ids = _slot_ids(meta, pstart)
    xs_sorted = _dispatch(h1, slot_ids, tail_start, nb_max * MOE_ROWS, DISPATCH_ROWS, total)
    ys = _experts(xs_sorted, block_e, n_used, w_gate[0], w_up[0], w_down[0], nb_max)
    y_p = _combine(h1, wcol, slot_ids, ys, ln2_g[0], ln2_b[0], COMBINE_ROWS, 0, n_prompt)
    y_s = _combine(h1, wcol, slot_ids, ys, ln2_g[0], ln2_b[0], nseq, n_prompt, nseq)

    kv_shape = (1, nbatch, WINDOW, ATT_KV_HEADS, HEAD_DIM)
    k_p = k_win.reshape(kv_shape)
    v_p = v_win.reshape(kv_shape)
    return (y_p.reshape(nbatch, seq, d), y_s.reshape(nseq, 1, d), gla_p[None], k_p, v_p, gla_s[None],
            k_s.reshape(cache_swa_k.shape), v_s.reshape(cache_swa_v.shape))
```

```python
import functools
import math

import jax
import jax.numpy as jnp
import numpy as np
from jax import lax
from jax.experimental import pallas as pl
from jax.experimental.pallas import tpu as pltpu

F32 = jnp.float32
BF16 = jnp.bfloat16

D_MODEL = 1024
N_META = 16
LN_EPS = 1e-5
GLA_HEADS = 4
GLA_DK = 128
GLA_DV = 256
GLA_RANK = 16
GLA_TAU = 16.0
HEAD_DIM = 64
ATT_HEADS = 16
ATT_KV_HEADS = 4
GQA_GROUP = 4
WINDOW = 128
REL_BUCKETS = 32
REL_MAX_DIST = 128
N_GROUPS = 4
EXPERTS_PER_GROUP = 8
N_EXPERTS = 32
TOP_K = 2
D_EXPERT = 512
DEPTH = 1
ALPHA = (2.0 * DEPTH) ** 0.25
LOG2E = math.log2(math.e)

LANES = 128
BLK = 128
VMEM_LIMIT = 56 * 1024 * 1024


def _cparams(sem):
    return pltpu.CompilerParams(dimension_semantics=sem, vmem_limit_bytes=VMEM_LIMIT)


def _layer_norm_rows(x, g, b):
    mu = jnp.mean(x, axis=-1, keepdims=True)
    xc = x - mu
    var = jnp.mean(xc * xc, axis=-1, keepdims=True)
    return xc * lax.rsqrt(var + LN_EPS) * g + b


_PROJ_OUTS = (
    ("lr", LANES, F32, None),
    ("qg", GLA_HEADS * GLA_DK, BF16, GLA_DK ** -0.5),
    ("kg", GLA_HEADS * GLA_DK, BF16, None),
    ("vg", GLA_HEADS * GLA_DV, BF16, None),
    ("rg", GLA_HEADS * GLA_DV, BF16, None),
    ("qa", ATT_HEADS * HEAD_DIM, BF16, HEAD_DIM ** -0.5 * LOG2E),
    ("ka", ATT_KV_HEADS * HEAD_DIM, F32, None),
    ("va", ATT_KV_HEADS * HEAD_DIM, F32, None),
    ("ga", D_MODEL, BF16, None),
    ("gb", D_MODEL, BF16, None),
)
_PROJ_W = sum(w for _, w, _, _ in _PROJ_OUTS)


def _prep_w_in(w_in):
    sizes = (512, 512, 1024, 1024, GLA_RANK, 1024, 256, 256, 1024, 1024)
    offs = np.cumsum((0,) + sizes)
    a = w_in[:, : offs[4]]
    lr = w_in[:, offs[4]: offs[5]]
    b = w_in[:, offs[5]:]
    pad = jnp.zeros((w_in.shape[0], LANES - GLA_RANK), w_in.dtype)
    return jnp.concatenate([lr, pad, a, b], axis=1).astype(BF16)


LAMAX_ROWS = 8


def _log_sigmoid(x):
    return jnp.minimum(x, 0.0) - jnp.log(1.0 + jnp.exp(-jnp.abs(x)))


def _ln_proj_body(x_ref, g_ref, b_ref, w_ref, gkup_ref, gkb_ref, *out_refs):
    la_ref, lamax_ref = out_refs[-2:]
    tm = x_ref.shape[0]
    xn = _layer_norm_rows(x_ref[...], g_ref[...], b_ref[...]).astype(BF16)
    c0 = 0
    plain_refs = iter(out_refs[:-2])
    for name, width, dtype, scale in _PROJ_OUTS:
        acc = jnp.dot(xn, w_ref[:, c0:c0 + width], preferred_element_type=F32)
        c0 += width
        if name == "lr":
            x = jnp.dot(acc.astype(BF16), gkup_ref[...], preferred_element_type=F32) + gkb_ref[...]
            la = _log_sigmoid(x) * (1.0 / GLA_TAU)
            la_ref[...] = la
            rows = []
            for r in range(tm // BLK):
                blk_max = jnp.max(jnp.max(jnp.abs(la[r * BLK:(r + 1) * BLK]), axis=0, keepdims=True), axis=1, keepdims=True)
                rows.append(jnp.broadcast_to(blk_max, (1, LANES)))
            rows.append(jnp.zeros((LAMAX_ROWS - tm // BLK, LANES), F32))
            lamax_ref[...] = jnp.concatenate(rows, axis=0)
            continue
        if scale is not None:
            acc = acc * scale
        next(plain_refs)[...] = acc.astype(dtype)


def _ln_proj(x2d, ln_g, ln_b, w_cat, gk_up, gk_bias, tm):
    m = x2d.shape[0]
    assert m % tm == 0 and tm % BLK == 0 and tm // BLK < LAMAX_ROWS
    names = [n for n, _, _, _ in _PROJ_OUTS if n != "lr"] + ["la", "lamax"]
    ladim = GLA_HEADS * GLA_DK
    gkup = jnp.concatenate([gk_up, jnp.zeros((LANES - GLA_RANK, ladim), gk_up.dtype)], axis=0).astype(BF16)
    out_shape = [jax.ShapeDtypeStruct((m, w), dt) for n, w, dt, _ in _PROJ_OUTS if n != "lr"]
    out_specs = [pl.BlockSpec((tm, w), lambda i: (i, 0)) for n, w, _, _ in _PROJ_OUTS if n != "lr"]
    out_shape += [jax.ShapeDtypeStruct((m, ladim), F32), jax.ShapeDtypeStruct((m // tm * LAMAX_ROWS, LANES), F32)]
    out_specs += [pl.BlockSpec((tm, ladim), lambda i: (i, 0)), pl.BlockSpec((LAMAX_ROWS, LANES), lambda i: (i, 0))]
    outs = pl.pallas_call(
        _ln_proj_body,
        grid=(m // tm,),
        in_specs=[
            pl.BlockSpec((tm, D_MODEL), lambda i: (i, 0)),
            pl.BlockSpec((1, D_MODEL), lambda i: (0, 0)),
            pl.BlockSpec((1, D_MODEL), lambda i: (0, 0)),
            pl.BlockSpec((D_MODEL, _PROJ_W), lambda i: (0, 0), pipeline_mode=pl.Buffered(1)),
            pl.BlockSpec((LANES, ladim), lambda i: (0, 0)),
            pl.BlockSpec((1, ladim), lambda i: (0, 0)),
        ],
        out_specs=out_specs,
        out_shape=out_shape,
        compiler_params=_cparams(("arbitrary",)),
        name="ln_proj",
    )(x2d, ln_g.reshape(1, -1), ln_b.reshape(1, -1), w_cat, gkup, gk_bias.reshape(1, -1))
    res = dict(zip(names, outs))
    res["lamax"] = res["lamax"].reshape(m // tm, LAMAX_ROWS, LANES)[:, :tm // BLK, 0].reshape(m // BLK)
    return res


_GLA_LEVELS = tuple(2 ** i for i in range(int(math.log2(BLK))))
GLA_SAFE_EXPONENT = 60.0


def _sigmoid(x):
    return 0.5 * jnp.tanh(0.5 * x) + 0.5


def _split_dot(a01, x):
    hi = x.astype(BF16)
    lo = (x - hi.astype(F32)).astype(BF16)
    n = x.shape[1]
    both = jnp.dot(a01, jnp.concatenate([hi, lo], axis=1), preferred_element_type=F32)
    return both[:, :n] + both[:, n:]


def _gla_anchor_exponent(b, la, s, row):
    if s == 1:
        return jnp.where(row % 2 == 1, la, 0.0)
    if s == 2:
        la_dn = pltpu.roll(la, 1, axis=0)
        la_up = pltpu.roll(la, BLK - 1, axis=0)
        r = row % 4
        return jnp.where(r == 0, la_up, jnp.where(r == 1, 0.0, jnp.where(r == 2, la, la + la_dn)))
    nb = BLK // (2 * s)
    b3 = b.reshape(nb, 2 * s, b.shape[-1])
    anchor = jnp.broadcast_to(b3[:, s - 1:s, :], b3.shape).reshape(b.shape)
    return -jnp.abs(b - anchor)


def _gla_body(nblk, lamax_ref, qm, km, vm, rm, lam, gam, qp, kp, vp, rp, lap, gap, gn_ref, tri_ref,
              y_ref, s_out_ref, s_ref):
    c = pl.program_id(1)
    is_meta = c == 0
    n_prompt_blocks = pl.num_programs(0) * nblk
    blk_max = lamax_ref[jnp.where(is_meta, n_prompt_blocks, pl.program_id(0) * nblk + c - 1)]

    @pl.when(is_meta)
    def _():
        s_ref[...] = jnp.zeros_like(s_ref)

    def pick(m_ref, p_ref):
        return jnp.where(is_meta, m_ref[...], p_ref[...])

    row = lax.broadcasted_iota(jnp.int32, (BLK, GLA_DK), 0)
    col_t = lax.broadcasted_iota(jnp.int32, (BLK, BLK), 1)
    row_t = lax.broadcasted_iota(jnp.int32, (BLK, BLK), 0)
    live = jnp.logical_or(jnp.logical_not(is_meta), row >= BLK - N_META)
    tri = tri_ref[...]
    q_all, k_all, v_all, r_all = pick(qm, qp), pick(km, kp), pick(vm, vp), pick(rm, rp)
    ga_all = pick(gam, gap)
    la_all = pick(lam, lap)
    nt = (((1,), (1,)), ((), ()))
    mid = BLK // 2 - 1

    def head(h, single_anchor):
        dk = slice(h * GLA_DK, (h + 1) * GLA_DK)
        dv = slice(h * GLA_DV, (h + 1) * GLA_DV)
        la = jnp.where(live, la_all[:, dk], 0.0)
        q = q_all[:, dk].astype(F32)
        k = jnp.where(live, k_all[:, dk].astype(F32), 0.0)
        v = v_all[:, dv]
        b = _split_dot(tri, la)
        yield
        b_last = b[BLK - 1:BLK, :]
        s_old = s_ref[h]
        if single_anchor:
            b_mid = b[mid:mid + 1, :]
            qe = q * jnp.exp(b - b_mid)
            ke = k * jnp.exp(b_mid - b)
            a = jnp.where(row_t >= col_t,
                          lax.dot_general(qe.astype(BF16), ke.astype(BF16), nt, preferred_element_type=F32), 0.0)
            qg = qe * jnp.exp(b_mid)
            kd = ke * jnp.exp(b_last - b_mid)
        else:
            a = jnp.where(row_t == col_t,
                          lax.dot_general(q.astype(BF16), k.astype(BF16), nt, preferred_element_type=F32), 0.0)
            for s in _GLA_LEVELS:
                e = jnp.exp(_gla_anchor_exponent(b, la, s, row))
                upper = (row // s) % 2 == 1
                q_s = jnp.where(upper, q * e, 0.0).astype(BF16)
                k_s = jnp.where(upper, 0.0, k * e).astype(BF16)
                p = lax.dot_general(q_s, k_s, nt, preferred_element_type=F32)
                a = a + jnp.where(row_t // (2 * s) == col_t // (2 * s), p, 0.0)
            qg = q * jnp.exp(b)
            kd = k * jnp.exp(b_last - b)
        yield
        o = jnp.dot(qg.astype(BF16), s_old.astype(BF16), preferred_element_type=F32)
        lhs = jnp.concatenate([jnp.transpose(kd).astype(BF16), a.astype(BF16)], axis=0)
        both = jnp.dot(lhs, v, preferred_element_type=F32)
        yield
        decay_col = jnp.transpose(jnp.broadcast_to(jnp.exp(b_last), (BLK, GLA_DK)))[:, :1]
        s_ref[h] = decay_col * s_old + both[:GLA_DK]
        o = o + both[GLA_DK:]
        o = o * lax.rsqrt(jnp.mean(o * o, axis=-1, keepdims=True) + LN_EPS) * gn_ref[...]
        r = r_all[:, dv].astype(F32)
        y = o * (r * _sigmoid(r)) * _sigmoid(ga_all[:, dv].astype(F32))
        y_ref[:, dv] = y.astype(y_ref.dtype)

    mild = blk_max * (BLK // 2) <= GLA_SAFE_EXPONENT

    def all_heads(single_anchor):
        running = [head(h, single_anchor) for h in range(GLA_HEADS)]
        while running:
            running = [g for g in running if next(g, True) is None]

    @pl.when(mild)
    def _():
        all_heads(True)

    @pl.when(jnp.logical_not(mild))
    def _():
        all_heads(False)

    @pl.when(c == nblk)
    def _():
        s_out_ref[...] = s_ref[...]


def _tri_incl():
    i = np.arange(BLK)
    return jnp.asarray((i[None, :] <= i[:, None]).astype(np.float32), dtype=BF16)


def _gla_prompt(pp, pe, gnorm, nbatch, nblk):
    names = ("qg", "kg", "vg", "rg", "la", "ga")
    lamax = jnp.concatenate([pp["lamax"], pe["lamax"][1:2]])
    m_specs = [pl.BlockSpec((BLK, pe[n].shape[1]), lambda b, c, lm: (1, 0)) for n in names]
    p_specs = [pl.BlockSpec((BLK, pp[n].shape[1]), lambda b, c, lm: (b * nblk + jnp.maximum(c - 1, 0), 0))
               for n in names]
    w_specs = [
        pl.BlockSpec((1, GLA_DV), lambda b, c, lm: (0, 0)),
        pl.BlockSpec((BLK, BLK), lambda b, c, lm: (0, 0)),
    ]
    y, s_fin = pl.pallas_call(
        functools.partial(_gla_body, nblk),
        grid_spec=pltpu.PrefetchScalarGridSpec(
            num_scalar_prefetch=1,
            grid=(nbatch, nblk + 1),
            in_specs=m_specs + p_specs + w_specs,
            out_specs=[
                pl.BlockSpec((BLK, D_MODEL), lambda b, c, lm: (b * nblk + jnp.maximum(c - 1, 0), 0)),
                pl.BlockSpec((None, GLA_HEADS, GLA_DK, GLA_DV), lambda b, c, lm: (b, 0, 0, 0)),
            ],
            scratch_shapes=[pltpu.VMEM((GLA_HEADS, GLA_DK, GLA_DV), F32)],
        ),
        out_shape=[
            jax.ShapeDtypeStruct((nbatch * nblk * BLK, D_MODEL), BF16),
            jax.ShapeDtypeStruct((nbatch, GLA_HEADS, GLA_DK, GLA_DV), F32),
        ],
        compiler_params=_cparams(("arbitrary", "arbitrary")),
        name="gla_prompt",
    )(lamax, *[pe[n] for n in names], *[pp[n] for n in names], gnorm.reshape(1, -1), _tri_incl())
    return y, s_fin


GLA_STEP_SEQS = 16


def _gla_step_body(q_ref, k_ref, v_ref, r_ref, la_ref, ga_ref, gn_ref, s_in_ref,
                   y_ref, s_out_ref, at_ref, kt_ref, qt_ref):
    g = pl.program_id(0)
    nseq = q_ref.shape[0]

    @pl.when(g == 0)
    def _():
        a = jnp.exp(la_ref[...])
        for h in range(GLA_HEADS):
            dk = slice(h * GLA_DK, (h + 1) * GLA_DK)
            at_ref[h] = jnp.transpose(a[:, dk])
            kt_ref[h] = jnp.transpose(k_ref[:, dk].astype(F32))
            qt_ref[h] = jnp.transpose(q_ref[:, dk].astype(F32))

    lane = lax.broadcasted_iota(jnp.int32, (GLA_DK, nseq), 1)
    ones = jnp.ones((nseq, GLA_DV), BF16)
    grp = pl.ds(pl.multiple_of(g * GLA_STEP_SEQS, GLA_STEP_SEQS), GLA_STEP_SEQS)
    r_grp = r_ref[grp, :].astype(F32)
    ga_grp = ga_ref[grp, :].astype(F32)
    for i in range(GLA_STEP_SEQS):
        n = g * GLA_STEP_SEQS + i
        sel = lane == n
        for h in range(GLA_HEADS):
            dv = slice(h * GLA_DV, (h + 1) * GLA_DV)
            a_sel = jnp.where(sel, at_ref[h], 0.0)
            k_sel = jnp.where(sel, kt_ref[h], 0.0).astype(BF16)
            q_sel = jnp.where(sel, qt_ref[h], 0.0).astype(BF16)
            decay = _split_dot_rhs(a_sel, ones)
            kv = jnp.dot(k_sel, v_ref[:, dv], preferred_element_type=F32)
            q_b = jnp.dot(q_sel, ones, preferred_element_type=F32)
            s_new = decay * s_in_ref[i, h] + kv
            s_out_ref[i, h] = s_new
            o = jnp.sum(q_b * s_new, axis=0, keepdims=True)
            o = o * lax.rsqrt(jnp.mean(o * o, axis=-1, keepdims=True) + LN_EPS) * gn_ref[...]
            r = r_grp[i:i + 1, dv]
            ga = ga_grp[i:i + 1, dv]
            y_ref[i:i + 1, dv] = (o * (r * _sigmoid(r)) * _sigmoid(ga)).astype(y_ref.dtype)


def _split_dot_rhs(x, b01):
    hi = x.astype(BF16)
    lo = (x - hi.astype(F32)).astype(BF16)
    return jnp.dot(hi, b01, preferred_element_type=F32) + jnp.dot(lo, b01, preferred_element_type=F32)


def _gla_step(pe, gnorm, state):
    nseq = state.shape[0]
    assert nseq == BLK and nseq % GLA_STEP_SEQS == 0
    names = ("qg", "kg", "vg", "rg", "la", "ga")
    t_specs = [pl.BlockSpec((nseq, pe[n].shape[1]), lambda g: (0, 0)) for n in names]
    st_spec = pl.BlockSpec((GLA_STEP_SEQS, GLA_HEADS, GLA_DK, GLA_DV), lambda g: (g, 0, 0, 0))
    return pl.pallas_call(
        _gla_step_body,
        grid=(nseq // GLA_STEP_SEQS,),
        in_specs=t_specs + [
            pl.BlockSpec((1, GLA_DV), lambda g: (0, 0)),
            st_spec,
        ],
        out_specs=[pl.BlockSpec((GLA_STEP_SEQS, D_MODEL), lambda g: (g, 0)), st_spec],
        out_shape=[jax.ShapeDtypeStruct((nseq, D_MODEL), F32), jax.ShapeDtypeStruct(state.shape, F32)],
        scratch_shapes=[pltpu.VMEM((GLA_HEADS, GLA_DK, nseq), F32) for _ in range(3)],
        compiler_params=_cparams(("arbitrary",)),
        name="gla_step",
    )(*[pe[n] for n in names], gnorm.reshape(1, -1), state)


HALF = LANES // 2


def _rel_bucket(dist):
    max_exact = REL_BUCKETS // 2
    d = jnp.maximum(dist, 0)
    large = max_exact + (jnp.log(jnp.maximum(d, 1).astype(F32) / max_exact)
                         / math.log(REL_MAX_DIST / max_exact) * (REL_BUCKETS - max_exact)).astype(jnp.int32)
    large = jnp.minimum(large, REL_BUCKETS - 1)
    return jnp.where(d < max_exact, d, large)


def _bias_lookup(rel_bias, dist):
    onehot = (_rel_bucket(dist)[..., None] == jnp.arange(REL_BUCKETS)).astype(F32)
    bias = jnp.einsum("...b,bh->h...", onehot, rel_bias.astype(F32), precision=lax.Precision.HIGHEST)
    return bias * LOG2E


def _swa_bias_tables(rel_bias):
    q = jnp.arange(BLK)[:, None]
    c = jnp.arange(2 * BLK)[None, :]
    dist = BLK + q - c
    bias = _bias_lookup(rel_bias, dist)
    inside = (dist >= 0) & (dist < WINDOW)
    first = inside & (c >= BLK - N_META)
    neg = jnp.float32(-jnp.inf)
    return jnp.stack([jnp.where(first[None], bias, neg), jnp.where(inside[None], bias, neg)])


def _dup_tiles(x):
    lane = lax.broadcasted_iota(jnp.int32, (x.shape[0], LANES), 1)
    low = lane < HALF
    out = []
    for t in range(2):
        tile = x[:, t * LANES:(t + 1) * LANES]
        swapped = pltpu.roll(tile, HALF, axis=1)
        out += [jnp.where(low, tile, swapped).astype(BF16), jnp.where(low, swapped, tile).astype(BF16)]
    return out


def _group_row_heads(j):
    tiles = [j * (GQA_GROUP // 2) + pair for pair in range(GQA_GROUP // 2)]
    return [2 * t for t in tiles] + [2 * t + 1 for t in tiles]


def _swa_body(nblk, q_ref, kc_ref, vc_ref, kp_ref, vp_ref, km_ref, vm_ref, gb_ref, yg_ref, tb_ref, sink_ref,
              o_ref, kw_ref, vw_ref):
    blk = pl.program_id(1)
    first = blk == 0

    @pl.when(blk == nblk - 1)
    def _():
        kw_ref[...] = kc_ref[...]
        vw_ref[...] = vc_ref[...]

    k_prev = jnp.where(first, km_ref[...], kp_ref[...])
    v_prev = jnp.where(first, vm_ref[...], vp_ref[...])
    k_tiles = _dup_tiles(jnp.concatenate([k_prev, kc_ref[...]], axis=0))
    v_tiles = _dup_tiles(jnp.concatenate([v_prev, vc_ref[...]], axis=0))
    variant = jnp.minimum(blk, 1)
    npair = GQA_GROUP // 2
    low = lax.broadcasted_iota(jnp.int32, (BLK, LANES), 1) < HALF
    ones = jnp.ones((2 * BLK, LANES), BF16)
    seg = lax.broadcasted_iota(jnp.int32, (GQA_GROUP * BLK, 1), 0) // BLK
    nt = (((1,), (1,)), ((), ()))
    for j in range(ATT_KV_HEADS):
        tiles = [j * npair + pair for pair in range(npair)]
        q_t = [q_ref[:, t * LANES:(t + 1) * LANES] for t in tiles]
        zero = jnp.zeros_like(q_t[0])
        q_st = jnp.concatenate([jnp.where(low, q, zero) for q in q_t] + [jnp.where(low, zero, q) for q in q_t], axis=0)
        heads = _group_row_heads(j)
        s = lax.dot_general(q_st, k_tiles[j], nt, preferred_element_type=F32)
        s = s + jnp.concatenate([tb_ref[variant, h] for h in heads], axis=0)
        sink = jnp.full((GQA_GROUP * BLK, 1), sink_ref[heads[0]], F32)
        for i in range(1, GQA_GROUP):
            sink = jnp.where(seg == i, sink_ref[heads[i]], sink)
        m = jnp.maximum(jnp.max(s, axis=-1, keepdims=True), sink)
        p = jnp.exp2(s - m).astype(BF16)
        pv = jnp.dot(p, jnp.concatenate([v_tiles[j], ones], axis=1), preferred_element_type=F32)
        o = pv[:, :LANES] / (pv[:, LANES:] + jnp.exp2(sink - m))
        for pair, t in enumerate(tiles):
            cols = slice(t * LANES, (t + 1) * LANES)
            gate = _sigmoid(gb_ref[:, cols].astype(F32))
            even = o[pair * BLK:(pair + 1) * BLK]
            odd = o[(npair + pair) * BLK:(npair + pair + 1) * BLK]
            o_ref[:, cols] = (gate * jnp.where(low, even, odd) + yg_ref[:, cols].astype(F32)).astype(o_ref.dtype)


def _swa_prompt(pp, pe, yg, rel_bias, sinks, nbatch, nblk):
    tb = _swa_bias_tables(rel_bias)
    kvw = ATT_KV_HEADS * HEAD_DIM
    sinks2 = sinks.astype(F32) * LOG2E
    cur = lambda b, c: (b * nblk + c, 0)
    prev = lambda b, c: (b * nblk + jnp.maximum(c - 1, 0), 0)
    return pl.pallas_call(
        functools.partial(_swa_body, nblk),
        grid=(nbatch, nblk),
        in_specs=[
            pl.BlockSpec((BLK, D_MODEL), cur),
            pl.BlockSpec((BLK, kvw), cur), pl.BlockSpec((BLK, kvw), cur),
            pl.BlockSpec((BLK, kvw), prev), pl.BlockSpec((BLK, kvw), prev),
            pl.BlockSpec((BLK, kvw), lambda b, c: (1, 0)), pl.BlockSpec((BLK, kvw), lambda b, c: (1, 0)),
            pl.BlockSpec((BLK, D_MODEL), cur),
            pl.BlockSpec((BLK, D_MODEL), cur),
            pl.BlockSpec(tb.shape, lambda b, c: (0, 0, 0, 0)),
            pl.BlockSpec(memory_space=pltpu.SMEM),
        ],
        out_specs=[pl.BlockSpec((BLK, D_MODEL), cur),
                   pl.BlockSpec((None, BLK, kvw), lambda b, c: (b, 0, 0)),
                   pl.BlockSpec((None, BLK, kvw), lambda b, c: (b, 0, 0))],
        out_shape=[jax.ShapeDtypeStruct((nbatch * nblk * BLK, D_MODEL), BF16),
                   jax.ShapeDtypeStruct((nbatch, BLK, kvw), F32), jax.ShapeDtypeStruct((nbatch, BLK, kvw), F32)],
        compiler_params=_cparams(("arbitrary", "arbitrary")),
        name="swa_prompt",
    )(pp["qa"], pp["ka"], pp["va"], pp["ka"], pp["va"], pe["ka"], pe["va"], pp["gb"], yg, tb, sinks2)


SWA_STEP_SEQS = 8
Q_TILES = ATT_HEADS // 2


def _swa_step_body(q_ref, kn_ref, vn_ref, ck_ref, cv_ref, gb_ref, yg_ref, tb_ref, sink_ref,
                   o_ref, ko_ref, vo_ref):
    row = lax.broadcasted_iota(jnp.int32, (WINDOW, ATT_KV_HEADS * HEAD_DIM), 0)
    low = lax.broadcasted_iota(jnp.int32, (Q_TILES, LANES), 1) < HALF
    mine16 = (lax.broadcasted_iota(jnp.int32, (2 * Q_TILES, LANES), 0) % Q_TILES) // (GQA_GROUP // 2)
    nt = (((1,), (1,)), ((), ()))
    seqs = range(SWA_STEP_SEQS)
    k_wide, v_wide = [], []
    for i in seqs:
        k_win = jnp.where(row == WINDOW - 1, kn_ref[i:i + 1, :], pltpu.roll(ck_ref[i], WINDOW - 1, axis=0))
        v_win = jnp.where(row == WINDOW - 1, vn_ref[i:i + 1, :], pltpu.roll(cv_ref[i], WINDOW - 1, axis=0))
        ko_ref[i] = k_win
        vo_ref[i] = v_win
        k_wide.append(jnp.concatenate(_dup_tiles(k_win), axis=1))
        v_wide.append(jnp.concatenate(_dup_tiles(v_win), axis=1))
    scores = []
    for i in seqs:
        q8 = q_ref[i]
        q16 = jnp.concatenate([jnp.where(low, q8, 0.0), jnp.where(low, 0.0, q8)], axis=0)
        q_wide = jnp.concatenate([jnp.where(mine16 == j, q16, 0.0) for j in range(ATT_KV_HEADS)], axis=1)
        scores.append(lax.dot_general(q_wide.astype(BF16), k_wide[i], nt, preferred_element_type=F32))
    probs, invs = [], []
    for i in seqs:
        s = scores[i] + tb_ref[...]
        sink = sink_ref[...]
        m = jnp.maximum(jnp.max(s, axis=-1, keepdims=True), sink)
        p = jnp.exp2(s - m)
        invs.append(1.0 / (jnp.sum(p, axis=-1, keepdims=True) + jnp.exp2(sink - m)))
        probs.append(p.astype(BF16))
    outs = [jnp.dot(probs[i], v_wide[i], preferred_element_type=F32) for i in seqs]
    for i in seqs:
        o = jnp.zeros((2 * Q_TILES, LANES), F32)
        for j in range(ATT_KV_HEADS):
            o = jnp.where(mine16 == j, outs[i][:, j * LANES:(j + 1) * LANES], o)
        o = o * invs[i]
        o = jnp.where(low, o[:Q_TILES], o[Q_TILES:])
        o_ref[i] = _sigmoid(gb_ref[i]) * o + yg_ref[i]


def _swa_step(pe, yg_s, cache_k, cache_v, rel_bias, sinks):
    nseq = cache_k.shape[0]
    kvw = ATT_KV_HEADS * HEAD_DIM
    as_tiles = lambda x: x[:nseq].astype(F32).reshape(nseq, Q_TILES, LANES)
    dist = (WINDOW - 1) - jnp.arange(WINDOW)
    bias = _bias_lookup(rel_bias, dist)
    tb = jnp.concatenate([bias[0::2], bias[1::2]], axis=0)
    sk = jnp.concatenate([sinks[0::2], sinks[1::2]])[:, None].astype(F32) * LOG2E
    g = SWA_STEP_SEQS
    tile_spec = pl.BlockSpec((g, Q_TILES, LANES), lambda s: (s, 0, 0))
    win_spec = pl.BlockSpec((g, WINDOW, kvw), lambda s: (s, 0, 0))
    new_spec = pl.BlockSpec((g, kvw), lambda s: (s, 0))
    o, ko, vo = pl.pallas_call(
        _swa_step_body,
        grid=(nseq // g,),
        in_specs=[tile_spec, new_spec, new_spec, win_spec, win_spec, tile_spec, tile_spec,
                  pl.BlockSpec(tb.shape, lambda s: (0, 0)), pl.BlockSpec(sk.shape, lambda s: (0, 0))],
        out_specs=[tile_spec, win_spec, win_spec],
        out_shape=[jax.ShapeDtypeStruct((nseq, Q_TILES, LANES), F32),
                   jax.ShapeDtypeStruct(cache_k.shape, F32), jax.ShapeDtypeStruct(cache_v.shape, F32)],
        compiler_params=_cparams(("arbitrary",)),
        name="swa_step",
    )(as_tiles(pe["qa"]), pe["ka"], pe["va"], cache_k, cache_v, as_tiles(pe["gb"]),
      yg_s.reshape(nseq, Q_TILES, LANES), tb, sk)
    return o.reshape(nseq, D_MODEL), ko, vo


ROUTER_ROWS = 40
META_ROWS = 8


def _split3_nt(a_hi, a_lo, x):
    nt = (((1,), (1,)), ((), ()))
    x_hi = x.astype(BF16)
    x_lo = (x - x_hi.astype(F32)).astype(BF16)
    return (lax.dot_general(a_hi, x_hi, nt, preferred_element_type=F32)
            + lax.dot_general(a_hi, x_lo, nt, preferred_element_type=F32)
            + lax.dot_general(a_lo, x_hi, nt, preferred_element_type=F32))


def _first_argmax_rows(v, ridx, nrows):
    vmax = jnp.max(v, axis=0, keepdims=True)
    idx = jnp.min(jnp.where(v == vmax, ridx, nrows), axis=0, keepdims=True)
    return vmax, idx


def _post_body(nsteps, *refs):
    h1_ref, meta_ref, wcol_ref = refs[-5:-2]
    i = pl.program_id(0)

    @pl.when(i < nsteps)
    def _():
        _post_tile(i, *refs)

    @pl.when(i >= nsteps)
    def _():
        h1_ref[...] = jnp.zeros_like(h1_ref)
        meta_ref[...] = jnp.zeros_like(meta_ref)
        wcol_ref[...] = jnp.zeros_like(wcol_ref)


def _post_tile(i, mg_ref, x_ref, lng_ref, lnb_ref, wo_ref, g1_ref, b1_ref, wrh_ref, wrl_ref, rb_ref, ut_ref,
               cin_ref, *rest):
    h1_ref, meta_ref, wcol_ref, cout_ref, carry_ref = rest[-5:]

    @pl.when(i == 0)
    def _():
        carry_ref[...] = cin_ref[...]

    tm = x_ref.shape[0]
    h = _layer_norm_rows(x_ref[...], lng_ref[...], lnb_ref[...])
    acc = jnp.dot(mg_ref[...].astype(BF16), wo_ref[...], preferred_element_type=F32)
    h1 = _layer_norm_rows(ALPHA * h + acc, g1_ref[...], b1_ref[...])
    h1_ref[...] = h1

    lt = _split3_nt(wrh_ref[...], wrl_ref[...], h1) + rb_ref[:, :1]
    ridx = lax.broadcasted_iota(jnp.int32, (EXPERTS_PER_GROUP, tm), 0)
    neg = jnp.float32(-jnp.inf)
    g_log = jnp.where(ridx < N_GROUPS, lt[N_EXPERTS:N_EXPERTS + EXPERTS_PER_GROUP], neg)
    g_max, grp = _first_argmax_rows(g_log, ridx, EXPERTS_PER_GROUP)
    p_grp = 1.0 / jnp.sum(jnp.exp(g_log - g_max), axis=0, keepdims=True)
    e_in = lt[0:EXPERTS_PER_GROUP]
    for gi in range(1, N_GROUPS):
        e_in = jnp.where(grp == gi, lt[gi * EXPERTS_PER_GROUP:(gi + 1) * EXPERTS_PER_GROUP], e_in)
    v0, i0 = _first_argmax_rows(e_in, ridx, EXPERTS_PER_GROUP)
    v1, i1 = _first_argmax_rows(jnp.where(ridx == i0, neg, e_in), ridx, EXPERTS_PER_GROUP)
    t = jnp.exp(v1 - v0)
    w0 = p_grp / (1.0 + t)
    w1 = p_grp * t / (1.0 + t)
    e0 = grp * EXPERTS_PER_GROUP + i0
    e1 = grp * EXPERTS_PER_GROUP + i1

    eidx = lax.broadcasted_iota(jnp.int32, (N_EXPERTS, tm), 0)
    hit0 = eidx == e0
    hit1 = eidx == e1
    oh = jnp.where(jnp.logical_or(hit0, hit1), 1.0, 0.0)
    before = jnp.dot(oh.astype(BF16), ut_ref[...], preferred_element_type=F32) + carry_ref[:, :1]
    r0 = jnp.sum(jnp.where(hit0, before, 0.0), axis=0, keepdims=True).astype(jnp.int32)
    r1 = jnp.sum(jnp.where(hit1, before, 0.0), axis=0, keepdims=True).astype(jnp.int32)
    carry_ref[...] = carry_ref[...] + jnp.sum(oh, axis=1, keepdims=True)
    cout_ref[...] = carry_ref[...]

    zi = jnp.zeros((META_ROWS - 4, tm), jnp.int32)
    meta_ref[...] = jnp.concatenate([e0, e1, r0, r1, zi], axis=0)
    wt = jnp.concatenate([w0, w1, jnp.zeros((LANES - 2, tm), F32)], axis=0)
    wcol_ref[...] = jnp.transpose(wt)


def _post(mg, x2d, prm, tm, row0, total_rows, carry_in, prev=None, zero_tail=False):
    m = x2d.shape[0]
    assert m % tm == 0 and row0 % tm == 0 and total_rows % tm == 0
    off = row0 // tm
    nsteps = m // tm
    last = nsteps - 1
    ut = jnp.asarray(np.triu(np.ones((tm, tm), np.float32), 1), dtype=BF16)
    full = lambda shape: pl.BlockSpec(shape, lambda i: (0,) * len(shape))
    in_specs = [
        pl.BlockSpec((tm, D_MODEL), lambda i: (jnp.minimum(i, last), 0)),
        pl.BlockSpec((tm, D_MODEL), lambda i: (jnp.minimum(i, last), 0)),
        full((1, D_MODEL)), full((1, D_MODEL)),
        full((D_MODEL, D_MODEL)),
        full((1, D_MODEL)), full((1, D_MODEL)),
        full((ROUTER_ROWS, D_MODEL)), full((ROUTER_ROWS, D_MODEL)), full((ROUTER_ROWS, LANES)),
        full((tm, tm)),
        full((N_EXPERTS, LANES)),
    ]
    args = [mg, x2d, prm["ln_emb_g"], prm["ln_emb_b"], prm["w_out"], prm["ln1_g"], prm["ln1_b"],
            prm["wr_hi"], prm["wr_lo"], prm["r_bias"], ut, carry_in]
    aliases = {}
    if prev is not None:
        for k, buf in enumerate(prev):
            in_specs.append(pl.BlockSpec(memory_space=pl.ANY))
            aliases[len(args)] = k
            args.append(buf)
    out_shape = [
        jax.ShapeDtypeStruct((total_rows, D_MODEL), F32),
        jax.ShapeDtypeStruct((META_ROWS, total_rows), jnp.int32),
        jax.ShapeDtypeStruct((total_rows, LANES), F32),
        jax.ShapeDtypeStruct((N_EXPERTS, LANES), F32),
    ]
    out_specs = [
        pl.BlockSpec((tm, D_MODEL), lambda i: (i + off, 0)),
        pl.BlockSpec((META_ROWS, tm), lambda i: (0, i + off)),
        pl.BlockSpec((tm, LANES), lambda i: (i + off, 0)),
        full((N_EXPERTS, LANES)),
    ]
    if prev is not None:
        assert len(prev) == 3
    return pl.pallas_call(
        functools.partial(_post_body, nsteps),
        grid=(nsteps + int(zero_tail),),
        in_specs=in_specs,
        out_specs=out_specs,
        out_shape=out_shape,
        input_output_aliases=aliases,
        scratch_shapes=[pltpu.VMEM((N_EXPERTS, LANES), F32)],
        compiler_params=_cparams(("arbitrary",)),
        name="post_attn",
    )(*args)


def _prep_post_params(ln_emb_g, ln_emb_b, w_out, ln1_g, ln1_b, w_rg, b_rg, w_re, b_re):
    row = lambda v: v.reshape(1, -1)
    wr = jnp.concatenate([w_re.T, w_rg.T, jnp.zeros((ROUTER_ROWS - N_EXPERTS - N_GROUPS, D_MODEL), F32)], axis=0)
    wr_hi = wr.astype(BF16)
    wr_lo = (wr - wr_hi.astype(F32)).astype(BF16)
    rb = jnp.concatenate([b_re, b_rg, jnp.zeros((ROUTER_ROWS - N_EXPERTS - N_GROUPS,), F32)])
    return dict(ln_emb_g=row(ln_emb_g), ln_emb_b=row(ln_emb_b), w_out=w_out.astype(BF16), ln1_g=row(ln1_g),
                ln1_b=row(ln1_b), wr_hi=wr_hi, wr_lo=wr_lo,
                r_bias=jnp.broadcast_to(rb[:, None], (ROUTER_ROWS, LANES)))


MOE_ROWS = 256
SUBLANES = 8
assert D_MODEL == SUBLANES * LANES


def _store_rows_as_tiles(ref, x):
    n = x.shape[0]
    for c in range(SUBLANES):
        ref[pl.ds(c, n, stride=SUBLANES), :] = x[:, c * LANES:(c + 1) * LANES]


def _load_rows_from_tiles(ref, n):
    return jnp.concatenate([ref[pl.ds(c, n, stride=SUBLANES), :] for c in range(SUBLANES)], axis=1)


def _tile_of_row(ref, r):
    return ref.at[pl.ds(pl.multiple_of(r * SUBLANES, SUBLANES), SUBLANES)]


def _moe_plan(counts, total_assign):
    nb_max = -(-total_assign // MOE_ROWS) + N_EXPERTS
    padded = (counts + MOE_ROWS - 1) // MOE_ROWS * MOE_ROWS
    pend = jnp.cumsum(padded)
    pstart = (pend - padded).astype(jnp.int32)
    block_start = jnp.arange(nb_max, dtype=jnp.int32) * MOE_ROWS
    n_ended = jnp.sum((pend[None, :] <= block_start[:, None]).astype(jnp.int32), axis=1)
    block_e = jnp.minimum(n_ended, N_EXPERTS - 1).astype(jnp.int32)
    n_used = (pend[-1] // MOE_ROWS).astype(jnp.int32).reshape(1)
    tail_start = jnp.where(padded > 0, pend - MOE_ROWS, -1)
    spare = pend[-1] + jnp.arange(N_EXPERTS, dtype=pend.dtype) * MOE_ROWS
    spare = jnp.where(spare < nb_max * MOE_ROWS, spare, -1)
    zero_blocks = jnp.concatenate([tail_start, spare]).astype(jnp.int32)
    return pstart, block_e, n_used, nb_max, zero_blocks


ROW_UNROLL = 8


def _slot_ids(meta, pstart):
    experts = meta[0:TOP_K]
    ranks = meta[TOP_K:2 * TOP_K]
    onehot = experts[..., None] == jnp.arange(N_EXPERTS, dtype=jnp.int32)
    return ranks + jnp.sum(jnp.where(onehot, pstart, 0), axis=-1)


def _for_row_groups(tm, fn):
    def group(g, c):
        t0 = pl.multiple_of(g * ROW_UNROLL, ROW_UNROLL)
        for r in range(ROW_UNROLL):
            for k in range(TOP_K):
                fn(t0 + r, k)
        return c

    lax.fori_loop(0, tm // ROW_UNROLL, group, 0)


def _dispatch_body(nsteps, tail_ref, s0_ref, s1_ref, h_ref, xs_ref, pk_ref, zero_ref, sems, zsem):
    i = pl.program_id(0)
    tm = h_ref.shape[0]
    slot_refs = (s0_ref, s1_ref)
    cur = i % 2

    blk_tiles = MOE_ROWS * SUBLANES

    @pl.when(i == 0)
    def _():
        zero_ref[...] = jnp.zeros_like(zero_ref)
        for e in range(2 * N_EXPERTS):
            @pl.when(tail_ref[e] >= 0)
            def _():
                start = pl.multiple_of(tail_ref[e] * SUBLANES, blk_tiles)
                pltpu.make_async_copy(zero_ref, xs_ref.at[pl.ds(start, blk_tiles)], zsem).start()
        for e in range(2 * N_EXPERTS):
            @pl.when(tail_ref[e] >= 0)
            def _():
                pltpu.make_async_copy(zero_ref, xs_ref.at[pl.ds(0, blk_tiles)], zsem).wait()

    _store_rows_as_tiles(pk_ref.at[cur], h_ref[...])

    def send(t, k):
        pltpu.make_async_copy(_tile_of_row(pk_ref.at[cur], t), _tile_of_row(xs_ref, slot_refs[k][0, t]),
                              sems.at[cur]).start(priority=k)

    def wait_buffer(buf):
        _for_row_groups(tm, lambda t, k: pltpu.make_async_copy(
            _tile_of_row(pk_ref.at[buf], t), _tile_of_row(xs_ref, 0), sems.at[buf]).wait())

    _for_row_groups(tm, send)

    @pl.when(i > 0)
    def _():
        wait_buffer(1 - cur)

    @pl.when(i == nsteps - 1)
    def _():
        wait_buffer(cur)


def _dispatch(h1, slot_ids, tail_start, nslots, tm, total):
    assert total % tm == 0 and total <= h1.shape[0] and tm % ROW_UNROLL == 0
    slot_spec = pl.BlockSpec((1, tm), lambda i, tl: (0, i), memory_space=pltpu.SMEM)
    return pl.pallas_call(
        functools.partial(_dispatch_body, total // tm),
        grid_spec=pltpu.PrefetchScalarGridSpec(
            num_scalar_prefetch=1,
            grid=(total // tm,),
            in_specs=[slot_spec, slot_spec, pl.BlockSpec((tm, D_MODEL), lambda i, tl: (i, 0))],
            out_specs=pl.BlockSpec(memory_space=pl.ANY),
            scratch_shapes=[pltpu.VMEM((2, tm * SUBLANES, LANES), F32), pltpu.VMEM((MOE_ROWS * SUBLANES, LANES), F32),
                            pltpu.SemaphoreType.DMA((2,)), pltpu.SemaphoreType.DMA(())],
        ),
        out_shape=jax.ShapeDtypeStruct((nslots * SUBLANES, LANES), F32),
        compiler_params=_cparams(("arbitrary",)),
        name="moe_dispatch",
    )(tail_start, slot_ids[0:1], slot_ids[1:2], h1)


def _expert_schedule(block_e, n_used):
    nb = block_e.shape[0]
    idx = jnp.arange(nb, dtype=jnp.int32)
    first = (idx < n_used[0]) & ((idx == 0) | (block_e != jnp.roll(block_e, 1)))
    parity = (jnp.cumsum(first.astype(jnp.int32)) - 1) % 2
    pos = jnp.where(first, idx, nb)
    at_or_after = jnp.flip(lax.cummin(jnp.flip(pos)))
    nxt = jnp.concatenate([at_or_after[1:], jnp.full((1,), nb, jnp.int32)])
    nexte = jnp.where(nxt < nb, block_e[jnp.minimum(nxt, nb - 1)], -1)
    return first.astype(jnp.int32), nexte.astype(jnp.int32), parity.astype(jnp.int32)


def _expert_body(be_ref, nu_ref, first_ref, nexte_ref, par_ref, xs_ref, wg_hbm, wu_hbm, wd_hbm, ys_ref,
                 wgf_ref, wuf_ref, wdf_ref, wgb_ref, wub_ref, wdb_ref, sems):
    i = pl.program_id(0)
    hbm = (wg_hbm, wu_hbm, wd_hbm)
    stage = (wgf_ref, wuf_ref, wdf_ref)

    def weight_copies(e, buf):
        return [pltpu.make_async_copy(hbm[w].at[e], stage[w].at[buf], sems.at[buf, w]) for w in range(3)]

    @pl.when(first_ref[i] == 1)
    def _():
        buf = par_ref[i]

        @pl.when(i == 0)
        def _():
            for c in weight_copies(be_ref[0], buf):
                c.start()

        for c in weight_copies(be_ref[i], buf):
            c.wait()

        @pl.when(nexte_ref[i] >= 0)
        def _():
            for c in weight_copies(nexte_ref[i], 1 - buf):
                c.start()

        wgb_ref[...] = wgf_ref[buf].astype(BF16)
        wub_ref[...] = wuf_ref[buf].astype(BF16)
        wdb_ref[...] = wdf_ref[buf].astype(BF16)

    @pl.when(i < nu_ref[0])
    def _():
        x = _load_rows_from_tiles(xs_ref, MOE_ROWS).astype(BF16)
        g = jnp.dot(x, wgb_ref[...], preferred_element_type=F32)
        hb = (g * _sigmoid(g)) * jnp.dot(x, wub_ref[...], preferred_element_type=F32)
        y = jnp.dot(hb.astype(BF16), wdb_ref[...], preferred_element_type=F32)
        _store_rows_as_tiles(ys_ref, y)

    @pl.when(i >= nu_ref[0])
    def _():
        ys_ref[...] = jnp.zeros_like(ys_ref)


def _experts(xs, block_e, n_used, w_gate, w_up, w_down, nb_max):
    first, nexte, parity = _expert_schedule(block_e, n_used)
    rows = lambda i, *_: (i, 0)
    used_rows = lambda i, be, nu, *_: (jnp.maximum(jnp.minimum(i, nu[0] - 1), 0), 0)
    any_spec = pl.BlockSpec(memory_space=pl.ANY)
    blk = (MOE_ROWS * SUBLANES, LANES)
    return pl.pallas_call(
        _expert_body,
        grid_spec=pltpu.PrefetchScalarGridSpec(
            num_scalar_prefetch=5,
            grid=(nb_max,),
            in_specs=[pl.BlockSpec(blk, used_rows), any_spec, any_spec, any_spec],
            out_specs=pl.BlockSpec(blk, rows),
            scratch_shapes=[
                pltpu.VMEM((2, D_MODEL, D_EXPERT), F32), pltpu.VMEM((2, D_MODEL, D_EXPERT), F32),
                pltpu.VMEM((2, D_EXPERT, D_MODEL), F32),
                pltpu.VMEM((D_MODEL, D_EXPERT), BF16), pltpu.VMEM((D_MODEL, D_EXPERT), BF16),
                pltpu.VMEM((D_EXPERT, D_MODEL), BF16),
                pltpu.SemaphoreType.DMA((2, 3)),
            ],
        ),
        out_shape=jax.ShapeDtypeStruct(xs.shape, F32),
        compiler_params=_cparams(("arbitrary",)),
        name="moe_experts",
    )(block_e, n_used, first, nexte, parity, xs, w_gate, w_up, w_down)


def _combine_body(nsteps, s0_ref, s1_ref, n0_ref, n1_ref, h_ref, w_ref, g2_ref, b2_ref, ys_ref, o_ref, buf_ref, sems):
    i = pl.program_id(0)
    tm = h_ref.shape[0]
    cur = i % 2

    def fetch(slot_refs, buf):
        _for_row_groups(tm, lambda t, k: pltpu.make_async_copy(
            _tile_of_row(ys_ref, slot_refs[k][0, t]), _tile_of_row(buf_ref.at[buf, k], t),
            sems.at[buf]).start(priority=k))

    @pl.when(i == 0)
    def _():
        fetch((s0_ref, s1_ref), cur)

    @pl.when(i + 1 < nsteps)
    def _():
        fetch((n0_ref, n1_ref), 1 - cur)

    _for_row_groups(tm, lambda t, k: pltpu.make_async_copy(
        _tile_of_row(ys_ref, 0), _tile_of_row(buf_ref.at[cur, k], t), sems.at[cur]).wait())
    w = w_ref[...]
    f = (w[:, 0:1] * _load_rows_from_tiles(buf_ref.at[cur, 0], tm)
         + w[:, 1:2] * _load_rows_from_tiles(buf_ref.at[cur, 1], tm))
    o_ref[...] = _layer_norm_rows(ALPHA * h_ref[...] + f, g2_ref[...], b2_ref[...])


def _combine(h1, wcol, slot_ids, ys, ln2_g, ln2_b, tm, row0, nrows):
    assert nrows % tm == 0 and row0 % tm == 0 and tm % ROW_UNROLL == 0
    off = row0 // tm
    nsteps = nrows // tm
    slot_spec = pl.BlockSpec((1, tm), lambda i: (0, i + off), memory_space=pltpu.SMEM)
    next_spec = pl.BlockSpec((1, tm), lambda i: (0, jnp.minimum(i + 1, nsteps - 1) + off), memory_space=pltpu.SMEM)
    return pl.pallas_call(
        functools.partial(_combine_body, nsteps),
        grid=(nsteps,),
        in_specs=[
            slot_spec, slot_spec, next_spec, next_spec,
            pl.BlockSpec((tm, D_MODEL), lambda i: (i + off, 0)),
            pl.BlockSpec((tm, LANES), lambda i: (i + off, 0)),
            pl.BlockSpec((1, D_MODEL), lambda i: (0, 0)),
            pl.BlockSpec((1, D_MODEL), lambda i: (0, 0)),
            pl.BlockSpec(memory_space=pl.ANY),
        ],
        out_specs=pl.BlockSpec((tm, D_MODEL), lambda i: (i, 0)),
        scratch_shapes=[pltpu.VMEM((2, TOP_K, tm * SUBLANES, LANES), F32), pltpu.SemaphoreType.DMA((2,))],
        out_shape=jax.ShapeDtypeStruct((nrows, D_MODEL), F32),
        compiler_params=_cparams(("arbitrary",)),
        name="moe_combine",
    )(slot_ids[0:1], slot_ids[1:2], slot_ids[0:1], slot_ids[1:2], h1, wcol, ln2_g.reshape(1, -1),
      ln2_b.reshape(1, -1), ys)


PROJ_ROWS = 512
POST_ROWS = 512
DISPATCH_ROWS = 384
COMBINE_ROWS = 256


def kernel(x_prompt, x_sample, state_gla, cache_swa_k, cache_swa_v, meta_tokens, ln_emb_g, ln_emb_b, rel_bias, w_in,
           gk_up, gk_bias, gla_norm_g, sinks, w_out, ln1_g, ln1_b, w_router_group, b_router_group, w_router_expert,
           b_router_expert, w_gate, w_up, w_down, ln2_g, ln2_b):
    nbatch, seq, d = x_prompt.shape
    nseq = x_sample.shape[0]
    assert w_in.shape[0] == DEPTH == 1 and d == D_MODEL and x_sample.shape[1] == 1
    assert seq % BLK == 0 and nseq == BLK and meta_tokens.shape[0] == N_META
    nblk = seq // BLK
    n_prompt = nbatch * seq
    total = n_prompt + nseq
    kvw = ATT_KV_HEADS * HEAD_DIM

    xp = x_prompt.reshape(n_prompt, d)
    xs = x_sample.reshape(nseq, d)
    extra = jnp.concatenate([xs, jnp.zeros((BLK - N_META, d), xs.dtype), meta_tokens.astype(xs.dtype)], axis=0)
    w_cat = _prep_w_in(w_in[0])
    pp = _ln_proj(xp, ln_emb_g, ln_emb_b, w_cat, gk_up[0], gk_bias[0], PROJ_ROWS)
    pe = _ln_proj(extra, ln_emb_g, ln_emb_b, w_cat, gk_up[0], gk_bias[0], 2 * BLK)

    yg, gla_p = _gla_prompt(pp, pe, gla_norm_g[0], nbatch, nblk)
    yg_s, gla_s = _gla_step(pe, gla_norm_g[0], state_gla[0])
    mg, k_win, v_win = _swa_prompt(pp, pe, yg, rel_bias, sinks[0], nbatch, nblk)
    mg_s, k_s, v_s = _swa_step(pe, yg_s, cache_swa_k[0].reshape(nseq, WINDOW, kvw),
                               cache_swa_v[0].reshape(nseq, WINDOW, kvw), rel_bias, sinks[0])

    prm = _prep_post_params(ln_emb_g, ln_emb_b, w_out[0], ln1_g[0], ln1_b[0], w_router_group[0], b_router_group[0],
                            w_router_expert[0], b_router_expert[0])
    carry0 = jnp.zeros((N_EXPERTS, LANES), F32)
    rows_alloc = n_prompt + POST_ROWS
    h1, meta, wcol, carry1 = _post(mg, xp, prm, POST_ROWS, 0, rows_alloc, carry0, zero_tail=True)
    h1, meta, wcol, carry2 = _post(mg_s, xs, prm, nseq, n_prompt, rows_alloc, carry1, prev=(h1, meta, wcol))

    counts = carry2[:, 0].astype(jnp.int32)
    pstart, block_e, n_used, nb_max, tail_start = _moe_plan(counts, TOP_K * total)
    slot_ids = _slot_ids(meta, pstart)
    xs_sorted = _dispatch(h1, slot_ids, tail_start, nb_max * MOE_ROWS, DISPATCH_ROWS, total)
    ys = _experts(xs_sorted, block_e, n_used, w_gate[0], w_up[0], w_down[0], nb_max)
    y_p = _combine(h1, wcol, slot_ids, ys, ln2_g[0], ln2_b[0], COMBINE_ROWS, 0, n_prompt)
    y_s = _combine(h1, wcol, slot_ids, ys, ln2_g[0], ln2_b[0], nseq, n_prompt, nseq)

    kv_shape = (1, nbatch, WINDOW, ATT_KV_HEADS, HEAD_DIM)
    k_p = k_win.reshape(kv_shape)
    v_p = v_win.reshape(kv_shape)
    return (y_p.reshape(nbatch, seq, d), y_s.reshape(nseq, 1, d), gla_p[None], k_p, v_p, gla_s[None],
            k_s.reshape(cache_swa_k.shape), v_s.reshape(cache_swa_v.shape))
```

```python
import functools
import math

import jax
import jax.numpy as jnp
import numpy as np
from jax import lax
from jax.experimental import pallas as pl
from jax.experimental.pallas import tpu as pltpu

F32 = jnp.float32
BF16 = jnp.bfloat16

D_MODEL = 1024
N_META = 16
LN_EPS = 1e-5
GLA_HEADS = 4
GLA_DK = 128
GLA_DV = 256
GLA_RANK = 16
GLA_TAU = 16.0
HEAD_DIM = 64
ATT_HEADS = 16
ATT_KV_HEADS = 4
GQA_GROUP = 4
WINDOW = 128
REL_BUCKETS = 32
REL_MAX_DIST = 128
N_GROUPS = 4
EXPERTS_PER_GROUP = 8
N_EXPERTS = 32
TOP_K = 2
D_EXPERT = 512
DEPTH = 1
ALPHA = (2.0 * DEPTH) ** 0.25
LOG2E = math.log2(math.e)

LANES = 128
BLK = 128
VMEM_LIMIT = 56 * 1024 * 1024


def _cparams(sem):
    return pltpu.CompilerParams(dimension_semantics=sem, vmem_limit_bytes=VMEM_LIMIT)


def _layer_norm_rows(x, g, b):
    mu = jnp.mean(x, axis=-1, keepdims=True)
    xc = x - mu
    var = jnp.mean(xc * xc, axis=-1, keepdims=True)
    return xc * lax.rsqrt(var + LN_EPS) * g + b


_PROJ_OUTS = (
    ("lr", LANES, F32, None),
    ("qg", GLA_HEADS * GLA_DK, BF16, GLA_DK ** -0.5),
    ("kg", GLA_HEADS * GLA_DK, BF16, None),
    ("vg", GLA_HEADS * GLA_DV, BF16, None),
    ("rg", GLA_HEADS * GLA_DV, BF16, None),
    ("qa", ATT_HEADS * HEAD_DIM, BF16, HEAD_DIM ** -0.5 * LOG2E),
    ("ka", ATT_KV_HEADS * HEAD_DIM, F32, None),
    ("va", ATT_KV_HEADS * HEAD_DIM, F32, None),
    ("ga", D_MODEL, BF16, None),
    ("gb", D_MODEL, BF16, None),
)
_PROJ_W = sum(w for _, w, _, _ in _PROJ_OUTS)


def _prep_w_in(w_in):
    sizes = (512, 512, 1024, 1024, GLA_RANK, 1024, 256, 256, 1024, 1024)
    offs = np.cumsum((0,) + sizes)
    a = w_in[:, : offs[4]]
    lr = w_in[:, offs[4]: offs[5]]
    b = w_in[:, offs[5]:]
    pad = jnp.zeros((w_in.shape[0], LANES - GLA_RANK), w_in.dtype)
    return jnp.concatenate([lr, pad, a, b], axis=1).astype(BF16)


LAMAX_ROWS = 8


def _log_sigmoid(x):
    return jnp.minimum(x, 0.0) - jnp.log(1.0 + jnp.exp(-jnp.abs(x)))


def _ln_proj_body(x_ref, g_ref, b_ref, w_ref, gkup_ref, gkb_ref, *out_refs):
    la_ref, lamax_ref = out_refs[-2:]
    tm = x_ref.shape[0]
    xn = _layer_norm_rows(x_ref[...], g_ref[...], b_ref[...]).astype(BF16)
    c0 = 0
    plain_refs = iter(out_refs[:-2])
    for name, width, dtype, scale in _PROJ_OUTS:
        acc = jnp.dot(xn, w_ref[:, c0:c0 + width], preferred_element_type=F32)
        c0 += width
        if name == "lr":
            x = jnp.dot(acc.astype(BF16), gkup_ref[...], preferred_element_type=F32) + gkb_ref[...]
            la = _log_sigmoid(x) * (1.0 / GLA_TAU)
            la_ref[...] = la
            rows = []
            for r in range(tm // BLK):
                blk_max = jnp.max(jnp.max(jnp.abs(la[r * BLK:(r + 1) * BLK]), axis=0, keepdims=True), axis=1, keepdims=True)
                rows.append(jnp.broadcast_to(blk_max, (1, LANES)))
            rows.append(jnp.zeros((LAMAX_ROWS - tm // BLK, LANES), F32))
            lamax_ref[...] = jnp.concatenate(rows, axis=0)
            continue
        if scale is not None:
            acc = acc * scale
        next(plain_refs)[...] = acc.astype(dtype)


def _ln_proj(x2d, ln_g, ln_b, w_cat, gk_up, gk_bias, tm):
    m = x2d.shape[0]
    assert m % tm == 0 and tm % BLK == 0 and tm // BLK < LAMAX_ROWS
    names = [n for n, _, _, _ in _PROJ_OUTS if n != "lr"] + ["la", "lamax"]
    ladim = GLA_HEADS * GLA_DK
    gkup = jnp.concatenate([gk_up, jnp.zeros((LANES - GLA_RANK, ladim), gk_up.dtype)], axis=0).astype(BF16)
    out_shape = [jax.ShapeDtypeStruct((m, w), dt) for n, w, dt, _ in _PROJ_OUTS if n != "lr"]
    out_specs = [pl.BlockSpec((tm, w), lambda i: (i, 0)) for n, w, _, _ in _PROJ_OUTS if n != "lr"]
    out_shape += [jax.ShapeDtypeStruct((m, ladim), F32), jax.ShapeDtypeStruct((m // tm * LAMAX_ROWS, LANES), F32)]
    out_specs += [pl.BlockSpec((tm, ladim), lambda i: (i, 0)), pl.BlockSpec((LAMAX_ROWS, LANES), lambda i: (i, 0))]
    outs = pl.pallas_call(
        _ln_proj_body,
        grid=(m // tm,),
        in_specs=[
            pl.BlockSpec((tm, D_MODEL), lambda i: (i, 0)),
            pl.BlockSpec((1, D_MODEL), lambda i: (0, 0)),
            pl.BlockSpec((1, D_MODEL), lambda i: (0, 0)),
            pl.BlockSpec((D_MODEL, _PROJ_W), lambda i: (0, 0), pipeline_mode=pl.Buffered(1)),
            pl.BlockSpec((LANES, ladim), lambda i: (0, 0)),
            pl.BlockSpec((1, ladim), lambda i: (0, 0)),
        ],
        out_specs=out_specs,
        out_shape=out_shape,
        compiler_params=_cparams(("arbitrary",)),
        name="ln_proj",
    )(x2d, ln_g.reshape(1, -1), ln_b.reshape(1, -1), w_cat, gkup, gk_bias.reshape(1, -1))
    res = dict(zip(names, outs))
    res["lamax"] = res["lamax"].reshape(m // tm, LAMAX_ROWS, LANES)[:, :tm // BLK, 0].reshape(m // BLK)
    return res


_GLA_LEVELS = tuple(2 ** i for i in range(int(math.log2(BLK))))
GLA_SAFE_EXPONENT = 60.0


def _sigmoid(x):
    return 0.5 * jnp.tanh(0.5 * x) + 0.5


def _split_dot(a01, x):
    hi = x.astype(BF16)
    lo = (x - hi.astype(F32)).astype(BF16)
    n = x.shape[1]
    both = jnp.dot(a01, jnp.concatenate([hi, lo], axis=1), preferred_element_type=F32)
    return both[:, :n] + both[:, n:]


def _gla_anchor_exponent(b, la, s, row):
    if s == 1:
        return jnp.where(row % 2 == 1, la, 0.0)
    if s == 2:
        la_dn = pltpu.roll(la, 1, axis=0)
        la_up = pltpu.roll(la, BLK - 1, axis=0)
        r = row % 4
        return jnp.where(r == 0, la_up, jnp.where(r == 1, 0.0, jnp.where(r == 2, la, la + la_dn)))
    nb = BLK // (2 * s)
    b3 = b.reshape(nb, 2 * s, b.shape[-1])
    anchor = jnp.broadcast_to(b3[:, s - 1:s, :], b3.shape).reshape(b.shape)
    return -jnp.abs(b - anchor)


def _gla_body(nblk, lamax_ref, qm, km, vm, rm, lam, gam, qp, kp, vp, rp, lap, gap, gn_ref, tri_ref,
              y_ref, s_out_ref, s_ref):
    c = pl.program_id(1)
    is_meta = c == 0
    n_prompt_blocks = pl.num_programs(0) * nblk
    blk_max = lamax_ref[jnp.where(is_meta, n_prompt_blocks, pl.program_id(0) * nblk + c - 1)]

    @pl.when(is_meta)
    def _():
        s_ref[...] = jnp.zeros_like(s_ref)

    def pick(m_ref, p_ref):
        return jnp.where(is_meta, m_ref[...], p_ref[...])

    row = lax.broadcasted_iota(jnp.int32, (BLK, GLA_DK), 0)
    col_t = lax.broadcasted_iota(jnp.int32, (BLK, BLK), 1)
    row_t = lax.broadcasted_iota(jnp.int32, (BLK, BLK), 0)
    live = jnp.logical_or(jnp.logical_not(is_meta), row >= BLK - N_META)
    tri = tri_ref[...]
    q_all, k_all, v_all, r_all = pick(qm, qp), pick(km, kp), pick(vm, vp), pick(rm, rp)
    ga_all = pick(gam, gap)
    la_all = pick(lam, lap)
    nt = (((1,), (1,)), ((), ()))
    mid = BLK // 2 - 1

    def head(h, single_anchor):
        dk = slice(h * GLA_DK, (h + 1) * GLA_DK)
        dv = slice(h * GLA_DV, (h + 1) * GLA_DV)
        la = jnp.where(live, la_all[:, dk], 0.0)
        q = q_all[:, dk].astype(F32)
        k = jnp.where(live, k_all[:, dk].astype(F32), 0.0)
        v = v_all[:, dv]
        b = _split_dot(tri, la)
        yield
        b_last = b[BLK - 1:BLK, :]
        s_old = s_ref[h]
        if single_anchor:
            b_mid = b[mid:mid + 1, :]
            qe = q * jnp.exp(b - b_mid)
            ke = k * jnp.exp(b_mid - b)
            a = jnp.where(row_t >= col_t,
                          lax.dot_general(qe.astype(BF16), ke.astype(BF16), nt, preferred_element_type=F32), 0.0)
            qg = qe * jnp.exp(b_mid)
            kd = ke * jnp.exp(b_last - b_mid)
        else:
            a = jnp.where(row_t == col_t,
                          lax.dot_general(q.astype(BF16), k.astype(BF16), nt, preferred_element_type=F32), 0.0)
            for s in _GLA_LEVELS:
                e = jnp.exp(_gla_anchor_exponent(b, la, s, row))
                upper = (row // s) % 2 == 1
                q_s = jnp.where(upper, q * e, 0.0).astype(BF16)
                k_s = jnp.where(upper, 0.0, k * e).astype(BF16)
                p = lax.dot_general(q_s, k_s, nt, preferred_element_type=F32)
                a = a + jnp.where(row_t // (2 * s) == col_t // (2 * s), p, 0.0)
            qg = q * jnp.exp(b)
            kd = k * jnp.exp(b_last - b)
        yield
        o = jnp.dot(qg.astype(BF16), s_old.astype(BF16), preferred_element_type=F32)
        lhs = jnp.concatenate([jnp.transpose(kd).astype(BF16), a.astype(BF16)], axis=0)
        both = jnp.dot(lhs, v, preferred_element_type=F32)
        yield
        decay_col = jnp.transpose(jnp.broadcast_to(jnp.exp(b_last), (BLK, GLA_DK)))[:, :1]
        s_ref[h] = decay_col * s_old + both[:GLA_DK]
        o = o + both[GLA_DK:]
        o = o * lax.rsqrt(jnp.mean(o * o, axis=-1, keepdims=True) + LN_EPS) * gn_ref[...]
        r = r_all[:, dv].astype(F32)
        y = o * (r * _sigmoid(r)) * _sigmoid(ga_all[:, dv].astype(F32))
        y_ref[:, dv] = y.astype(y_ref.dtype)

    mild = blk_max * (BLK // 2) <= GLA_SAFE_EXPONENT

    def all_heads(single_anchor):
        running = [head(h, single_anchor) for h in range(GLA_HEADS)]
        while running:
            running = [g for g in running if next(g, True) is None]

    @pl.when(mild)
    def _():
        all_heads(True)

    @pl.when(jnp.logical_not(mild))
    def _():
        all_heads(False)

    @pl.when(c == nblk)
    def _():
        s_out_ref[...] = s_ref[...]


def _tri_incl():
    i = np.arange(BLK)
    return jnp.asarray((i[None, :] <= i[:, None]).astype(np.float32), dtype=BF16)


def _gla_prompt(pp, pe, gnorm, nbatch, nblk):
    names = ("qg", "kg", "vg", "rg", "la", "ga")
    lamax = jnp.concatenate([pp["lamax"], pe["lamax"][1:2]])
    m_specs = [pl.BlockSpec((BLK, pe[n].shape[1]), lambda b, c, lm: (1, 0)) for n in names]
    p_specs = [pl.BlockSpec((BLK, pp[n].shape[1]), lambda b, c, lm: (b * nblk + jnp.maximum(c - 1, 0), 0))
               for n in names]
    w_specs = [
        pl.BlockSpec((1, GLA_DV), lambda b, c, lm: (0, 0)),
        pl.BlockSpec((BLK, BLK), lambda b, c, lm: (0, 0)),
    ]
    y, s_fin = pl.pallas_call(
        functools.partial(_gla_body, nblk),
        grid_spec=pltpu.PrefetchScalarGridSpec(
            num_scalar_prefetch=1,
            grid=(nbatch, nblk + 1),
            in_specs=m_specs + p_specs + w_specs,
            out_specs=[
                pl.BlockSpec((BLK, D_MODEL), lambda b, c, lm: (b * nblk + jnp.maximum(c - 1, 0), 0)),
                pl.BlockSpec((None, GLA_HEADS, GLA_DK, GLA_DV), lambda b, c, lm: (b, 0, 0, 0)),
            ],
            scratch_shapes=[pltpu.VMEM((GLA_HEADS, GLA_DK, GLA_DV), F32)],
        ),
        out_shape=[
            jax.ShapeDtypeStruct((nbatch * nblk * BLK, D_MODEL), BF16),
            jax.ShapeDtypeStruct((nbatch, GLA_HEADS, GLA_DK, GLA_DV), F32),
        ],
        compiler_params=_cparams(("arbitrary", "arbitrary")),
        name="gla_prompt",
    )(lamax, *[pe[n] for n in names], *[pp[n] for n in names], gnorm.reshape(1, -1), _tri_incl())
    return y, s_fin


GLA_STEP_SEQS = 16


def _gla_step_body(q_ref, k_ref, v_ref, r_ref, la_ref, ga_ref, gn_ref, s_in_ref,
                   y_ref, s_out_ref, at_ref, kt_ref, qt_ref):
    g = pl.program_id(0)
    nseq = q_ref.shape[0]

    @pl.when(g == 0)
    def _():
        a = jnp.exp(la_ref[...])
        for h in range(GLA_HEADS):
            dk = slice(h * GLA_DK, (h + 1) * GLA_DK)
            at_ref[h] = jnp.transpose(a[:, dk])
            kt_ref[h] = jnp.transpose(k_ref[:, dk].astype(F32))
            qt_ref[h] = jnp.transpose(q_ref[:, dk].astype(F32))

    lane = lax.broadcasted_iota(jnp.int32, (GLA_DK, nseq), 1)
    ones = jnp.ones((nseq, GLA_DV), BF16)
    grp = pl.ds(pl.multiple_of(g * GLA_STEP_SEQS, GLA_STEP_SEQS), GLA_STEP_SEQS)
    r_grp = r_ref[grp, :].astype(F32)
    ga_grp = ga_ref[grp, :].astype(F32)
    for i in range(GLA_STEP_SEQS):
        n = g * GLA_STEP_SEQS + i
        sel = lane == n
        for h in range(GLA_HEADS):
            dv = slice(h * GLA_DV, (h + 1) * GLA_DV)
            a_sel = jnp.where(sel, at_ref[h], 0.0)
            k_sel = jnp.where(sel, kt_ref[h], 0.0).astype(BF16)
            q_sel = jnp.where(sel, qt_ref[h], 0.0).astype(BF16)
            decay = _split_dot_rhs(a_sel, ones)
            kv = jnp.dot(k_sel, v_ref[:, dv], preferred_element_type=F32)
            q_b = jnp.dot(q_sel, ones, preferred_element_type=F32)
            s_new = decay * s_in_ref[i, h] + kv
            s_out_ref[i, h] = s_new
            o = jnp.sum(q_b * s_new, axis=0, keepdims=True)
            o = o * lax.rsqrt(jnp.mean(o * o, axis=-1, keepdims=True) + LN_EPS) * gn_ref[...]
            r = r_grp[i:i + 1, dv]
            ga = ga_grp[i:i + 1, dv]
            y_ref[i:i + 1, dv] = (o * (r * _sigmoid(r)) * _sigmoid(ga)).astype(y_ref.dtype)


def _split_dot_rhs(x, b01):
    hi = x.astype(BF16)
    lo = (x - hi.astype(F32)).astype(BF16)
    return jnp.dot(hi, b01, preferred_element_type=F32) + jnp.dot(lo, b01, preferred_element_type=F32)


def _gla_step(pe, gnorm, state):
    nseq = state.shape[0]
    assert nseq == BLK and nseq % GLA_STEP_SEQS == 0
    names = ("qg", "kg", "vg", "rg", "la", "ga")
    t_specs = [pl.BlockSpec((nseq, pe[n].shape[1]), lambda g: (0, 0)) for n in names]
    st_spec = pl.BlockSpec((GLA_STEP_SEQS, GLA_HEADS, GLA_DK, GLA_DV), lambda g: (g, 0, 0, 0))
    return pl.pallas_call(
        _gla_step_body,
        grid=(nseq // GLA_STEP_SEQS,),
        in_specs=t_specs + [
            pl.BlockSpec((1, GLA_DV), lambda g: (0, 0)),
            st_spec,
        ],
        out_specs=[pl.BlockSpec((GLA_STEP_SEQS, D_MODEL), lambda g: (g, 0)), st_spec],
        out_shape=[jax.ShapeDtypeStruct((nseq, D_MODEL), F32), jax.ShapeDtypeStruct(state.shape, F32)],
        scratch_shapes=[pltpu.VMEM((GLA_HEADS, GLA_DK, nseq), F32) for _ in range(3)],
        compiler_params=_cparams(("arbitrary",)),
        name="gla_step",
    )(*[pe[n] for n in names], gnorm.reshape(1, -1), state)


HALF = LANES // 2


def _rel_bucket(dist):
    max_exact = REL_BUCKETS // 2
    d = jnp.maximum(dist, 0)
    large = max_exact + (jnp.log(jnp.maximum(d, 1).astype(F32) / max_exact)
                         / math.log(REL_MAX_DIST / max_exact) * (REL_BUCKETS - max_exact)).astype(jnp.int32)
    large = jnp.minimum(large, REL_BUCKETS - 1)
    return jnp.where(d < max_exact, d, large)


def _bias_lookup(rel_bias, dist):
    onehot = (_rel_bucket(dist)[..., None] == jnp.arange(REL_BUCKETS)).astype(F32)
    bias = jnp.einsum("...b,bh->h...", onehot, rel_bias.astype(F32), precision=lax.Precision.HIGHEST)
    return bias * LOG2E


def _swa_bias_tables(rel_bias):
    q = jnp.arange(BLK)[:, None]
    c = jnp.arange(2 * BLK)[None, :]
    dist = BLK + q - c
    bias = _bias_lookup(rel_bias, dist)
    inside = (dist >= 0) & (dist < WINDOW)
    first = inside & (c >= BLK - N_META)
    neg = jnp.float32(-jnp.inf)
    return jnp.stack([jnp.where(first[None], bias, neg), jnp.where(inside[None], bias, neg)])


def _dup_tiles(x):
    lane = lax.broadcasted_iota(jnp.int32, (x.shape[0], LANES), 1)
    low = lane < HALF
    out = []
    for t in range(2):
        tile = x[:, t * LANES:(t + 1) * LANES]
        swapped = pltpu.roll(tile, HALF, axis=1)
        out += [jnp.where(low, tile, swapped).astype(BF16), jnp.where(low, swapped, tile).astype(BF16)]
    return out


def _group_row_heads(j):
    tiles = [j * (GQA_GROUP // 2) + pair for pair in range(GQA_GROUP // 2)]
    return [2 * t for t in tiles] + [2 * t + 1 for t in tiles]


def _swa_body(nblk, q_ref, kc_ref, vc_ref, kp_ref, vp_ref, km_ref, vm_ref, gb_ref, yg_ref, tb_ref, sink_ref,
              o_ref, kw_ref, vw_ref):
    blk = pl.program_id(1)
    first = blk == 0

    @pl.when(blk == nblk - 1)
    def _():
        kw_ref[...] = kc_ref[...]
        vw_ref[...] = vc_ref[...]

    k_prev = jnp.where(first, km_ref[...], kp_ref[...])
    v_prev = jnp.where(first, vm_ref[...], vp_ref[...])
    k_tiles = _dup_tiles(jnp.concatenate([k_prev, kc_ref[...]], axis=0))
    v_tiles = _dup_tiles(jnp.concatenate([v_prev, vc_ref[...]], axis=0))
    variant = jnp.minimum(blk, 1)
    npair = GQA_GROUP // 2
    low = lax.broadcasted_iota(jnp.int32, (BLK, LANES), 1) < HALF
    ones = jnp.ones((2 * BLK, LANES), BF16)
    seg = lax.broadcasted_iota(jnp.int32, (GQA_GROUP * BLK, 1), 0) // BLK
    nt = (((1,), (1,)), ((), ()))
    for j in range(ATT_KV_HEADS):
        tiles = [j * npair + pair for pair in range(npair)]
        q_t = [q_ref[:, t * LANES:(t + 1) * LANES] for t in tiles]
        zero = jnp.zeros_like(q_t[0])
        q_st = jnp.concatenate([jnp.where(low, q, zero) for q in q_t] + [jnp.where(low, zero, q) for q in q_t], axis=0)
        heads = _group_row_heads(j)
        s = lax.dot_general(q_st, k_tiles[j], nt, preferred_element_type=F32)
        s = s + jnp.concatenate([tb_ref[variant, h] for h in heads], axis=0)
        sink = jnp.full((GQA_GROUP * BLK, 1), sink_ref[heads[0]], F32)
        for i in range(1, GQA_GROUP):
            sink = jnp.where(seg == i, sink_ref[heads[i]], sink)
        m = jnp.maximum(jnp.max(s, axis=-1, keepdims=True), sink)
        p = jnp.exp2(s - m).astype(BF16)
        pv = jnp.dot(p, jnp.concatenate([v_tiles[j], ones], axis=1), preferred_element_type=F32)
        o = pv[:, :LANES] / (pv[:, LANES:] + jnp.exp2(sink - m))
        for pair, t in enumerate(tiles):
            cols = slice(t * LANES, (t + 1) * LANES)
            gate = _sigmoid(gb_ref[:, cols].astype(F32))
            even = o[pair * BLK:(pair + 1) * BLK]
            odd = o[(npair + pair) * BLK:(npair + pair + 1) * BLK]
            o_ref[:, cols] = (gate * jnp.where(low, even, odd) + yg_ref[:, cols].astype(F32)).astype(o_ref.dtype)


def _swa_prompt(pp, pe, yg, rel_bias, sinks, nbatch, nblk):
    tb = _swa_bias_tables(rel_bias)
    kvw = ATT_KV_HEADS * HEAD_DIM
    sinks2 = sinks.astype(F32) * LOG2E
    cur = lambda b, c: (b * nblk + c, 0)
    prev = lambda b, c: (b * nblk + jnp.maximum(c - 1, 0), 0)
    return pl.pallas_call(
        functools.partial(_swa_body, nblk),
        grid=(nbatch, nblk),
        in_specs=[
            pl.BlockSpec((BLK, D_MODEL), cur),
            pl.BlockSpec((BLK, kvw), cur), pl.BlockSpec((BLK, kvw), cur),
            pl.BlockSpec((BLK, kvw), prev), pl.BlockSpec((BLK, kvw), prev),
            pl.BlockSpec((BLK, kvw), lambda b, c: (1, 0)), pl.BlockSpec((BLK, kvw), lambda b, c: (1, 0)),
            pl.BlockSpec((BLK, D_MODEL), cur),
            pl.BlockSpec((BLK, D_MODEL), cur),
            pl.BlockSpec(tb.shape, lambda b, c: (0, 0, 0, 0)),
            pl.BlockSpec(memory_space=pltpu.SMEM),
        ],
        out_specs=[pl.BlockSpec((BLK, D_MODEL), cur),
                   pl.BlockSpec((None, BLK, kvw), lambda b, c: (b, 0, 0)),
                   pl.BlockSpec((None, BLK, kvw), lambda b, c: (b, 0, 0))],
        out_shape=[jax.ShapeDtypeStruct((nbatch * nblk * BLK, D_MODEL), BF16),
                   jax.ShapeDtypeStruct((nbatch, BLK, kvw), F32), jax.ShapeDtypeStruct((nbatch, BLK, kvw), F32)],
        compiler_params=_cparams(("arbitrary", "arbitrary")),
        name="swa_prompt",
    )(pp["qa"], pp["ka"], pp["va"], pp["ka"], pp["va"], pe["ka"], pe["va"], pp["gb"], yg, tb, sinks2)


SWA_STEP_SEQS = 8
Q_TILES = ATT_HEADS // 2


def _swa_step_body(q_ref, kn_ref, vn_ref, ck_ref, cv_ref, gb_ref, yg_ref, tb_ref, sink_ref,
                   o_ref, ko_ref, vo_ref):
    row = lax.broadcasted_iota(jnp.int32, (WINDOW, ATT_KV_HEADS * HEAD_DIM), 0)
    low = lax.broadcasted_iota(jnp.int32, (Q_TILES, LANES), 1) < HALF
    mine16 = (lax.broadcasted_iota(jnp.int32, (2 * Q_TILES, LANES), 0) % Q_TILES) // (GQA_GROUP // 2)
    nt = (((1,), (1,)), ((), ()))
    seqs = range(SWA_STEP_SEQS)
    k_wide, v_wide = [], []
    for i in seqs:
        k_win = jnp.where(row == WINDOW - 1, kn_ref[i:i + 1, :], pltpu.roll(ck_ref[i], WINDOW - 1, axis=0))
        v_win = jnp.where(row == WINDOW - 1, vn_ref[i:i + 1, :], pltpu.roll(cv_ref[i], WINDOW - 1, axis=0))
        ko_ref[i] = k_win
        vo_ref[i] = v_win
        k_wide.append(jnp.concatenate(_dup_tiles(k_win), axis=1))
        v_wide.append(jnp.concatenate(_dup_tiles(v_win), axis=1))
    scores = []
    for i in seqs:
        q8 = q_ref[i]
        q16 = jnp.concatenate([jnp.where(low, q8, 0.0), jnp.where(low, 0.0, q8)], axis=0)
        q_wide = jnp.concatenate([jnp.where(mine16 == j, q16, 0.0) for j in range(ATT_KV_HEADS)], axis=1)
        scores.append(lax.dot_general(q_wide.astype(BF16), k_wide[i], nt, preferred_element_type=F32))
    probs, invs = [], []
    for i in seqs:
        s = scores[i] + tb_ref[...]
        sink = sink_ref[...]
        m = jnp.maximum(jnp.max(s, axis=-1, keepdims=True), sink)
        p = jnp.exp2(s - m)
        invs.append(1.0 / (jnp.sum(p, axis=-1, keepdims=True) + jnp.exp2(sink - m)))
        probs.append(p.astype(BF16))
    outs = [jnp.dot(probs[i], v_wide[i], preferred_element_type=F32) for i in seqs]
    for i in seqs:
        o = jnp.zeros((2 * Q_TILES, LANES), F32)
        for j in range(ATT_KV_HEADS):
            o = jnp.where(mine16 == j, outs[i][:, j * LANES:(j + 1) * LANES], o)
        o = o * invs[i]
        o = jnp.where(low, o[:Q_TILES], o[Q_TILES:])
        o_ref[i] = _sigmoid(gb_ref[i]) * o + yg_ref[i]


def _swa_step(pe, yg_s, cache_k, cache_v, rel_bias, sinks):
    nseq = cache_k.shape[0]
    kvw = ATT_KV_HEADS * HEAD_DIM
    as_tiles = lambda x: x[:nseq].astype(F32).reshape(nseq, Q_TILES, LANES)
    dist = (WINDOW - 1) - jnp.arange(WINDOW)
    bias = _bias_lookup(rel_bias, dist)
    tb = jnp.concatenate([bias[0::2], bias[1::2]], axis=0)
    sk = jnp.concatenate([sinks[0::2], sinks[1::2]])[:, None].astype(F32) * LOG2E
    g = SWA_STEP_SEQS
    tile_spec = pl.BlockSpec((g, Q_TILES, LANES), lambda s: (s, 0, 0))
    win_spec = pl.BlockSpec((g, WINDOW, kvw), lambda s: (s, 0, 0))
    new_spec = pl.BlockSpec((g, kvw), lambda s: (s, 0))
    o, ko, vo = pl.pallas_call(
        _swa_step_body,
        grid=(nseq // g,),
        in_specs=[tile_spec, new_spec, new_spec, win_spec, win_spec, tile_spec, tile_spec,
                  pl.BlockSpec(tb.shape, lambda s: (0, 0)), pl.BlockSpec(sk.shape, lambda s: (0, 0))],
        out_specs=[tile_spec, win_spec, win_spec],
        out_shape=[jax.ShapeDtypeStruct((nseq, Q_TILES, LANES), F32),
                   jax.ShapeDtypeStruct(cache_k.shape, F32), jax.ShapeDtypeStruct(cache_v.shape, F32)],
        compiler_params=_cparams(("arbitrary",)),
        name="swa_step",
    )(as_tiles(pe["qa"]), pe["ka"], pe["va"], cache_k, cache_v, as_tiles(pe["gb"]),
      yg_s.reshape(nseq, Q_TILES, LANES), tb, sk)
    return o.reshape(nseq, D_MODEL), ko, vo


ROUTER_ROWS = 40
META_ROWS = 8


def _split3_nt(a_hi, a_lo, x):
    nt = (((1,), (1,)), ((), ()))
    x_hi = x.astype(BF16)
    x_lo = (x - x_hi.astype(F32)).astype(BF16)
    return (lax.dot_general(a_hi, x_hi, nt, preferred_element_type=F32)
            + lax.dot_general(a_hi, x_lo, nt, preferred_element_type=F32)
            + lax.dot_general(a_lo, x_hi, nt, preferred_element_type=F32))


def _first_argmax_rows(v, ridx, nrows):
    vmax = jnp.max(v, axis=0, keepdims=True)
    idx = jnp.min(jnp.where(v == vmax, ridx, nrows), axis=0, keepdims=True)
    return vmax, idx


def _post_body(nsteps, *refs):
    h1_ref, meta_ref, wcol_ref = refs[-5:-2]
    i = pl.program_id(0)

    @pl.when(i < nsteps)
    def _():
        _post_tile(i, *refs)

    @pl.when(i >= nsteps)
    def _():
        h1_ref[...] = jnp.zeros_like(h1_ref)
        meta_ref[...] = jnp.zeros_like(meta_ref)
        wcol_ref[...] = jnp.zeros_like(wcol_ref)


def _post_tile(i, mg_ref, x_ref, lng_ref, lnb_ref, wo_ref, g1_ref, b1_ref, wrh_ref, wrl_ref, rb_ref, ut_ref,
               cin_ref, *rest):
    h1_ref, meta_ref, wcol_ref, cout_ref, carry_ref = rest[-5:]

    @pl.when(i == 0)
    def _():
        carry_ref[...] = cin_ref[...]

    tm = x_ref.shape[0]
    h = _layer_norm_rows(x_ref[...], lng_ref[...], lnb_ref[...])
    acc = jnp.dot(mg_ref[...].astype(BF16), wo_ref[...], preferred_element_type=F32)
    h1 = _layer_norm_rows(ALPHA * h + acc, g1_ref[...], b1_ref[...])
    _store_rows_as_tiles(h1_ref, h1)

    lt = _split3_nt(wrh_ref[...], wrl_ref[...], h1) + rb_ref[:, :1]
    ridx = lax.broadcasted_iota(jnp.int32, (EXPERTS_PER_GROUP, tm), 0)
    neg = jnp.float32(-jnp.inf)
    g_log = jnp.where(ridx < N_GROUPS, lt[N_EXPERTS:N_EXPERTS + EXPERTS_PER_GROUP], neg)
    g_max, grp = _first_argmax_rows(g_log, ridx, EXPERTS_PER_GROUP)
    p_grp = 1.0 / jnp.sum(jnp.exp(g_log - g_max), axis=0, keepdims=True)
    e_in = lt[0:EXPERTS_PER_GROUP]
    for gi in range(1, N_GROUPS):
        e_in = jnp.where(grp == gi, lt[gi * EXPERTS_PER_GROUP:(gi + 1) * EXPERTS_PER_GROUP], e_in)
    v0, i0 = _first_argmax_rows(e_in, ridx, EXPERTS_PER_GROUP)
    v1, i1 = _first_argmax_rows(jnp.where(ridx == i0, neg, e_in), ridx, EXPERTS_PER_GROUP)
    t = jnp.exp(v1 - v0)
    w0 = p_grp / (1.0 + t)
    w1 = p_grp * t / (1.0 + t)
    e0 = grp * EXPERTS_PER_GROUP + i0
    e1 = grp * EXPERTS_PER_GROUP + i1

    eidx = lax.broadcasted_iota(jnp.int32, (N_EXPERTS, tm), 0)
    hit0 = eidx == e0
    hit1 = eidx == e1
    oh = jnp.where(jnp.logical_or(hit0, hit1), 1.0, 0.0)
    before = jnp.dot(oh.astype(BF16), ut_ref[...], preferred_element_type=F32) + carry_ref[:, :1]
    r0 = jnp.sum(jnp.where(hit0, before, 0.0), axis=0, keepdims=True).astype(jnp.int32)
    r1 = jnp.sum(jnp.where(hit1, before, 0.0), axis=0, keepdims=True).astype(jnp.int32)
    carry_ref[...] = carry_ref[...] + jnp.sum(oh, axis=1, keepdims=True)
    cout_ref[...] = carry_ref[...]

    zi = jnp.zeros((META_ROWS - 4, tm), jnp.int32)
    meta_ref[...] = jnp.concatenate([e0, e1, r0, r1, zi], axis=0)
    wt = jnp.concatenate([w0, w1, jnp.zeros((LANES - 2, tm), F32)], axis=0)
    wcol_ref[...] = jnp.transpose(wt)


def _post(mg, x2d, prm, tm, row0, total_rows, carry_in, prev=None, zero_tail=False):
    m = x2d.shape[0]
    assert m % tm == 0 and row0 % tm == 0 and total_rows % tm == 0
    off = row0 // tm
    nsteps = m // tm
    last = nsteps - 1
    ut = jnp.asarray(np.triu(np.ones((tm, tm), np.float32), 1), dtype=BF16)
    full = lambda shape: pl.BlockSpec(shape, lambda i: (0,) * len(shape))
    in_specs = [
        pl.BlockSpec((tm, D_MODEL), lambda i: (jnp.minimum(i, last), 0)),
        pl.BlockSpec((tm, D_MODEL), lambda i: (jnp.minimum(i, last), 0)),
        full((1, D_MODEL)), full((1, D_MODEL)),
        full((D_MODEL, D_MODEL)),
        full((1, D_MODEL)), full((1, D_MODEL)),
        full((ROUTER_ROWS, D_MODEL)), full((ROUTER_ROWS, D_MODEL)), full((ROUTER_ROWS, LANES)),
        full((tm, tm)),
        full((N_EXPERTS, LANES)),
    ]
    args = [mg, x2d, prm["ln_emb_g"], prm["ln_emb_b"], prm["w_out"], prm["ln1_g"], prm["ln1_b"],
            prm["wr_hi"], prm["wr_lo"], prm["r_bias"], ut, carry_in]
    aliases = {}
    if prev is not None:
        for k, buf in enumerate(prev):
            in_specs.append(pl.BlockSpec(memory_space=pl.ANY))
            aliases[len(args)] = k
            args.append(buf)
    out_shape = [
        jax.ShapeDtypeStruct((total_rows * SUBLANES, LANES), F32),
        jax.ShapeDtypeStruct((META_ROWS, total_rows), jnp.int32),
        jax.ShapeDtypeStruct((total_rows, LANES), F32),
        jax.ShapeDtypeStruct((N_EXPERTS, LANES), F32),
    ]
    out_specs = [
        pl.BlockSpec((tm * SUBLANES, LANES), lambda i: (i + off, 0)),
        pl.BlockSpec((META_ROWS, tm), lambda i: (0, i + off)),
        pl.BlockSpec((tm, LANES), lambda i: (i + off, 0)),
        full((N_EXPERTS, LANES)),
    ]
    if prev is not None:
        assert len(prev) == 3
    return pl.pallas_call(
        functools.partial(_post_body, nsteps),
        grid=(nsteps + int(zero_tail),),
        in_specs=in_specs,
        out_specs=out_specs,
        out_shape=out_shape,
        input_output_aliases=aliases,
        scratch_shapes=[pltpu.VMEM((N_EXPERTS, LANES), F32)],
        compiler_params=_cparams(("arbitrary",)),
        name="post_attn",
    )(*args)


def _prep_post_params(ln_emb_g, ln_emb_b, w_out, ln1_g, ln1_b, w_rg, b_rg, w_re, b_re):
    row = lambda v: v.reshape(1, -1)
    wr = jnp.concatenate([w_re.T, w_rg.T, jnp.zeros((ROUTER_ROWS - N_EXPERTS - N_GROUPS, D_MODEL), F32)], axis=0)
    wr_hi = wr.astype(BF16)
    wr_lo = (wr - wr_hi.astype(F32)).astype(BF16)
    rb = jnp.concatenate([b_re, b_rg, jnp.zeros((ROUTER_ROWS - N_EXPERTS - N_GROUPS,), F32)])
    return dict(ln_emb_g=row(ln_emb_g), ln_emb_b=row(ln_emb_b), w_out=w_out.astype(BF16), ln1_g=row(ln1_g),
                ln1_b=row(ln1_b), wr_hi=wr_hi, wr_lo=wr_lo,
                r_bias=jnp.broadcast_to(rb[:, None], (ROUTER_ROWS, LANES)))


MOE_ROWS = 256
SUBLANES = 8
assert D_MODEL == SUBLANES * LANES


def _store_rows_as_tiles(ref, x):
    n = x.shape[0]
    for c in range(SUBLANES):
        ref[pl.ds(c, n, stride=SUBLANES), :] = x[:, c * LANES:(c + 1) * LANES]


def _load_rows_from_tiles(ref, n):
    return jnp.concatenate([ref[pl.ds(c, n, stride=SUBLANES), :] for c in range(SUBLANES)], axis=1)


def _tile_of_row(ref, r):
    return ref.at[pl.ds(pl.multiple_of(r * SUBLANES, SUBLANES), SUBLANES)]


def _moe_plan(counts, total_assign):
    nb_max = -(-total_assign // MOE_ROWS) + N_EXPERTS
    padded = (counts + MOE_ROWS - 1) // MOE_ROWS * MOE_ROWS
    pend = jnp.cumsum(padded)
    pstart = (pend - padded).astype(jnp.int32)
    block_start = jnp.arange(nb_max, dtype=jnp.int32) * MOE_ROWS
    n_ended = jnp.sum((pend[None, :] <= block_start[:, None]).astype(jnp.int32), axis=1)
    block_e = jnp.minimum(n_ended, N_EXPERTS - 1).astype(jnp.int32)
    n_used = (pend[-1] // MOE_ROWS).astype(jnp.int32).reshape(1)
    tail_start = jnp.where(padded > 0, pend - MOE_ROWS, -1)
    spare = pend[-1] + jnp.arange(N_EXPERTS, dtype=pend.dtype) * MOE_ROWS
    spare = jnp.where(spare < nb_max * MOE_ROWS, spare, -1)
    zero_blocks = jnp.concatenate([tail_start, spare]).astype(jnp.int32)
    return pstart, block_e, n_used, nb_max, zero_blocks


ROW_UNROLL = 8


def _slot_ids(meta, pstart):
    experts = meta[0:TOP_K]
    ranks = meta[TOP_K:2 * TOP_K]
    onehot = experts[..., None] == jnp.arange(N_EXPERTS, dtype=jnp.int32)
    return ranks + jnp.sum(jnp.where(onehot, pstart, 0), axis=-1)


def _for_row_groups(tm, fn, read=None):
    def group(g, c):
        t0 = pl.multiple_of(g * ROW_UNROLL, ROW_UNROLL)
        items = [(t0 + r, k) for r in range(ROW_UNROLL) for k in range(TOP_K)]
        if read is None:
            for t, k in items:
                fn(t, k)
        else:
            vals = [read(t, k) for t, k in items]
            for (t, k), v in zip(items, vals):
                fn(t, k, v)
        return c

    lax.fori_loop(0, tm // ROW_UNROLL, group, 0)


def _dispatch_body(nsteps, tm, tail_ref, s0_ref, s1_ref, h_ref, xs_ref, zero_ref, sems, zsem):
    i = pl.program_id(0)
    slot_refs = (s0_ref, s1_ref)
    cur = i % 2
    row0 = i * tm
    blk_tiles = MOE_ROWS * SUBLANES

    @pl.when(i == 0)
    def _():
        zero_ref[...] = jnp.zeros_like(zero_ref)
        for e in range(2 * N_EXPERTS):
            @pl.when(tail_ref[e] >= 0)
            def _():
                start = pl.multiple_of(tail_ref[e] * SUBLANES, blk_tiles)
                pltpu.make_async_copy(zero_ref, xs_ref.at[pl.ds(start, blk_tiles)], zsem).start()
        for e in range(2 * N_EXPERTS):
            @pl.when(tail_ref[e] >= 0)
            def _():
                pltpu.make_async_copy(zero_ref, xs_ref.at[pl.ds(0, blk_tiles)], zsem).wait()

    def send(t, k, slot):
        pltpu.make_async_copy(_tile_of_row(h_ref, row0 + t), _tile_of_row(xs_ref, slot),
                              sems.at[cur]).start(priority=k)

    def wait_buffer(buf):
        _for_row_groups(tm, lambda t, k: pltpu.make_async_copy(
            _tile_of_row(h_ref, 0), _tile_of_row(xs_ref, 0), sems.at[buf]).wait())

    _for_row_groups(tm, send, read=lambda t, k: slot_refs[k][0, t])

    @pl.when(i > 0)
    def _():
        wait_buffer(1 - cur)

    @pl.when(i == nsteps - 1)
    def _():
        wait_buffer(cur)


def _dispatch(h1, slot_ids, tail_start, nslots, tm, total):
    assert total % tm == 0 and total * SUBLANES <= h1.shape[0] and tm % ROW_UNROLL == 0
    slot_spec = pl.BlockSpec((1, tm), lambda i, tl: (0, i), memory_space=pltpu.SMEM)
    return pl.pallas_call(
        functools.partial(_dispatch_body, total // tm, tm),
        grid_spec=pltpu.PrefetchScalarGridSpec(
            num_scalar_prefetch=1,
            grid=(total // tm,),
            in_specs=[slot_spec, slot_spec, pl.BlockSpec(memory_space=pl.ANY)],
            out_specs=pl.BlockSpec(memory_space=pl.ANY),
            scratch_shapes=[pltpu.VMEM((MOE_ROWS * SUBLANES, LANES), F32),
                            pltpu.SemaphoreType.DMA((2,)), pltpu.SemaphoreType.DMA(())],
        ),
        out_shape=jax.ShapeDtypeStruct((nslots * SUBLANES, LANES), F32),
        compiler_params=_cparams(("arbitrary",)),
        name="moe_dispatch",
    )(tail_start, slot_ids[0:1], slot_ids[1:2], h1)


def _expert_schedule(block_e, n_used):
    nb = block_e.shape[0]
    idx = jnp.arange(nb, dtype=jnp.int32)
    first = (idx < n_used[0]) & ((idx == 0) | (block_e != jnp.roll(block_e, 1)))
    parity = (jnp.cumsum(first.astype(jnp.int32)) - 1) % 2
    pos = jnp.where(first, idx, nb)
    at_or_after = jnp.flip(lax.cummin(jnp.flip(pos)))
    nxt = jnp.concatenate([at_or_after[1:], jnp.full((1,), nb, jnp.int32)])
    nexte = jnp.where(nxt < nb, block_e[jnp.minimum(nxt, nb - 1)], -1)
    return first.astype(jnp.int32), nexte.astype(jnp.int32), parity.astype(jnp.int32)


def _expert_body(be_ref, nu_ref, first_ref, nexte_ref, par_ref, xs_ref, wg_hbm, wu_hbm, wd_hbm, ys_ref,
                 wgf_ref, wuf_ref, wdf_ref, wgb_ref, wub_ref, wdb_ref, sems):
    i = pl.program_id(0)
    hbm = (wg_hbm, wu_hbm, wd_hbm)
    stage = (wgf_ref, wuf_ref, wdf_ref)

    def weight_copies(e, buf):
        return [pltpu.make_async_copy(hbm[w].at[e], stage[w].at[buf], sems.at[buf, w]) for w in range(3)]

    @pl.when(first_ref[i] == 1)
    def _():
        buf = par_ref[i]

        @pl.when(i == 0)
        def _():
            for c in weight_copies(be_ref[0], buf):
                c.start()

        for c in weight_copies(be_ref[i], buf):
            c.wait()

        @pl.when(nexte_ref[i] >= 0)
        def _():
            for c in weight_copies(nexte_ref[i], 1 - buf):
                c.start()

        wgb_ref[...] = wgf_ref[buf].astype(BF16)
        wub_ref[...] = wuf_ref[buf].astype(BF16)
        wdb_ref[...] = wdf_ref[buf].astype(BF16)

    @pl.when(i < nu_ref[0])
    def _():
        x = _load_rows_from_tiles(xs_ref, MOE_ROWS).astype(BF16)
        g = jnp.dot(x, wgb_ref[...], preferred_element_type=F32)
        hb = (g * _sigmoid(g)) * jnp.dot(x, wub_ref[...], preferred_element_type=F32)
        y = jnp.dot(hb.astype(BF16), wdb_ref[...], preferred_element_type=F32)
        _store_rows_as_tiles(ys_ref, y)

    @pl.when(i >= nu_ref[0])
    def _():
        ys_ref[...] = jnp.zeros_like(ys_ref)


def _experts(xs, block_e, n_used, w_gate, w_up, w_down, nb_max):
    first, nexte, parity = _expert_schedule(block_e, n_used)
    rows = lambda i, *_: (i, 0)
    used_rows = lambda i, be, nu, *_: (jnp.maximum(jnp.minimum(i, nu[0] - 1), 0), 0)
    any_spec = pl.BlockSpec(memory_space=pl.ANY)
    blk = (MOE_ROWS * SUBLANES, LANES)
    return pl.pallas_call(
        _expert_body,
        grid_spec=pltpu.PrefetchScalarGridSpec(
            num_scalar_prefetch=5,
            grid=(nb_max,),
            in_specs=[pl.BlockSpec(blk, used_rows), any_spec, any_spec, any_spec],
            out_specs=pl.BlockSpec(blk, rows),
            scratch_shapes=[
                pltpu.VMEM((2, D_MODEL, D_EXPERT), F32), pltpu.VMEM((2, D_MODEL, D_EXPERT), F32),
                pltpu.VMEM((2, D_EXPERT, D_MODEL), F32),
                pltpu.VMEM((D_MODEL, D_EXPERT), BF16), pltpu.VMEM((D_MODEL, D_EXPERT), BF16),
                pltpu.VMEM((D_EXPERT, D_MODEL), BF16),
                pltpu.SemaphoreType.DMA((2, 3)),
            ],
        ),
        out_shape=jax.ShapeDtypeStruct(xs.shape, F32),
        compiler_params=_cparams(("arbitrary",)),
        name="moe_experts",
    )(block_e, n_used, first, nexte, parity, xs, w_gate, w_up, w_down)


def _combine_body(nsteps, s0_ref, s1_ref, n0_ref, n1_ref, h_ref, w_ref, g2_ref, b2_ref, ys_ref, o_ref, buf_ref, sems):
    i = pl.program_id(0)
    tm = o_ref.shape[0]
    cur = i % 2

    def fetch(slot_refs, buf):
        _for_row_groups(tm, lambda t, k, slot: pltpu.make_async_copy(
            _tile_of_row(ys_ref, slot), _tile_of_row(buf_ref.at[buf, k], t),
            sems.at[buf]).start(priority=k), read=lambda t, k: slot_refs[k][0, t])

    @pl.when(i == 0)
    def _():
        fetch((s0_ref, s1_ref), cur)

    @pl.when(i + 1 < nsteps)
    def _():
        fetch((n0_ref, n1_ref), 1 - cur)

    _for_row_groups(tm, lambda t, k: pltpu.make_async_copy(
        _tile_of_row(ys_ref, 0), _tile_of_row(buf_ref.at[cur, k], t), sems.at[cur]).wait())
    w = w_ref[...]
    f = (w[:, 0:1] * _load_rows_from_tiles(buf_ref.at[cur, 0], tm)
         + w[:, 1:2] * _load_rows_from_tiles(buf_ref.at[cur, 1], tm))
    h = _load_rows_from_tiles(h_ref, tm)
    o_ref[...] = _layer_norm_rows(ALPHA * h + f, g2_ref[...], b2_ref[...])


def _combine(h1, wcol, slot_ids, ys, ln2_g, ln2_b, tm, row0, nrows):
    assert nrows % tm == 0 and row0 % tm == 0 and tm % ROW_UNROLL == 0
    off = row0 // tm
    nsteps = nrows // tm
    slot_spec = pl.BlockSpec((1, tm), lambda i: (0, i + off), memory_space=pltpu.SMEM)
    next_spec = pl.BlockSpec((1, tm), lambda i: (0, jnp.minimum(i + 1, nsteps - 1) + off), memory_space=pltpu.SMEM)
    return pl.pallas_call(
        functools.partial(_combine_body, nsteps),
        grid=(nsteps,),
        in_specs=[
            slot_spec, slot_spec, next_spec, next_spec,
            pl.BlockSpec((tm * SUBLANES, LANES), lambda i: (i + off, 0)),
            pl.BlockSpec((tm, LANES), lambda i: (i + off, 0)),
            pl.BlockSpec((1, D_MODEL), lambda i: (0, 0)),
            pl.BlockSpec((1, D_MODEL), lambda i: (0, 0)),
            pl.BlockSpec(memory_space=pl.ANY),
        ],
        out_specs=pl.BlockSpec((tm, D_MODEL), lambda i: (i, 0)),
        scratch_shapes=[pltpu.VMEM((2, TOP_K, tm * SUBLANES, LANES), F32), pltpu.SemaphoreType.DMA((2,))],
        out_shape=jax.ShapeDtypeStruct((nrows, D_MODEL), F32),
        compiler_params=_cparams(("arbitrary",)),
        name="moe_combine",
    )(slot_ids[0:1], slot_ids[1:2], slot_ids[0:1], slot_ids[1:2], h1, wcol, ln2_g.reshape(1, -1),
      ln2_b.reshape(1, -1), ys)


PROJ_ROWS = 512
POST_ROWS = 512
DISPATCH_ROWS = 384
COMBINE_ROWS = 256


def kernel(x_prompt, x_sample, state_gla, cache_swa_k, cache_swa_v, meta_tokens, ln_emb_g, ln_emb_b, rel_bias, w_in,
           gk_up, gk_bias, gla_norm_g, sinks, w_out, ln1_g, ln1_b, w_router_group, b_router_group, w_router_expert,
           b_router_expert, w_gate, w_up, w_down, ln2_g, ln2_b):
    nbatch, seq, d = x_prompt.shape
    nseq = x_sample.shape[0]
    assert w_in.shape[0] == DEPTH == 1 and d == D_MODEL and x_sample.shape[1] == 1
    assert seq % BLK == 0 and nseq == BLK and meta_tokens.shape[0] == N_META
    nblk = seq // BLK
    n_prompt = nbatch * seq
    total = n_prompt + nseq
    kvw = ATT_KV_HEADS * HEAD_DIM

    xp = x_prompt.reshape(n_prompt, d)
    xs = x_sample.reshape(nseq, d)
    extra = jnp.concatenate([xs, jnp.zeros((BLK - N_META, d), xs.dtype), meta_tokens.astype(xs.dtype)], axis=0)
    w_cat = _prep_w_in(w_in[0])
    pp = _ln_proj(xp, ln_emb_g, ln_emb_b, w_cat, gk_up[0], gk_bias[0], PROJ_ROWS)
    pe = _ln_proj(extra, ln_emb_g, ln_emb_b, w_cat, gk_up[0], gk_bias[0], 2 * BLK)

    yg, gla_p = _gla_prompt(pp, pe, gla_norm_g[0], nbatch, nblk)
    yg_s, gla_s = _gla_step(pe, gla_norm_g[0], state_gla[0])
    mg, k_win, v_win = _swa_prompt(pp, pe, yg, rel_bias, sinks[0], nbatch, nblk)
    mg_s, k_s, v_s = _swa_step(pe, yg_s, cache_swa_k[0].reshape(nseq, WINDOW, kvw),
                               cache_swa_v[0].reshape(nseq, WINDOW, kvw), rel_bias, sinks[0])

    prm = _prep_post_params(ln_emb_g, ln_emb_b, w_out[0], ln1_g[0], ln1_b[0], w_router_group[0], b_router_group[0],
                            w_router_expert[0], b_router_expert[0])
    carry0 = jnp.zeros((N_EXPERTS, LANES), F32)
    rows_alloc = n_prompt + POST_ROWS
    h1, meta, wcol, carry1 = _post(mg, xp, prm, POST_ROWS, 0, rows_alloc, carry0, zero_tail=True)
    h1, meta, wcol, carry2 = _post(mg_s, xs, prm, nseq, n_prompt, rows_alloc, carry1, prev=(h1, meta, wcol))

    counts = carry2[:, 0].astype(jnp.int32)
    pstart, block_e, n_used, nb_max, tail_start = _moe_plan(counts, TOP_K * total)
    slot_ids = _slot_ids(meta, pstart)
    xs_sorted = _dispatch(h1, slot_ids, tail_start, nb_max * MOE_ROWS, DISPATCH_ROWS, total)
    ys = _experts(xs_sorted, block_e, n_used, w_gate[0], w_up[0], w_down[0], nb_max)
    y_p = _combine(h1, wcol, slot_ids, ys, ln2_g[0], ln2_b[0], COMBINE_ROWS, 0, n_prompt)
    y_s = _combine(h1, wcol, slot_ids, ys, ln2_g[0], ln2_b[0], nseq, n_prompt, nseq)

    kv_shape = (1, nbatch, WINDOW, ATT_KV_HEADS, HEAD_DIM)
    k_p = k_win.reshape(kv_shape)
    v_p = v_win.reshape(kv_shape)
    return (y_p.reshape(nbatch, seq, d), y_s.reshape(nseq, 1, d), gla_p[None], k_p, v_p, gla_s[None],
            k_s.reshape(cache_swa_k.shape), v_s.reshape(cache_swa_v.shape))
```

```python
import functools
import math

import jax
import jax.numpy as jnp
import numpy as np
from jax import lax
from jax.experimental import pallas as pl
from jax.experimental.pallas import tpu as pltpu

F32 = jnp.float32
BF16 = jnp.bfloat16

D_MODEL = 1024
N_META = 16
LN_EPS = 1e-5
GLA_HEADS = 4
GLA_DK = 128
GLA_DV = 256
GLA_RANK = 16
GLA_TAU = 16.0
HEAD_DIM = 64
ATT_HEADS = 16
ATT_KV_HEADS = 4
GQA_GROUP = 4
WINDOW = 128
REL_BUCKETS = 32
REL_MAX_DIST = 128
N_GROUPS = 4
EXPERTS_PER_GROUP = 8
N_EXPERTS = 32
TOP_K = 2
D_EXPERT = 512
DEPTH = 1
ALPHA = (2.0 * DEPTH) ** 0.25
LOG2E = math.log2(math.e)

LANES = 128
BLK = 128
VMEM_LIMIT = 56 * 1024 * 1024


def _cparams(sem):
    return pltpu.CompilerParams(dimension_semantics=sem, vmem_limit_bytes=VMEM_LIMIT)


def _layer_norm_rows(x, g, b):
    mu = jnp.mean(x, axis=-1, keepdims=True)
    xc = x - mu
    var = jnp.mean(xc * xc, axis=-1, keepdims=True)
    return xc * lax.rsqrt(var + LN_EPS) * g + b


_PROJ_OUTS = (
    ("lr", LANES, F32, None),
    ("qg", GLA_HEADS * GLA_DK, BF16, GLA_DK ** -0.5),
    ("kg", GLA_HEADS * GLA_DK, BF16, None),
    ("vg", GLA_HEADS * GLA_DV, BF16, None),
    ("rg", GLA_HEADS * GLA_DV, BF16, None),
    ("qa", ATT_HEADS * HEAD_DIM, BF16, HEAD_DIM ** -0.5 * LOG2E),
    ("ka", ATT_KV_HEADS * HEAD_DIM, F32, None),
    ("va", ATT_KV_HEADS * HEAD_DIM, F32, None),
    ("ga", D_MODEL, BF16, None),
    ("gb", D_MODEL, BF16, None),
)
_PROJ_W = sum(w for _, w, _, _ in _PROJ_OUTS)


def _prep_w_in(w_in):
    sizes = (512, 512, 1024, 1024, GLA_RANK, 1024, 256, 256, 1024, 1024)
    offs = np.cumsum((0,) + sizes)
    a = w_in[:, : offs[4]]
    lr = w_in[:, offs[4]: offs[5]]
    b = w_in[:, offs[5]:]
    pad = jnp.zeros((w_in.shape[0], LANES - GLA_RANK), w_in.dtype)
    return jnp.concatenate([lr, pad, a, b], axis=1).astype(BF16)


LAMAX_ROWS = 8


def _log_sigmoid(x):
    return jnp.minimum(x, 0.0) - jnp.log(1.0 + jnp.exp(-jnp.abs(x)))


def _ln_proj_body(x_ref, g_ref, b_ref, w_ref, gkup_ref, gkb_ref, *out_refs):
    la_ref, lamax_ref = out_refs[-2:]
    tm = x_ref.shape[0]
    xn = _layer_norm_rows(x_ref[...], g_ref[...], b_ref[...]).astype(BF16)
    c0 = 0
    plain_refs = iter(out_refs[:-2])
    for name, width, dtype, scale in _PROJ_OUTS:
        acc = jnp.dot(xn, w_ref[:, c0:c0 + width], preferred_element_type=F32)
        c0 += width
        if name == "lr":
            x = jnp.dot(acc.astype(BF16), gkup_ref[...], preferred_element_type=F32) + gkb_ref[...]
            la = _log_sigmoid(x) * (1.0 / GLA_TAU)
            la_ref[...] = la
            rows = []
            for r in range(tm // BLK):
                blk_max = jnp.max(jnp.max(jnp.abs(la[r * BLK:(r + 1) * BLK]), axis=0, keepdims=True), axis=1, keepdims=True)
                rows.append(jnp.broadcast_to(blk_max, (1, LANES)))
            rows.append(jnp.zeros((LAMAX_ROWS - tm // BLK, LANES), F32))
            lamax_ref[...] = jnp.concatenate(rows, axis=0)
            continue
        if scale is not None:
            acc = acc * scale
        next(plain_refs)[...] = acc.astype(dtype)


def _ln_proj(x2d, ln_g, ln_b, w_cat, gk_up, gk_bias, tm):
    m = x2d.shape[0]
    assert m % tm == 0 and tm % BLK == 0 and tm // BLK < LAMAX_ROWS
    names = [n for n, _, _, _ in _PROJ_OUTS if n != "lr"] + ["la", "lamax"]
    ladim = GLA_HEADS * GLA_DK
    gkup = jnp.concatenate([gk_up, jnp.zeros((LANES - GLA_RANK, ladim), gk_up.dtype)], axis=0).astype(BF16)
    out_shape = [jax.ShapeDtypeStruct((m, w), dt) for n, w, dt, _ in _PROJ_OUTS if n != "lr"]
    out_specs = [pl.BlockSpec((tm, w), lambda i: (i, 0)) for n, w, _, _ in _PROJ_OUTS if n != "lr"]
    out_shape += [jax.ShapeDtypeStruct((m, ladim), F32), jax.ShapeDtypeStruct((m // tm * LAMAX_ROWS, LANES), F32)]
    out_specs += [pl.BlockSpec((tm, ladim), lambda i: (i, 0)), pl.BlockSpec((LAMAX_ROWS, LANES), lambda i: (i, 0))]
    outs = pl.pallas_call(
        _ln_proj_body,
        grid=(m // tm,),
        in_specs=[
            pl.BlockSpec((tm, D_MODEL), lambda i: (i, 0)),
            pl.BlockSpec((1, D_MODEL), lambda i: (0, 0)),
            pl.BlockSpec((1, D_MODEL), lambda i: (0, 0)),
            pl.BlockSpec((D_MODEL, _PROJ_W), lambda i: (0, 0), pipeline_mode=pl.Buffered(1)),
            pl.BlockSpec((LANES, ladim), lambda i: (0, 0)),
            pl.BlockSpec((1, ladim), lambda i: (0, 0)),
        ],
        out_specs=out_specs,
        out_shape=out_shape,
        compiler_params=_cparams(("arbitrary",)),
        name="ln_proj",
    )(x2d, ln_g.reshape(1, -1), ln_b.reshape(1, -1), w_cat, gkup, gk_bias.reshape(1, -1))
    res = dict(zip(names, outs))
    res["lamax"] = res["lamax"].reshape(m // tm, LAMAX_ROWS, LANES)[:, :tm // BLK, 0].reshape(m // BLK)
    return res


_GLA_LEVELS = tuple(2 ** i for i in range(int(math.log2(BLK))))
GLA_SAFE_EXPONENT = 60.0


def _sigmoid(x):
    return 0.5 * jnp.tanh(0.5 * x) + 0.5


def _split_dot(a01, x):
    hi = x.astype(BF16)
    lo = (x - hi.astype(F32)).astype(BF16)
    n = x.shape[1]
    both = jnp.dot(a01, jnp.concatenate([hi, lo], axis=1), preferred_element_type=F32)
    return both[:, :n] + both[:, n:]


def _gla_anchor_exponent(b, la, s, row):
    if s == 1:
        return jnp.where(row % 2 == 1, la, 0.0)
    if s == 2:
        la_dn = pltpu.roll(la, 1, axis=0)
        la_up = pltpu.roll(la, BLK - 1, axis=0)
        r = row % 4
        return jnp.where(r == 0, la_up, jnp.where(r == 1, 0.0, jnp.where(r == 2, la, la + la_dn)))
    nb = BLK // (2 * s)
    b3 = b.reshape(nb, 2 * s, b.shape[-1])
    anchor = jnp.broadcast_to(b3[:, s - 1:s, :], b3.shape).reshape(b.shape)
    return -jnp.abs(b - anchor)


def _gla_body(nblk, lamax_ref, qm, km, vm, rm, lam, gam, qp, kp, vp, rp, lap, gap, gn_ref, tri_ref,
              y_ref, s_out_ref, s_ref):
    c = pl.program_id(1)
    is_meta = c == 0
    n_prompt_blocks = pl.num_programs(0) * nblk
    blk_max = lamax_ref[jnp.where(is_meta, n_prompt_blocks, pl.program_id(0) * nblk + c - 1)]

    @pl.when(is_meta)
    def _():
        s_ref[...] = jnp.zeros_like(s_ref)

    def pick(m_ref, p_ref):
        return jnp.where(is_meta, m_ref[...], p_ref[...])

    row = lax.broadcasted_iota(jnp.int32, (BLK, GLA_DK), 0)
    col_t = lax.broadcasted_iota(jnp.int32, (BLK, BLK), 1)
    row_t = lax.broadcasted_iota(jnp.int32, (BLK, BLK), 0)
    live = jnp.logical_or(jnp.logical_not(is_meta), row >= BLK - N_META)
    tri = tri_ref[...]
    q_all, k_all, v_all, r_all = pick(qm, qp), pick(km, kp), pick(vm, vp), pick(rm, rp)
    ga_all = pick(gam, gap)
    la_all = pick(lam, lap)
    nt = (((1,), (1,)), ((), ()))
    mid = BLK // 2 - 1

    def head(h, single_anchor):
        dk = slice(h * GLA_DK, (h + 1) * GLA_DK)
        dv = slice(h * GLA_DV, (h + 1) * GLA_DV)
        la = jnp.where(live, la_all[:, dk], 0.0)
        q = q_all[:, dk].astype(F32)
        k = jnp.where(live, k_all[:, dk].astype(F32), 0.0)
        v = v_all[:, dv]
        b = _split_dot(tri, la)
        yield
        b_last = b[BLK - 1:BLK, :]
        s_old = s_ref[h]
        if single_anchor:
            b_mid = b[mid:mid + 1, :]
            qe = q * jnp.exp(b - b_mid)
            ke = k * jnp.exp(b_mid - b)
            a = jnp.where(row_t >= col_t,
                          lax.dot_general(qe.astype(BF16), ke.astype(BF16), nt, preferred_element_type=F32), 0.0)
            qg = qe * jnp.exp(b_mid)
            kd = ke * jnp.exp(b_last - b_mid)
        else:
            a = jnp.where(row_t == col_t,
                          lax.dot_general(q.astype(BF16), k.astype(BF16), nt, preferred_element_type=F32), 0.0)
            for s in _GLA_LEVELS:
                e = jnp.exp(_gla_anchor_exponent(b, la, s, row))
                upper = (row // s) % 2 == 1
                q_s = jnp.where(upper, q * e, 0.0).astype(BF16)
                k_s = jnp.where(upper, 0.0, k * e).astype(BF16)
                p = lax.dot_general(q_s, k_s, nt, preferred_element_type=F32)
                a = a + jnp.where(row_t // (2 * s) == col_t // (2 * s), p, 0.0)
            qg = q * jnp.exp(b)
            kd = k * jnp.exp(b_last - b)
        yield
        o = jnp.dot(qg.astype(BF16), s_old.astype(BF16), preferred_element_type=F32)
        lhs = jnp.concatenate([jnp.transpose(kd).astype(BF16), a.astype(BF16)], axis=0)
        both = jnp.dot(lhs, v, preferred_element_type=F32)
        yield
        decay_col = jnp.transpose(jnp.broadcast_to(jnp.exp(b_last), (BLK, GLA_DK)))[:, :1]
        s_ref[h] = decay_col * s_old + both[:GLA_DK]
        o = o + both[GLA_DK:]
        o = o * lax.rsqrt(jnp.mean(o * o, axis=-1, keepdims=True) + LN_EPS) * gn_ref[...]
        r = r_all[:, dv].astype(F32)
        y = o * (r * _sigmoid(r)) * _sigmoid(ga_all[:, dv].astype(F32))
        y_ref[:, dv] = y.astype(y_ref.dtype)

    mild = blk_max * (BLK // 2) <= GLA_SAFE_EXPONENT

    def all_heads(single_anchor):
        running = [head(h, single_anchor) for h in range(GLA_HEADS)]
        while running:
            running = [g for g in running if next(g, True) is None]

    @pl.when(mild)
    def _():
        all_heads(True)

    @pl.when(jnp.logical_not(mild))
    def _():
        all_heads(False)

    @pl.when(c == nblk)
    def _():
        s_out_ref[...] = s_ref[...]


def _tri_incl():
    i = np.arange(BLK)
    return jnp.asarray((i[None, :] <= i[:, None]).astype(np.float32), dtype=BF16)


def _gla_prompt(pp, pe, gnorm, nbatch, nblk):
    names = ("qg", "kg", "vg", "rg", "la", "ga")
    lamax = jnp.concatenate([pp["lamax"], pe["lamax"][1:2]])
    m_specs = [pl.BlockSpec((BLK, pe[n].shape[1]), lambda b, c, lm: (1, 0)) for n in names]
    p_specs = [pl.BlockSpec((BLK, pp[n].shape[1]), lambda b, c, lm: (b * nblk + jnp.maximum(c - 1, 0), 0))
               for n in names]
    w_specs = [
        pl.BlockSpec((1, GLA_DV), lambda b, c, lm: (0, 0)),
        pl.BlockSpec((BLK, BLK), lambda b, c, lm: (0, 0)),
    ]
    y, s_fin = pl.pallas_call(
        functools.partial(_gla_body, nblk),
        grid_spec=pltpu.PrefetchScalarGridSpec(
            num_scalar_prefetch=1,
            grid=(nbatch, nblk + 1),
            in_specs=m_specs + p_specs + w_specs,
            out_specs=[
                pl.BlockSpec((BLK, D_MODEL), lambda b, c, lm: (b * nblk + jnp.maximum(c - 1, 0), 0)),
                pl.BlockSpec((None, GLA_HEADS, GLA_DK, GLA_DV), lambda b, c, lm: (b, 0, 0, 0)),
            ],
            scratch_shapes=[pltpu.VMEM((GLA_HEADS, GLA_DK, GLA_DV), F32)],
        ),
        out_shape=[
            jax.ShapeDtypeStruct((nbatch * nblk * BLK, D_MODEL), BF16),
            jax.ShapeDtypeStruct((nbatch, GLA_HEADS, GLA_DK, GLA_DV), F32),
        ],
        compiler_params=_cparams(("arbitrary", "arbitrary")),
        name="gla_prompt",
    )(lamax, *[pe[n] for n in names], *[pp[n] for n in names], gnorm.reshape(1, -1), _tri_incl())
    return y, s_fin


GLA_STEP_SEQS = 16


def _gla_step_body(q_ref, k_ref, v_ref, r_ref, la_ref, ga_ref, gn_ref, s_in_ref,
                   y_ref, s_out_ref, at_ref, kt_ref, qt_ref):
    g = pl.program_id(0)
    nseq = q_ref.shape[0]

    @pl.when(g == 0)
    def _():
        a = jnp.exp(la_ref[...])
        for h in range(GLA_HEADS):
            dk = slice(h * GLA_DK, (h + 1) * GLA_DK)
            at_ref[h] = jnp.transpose(a[:, dk])
            kt_ref[h] = jnp.transpose(k_ref[:, dk].astype(F32))
            qt_ref[h] = jnp.transpose(q_ref[:, dk].astype(F32))

    lane = lax.broadcasted_iota(jnp.int32, (GLA_DK, nseq), 1)
    ones = jnp.ones((nseq, GLA_DV), BF16)
    grp = pl.ds(pl.multiple_of(g * GLA_STEP_SEQS, GLA_STEP_SEQS), GLA_STEP_SEQS)
    r_grp = r_ref[grp, :].astype(F32)
    ga_grp = ga_ref[grp, :].astype(F32)
    for i in range(GLA_STEP_SEQS):
        n = g * GLA_STEP_SEQS + i
        sel = lane == n
        for h in range(GLA_HEADS):
            dv = slice(h * GLA_DV, (h + 1) * GLA_DV)
            a_sel = jnp.where(sel, at_ref[h], 0.0)
            k_sel = jnp.where(sel, kt_ref[h], 0.0).astype(BF16)
            q_sel = jnp.where(sel, qt_ref[h], 0.0).astype(BF16)
            decay = _split_dot_rhs(a_sel, ones)
            kv = jnp.dot(k_sel, v_ref[:, dv], preferred_element_type=F32)
            q_b = jnp.dot(q_sel, ones, preferred_element_type=F32)
            s_new = decay * s_in_ref[i, h] + kv
            s_out_ref[i, h] = s_new
            o = jnp.sum(q_b * s_new, axis=0, keepdims=True)
            o = o * lax.rsqrt(jnp.mean(o * o, axis=-1, keepdims=True) + LN_EPS) * gn_ref[...]
            r = r_grp[i:i + 1, dv]
            ga = ga_grp[i:i + 1, dv]
            y_ref[i:i + 1, dv] = (o * (r * _sigmoid(r)) * _sigmoid(ga)).astype(y_ref.dtype)


def _split_dot_rhs(x, b01):
    hi = x.astype(BF16)
    lo = (x - hi.astype(F32)).astype(BF16)
    return jnp.dot(hi, b01, preferred_element_type=F32) + jnp.dot(lo, b01, preferred_element_type=F32)


def _gla_step(pe, gnorm, state):
    nseq = state.shape[0]
    assert nseq == BLK and nseq % GLA_STEP_SEQS == 0
    names = ("qg", "kg", "vg", "rg", "la", "ga")
    t_specs = [pl.BlockSpec((nseq, pe[n].shape[1]), lambda g: (0, 0)) for n in names]
    st_spec = pl.BlockSpec((GLA_STEP_SEQS, GLA_HEADS, GLA_DK, GLA_DV), lambda g: (g, 0, 0, 0))
    return pl.pallas_call(
        _gla_step_body,
        grid=(nseq // GLA_STEP_SEQS,),
        in_specs=t_specs + [
            pl.BlockSpec((1, GLA_DV), lambda g: (0, 0)),
            st_spec,
        ],
        out_specs=[pl.BlockSpec((GLA_STEP_SEQS, D_MODEL), lambda g: (g, 0)), st_spec],
        out_shape=[jax.ShapeDtypeStruct((nseq, D_MODEL), F32), jax.ShapeDtypeStruct(state.shape, F32)],
        scratch_shapes=[pltpu.VMEM((GLA_HEADS, GLA_DK, nseq), F32) for _ in range(3)],
        compiler_params=_cparams(("arbitrary",)),
        name="gla_step",
    )(*[pe[n] for n in names], gnorm.reshape(1, -1), state)


HALF = LANES // 2


def _rel_bucket(dist):
    max_exact = REL_BUCKETS // 2
    d = jnp.maximum(dist, 0)
    large = max_exact + (jnp.log(jnp.maximum(d, 1).astype(F32) / max_exact)
                         / math.log(REL_MAX_DIST / max_exact) * (REL_BUCKETS - max_exact)).astype(jnp.int32)
    large = jnp.minimum(large, REL_BUCKETS - 1)
    return jnp.where(d < max_exact, d, large)


def _bias_lookup(rel_bias, dist):
    onehot = (_rel_bucket(dist)[..., None] == jnp.arange(REL_BUCKETS)).astype(F32)
    bias = jnp.einsum("...b,bh->h...", onehot, rel_bias.astype(F32), precision=lax.Precision.HIGHEST)
    return bias * LOG2E


def _swa_bias_tables(rel_bias):
    q = jnp.arange(BLK)[:, None]
    c = jnp.arange(2 * BLK)[None, :]
    dist = BLK + q - c
    bias = _bias_lookup(rel_bias, dist)
    inside = (dist >= 0) & (dist < WINDOW)
    first = inside & (c >= BLK - N_META)
    neg = jnp.float32(-jnp.inf)
    return jnp.stack([jnp.where(first[None], bias, neg), jnp.where(inside[None], bias, neg)])


def _dup_tiles(x):
    lane = lax.broadcasted_iota(jnp.int32, (x.shape[0], LANES), 1)
    low = lane < HALF
    out = []
    for t in range(2):
        tile = x[:, t * LANES:(t + 1) * LANES]
        swapped = pltpu.roll(tile, HALF, axis=1)
        out += [jnp.where(low, tile, swapped).astype(BF16), jnp.where(low, swapped, tile).astype(BF16)]
    return out


def _group_row_heads(j):
    tiles = [j * (GQA_GROUP // 2) + pair for pair in range(GQA_GROUP // 2)]
    return [2 * t for t in tiles] + [2 * t + 1 for t in tiles]


def _swa_body(nblk, q_ref, kc_ref, vc_ref, kp_ref, vp_ref, km_ref, vm_ref, gb_ref, yg_ref, tb_ref, sink_ref,
              o_ref, kw_ref, vw_ref):
    blk = pl.program_id(1)
    first = blk == 0

    @pl.when(blk == nblk - 1)
    def _():
        kw_ref[...] = kc_ref[...]
        vw_ref[...] = vc_ref[...]

    k_prev = jnp.where(first, km_ref[...], kp_ref[...])
    v_prev = jnp.where(first, vm_ref[...], vp_ref[...])
    k_tiles = _dup_tiles(jnp.concatenate([k_prev, kc_ref[...]], axis=0))
    v_tiles = _dup_tiles(jnp.concatenate([v_prev, vc_ref[...]], axis=0))
    variant = jnp.minimum(blk, 1)
    npair = GQA_GROUP // 2
    low = lax.broadcasted_iota(jnp.int32, (BLK, LANES), 1) < HALF
    ones = jnp.ones((2 * BLK, LANES), BF16)
    seg = lax.broadcasted_iota(jnp.int32, (GQA_GROUP * BLK, 1), 0) // BLK
    nt = (((1,), (1,)), ((), ()))
    for j in range(ATT_KV_HEADS):
        tiles = [j * npair + pair for pair in range(npair)]
        q_t = [q_ref[:, t * LANES:(t + 1) * LANES] for t in tiles]
        zero = jnp.zeros_like(q_t[0])
        q_st = jnp.concatenate([jnp.where(low, q, zero) for q in q_t] + [jnp.where(low, zero, q) for q in q_t], axis=0)
        heads = _group_row_heads(j)
        s = lax.dot_general(q_st, k_tiles[j], nt, preferred_element_type=F32)
        s = s + jnp.concatenate([tb_ref[variant, h] for h in heads], axis=0)
        sink = jnp.full((GQA_GROUP * BLK, 1), sink_ref[heads[0]], F32)
        for i in range(1, GQA_GROUP):
            sink = jnp.where(seg == i, sink_ref[heads[i]], sink)
        m = jnp.maximum(jnp.max(s, axis=-1, keepdims=True), sink)
        p = jnp.exp2(s - m).astype(BF16)
        pv = jnp.dot(p, jnp.concatenate([v_tiles[j], ones], axis=1), preferred_element_type=F32)
        o = pv[:, :LANES] / (pv[:, LANES:] + jnp.exp2(sink - m))
        for pair, t in enumerate(tiles):
            cols = slice(t * LANES, (t + 1) * LANES)
            gate = _sigmoid(gb_ref[:, cols].astype(F32))
            even = o[pair * BLK:(pair + 1) * BLK]
            odd = o[(npair + pair) * BLK:(npair + pair + 1) * BLK]
            o_ref[:, cols] = (gate * jnp.where(low, even, odd) + yg_ref[:, cols].astype(F32)).astype(o_ref.dtype)


def _swa_prompt(pp, pe, yg, rel_bias, sinks, nbatch, nblk):
    tb = _swa_bias_tables(rel_bias)
    kvw = ATT_KV_HEADS * HEAD_DIM
    sinks2 = sinks.astype(F32) * LOG2E
    cur = lambda b, c: (b * nblk + c, 0)
    prev = lambda b, c: (b * nblk + jnp.maximum(c - 1, 0), 0)
    return pl.pallas_call(
        functools.partial(_swa_body, nblk),
        grid=(nbatch, nblk),
        in_specs=[
            pl.BlockSpec((BLK, D_MODEL), cur),
            pl.BlockSpec((BLK, kvw), cur), pl.BlockSpec((BLK, kvw), cur),
            pl.BlockSpec((BLK, kvw), prev), pl.BlockSpec((BLK, kvw), prev),
            pl.BlockSpec((BLK, kvw), lambda b, c: (1, 0)), pl.BlockSpec((BLK, kvw), lambda b, c: (1, 0)),
            pl.BlockSpec((BLK, D_MODEL), cur),
            pl.BlockSpec((BLK, D_MODEL), cur),
            pl.BlockSpec(tb.shape, lambda b, c: (0, 0, 0, 0)),
            pl.BlockSpec(memory_space=pltpu.SMEM),
        ],
        out_specs=[pl.BlockSpec((BLK, D_MODEL), cur),
                   pl.BlockSpec((None, BLK, kvw), lambda b, c: (b, 0, 0)),
                   pl.BlockSpec((None, BLK, kvw), lambda b, c: (b, 0, 0))],
        out_shape=[jax.ShapeDtypeStruct((nbatch * nblk * BLK, D_MODEL), BF16),
                   jax.ShapeDtypeStruct((nbatch, BLK, kvw), F32), jax.ShapeDtypeStruct((nbatch, BLK, kvw), F32)],
        compiler_params=_cparams(("arbitrary", "arbitrary")),
        name="swa_prompt",
    )(pp["qa"], pp["ka"], pp["va"], pp["ka"], pp["va"], pe["ka"], pe["va"], pp["gb"], yg, tb, sinks2)


SWA_STEP_SEQS = 8
Q_TILES = ATT_HEADS // 2


def _swa_step_body(q_ref, kn_ref, vn_ref, ck_ref, cv_ref, gb_ref, yg_ref, tb_ref, sink_ref,
                   o_ref, ko_ref, vo_ref):
    row = lax.broadcasted_iota(jnp.int32, (WINDOW, ATT_KV_HEADS * HEAD_DIM), 0)
    low = lax.broadcasted_iota(jnp.int32, (Q_TILES, LANES), 1) < HALF
    mine16 = (lax.broadcasted_iota(jnp.int32, (2 * Q_TILES, LANES), 0) % Q_TILES) // (GQA_GROUP // 2)
    nt = (((1,), (1,)), ((), ()))
    seqs = range(SWA_STEP_SEQS)
    k_wide, v_wide = [], []
    for i in seqs:
        k_win = jnp.where(row == WINDOW - 1, kn_ref[i:i + 1, :], pltpu.roll(ck_ref[i], WINDOW - 1, axis=0))
        v_win = jnp.where(row == WINDOW - 1, vn_ref[i:i + 1, :], pltpu.roll(cv_ref[i], WINDOW - 1, axis=0))
        ko_ref[i] = k_win
        vo_ref[i] = v_win
        k_wide.append(jnp.concatenate(_dup_tiles(k_win), axis=1))
        v_wide.append(jnp.concatenate(_dup_tiles(v_win), axis=1))
    scores = []
    for i in seqs:
        q8 = q_ref[i]
        q16 = jnp.concatenate([jnp.where(low, q8, 0.0), jnp.where(low, 0.0, q8)], axis=0)
        q_wide = jnp.concatenate([jnp.where(mine16 == j, q16, 0.0) for j in range(ATT_KV_HEADS)], axis=1)
        scores.append(lax.dot_general(q_wide.astype(BF16), k_wide[i], nt, preferred_element_type=F32))
    probs, invs = [], []
    for i in seqs:
        s = scores[i] + tb_ref[...]
        sink = sink_ref[...]
        m = jnp.maximum(jnp.max(s, axis=-1, keepdims=True), sink)
        p = jnp.exp2(s - m)
        invs.append(1.0 / (jnp.sum(p, axis=-1, keepdims=True) + jnp.exp2(sink - m)))
        probs.append(p.astype(BF16))
    outs = [jnp.dot(probs[i], v_wide[i], preferred_element_type=F32) for i in seqs]
    for i in seqs:
        o = jnp.zeros((2 * Q_TILES, LANES), F32)
        for j in range(ATT_KV_HEADS):
            o = jnp.where(mine16 == j, outs[i][:, j * LANES:(j + 1) * LANES], o)
        o = o * invs[i]
        o = jnp.where(low, o[:Q_TILES], o[Q_TILES:])
        o_ref[i] = _sigmoid(gb_ref[i]) * o + yg_ref[i]


def _swa_step(pe, yg_s, cache_k, cache_v, rel_bias, sinks):
    nseq = cache_k.shape[0]
    kvw = ATT_KV_HEADS * HEAD_DIM
    as_tiles = lambda x: x[:nseq].astype(F32).reshape(nseq, Q_TILES, LANES)
    dist = (WINDOW - 1) - jnp.arange(WINDOW)
    bias = _bias_lookup(rel_bias, dist)
    tb = jnp.concatenate([bias[0::2], bias[1::2]], axis=0)
    sk = jnp.concatenate([sinks[0::2], sinks[1::2]])[:, None].astype(F32) * LOG2E
    g = SWA_STEP_SEQS
    tile_spec = pl.BlockSpec((g, Q_TILES, LANES), lambda s: (s, 0, 0))
    win_spec = pl.BlockSpec((g, WINDOW, kvw), lambda s: (s, 0, 0))
    new_spec = pl.BlockSpec((g, kvw), lambda s: (s, 0))
    o, ko, vo = pl.pallas_call(
        _swa_step_body,
        grid=(nseq // g,),
        in_specs=[tile_spec, new_spec, new_spec, win_spec, win_spec, tile_spec, tile_spec,
                  pl.BlockSpec(tb.shape, lambda s: (0, 0)), pl.BlockSpec(sk.shape, lambda s: (0, 0))],
        out_specs=[tile_spec, win_spec, win_spec],
        out_shape=[jax.ShapeDtypeStruct((nseq, Q_TILES, LANES), F32),
                   jax.ShapeDtypeStruct(cache_k.shape, F32), jax.ShapeDtypeStruct(cache_v.shape, F32)],
        compiler_params=_cparams(("arbitrary",)),
        name="swa_step",
    )(as_tiles(pe["qa"]), pe["ka"], pe["va"], cache_k, cache_v, as_tiles(pe["gb"]),
      yg_s.reshape(nseq, Q_TILES, LANES), tb, sk)
    return o.reshape(nseq, D_MODEL), ko, vo


ROUTER_ROWS = 40
META_ROWS = 8


def _split3_nt(a_hi, a_lo, x):
    nt = (((1,), (1,)), ((), ()))
    x_hi = x.astype(BF16)
    x_lo = (x - x_hi.astype(F32)).astype(BF16)
    return (lax.dot_general(a_hi, x_hi, nt, preferred_element_type=F32)
            + lax.dot_general(a_hi, x_lo, nt, preferred_element_type=F32)
            + lax.dot_general(a_lo, x_hi, nt, preferred_element_type=F32))


def _first_argmax_rows(v, ridx, nrows):
    vmax = jnp.max(v, axis=0, keepdims=True)
    idx = jnp.min(jnp.where(v == vmax, ridx, nrows), axis=0, keepdims=True)
    return vmax, idx


def _post_body(nsteps, *refs):
    h1_ref, meta_ref, wcol_ref = refs[-5:-2]
    i = pl.program_id(0)

    @pl.when(i < nsteps)
    def _():
        _post_tile(i, *refs)

    @pl.when(i >= nsteps)
    def _():
        h1_ref[...] = jnp.zeros_like(h1_ref)
        meta_ref[...] = jnp.zeros_like(meta_ref)
        wcol_ref[...] = jnp.zeros_like(wcol_ref)


def _post_tile(i, mg_ref, x_ref, lng_ref, lnb_ref, wo_ref, g1_ref, b1_ref, wrh_ref, wrl_ref, rb_ref, ut_ref,
               cin_ref, *rest):
    h1_ref, meta_ref, wcol_ref, cout_ref, carry_ref = rest[-5:]

    @pl.when(i == 0)
    def _():
        carry_ref[...] = cin_ref[...]

    tm = x_ref.shape[0]
    h = _layer_norm_rows(x_ref[...], lng_ref[...], lnb_ref[...])
    acc = jnp.dot(mg_ref[...].astype(BF16), wo_ref[...], preferred_element_type=F32)
    h1 = _layer_norm_rows(ALPHA * h + acc, g1_ref[...], b1_ref[...])
    h1_ref[...] = h1

    lt = _split3_nt(wrh_ref[...], wrl_ref[...], h1) + rb_ref[:, :1]
    ridx = lax.broadcasted_iota(jnp.int32, (EXPERTS_PER_GROUP, tm), 0)
    neg = jnp.float32(-jnp.inf)
    g_log = jnp.where(ridx < N_GROUPS, lt[N_EXPERTS:N_EXPERTS + EXPERTS_PER_GROUP], neg)
    g_max, grp = _first_argmax_rows(g_log, ridx, EXPERTS_PER_GROUP)
    p_grp = 1.0 / jnp.sum(jnp.exp(g_log - g_max), axis=0, keepdims=True)
    e_in = lt[0:EXPERTS_PER_GROUP]
    for gi in range(1, N_GROUPS):
        e_in = jnp.where(grp == gi, lt[gi * EXPERTS_PER_GROUP:(gi + 1) * EXPERTS_PER_GROUP], e_in)
    v0, i0 = _first_argmax_rows(e_in, ridx, EXPERTS_PER_GROUP)
    v1, i1 = _first_argmax_rows(jnp.where(ridx == i0, neg, e_in), ridx, EXPERTS_PER_GROUP)
    t = jnp.exp(v1 - v0)
    w0 = p_grp / (1.0 + t)
    w1 = p_grp * t / (1.0 + t)
    e0 = grp * EXPERTS_PER_GROUP + i0
    e1 = grp * EXPERTS_PER_GROUP + i1

    eidx = lax.broadcasted_iota(jnp.int32, (N_EXPERTS, tm), 0)
    hit0 = eidx == e0
    hit1 = eidx == e1
    oh = jnp.where(jnp.logical_or(hit0, hit1), 1.0, 0.0)
    before = jnp.dot(oh.astype(BF16), ut_ref[...], preferred_element_type=F32) + carry_ref[:, :1]
    r0 = jnp.sum(jnp.where(hit0, before, 0.0), axis=0, keepdims=True).astype(jnp.int32)
    r1 = jnp.sum(jnp.where(hit1, before, 0.0), axis=0, keepdims=True).astype(jnp.int32)
    carry_ref[...] = carry_ref[...] + jnp.sum(oh, axis=1, keepdims=True)
    cout_ref[...] = carry_ref[...]

    zi = jnp.zeros((META_ROWS - 4, tm), jnp.int32)
    meta_ref[...] = jnp.concatenate([e0, e1, r0, r1, zi], axis=0)
    wt = jnp.concatenate([w0, w1, jnp.zeros((LANES - 2, tm), F32)], axis=0)
    wcol_ref[...] = jnp.transpose(wt)


def _post(mg, x2d, prm, tm, row0, total_rows, carry_in, prev=None, zero_tail=False):
    m = x2d.shape[0]
    assert m % tm == 0 and row0 % tm == 0 and total_rows % tm == 0
    off = row0 // tm
    nsteps = m // tm
    last = nsteps - 1
    ut = jnp.asarray(np.triu(np.ones((tm, tm), np.float32), 1), dtype=BF16)
    full = lambda shape: pl.BlockSpec(shape, lambda i: (0,) * len(shape))
    in_specs = [
        pl.BlockSpec((tm, D_MODEL), lambda i: (jnp.minimum(i, last), 0)),
        pl.BlockSpec((tm, D_MODEL), lambda i: (jnp.minimum(i, last), 0)),
        full((1, D_MODEL)), full((1, D_MODEL)),
        full((D_MODEL, D_MODEL)),
        full((1, D_MODEL)), full((1, D_MODEL)),
        full((ROUTER_ROWS, D_MODEL)), full((ROUTER_ROWS, D_MODEL)), full((ROUTER_ROWS, LANES)),
        full((tm, tm)),
        full((N_EXPERTS, LANES)),
    ]
    args = [mg, x2d, prm["ln_emb_g"], prm["ln_emb_b"], prm["w_out"], prm["ln1_g"], prm["ln1_b"],
            prm["wr_hi"], prm["wr_lo"], prm["r_bias"], ut, carry_in]
    aliases = {}
    if prev is not None:
        for k, buf in enumerate(prev):
            in_specs.append(pl.BlockSpec(memory_space=pl.ANY))
            aliases[len(args)] = k
            args.append(buf)
    out_shape = [
        jax.ShapeDtypeStruct((total_rows, D_MODEL), F32),
        jax.ShapeDtypeStruct((META_ROWS, total_rows), jnp.int32),
        jax.ShapeDtypeStruct((total_rows, LANES), F32),
        jax.ShapeDtypeStruct((N_EXPERTS, LANES), F32),
    ]
    out_specs = [
        pl.BlockSpec((tm, D_MODEL), lambda i: (i + off, 0)),
        pl.BlockSpec((META_ROWS, tm), lambda i: (0, i + off)),
        pl.BlockSpec((tm, LANES), lambda i: (i + off, 0)),
        full((N_EXPERTS, LANES)),
    ]
    if prev is not None:
        assert len(prev) == 3
    return pl.pallas_call(
        functools.partial(_post_body, nsteps),
        grid=(nsteps + int(zero_tail),),
        in_specs=in_specs,
        out_specs=out_specs,
        out_shape=out_shape,
        input_output_aliases=aliases,
        scratch_shapes=[pltpu.VMEM((N_EXPERTS, LANES), F32)],
        compiler_params=_cparams(("arbitrary",)),
        name="post_attn",
    )(*args)


def _prep_post_params(ln_emb_g, ln_emb_b, w_out, ln1_g, ln1_b, w_rg, b_rg, w_re, b_re):
    row = lambda v: v.reshape(1, -1)
    wr = jnp.concatenate([w_re.T, w_rg.T, jnp.zeros((ROUTER_ROWS - N_EXPERTS - N_GROUPS, D_MODEL), F32)], axis=0)
    wr_hi = wr.astype(BF16)
    wr_lo = (wr - wr_hi.astype(F32)).astype(BF16)
    rb = jnp.concatenate([b_re, b_rg, jnp.zeros((ROUTER_ROWS - N_EXPERTS - N_GROUPS,), F32)])
    return dict(ln_emb_g=row(ln_emb_g), ln_emb_b=row(ln_emb_b), w_out=w_out.astype(BF16), ln1_g=row(ln1_g),
                ln1_b=row(ln1_b), wr_hi=wr_hi, wr_lo=wr_lo,
                r_bias=jnp.broadcast_to(rb[:, None], (ROUTER_ROWS, LANES)))


MOE_ROWS = 256
EXPERT_BLOCKS_PER_STEP = 2
SUBLANES = 8
assert D_MODEL == SUBLANES * LANES


def _store_rows_as_tiles(ref, x, first_row=0):
    n = x.shape[0]
    for c in range(SUBLANES):
        ref[pl.ds(first_row * SUBLANES + c, n, stride=SUBLANES), :] = x[:, c * LANES:(c + 1) * LANES]


def _load_rows_from_tiles(ref, n, first_row=0):
    return jnp.concatenate([ref[pl.ds(first_row * SUBLANES + c, n, stride=SUBLANES), :] for c in range(SUBLANES)],
                           axis=1)


def _tile_of_row(ref, r):
    return ref.at[pl.ds(pl.multiple_of(r * SUBLANES, SUBLANES), SUBLANES)]


def _moe_plan(counts, total_assign):
    nb_max = -(-total_assign // MOE_ROWS) + N_EXPERTS
    nb_max += -nb_max % EXPERT_BLOCKS_PER_STEP
    padded = (counts + MOE_ROWS - 1) // MOE_ROWS * MOE_ROWS
    pend = jnp.cumsum(padded)
    pstart = (pend - padded).astype(jnp.int32)
    block_start = jnp.arange(nb_max, dtype=jnp.int32) * MOE_ROWS
    n_ended = jnp.sum((pend[None, :] <= block_start[:, None]).astype(jnp.int32), axis=1)
    block_e = jnp.minimum(n_ended, N_EXPERTS - 1).astype(jnp.int32)
    n_used = (pend[-1] // MOE_ROWS).astype(jnp.int32).reshape(1)
    tail_start = jnp.where(padded > 0, pend - MOE_ROWS, -1)
    spare = pend[-1] + jnp.arange(N_EXPERTS + EXPERT_BLOCKS_PER_STEP, dtype=pend.dtype) * MOE_ROWS
    spare = jnp.where(spare < nb_max * MOE_ROWS, spare, -1)
    zero_blocks = jnp.concatenate([tail_start, spare]).astype(jnp.int32)
    return pstart, block_e, n_used, nb_max, zero_blocks


ROW_UNROLL = 8


def _slot_ids(meta, pstart):
    experts = meta[0:TOP_K]
    ranks = meta[TOP_K:2 * TOP_K]
    onehot = experts[..., None] == jnp.arange(N_EXPERTS, dtype=jnp.int32)
    return ranks + jnp.sum(jnp.where(onehot, pstart, 0), axis=-1)


def _for_row_groups(tm, fn, read=None):
    def group(g, c):
        t0 = pl.multiple_of(g * ROW_UNROLL, ROW_UNROLL)
        items = [(t0 + r, k) for r in range(ROW_UNROLL) for k in range(TOP_K)]
        if read is None:
            for t, k in items:
                fn(t, k)
        else:
            vals = [read(t, k) for t, k in items]
            for (t, k), v in zip(items, vals):
                fn(t, k, v)
        return c

    lax.fori_loop(0, tm // ROW_UNROLL, group, 0)


def _dispatch_body(nsteps, tail_ref, s0_ref, s1_ref, h_ref, xs_ref, pk_ref, zero_ref, sems, zsem):
    i = pl.program_id(0)
    tm = h_ref.shape[0]
    slot_refs = (s0_ref, s1_ref)
    cur = i % 2
    blk_tiles = MOE_ROWS * SUBLANES

    @pl.when(i == 0)
    def _():
        zero_ref[...] = jnp.zeros_like(zero_ref)
        for e in range(tail_ref.shape[0]):
            @pl.when(tail_ref[e] >= 0)
            def _():
                start = pl.multiple_of(tail_ref[e] * SUBLANES, blk_tiles)
                pltpu.make_async_copy(zero_ref, xs_ref.at[pl.ds(start, blk_tiles)], zsem).start()
        for e in range(tail_ref.shape[0]):
            @pl.when(tail_ref[e] >= 0)
            def _():
                pltpu.make_async_copy(zero_ref, xs_ref.at[pl.ds(0, blk_tiles)], zsem).wait()

    _store_rows_as_tiles(pk_ref.at[cur], h_ref[...])

    def send(t, k, slot):
        pltpu.make_async_copy(_tile_of_row(pk_ref.at[cur], t), _tile_of_row(xs_ref, slot),
                              sems.at[cur]).start(priority=k)

    def wait_buffer(buf):
        _for_row_groups(tm, lambda t, k: pltpu.make_async_copy(
            _tile_of_row(pk_ref.at[buf], t), _tile_of_row(xs_ref, 0), sems.at[buf]).wait())

    _for_row_groups(tm, send, read=lambda t, k: slot_refs[k][0, t])

    @pl.when(i > 0)
    def _():
        wait_buffer(1 - cur)

    @pl.when(i == nsteps - 1)
    def _():
        wait_buffer(cur)


def _dispatch(h1, slot_ids, tail_start, nslots, tm, total):
    assert total % tm == 0 and total <= h1.shape[0] and tm % ROW_UNROLL == 0
    slot_spec = pl.BlockSpec((1, tm), lambda i, tl: (0, i), memory_space=pltpu.SMEM)
    return pl.pallas_call(
        functools.partial(_dispatch_body, total // tm),
        grid_spec=pltpu.PrefetchScalarGridSpec(
            num_scalar_prefetch=1,
            grid=(total // tm,),
            in_specs=[slot_spec, slot_spec, pl.BlockSpec((tm, D_MODEL), lambda i, tl: (i, 0))],
            out_specs=pl.BlockSpec(memory_space=pl.ANY),
            scratch_shapes=[pltpu.VMEM((2, tm * SUBLANES, LANES), F32), pltpu.VMEM((MOE_ROWS * SUBLANES, LANES), F32),
                            pltpu.SemaphoreType.DMA((2,)), pltpu.SemaphoreType.DMA(())],
        ),
        out_shape=jax.ShapeDtypeStruct((nslots * SUBLANES, LANES), F32),
        compiler_params=_cparams(("arbitrary",)),
        name="moe_dispatch",
    )(tail_start, slot_ids[0:1], slot_ids[1:2], h1)


def _expert_schedule(block_e, n_used):
    nb = block_e.shape[0]
    idx = jnp.arange(nb, dtype=jnp.int32)
    first = (idx < n_used[0]) & ((idx == 0) | (block_e != jnp.roll(block_e, 1)))
    parity = (jnp.cumsum(first.astype(jnp.int32)) - 1) % 2
    pos = jnp.where(first, idx, nb)
    at_or_after = jnp.flip(lax.cummin(jnp.flip(pos)))
    nxt = jnp.concatenate([at_or_after[1:], jnp.full((1,), nb, jnp.int32)])
    nexte = jnp.where(nxt < nb, block_e[jnp.minimum(nxt, nb - 1)], -1)
    return first.astype(jnp.int32), nexte.astype(jnp.int32), parity.astype(jnp.int32)


def _expert_body(be_ref, nu_ref, first_ref, nexte_ref, par_ref, xs_ref, wg_hbm, wu_hbm, wd_hbm, ys_ref,
                 wgf_ref, wuf_ref, wdf_ref, wgb_ref, wub_ref, wdb_ref, sems):
    i = pl.program_id(0)
    hbm = (wg_hbm, wu_hbm, wd_hbm)
    stage = (wgf_ref, wuf_ref, wdf_ref)

    def weight_copies(e, buf):
        return [pltpu.make_async_copy(hbm[w].at[e], stage[w].at[buf], sems.at[buf, w]) for w in range(3)]

    def prepare(b):
        @pl.when(first_ref[b] == 1)
        def _():
            buf = par_ref[b]

            @pl.when(b == 0)
            def _():
                for c in weight_copies(be_ref[0], buf):
                    c.start()

            for c in weight_copies(be_ref[b], buf):
                c.wait()

            @pl.when(nexte_ref[b] >= 0)
            def _():
                for c in weight_copies(nexte_ref[b], 1 - buf):
                    c.start()

            wgb_ref[buf] = wgf_ref[buf].astype(BF16)
            wub_ref[buf] = wuf_ref[buf].astype(BF16)
            wdb_ref[buf] = wdf_ref[buf].astype(BF16)

    def ffn(b, sub):
        buf = par_ref[b]
        x = _load_rows_from_tiles(xs_ref, MOE_ROWS, sub * MOE_ROWS).astype(BF16)
        yield
        g = jnp.dot(x, wgb_ref[buf], preferred_element_type=F32)
        u = jnp.dot(x, wub_ref[buf], preferred_element_type=F32)
        yield
        y = jnp.dot(((g * _sigmoid(g)) * u).astype(BF16), wdb_ref[buf], preferred_element_type=F32)
        yield
        _store_rows_as_tiles(ys_ref, y, sub * MOE_ROWS)

    def run(gens):
        while gens:
            gens = [g for g in gens if next(g, True) is None]

    def zero(sub):
        ys_ref[pl.ds(sub * MOE_ROWS * SUBLANES, MOE_ROWS * SUBLANES), :] = jnp.zeros(
            (MOE_ROWS * SUBLANES, LANES), F32)

    blocks = [EXPERT_BLOCKS_PER_STEP * i + sub for sub in range(EXPERT_BLOCKS_PER_STEP)]
    for b in blocks:
        prepare(b)
    n_live = jnp.clip(nu_ref[0] - blocks[0], 0, EXPERT_BLOCKS_PER_STEP)
    for live in range(EXPERT_BLOCKS_PER_STEP + 1):
        @pl.when(n_live == live)
        def _():
            run([ffn(blocks[sub], sub) for sub in range(live)])
            for sub in range(live, EXPERT_BLOCKS_PER_STEP):
                zero(sub)


def _experts(xs, block_e, n_used, w_gate, w_up, w_down, nb_max):
    assert nb_max % EXPERT_BLOCKS_PER_STEP == 0
    first, nexte, parity = _expert_schedule(block_e, n_used)
    rows = lambda i, *_: (i, 0)
    any_spec = pl.BlockSpec(memory_space=pl.ANY)
    blk = (EXPERT_BLOCKS_PER_STEP * MOE_ROWS * SUBLANES, LANES)
    return pl.pallas_call(
        _expert_body,
        grid_spec=pltpu.PrefetchScalarGridSpec(
            num_scalar_prefetch=5,
            grid=(nb_max // EXPERT_BLOCKS_PER_STEP,),
            in_specs=[pl.BlockSpec(blk, rows), any_spec, any_spec, any_spec],
            out_specs=pl.BlockSpec(blk, rows),
            scratch_shapes=[
                pltpu.VMEM((2, D_MODEL, D_EXPERT), F32), pltpu.VMEM((2, D_MODEL, D_EXPERT), F32),
                pltpu.VMEM((2, D_EXPERT, D_MODEL), F32),
                pltpu.VMEM((2, D_MODEL, D_EXPERT), BF16), pltpu.VMEM((2, D_MODEL, D_EXPERT), BF16),
                pltpu.VMEM((2, D_EXPERT, D_MODEL), BF16),
                pltpu.SemaphoreType.DMA((2, 3)),
            ],
        ),
        out_shape=jax.ShapeDtypeStruct(xs.shape, F32),
        compiler_params=_cparams(("arbitrary",)),
        name="moe_experts",
    )(block_e, n_used, first, nexte, parity, xs, w_gate, w_up, w_down)


def _combine_body(nsteps, s0_ref, s1_ref, n0_ref, n1_ref, h_ref, w_ref, g2_ref, b2_ref, ys_ref, o_ref, buf_ref, sems):
    i = pl.program_id(0)
    tm = o_ref.shape[0]
    cur = i % 2

    def fetch(slot_refs, buf):
        _for_row_groups(tm, lambda t, k, slot: pltpu.make_async_copy(
            _tile_of_row(ys_ref, slot), _tile_of_row(buf_ref.at[buf, k], t),
            sems.at[buf]).start(priority=k), read=lambda t, k: slot_refs[k][0, t])

    @pl.when(i == 0)
    def _():
        fetch((s0_ref, s1_ref), cur)

    @pl.when(i + 1 < nsteps)
    def _():
        fetch((n0_ref, n1_ref), 1 - cur)

    _for_row_groups(tm, lambda t, k: pltpu.make_async_copy(
        _tile_of_row(ys_ref, 0), _tile_of_row(buf_ref.at[cur, k], t), sems.at[cur]).wait())
    w = w_ref[...]
    f = (w[:, 0:1] * _load_rows_from_tiles(buf_ref.at[cur, 0], tm)
         + w[:, 1:2] * _load_rows_from_tiles(buf_ref.at[cur, 1], tm))
    o_ref[...] = _layer_norm_rows(ALPHA * h_ref[...] + f, g2_ref[...], b2_ref[...])


def _combine(h1, wcol, slot_ids, ys, ln2_g, ln2_b, tm, row0, nrows):
    assert nrows % tm == 0 and row0 % tm == 0 and tm % ROW_UNROLL == 0
    off = row0 // tm
    nsteps = nrows // tm
    slot_spec = pl.BlockSpec((1, tm), lambda i: (0, i + off), memory_space=pltpu.SMEM)
    next_spec = pl.BlockSpec((1, tm), lambda i: (0, jnp.minimum(i + 1, nsteps - 1) + off), memory_space=pltpu.SMEM)
    return pl.pallas_call(
        functools.partial(_combine_body, nsteps),
        grid=(nsteps,),
        in_specs=[
            slot_spec, slot_spec, next_spec, next_spec,
            pl.BlockSpec((tm, D_MODEL), lambda i: (i + off, 0)),
            pl.BlockSpec((tm, LANES), lambda i: (i + off, 0)),
            pl.BlockSpec((1, D_MODEL), lambda i: (0, 0)),
            pl.BlockSpec((1, D_MODEL), lambda i: (0, 0)),
            pl.BlockSpec(memory_space=pl.ANY),
        ],
        out_specs=pl.BlockSpec((tm, D_MODEL), lambda i: (i, 0)),
        scratch_shapes=[pltpu.VMEM((2, TOP_K, tm * SUBLANES, LANES), F32), pltpu.SemaphoreType.DMA((2,))],
        out_shape=jax.ShapeDtypeStruct((nrows, D_MODEL), F32),
        compiler_params=_cparams(("arbitrary",)),
        name="moe_combine",
    )(slot_ids[0:1], slot_ids[1:2], slot_ids[0:1], slot_ids[1:2], h1, wcol, ln2_g.reshape(1, -1),
      ln2_b.reshape(1, -1), ys)


PROJ_ROWS = 512
POST_ROWS = 512
DISPATCH_ROWS = 384
COMBINE_ROWS = 256


def kernel(x_prompt, x_sample, state_gla, cache_swa_k, cache_swa_v, meta_tokens, ln_emb_g, ln_emb_b, rel_bias, w_in,
           gk_up, gk_bias, gla_norm_g, sinks, w_out, ln1_g, ln1_b, w_router_group, b_router_group, w_router_expert,
           b_router_expert, w_gate, w_up, w_down, ln2_g, ln2_b):
    nbatch, seq, d = x_prompt.shape
    nseq = x_sample.shape[0]
    assert w_in.shape[0] == DEPTH == 1 and d == D_MODEL and x_sample.shape[1] == 1
    assert seq % BLK == 0 and nseq == BLK and meta_tokens.shape[0] == N_META
    nblk = seq // BLK
    n_prompt = nbatch * seq
    total = n_prompt + nseq
    kvw = ATT_KV_HEADS * HEAD_DIM

    xp = x_prompt.reshape(n_prompt, d)
    xs = x_sample.reshape(nseq, d)
    extra = jnp.concatenate([xs, jnp.zeros((BLK - N_META, d), xs.dtype), meta_tokens.astype(xs.dtype)], axis=0)
    w_cat = _prep_w_in(w_in[0])
    pp = _ln_proj(xp, ln_emb_g, ln_emb_b, w_cat, gk_up[0], gk_bias[0], PROJ_ROWS)
    pe = _ln_proj(extra, ln_emb_g, ln_emb_b, w_cat, gk_up[0], gk_bias[0], 2 * BLK)

    yg, gla_p = _gla_prompt(pp, pe, gla_norm_g[0], nbatch, nblk)
    yg_s, gla_s = _gla_step(pe, gla_norm_g[0], state_gla[0])
    mg, k_win, v_win = _swa_prompt(pp, pe, yg, rel_bias, sinks[0], nbatch, nblk)
    mg_s, k_s, v_s = _swa_step(pe, yg_s, cache_swa_k[0].reshape(nseq, WINDOW, kvw),
                               cache_swa_v[0].reshape(nseq, WINDOW, kvw), rel_bias, sinks[0])

    prm = _prep_post_params(ln_emb_g, ln_emb_b, w_out[0], ln1_g[0], ln1_b[0], w_router_group[0], b_router_group[0],
                            w_router_expert[0], b_router_expert[0])
    carry0 = jnp.zeros((N_EXPERTS, LANES), F32)
    rows_alloc = n_prompt + POST_ROWS
    h1, meta, wcol, carry1 = _post(mg, xp, prm, POST_ROWS, 0, rows_alloc, carry0, zero_tail=True)
    h1, meta, wcol, carry2 = _post(mg_s, xs, prm, nseq, n_prompt, rows_alloc, carry1, prev=(h1, meta, wcol))

    counts = carry2[:, 0].astype(jnp.int32)
    pstart, block_e, n_used, nb_max, tail_start = _moe_plan(counts, TOP_K * total)
    slot_ids = _slot_ids(meta, pstart)
    xs_sorted = _dispatch(h1, slot_ids, tail_start, nb_max * MOE_ROWS, DISPATCH_ROWS, total)
    ys = _experts(xs_sorted, block_e, n_used, w_gate[0], w_up[0], w_down[0], nb_max)
    y_p = _combine(h1, wcol, slot_ids, ys, ln2_g[0], ln2_b[0], COMBINE_ROWS, 0, n_prompt)
    y_s = _combine(h1, wcol, slot_ids, ys, ln2_g[0], ln2_b[0], nseq, n_prompt, nseq)

    kv_shape = (1, nbatch, WINDOW, ATT_KV_HEADS, HEAD_DIM)
    k_p = k_win.reshape(kv_shape)
    v_p = v_win.reshape(kv_shape)
    return (y_p.reshape(nbatch, seq, d), y_s.reshape(nseq, 1, d), gla_p[None], k_p, v_p, gla_s[None],
            k_s.reshape(cache_swa_k.shape), v_s.reshape(cache_swa_v.shape))
```

```python
import functools
import math

import jax
import jax.numpy as jnp
import numpy as np
from jax import lax
from jax.experimental import pallas as pl
from jax.experimental.pallas import tpu as pltpu

F32 = jnp.float32
BF16 = jnp.bfloat16

D_MODEL = 1024
N_META = 16
LN_EPS = 1e-5
GLA_HEADS = 4
GLA_DK = 128
GLA_DV = 256
GLA_RANK = 16
GLA_TAU = 16.0
HEAD_DIM = 64
ATT_HEADS = 16
ATT_KV_HEADS = 4
GQA_GROUP = 4
WINDOW = 128
REL_BUCKETS = 32
REL_MAX_DIST = 128
N_GROUPS = 4
EXPERTS_PER_GROUP = 8
N_EXPERTS = 32
TOP_K = 2
D_EXPERT = 512
DEPTH = 1
ALPHA = (2.0 * DEPTH) ** 0.25
LOG2E = math.log2(math.e)

LANES = 128
BLK = 128
VMEM_LIMIT = 56 * 1024 * 1024


def _cparams(sem):
    return pltpu.CompilerParams(dimension_semantics=sem, vmem_limit_bytes=VMEM_LIMIT)


def _layer_norm_rows(x, g, b):
    mu = jnp.mean(x, axis=-1, keepdims=True)
    xc = x - mu
    var = jnp.mean(xc * xc, axis=-1, keepdims=True)
    return xc * lax.rsqrt(var + LN_EPS) * g + b


_PROJ_OUTS = (
    ("lr", LANES, F32, None),
    ("qg", GLA_HEADS * GLA_DK, BF16, GLA_DK ** -0.5),
    ("kg", GLA_HEADS * GLA_DK, BF16, None),
    ("vg", GLA_HEADS * GLA_DV, BF16, None),
    ("rg", GLA_HEADS * GLA_DV, BF16, None),
    ("qa", ATT_HEADS * HEAD_DIM, BF16, HEAD_DIM ** -0.5 * LOG2E),
    ("ka", ATT_KV_HEADS * HEAD_DIM, F32, None),
    ("va", ATT_KV_HEADS * HEAD_DIM, F32, None),
    ("ga", D_MODEL, BF16, None),
    ("gb", D_MODEL, BF16, None),
)
_PROJ_W = sum(w for _, w, _, _ in _PROJ_OUTS)


def _prep_w_in(w_in):
    sizes = (512, 512, 1024, 1024, GLA_RANK, 1024, 256, 256, 1024, 1024)
    offs = np.cumsum((0,) + sizes)
    a = w_in[:, : offs[4]]
    lr = w_in[:, offs[4]: offs[5]]
    b = w_in[:, offs[5]:]
    pad = jnp.zeros((w_in.shape[0], LANES - GLA_RANK), w_in.dtype)
    return jnp.concatenate([lr, pad, a, b], axis=1).astype(BF16)


LAMAX_ROWS = 8


def _log_sigmoid(x):
    return jnp.minimum(x, 0.0) - jnp.log(1.0 + jnp.exp(-jnp.abs(x)))


def _ln_proj_body(x_ref, g_ref, b_ref, w_ref, gkup_ref, gkb_ref, *out_refs):
    la_ref, lamax_ref = out_refs[-2:]
    tm = x_ref.shape[0]
    xn = _layer_norm_rows(x_ref[...], g_ref[...], b_ref[...]).astype(BF16)
    c0 = 0
    plain_refs = iter(out_refs[:-2])
    for name, width, dtype, scale in _PROJ_OUTS:
        acc = jnp.dot(xn, w_ref[:, c0:c0 + width], preferred_element_type=F32)
        c0 += width
        if name == "lr":
            x = jnp.dot(acc.astype(BF16), gkup_ref[...], preferred_element_type=F32) + gkb_ref[...]
            la = _log_sigmoid(x) * (1.0 / GLA_TAU)
            la_ref[...] = la
            rows = []
            for r in range(tm // BLK):
                blk_max = jnp.max(jnp.max(jnp.abs(la[r * BLK:(r + 1) * BLK]), axis=0, keepdims=True), axis=1, keepdims=True)
                rows.append(jnp.broadcast_to(blk_max, (1, LANES)))
            rows.append(jnp.zeros((LAMAX_ROWS - tm // BLK, LANES), F32))
            lamax_ref[...] = jnp.concatenate(rows, axis=0)
            continue
        if scale is not None:
            acc = acc * scale
        next(plain_refs)[...] = acc.astype(dtype)


def _ln_proj(x2d, ln_g, ln_b, w_cat, gk_up, gk_bias, tm):
    m = x2d.shape[0]
    assert m % tm == 0 and tm % BLK == 0 and tm // BLK < LAMAX_ROWS
    names = [n for n, _, _, _ in _PROJ_OUTS if n != "lr"] + ["la", "lamax"]
    ladim = GLA_HEADS * GLA_DK
    gkup = jnp.concatenate([gk_up, jnp.zeros((LANES - GLA_RANK, ladim), gk_up.dtype)], axis=0).astype(BF16)
    out_shape = [jax.ShapeDtypeStruct((m, w), dt) for n, w, dt, _ in _PROJ_OUTS if n != "lr"]
    out_specs = [pl.BlockSpec((tm, w), lambda i: (i, 0)) for n, w, _, _ in _PROJ_OUTS if n != "lr"]
    out_shape += [jax.ShapeDtypeStruct((m, ladim), F32), jax.ShapeDtypeStruct((m // tm * LAMAX_ROWS, LANES), F32)]
    out_specs += [pl.BlockSpec((tm, ladim), lambda i: (i, 0)), pl.BlockSpec((LAMAX_ROWS, LANES), lambda i: (i, 0))]
    outs = pl.pallas_call(
        _ln_proj_body,
        grid=(m // tm,),
        in_specs=[
            pl.BlockSpec((tm, D_MODEL), lambda i: (i, 0)),
            pl.BlockSpec((1, D_MODEL), lambda i: (0, 0)),
            pl.BlockSpec((1, D_MODEL), lambda i: (0, 0)),
            pl.BlockSpec((D_MODEL, _PROJ_W), lambda i: (0, 0), pipeline_mode=pl.Buffered(1)),
            pl.BlockSpec((LANES, ladim), lambda i: (0, 0)),
            pl.BlockSpec((1, ladim), lambda i: (0, 0)),
        ],
        out_specs=out_specs,
        out_shape=out_shape,
        compiler_params=_cparams(("arbitrary",)),
        name="ln_proj",
    )(x2d, ln_g.reshape(1, -1), ln_b.reshape(1, -1), w_cat, gkup, gk_bias.reshape(1, -1))
    res = dict(zip(names, outs))
    res["lamax"] = res["lamax"].reshape(m // tm, LAMAX_ROWS, LANES)[:, :tm // BLK, 0].reshape(m // BLK)
    return res


_GLA_LEVELS = tuple(2 ** i for i in range(int(math.log2(BLK))))
GLA_SAFE_EXPONENT = 60.0


def _sigmoid(x):
    return 0.5 * jnp.tanh(0.5 * x) + 0.5


def _split_dot(a01, x):
    hi = x.astype(BF16)
    lo = (x - hi.astype(F32)).astype(BF16)
    n = x.shape[1]
    both = jnp.dot(a01, jnp.concatenate([hi, lo], axis=1), preferred_element_type=F32)
    return both[:, :n] + both[:, n:]


def _gla_anchor_exponent(b, la, s, row):
    if s == 1:
        return jnp.where(row % 2 == 1, la, 0.0)
    if s == 2:
        la_dn = pltpu.roll(la, 1, axis=0)
        la_up = pltpu.roll(la, BLK - 1, axis=0)
        r = row % 4
        return jnp.where(r == 0, la_up, jnp.where(r == 1, 0.0, jnp.where(r == 2, la, la + la_dn)))
    nb = BLK // (2 * s)
    b3 = b.reshape(nb, 2 * s, b.shape[-1])
    anchor = jnp.broadcast_to(b3[:, s - 1:s, :], b3.shape).reshape(b.shape)
    return -jnp.abs(b - anchor)


def _gla_body(nblk, lamax_ref, qm, km, vm, rm, lam, gam, qp, kp, vp, rp, lap, gap, gn_ref, tri_ref,
              y_ref, s_out_ref, s_ref):
    c = pl.program_id(1)
    is_meta = c == 0
    n_prompt_blocks = pl.num_programs(0) * nblk
    blk_max = lamax_ref[jnp.where(is_meta, n_prompt_blocks, pl.program_id(0) * nblk + c - 1)]

    @pl.when(is_meta)
    def _():
        s_ref[...] = jnp.zeros_like(s_ref)

    def pick(m_ref, p_ref):
        return jnp.where(is_meta, m_ref[...], p_ref[...])

    row = lax.broadcasted_iota(jnp.int32, (BLK, GLA_DK), 0)
    col_t = lax.broadcasted_iota(jnp.int32, (BLK, BLK), 1)
    row_t = lax.broadcasted_iota(jnp.int32, (BLK, BLK), 0)
    live = jnp.logical_or(jnp.logical_not(is_meta), row >= BLK - N_META)
    tri = tri_ref[...]
    q_all, k_all, v_all, r_all = pick(qm, qp), pick(km, kp), pick(vm, vp), pick(rm, rp)
    ga_all = pick(gam, gap)
    la_all = pick(lam, lap)
    nt = (((1,), (1,)), ((), ()))
    mid = BLK // 2 - 1

    def head(h, single_anchor):
        dk = slice(h * GLA_DK, (h + 1) * GLA_DK)
        dv = slice(h * GLA_DV, (h + 1) * GLA_DV)
        la = jnp.where(live, la_all[:, dk], 0.0)
        q = q_all[:, dk].astype(F32)
        k = jnp.where(live, k_all[:, dk].astype(F32), 0.0)
        v = v_all[:, dv]
        b = _split_dot(tri, la)
        yield
        b_last = b[BLK - 1:BLK, :]
        s_old = s_ref[h]
        if single_anchor:
            b_mid = b[mid:mid + 1, :]
            qe = q * jnp.exp(b - b_mid)
            ke = k * jnp.exp(b_mid - b)
            a = jnp.where(row_t >= col_t,
                          lax.dot_general(qe.astype(BF16), ke.astype(BF16), nt, preferred_element_type=F32), 0.0)
            qg = qe * jnp.exp(b_mid)
            kd = ke * jnp.exp(b_last - b_mid)
        else:
            a = jnp.where(row_t == col_t,
                          lax.dot_general(q.astype(BF16), k.astype(BF16), nt, preferred_element_type=F32), 0.0)
            for s in _GLA_LEVELS:
                e = jnp.exp(_gla_anchor_exponent(b, la, s, row))
                upper = (row // s) % 2 == 1
                q_s = jnp.where(upper, q * e, 0.0).astype(BF16)
                k_s = jnp.where(upper, 0.0, k * e).astype(BF16)
                p = lax.dot_general(q_s, k_s, nt, preferred_element_type=F32)
                a = a + jnp.where(row_t // (2 * s) == col_t // (2 * s), p, 0.0)
            qg = q * jnp.exp(b)
            kd = k * jnp.exp(b_last - b)
        yield
        o = jnp.dot(qg.astype(BF16), s_old.astype(BF16), preferred_element_type=F32)
        lhs = jnp.concatenate([jnp.transpose(kd).astype(BF16), a.astype(BF16)], axis=0)
        both = jnp.dot(lhs, v, preferred_element_type=F32)
        yield
        decay_col = jnp.transpose(jnp.broadcast_to(jnp.exp(b_last), (BLK, GLA_DK)))[:, :1]
        s_ref[h] = decay_col * s_old + both[:GLA_DK]
        o = o + both[GLA_DK:]
        o = o * lax.rsqrt(jnp.mean(o * o, axis=-1, keepdims=True) + LN_EPS) * gn_ref[...]
        r = r_all[:, dv].astype(F32)
        y = o * (r * _sigmoid(r)) * _sigmoid(ga_all[:, dv].astype(F32))
        y_ref[:, dv] = y.astype(y_ref.dtype)

    mild = blk_max * (BLK // 2) <= GLA_SAFE_EXPONENT

    def all_heads(single_anchor):
        running = [head(h, single_anchor) for h in range(GLA_HEADS)]
        while running:
            running = [g for g in running if next(g, True) is None]

    @pl.when(mild)
    def _():
        all_heads(True)

    @pl.when(jnp.logical_not(mild))
    def _():
        all_heads(False)

    @pl.when(c == nblk)
    def _():
        s_out_ref[...] = s_ref[...]


def _tri_incl():
    i = np.arange(BLK)
    return jnp.asarray((i[None, :] <= i[:, None]).astype(np.float32), dtype=BF16)


def _gla_prompt(pp, pe, gnorm, nbatch, nblk):
    names = ("qg", "kg", "vg", "rg", "la", "ga")
    lamax = jnp.concatenate([pp["lamax"], pe["lamax"][1:2]])
    m_specs = [pl.BlockSpec((BLK, pe[n].shape[1]), lambda b, c, lm: (1, 0)) for n in names]
    p_specs = [pl.BlockSpec((BLK, pp[n].shape[1]), lambda b, c, lm: (b * nblk + jnp.maximum(c - 1, 0), 0))
               for n in names]
    w_specs = [
        pl.BlockSpec((1, GLA_DV), lambda b, c, lm: (0, 0)),
        pl.BlockSpec((BLK, BLK), lambda b, c, lm: (0, 0)),
    ]
    y, s_fin = pl.pallas_call(
        functools.partial(_gla_body, nblk),
        grid_spec=pltpu.PrefetchScalarGridSpec(
            num_scalar_prefetch=1,
            grid=(nbatch, nblk + 1),
            in_specs=m_specs + p_specs + w_specs,
            out_specs=[
                pl.BlockSpec((BLK, D_MODEL), lambda b, c, lm: (b * nblk + jnp.maximum(c - 1, 0), 0)),
                pl.BlockSpec((None, GLA_HEADS, GLA_DK, GLA_DV), lambda b, c, lm: (b, 0, 0, 0)),
            ],
            scratch_shapes=[pltpu.VMEM((GLA_HEADS, GLA_DK, GLA_DV), F32)],
        ),
        out_shape=[
            jax.ShapeDtypeStruct((nbatch * nblk * BLK, D_MODEL), BF16),
            jax.ShapeDtypeStruct((nbatch, GLA_HEADS, GLA_DK, GLA_DV), F32),
        ],
        compiler_params=_cparams(("arbitrary", "arbitrary")),
        name="gla_prompt",
    )(lamax, *[pe[n] for n in names], *[pp[n] for n in names], gnorm.reshape(1, -1), _tri_incl())
    return y, s_fin


GLA_STEP_SEQS = 16


def _gla_step_body(q_ref, k_ref, v_ref, r_ref, la_ref, ga_ref, gn_ref, s_in_ref,
                   y_ref, s_out_ref, at_ref, kt_ref, qt_ref):
    g = pl.program_id(0)
    nseq = q_ref.shape[0]

    @pl.when(g == 0)
    def _():
        a = jnp.exp(la_ref[...])
        for h in range(GLA_HEADS):
            dk = slice(h * GLA_DK, (h + 1) * GLA_DK)
            at_ref[h] = jnp.transpose(a[:, dk])
            kt_ref[h] = jnp.transpose(k_ref[:, dk].astype(F32))
            qt_ref[h] = jnp.transpose(q_ref[:, dk].astype(F32))

    lane = lax.broadcasted_iota(jnp.int32, (GLA_DK, nseq), 1)
    ones = jnp.ones((nseq, GLA_DV), BF16)
    grp = pl.ds(pl.multiple_of(g * GLA_STEP_SEQS, GLA_STEP_SEQS), GLA_STEP_SEQS)
    r_grp = r_ref[grp, :].astype(F32)
    ga_grp = ga_ref[grp, :].astype(F32)
    for i in range(GLA_STEP_SEQS):
        n = g * GLA_STEP_SEQS + i
        sel = lane == n
        for h in range(GLA_HEADS):
            dv = slice(h * GLA_DV, (h + 1) * GLA_DV)
            a_sel = jnp.where(sel, at_ref[h], 0.0)
            k_sel = jnp.where(sel, kt_ref[h], 0.0).astype(BF16)
            q_sel = jnp.where(sel, qt_ref[h], 0.0).astype(BF16)
            decay = _split_dot_rhs(a_sel, ones)
            kv = jnp.dot(k_sel, v_ref[:, dv], preferred_element_type=F32)
            q_b = jnp.dot(q_sel, ones, preferred_element_type=F32)
            s_new = decay * s_in_ref[i, h] + kv
            s_out_ref[i, h] = s_new
            o = jnp.sum(q_b * s_new, axis=0, keepdims=True)
            o = o * lax.rsqrt(jnp.mean(o * o, axis=-1, keepdims=True) + LN_EPS) * gn_ref[...]
            r = r_grp[i:i + 1, dv]
            ga = ga_grp[i:i + 1, dv]
            y_ref[i:i + 1, dv] = (o * (r * _sigmoid(r)) * _sigmoid(ga)).astype(y_ref.dtype)


def _split_dot_rhs(x, b01):
    hi = x.astype(BF16)
    lo = (x - hi.astype(F32)).astype(BF16)
    return jnp.dot(hi, b01, preferred_element_type=F32) + jnp.dot(lo, b01, preferred_element_type=F32)


def _gla_step(pe, gnorm, state):
    nseq = state.shape[0]
    assert nseq == BLK and nseq % GLA_STEP_SEQS == 0
    names = ("qg", "kg", "vg", "rg", "la", "ga")
    t_specs = [pl.BlockSpec((nseq, pe[n].shape[1]), lambda g: (0, 0)) for n in names]
    st_spec = pl.BlockSpec((GLA_STEP_SEQS, GLA_HEADS, GLA_DK, GLA_DV), lambda g: (g, 0, 0, 0))
    return pl.pallas_call(
        _gla_step_body,
        grid=(nseq // GLA_STEP_SEQS,),
        in_specs=t_specs + [
            pl.BlockSpec((1, GLA_DV), lambda g: (0, 0)),
            st_spec,
        ],
        out_specs=[pl.BlockSpec((GLA_STEP_SEQS, D_MODEL), lambda g: (g, 0)), st_spec],
        out_shape=[jax.ShapeDtypeStruct((nseq, D_MODEL), F32), jax.ShapeDtypeStruct(state.shape, F32)],
        scratch_shapes=[pltpu.VMEM((GLA_HEADS, GLA_DK, nseq), F32) for _ in range(3)],
        compiler_params=_cparams(("arbitrary",)),
        name="gla_step",
    )(*[pe[n] for n in names], gnorm.reshape(1, -1), state)


HALF = LANES // 2
SWA_BLOCKS_PER_STEP = 2


def _rel_bucket(dist):
    max_exact = REL_BUCKETS // 2
    d = jnp.maximum(dist, 0)
    large = max_exact + (jnp.log(jnp.maximum(d, 1).astype(F32) / max_exact)
                         / math.log(REL_MAX_DIST / max_exact) * (REL_BUCKETS - max_exact)).astype(jnp.int32)
    large = jnp.minimum(large, REL_BUCKETS - 1)
    return jnp.where(d < max_exact, d, large)


def _bias_lookup(rel_bias, dist):
    onehot = (_rel_bucket(dist)[..., None] == jnp.arange(REL_BUCKETS)).astype(F32)
    bias = jnp.einsum("...b,bh->h...", onehot, rel_bias.astype(F32), precision=lax.Precision.HIGHEST)
    return bias * LOG2E


def _swa_bias_tables(rel_bias):
    q = jnp.arange(BLK)[:, None]
    c = jnp.arange(2 * BLK)[None, :]
    dist = BLK + q - c
    bias = _bias_lookup(rel_bias, dist)
    inside = (dist >= 0) & (dist < WINDOW)
    first = inside & (c >= BLK - N_META)
    neg = jnp.float32(-jnp.inf)
    return jnp.stack([jnp.where(first[None], bias, neg), jnp.where(inside[None], bias, neg)])


def _dup_tiles(x):
    lane = lax.broadcasted_iota(jnp.int32, (x.shape[0], LANES), 1)
    low = lane < HALF
    out = []
    for t in range(2):
        tile = x[:, t * LANES:(t + 1) * LANES]
        swapped = pltpu.roll(tile, HALF, axis=1)
        out += [jnp.where(low, tile, swapped).astype(BF16), jnp.where(low, swapped, tile).astype(BF16)]
    return out


def _group_row_heads(j):
    tiles = [j * (GQA_GROUP // 2) + pair for pair in range(GQA_GROUP // 2)]
    return [2 * t for t in tiles] + [2 * t + 1 for t in tiles]


def _swa_body(nblk, q_ref, kc_ref, vc_ref, kp_ref, vp_ref, km_ref, vm_ref, gb_ref, yg_ref, tb_ref, sink_ref,
              o_ref, kw_ref, vw_ref):
    step = pl.program_id(1)
    nsub = SWA_BLOCKS_PER_STEP
    last = slice((nsub - 1) * BLK, nsub * BLK)

    @pl.when(step == nblk // nsub - 1)
    def _():
        kw_ref[...] = kc_ref[last, :]
        vw_ref[...] = vc_ref[last, :]

    npair = GQA_GROUP // 2
    low = lax.broadcasted_iota(jnp.int32, (BLK, LANES), 1) < HALF
    ones = jnp.ones((2 * BLK, LANES), BF16)
    seg = lax.broadcasted_iota(jnp.int32, (GQA_GROUP * BLK, 1), 0) // BLK
    nt = (((1,), (1,)), ((), ()))

    def block(sub):
        rows = slice(sub * BLK, (sub + 1) * BLK)
        if sub == 0:
            k_prev = jnp.where(step == 0, km_ref[...], kp_ref[...])
            v_prev = jnp.where(step == 0, vm_ref[...], vp_ref[...])
            variant = jnp.minimum(step, 1)
        else:
            before = slice((sub - 1) * BLK, sub * BLK)
            k_prev, v_prev, variant = kc_ref[before, :], vc_ref[before, :], 1
        k_tiles = _dup_tiles(jnp.concatenate([k_prev, kc_ref[rows, :]], axis=0))
        v_tiles = _dup_tiles(jnp.concatenate([v_prev, vc_ref[rows, :]], axis=0))
        for j in range(ATT_KV_HEADS):
            tiles = [j * npair + pair for pair in range(npair)]
            q_t = [q_ref[rows, t * LANES:(t + 1) * LANES] for t in tiles]
            zero = jnp.zeros_like(q_t[0])
            q_st = jnp.concatenate([jnp.where(low, q, zero) for q in q_t] + [jnp.where(low, zero, q) for q in q_t],
                                   axis=0)
            heads = _group_row_heads(j)
            s = lax.dot_general(q_st, k_tiles[j], nt, preferred_element_type=F32)
            yield
            s = s + jnp.concatenate([tb_ref[variant, h] for h in heads], axis=0)
            sink = jnp.full((GQA_GROUP * BLK, 1), sink_ref[heads[0]], F32)
            for i in range(1, GQA_GROUP):
                sink = jnp.where(seg == i, sink_ref[heads[i]], sink)
            m = jnp.maximum(jnp.max(s, axis=-1, keepdims=True), sink)
            p = jnp.exp2(s - m).astype(BF16)
            pv = jnp.dot(p, jnp.concatenate([v_tiles[j], ones], axis=1), preferred_element_type=F32)
            yield
            o = pv[:, :LANES] / (pv[:, LANES:] + jnp.exp2(sink - m))
            for pair, t in enumerate(tiles):
                cols = slice(t * LANES, (t + 1) * LANES)
                gate = _sigmoid(gb_ref[rows, cols].astype(F32))
                even = o[pair * BLK:(pair + 1) * BLK]
                odd = o[(npair + pair) * BLK:(npair + pair + 1) * BLK]
                o_ref[rows, cols] = (gate * jnp.where(low, even, odd)
                                     + yg_ref[rows, cols].astype(F32)).astype(o_ref.dtype)

    running = [block(sub) for sub in range(nsub)]
    while running:
        running = [g for g in running if next(g, True) is None]


def _swa_prompt(pp, pe, yg, rel_bias, sinks, nbatch, nblk):
    tb = _swa_bias_tables(rel_bias)
    kvw = ATT_KV_HEADS * HEAD_DIM
    sinks2 = sinks.astype(F32) * LOG2E
    nsub = SWA_BLOCKS_PER_STEP
    assert nblk % nsub == 0
    steps = nblk // nsub
    rows = nsub * BLK
    cur = lambda b, c: (b * steps + c, 0)
    prev = lambda b, c: (b * nblk + jnp.maximum(c * nsub - 1, 0), 0)
    return pl.pallas_call(
        functools.partial(_swa_body, nblk),
        grid=(nbatch, steps),
        in_specs=[
            pl.BlockSpec((rows, D_MODEL), cur),
            pl.BlockSpec((rows, kvw), cur), pl.BlockSpec((rows, kvw), cur),
            pl.BlockSpec((BLK, kvw), prev), pl.BlockSpec((BLK, kvw), prev),
            pl.BlockSpec((BLK, kvw), lambda b, c: (1, 0)), pl.BlockSpec((BLK, kvw), lambda b, c: (1, 0)),
            pl.BlockSpec((rows, D_MODEL), cur),
            pl.BlockSpec((rows, D_MODEL), cur),
            pl.BlockSpec(tb.shape, lambda b, c: (0, 0, 0, 0)),
            pl.BlockSpec(memory_space=pltpu.SMEM),
        ],
        out_specs=[pl.BlockSpec((rows, D_MODEL), cur),
                   pl.BlockSpec((None, BLK, kvw), lambda b, c: (b, 0, 0)),
                   pl.BlockSpec((None, BLK, kvw), lambda b, c: (b, 0, 0))],
        out_shape=[jax.ShapeDtypeStruct((nbatch * nblk * BLK, D_MODEL), BF16),
                   jax.ShapeDtypeStruct((nbatch, BLK, kvw), F32), jax.ShapeDtypeStruct((nbatch, BLK, kvw), F32)],
        compiler_params=_cparams(("arbitrary", "arbitrary")),
        name="swa_prompt",
    )(pp["qa"], pp["ka"], pp["va"], pp["ka"], pp["va"], pe["ka"], pe["va"], pp["gb"], yg, tb, sinks2)


SWA_STEP_SEQS = 8
Q_TILES = ATT_HEADS // 2


def _swa_step_body(q_ref, kn_ref, vn_ref, ck_ref, cv_ref, gb_ref, yg_ref, tb_ref, sink_ref,
                   o_ref, ko_ref, vo_ref):
    row = lax.broadcasted_iota(jnp.int32, (WINDOW, ATT_KV_HEADS * HEAD_DIM), 0)
    low = lax.broadcasted_iota(jnp.int32, (Q_TILES, LANES), 1) < HALF
    mine16 = (lax.broadcasted_iota(jnp.int32, (2 * Q_TILES, LANES), 0) % Q_TILES) // (GQA_GROUP // 2)
    nt = (((1,), (1,)), ((), ()))
    seqs = range(SWA_STEP_SEQS)
    k_wide, v_wide = [], []
    for i in seqs:
        k_win = jnp.where(row == WINDOW - 1, kn_ref[i:i + 1, :], pltpu.roll(ck_ref[i], WINDOW - 1, axis=0))
        v_win = jnp.where(row == WINDOW - 1, vn_ref[i:i + 1, :], pltpu.roll(cv_ref[i], WINDOW - 1, axis=0))
        ko_ref[i] = k_win
        vo_ref[i] = v_win
        k_wide.append(jnp.concatenate(_dup_tiles(k_win), axis=1))
        v_wide.append(jnp.concatenate(_dup_tiles(v_win), axis=1))
    scores = []
    for i in seqs:
        q8 = q_ref[i]
        q16 = jnp.concatenate([jnp.where(low, q8, 0.0), jnp.where(low, 0.0, q8)], axis=0)
        q_wide = jnp.concatenate([jnp.where(mine16 == j, q16, 0.0) for j in range(ATT_KV_HEADS)], axis=1)
        scores.append(lax.dot_general(q_wide.astype(BF16), k_wide[i], nt, preferred_element_type=F32))
    probs, invs = [], []
    for i in seqs:
        s = scores[i] + tb_ref[...]
        sink = sink_ref[...]
        m = jnp.maximum(jnp.max(s, axis=-1, keepdims=True), sink)
        p = jnp.exp2(s - m)
        invs.append(1.0 / (jnp.sum(p, axis=-1, keepdims=True) + jnp.exp2(sink - m)))
        probs.append(p.astype(BF16))
    outs = [jnp.dot(probs[i], v_wide[i], preferred_element_type=F32) for i in seqs]
    for i in seqs:
        o = jnp.zeros((2 * Q_TILES, LANES), F32)
        for j in range(ATT_KV_HEADS):
            o = jnp.where(mine16 == j, outs[i][:, j * LANES:(j + 1) * LANES], o)
        o = o * invs[i]
        o = jnp.where(low, o[:Q_TILES], o[Q_TILES:])
        o_ref[i] = _sigmoid(gb_ref[i]) * o + yg_ref[i]


def _swa_step(pe, yg_s, cache_k, cache_v, rel_bias, sinks):
    nseq = cache_k.shape[0]
    kvw = ATT_KV_HEADS * HEAD_DIM
    as_tiles = lambda x: x[:nseq].astype(F32).reshape(nseq, Q_TILES, LANES)
    dist = (WINDOW - 1) - jnp.arange(WINDOW)
    bias = _bias_lookup(rel_bias, dist)
    tb = jnp.concatenate([bias[0::2], bias[1::2]], axis=0)
    sk = jnp.concatenate([sinks[0::2], sinks[1::2]])[:, None].astype(F32) * LOG2E
    g = SWA_STEP_SEQS
    tile_spec = pl.BlockSpec((g, Q_TILES, LANES), lambda s: (s, 0, 0))
    win_spec = pl.BlockSpec((g, WINDOW, kvw), lambda s: (s, 0, 0))
    new_spec = pl.BlockSpec((g, kvw), lambda s: (s, 0))
    o, ko, vo = pl.pallas_call(
        _swa_step_body,
        grid=(nseq // g,),
        in_specs=[tile_spec, new_spec, new_spec, win_spec, win_spec, tile_spec, tile_spec,
                  pl.BlockSpec(tb.shape, lambda s: (0, 0)), pl.BlockSpec(sk.shape, lambda s: (0, 0))],
        out_specs=[tile_spec, win_spec, win_spec],
        out_shape=[jax.ShapeDtypeStruct((nseq, Q_TILES, LANES), F32),
                   jax.ShapeDtypeStruct(cache_k.shape, F32), jax.ShapeDtypeStruct(cache_v.shape, F32)],
        compiler_params=_cparams(("arbitrary",)),
        name="swa_step",
    )(as_tiles(pe["qa"]), pe["ka"], pe["va"], cache_k, cache_v, as_tiles(pe["gb"]),
      yg_s.reshape(nseq, Q_TILES, LANES), tb, sk)
    return o.reshape(nseq, D_MODEL), ko, vo


ROUTER_ROWS = 40
META_ROWS = 8


def _split3_nt(a_hi, a_lo, x):
    nt = (((1,), (1,)), ((), ()))
    x_hi = x.astype(BF16)
    x_lo = (x - x_hi.astype(F32)).astype(BF16)
    return (lax.dot_general(a_hi, x_hi, nt, preferred_element_type=F32)
            + lax.dot_general(a_hi, x_lo, nt, preferred_element_type=F32)
            + lax.dot_general(a_lo, x_hi, nt, preferred_element_type=F32))


def _first_argmax_rows(v, ridx, nrows):
    vmax = jnp.max(v, axis=0, keepdims=True)
    idx = jnp.min(jnp.where(v == vmax, ridx, nrows), axis=0, keepdims=True)
    return vmax, idx


def _post_body(nsteps, *refs):
    h1_ref, meta_ref, wcol_ref = refs[-5:-2]
    i = pl.program_id(0)

    @pl.when(i < nsteps)
    def _():
        _post_tile(i, *refs)

    @pl.when(i >= nsteps)
    def _():
        h1_ref[...] = jnp.zeros_like(h1_ref)
        meta_ref[...] = jnp.zeros_like(meta_ref)
        wcol_ref[...] = jnp.zeros_like(wcol_ref)


def _post_tile(i, mg_ref, x_ref, lng_ref, lnb_ref, wo_ref, g1_ref, b1_ref, wrh_ref, wrl_ref, rb_ref, ut_ref,
               cin_ref, *rest):
    h1_ref, meta_ref, wcol_ref, cout_ref, carry_ref = rest[-5:]

    @pl.when(i == 0)
    def _():
        carry_ref[...] = cin_ref[...]

    tm = x_ref.shape[0]
    h = _layer_norm_rows(x_ref[...], lng_ref[...], lnb_ref[...])
    acc = jnp.dot(mg_ref[...].astype(BF16), wo_ref[...], preferred_element_type=F32)
    h1 = _layer_norm_rows(ALPHA * h + acc, g1_ref[...], b1_ref[...])
    h1_ref[...] = h1

    lt = _split3_nt(wrh_ref[...], wrl_ref[...], h1) + rb_ref[:, :1]
    ridx = lax.broadcasted_iota(jnp.int32, (EXPERTS_PER_GROUP, tm), 0)
    neg = jnp.float32(-jnp.inf)
    g_log = jnp.where(ridx < N_GROUPS, lt[N_EXPERTS:N_EXPERTS + EXPERTS_PER_GROUP], neg)
    g_max, grp = _first_argmax_rows(g_log, ridx, EXPERTS_PER_GROUP)
    p_grp = 1.0 / jnp.sum(jnp.exp(g_log - g_max), axis=0, keepdims=True)
    e_in = lt[0:EXPERTS_PER_GROUP]
    for gi in range(1, N_GROUPS):
        e_in = jnp.where(grp == gi, lt[gi * EXPERTS_PER_GROUP:(gi + 1) * EXPERTS_PER_GROUP], e_in)
    v0, i0 = _first_argmax_rows(e_in, ridx, EXPERTS_PER_GROUP)
    v1, i1 = _first_argmax_rows(jnp.where(ridx == i0, neg, e_in), ridx, EXPERTS_PER_GROUP)
    t = jnp.exp(v1 - v0)
    w0 = p_grp / (1.0 + t)
    w1 = p_grp * t / (1.0 + t)
    e0 = grp * EXPERTS_PER_GROUP + i0
    e1 = grp * EXPERTS_PER_GROUP + i1

    eidx = lax.broadcasted_iota(jnp.int32, (N_EXPERTS, tm), 0)
    hit0 = eidx == e0
    hit1 = eidx == e1
    oh = jnp.where(jnp.logical_or(hit0, hit1), 1.0, 0.0)
    before = jnp.dot(oh.astype(BF16), ut_ref[...], preferred_element_type=F32) + carry_ref[:, :1]
    r0 = jnp.sum(jnp.where(hit0, before, 0.0), axis=0, keepdims=True).astype(jnp.int32)
    r1 = jnp.sum(jnp.where(hit1, before, 0.0), axis=0, keepdims=True).astype(jnp.int32)
    carry_ref[...] = carry_ref[...] + jnp.sum(oh, axis=1, keepdims=True)
    cout_ref[...] = carry_ref[...]

    zi = jnp.zeros((META_ROWS - 4, tm), jnp.int32)
    meta_ref[...] = jnp.concatenate([e0, e1, r0, r1, zi], axis=0)
    wt = jnp.concatenate([w0, w1, jnp.zeros((LANES - 2, tm), F32)], axis=0)
    wcol_ref[...] = jnp.transpose(wt)


def _post(mg, x2d, prm, tm, row0, total_rows, carry_in, prev=None, zero_tail=False):
    m = x2d.shape[0]
    assert m % tm == 0 and row0 % tm == 0 and total_rows % tm == 0
    off = row0 // tm
    nsteps = m // tm
    last = nsteps - 1
    ut = jnp.asarray(np.triu(np.ones((tm, tm), np.float32), 1), dtype=BF16)
    full = lambda shape: pl.BlockSpec(shape, lambda i: (0,) * len(shape))
    in_specs = [
        pl.BlockSpec((tm, D_MODEL), lambda i: (jnp.minimum(i, last), 0)),
        pl.BlockSpec((tm, D_MODEL), lambda i: (jnp.minimum(i, last), 0)),
        full((1, D_MODEL)), full((1, D_MODEL)),
        full((D_MODEL, D_MODEL)),
        full((1, D_MODEL)), full((1, D_MODEL)),
        full((ROUTER_ROWS, D_MODEL)), full((ROUTER_ROWS, D_MODEL)), full((ROUTER_ROWS, LANES)),
        full((tm, tm)),
        full((N_EXPERTS, LANES)),
    ]
    args = [mg, x2d, prm["ln_emb_g"], prm["ln_emb_b"], prm["w_out"], prm["ln1_g"], prm["ln1_b"],
            prm["wr_hi"], prm["wr_lo"], prm["r_bias"], ut, carry_in]
    aliases = {}
    if prev is not None:
        for k, buf in enumerate(prev):
            in_specs.append(pl.BlockSpec(memory_space=pl.ANY))
            aliases[len(args)] = k
            args.append(buf)
    out_shape = [
        jax.ShapeDtypeStruct((total_rows, D_MODEL), F32),
        jax.ShapeDtypeStruct((META_ROWS, total_rows), jnp.int32),
        jax.ShapeDtypeStruct((total_rows, LANES), F32),
        jax.ShapeDtypeStruct((N_EXPERTS, LANES), F32),
    ]
    out_specs = [
        pl.BlockSpec((tm, D_MODEL), lambda i: (i + off, 0)),
        pl.BlockSpec((META_ROWS, tm), lambda i: (0, i + off)),
        pl.BlockSpec((tm, LANES), lambda i: (i + off, 0)),
        full((N_EXPERTS, LANES)),
    ]
    if prev is not None:
        assert len(prev) == 3
    return pl.pallas_call(
        functools.partial(_post_body, nsteps),
        grid=(nsteps + int(zero_tail),),
        in_specs=in_specs,
        out_specs=out_specs,
        out_shape=out_shape,
        input_output_aliases=aliases,
        scratch_shapes=[pltpu.VMEM((N_EXPERTS, LANES), F32)],
        compiler_params=_cparams(("arbitrary",)),
        name="post_attn",
    )(*args)


def _prep_post_params(ln_emb_g, ln_emb_b, w_out, ln1_g, ln1_b, w_rg, b_rg, w_re, b_re):
    row = lambda v: v.reshape(1, -1)
    wr = jnp.concatenate([w_re.T, w_rg.T, jnp.zeros((ROUTER_ROWS - N_EXPERTS - N_GROUPS, D_MODEL), F32)], axis=0)
    wr_hi = wr.astype(BF16)
    wr_lo = (wr - wr_hi.astype(F32)).astype(BF16)
    rb = jnp.concatenate([b_re, b_rg, jnp.zeros((ROUTER_ROWS - N_EXPERTS - N_GROUPS,), F32)])
    return dict(ln_emb_g=row(ln_emb_g), ln_emb_b=row(ln_emb_b), w_out=w_out.astype(BF16), ln1_g=row(ln1_g),
                ln1_b=row(ln1_b), wr_hi=wr_hi, wr_lo=wr_lo,
                r_bias=jnp.broadcast_to(rb[:, None], (ROUTER_ROWS, LANES)))


MOE_ROWS = 256
EXPERT_BLOCKS_PER_STEP = 2
assert EXPERT_BLOCKS_PER_STEP <= 2
SUBLANES = 8
assert D_MODEL == SUBLANES * LANES


def _store_rows_as_tiles(ref, x, first_row=0):
    n = x.shape[0]
    for c in range(SUBLANES):
        ref[pl.ds(first_row * SUBLANES + c, n, stride=SUBLANES), :] = x[:, c * LANES:(c + 1) * LANES]


def _load_rows_from_tiles(ref, n, first_row=0):
    return jnp.concatenate([ref[pl.ds(first_row * SUBLANES + c, n, stride=SUBLANES), :] for c in range(SUBLANES)],
                           axis=1)


def _tile_of_row(ref, r):
    return ref.at[pl.ds(pl.multiple_of(r * SUBLANES, SUBLANES), SUBLANES)]


def _moe_plan(counts, total_assign):
    nb_max = -(-total_assign // MOE_ROWS) + N_EXPERTS
    nb_max += -nb_max % EXPERT_BLOCKS_PER_STEP
    padded = (counts + MOE_ROWS - 1) // MOE_ROWS * MOE_ROWS
    pend = jnp.cumsum(padded)
    pstart = (pend - padded).astype(jnp.int32)
    block_start = jnp.arange(nb_max, dtype=jnp.int32) * MOE_ROWS
    n_ended = jnp.sum((pend[None, :] <= block_start[:, None]).astype(jnp.int32), axis=1)
    block_e = jnp.minimum(n_ended, N_EXPERTS - 1).astype(jnp.int32)
    n_used = (pend[-1] // MOE_ROWS).astype(jnp.int32).reshape(1)
    tail_start = jnp.where(padded > 0, pend - MOE_ROWS, -1)
    spare = pend[-1] + jnp.arange(N_EXPERTS + EXPERT_BLOCKS_PER_STEP, dtype=pend.dtype) * MOE_ROWS
    spare = jnp.where(spare < nb_max * MOE_ROWS, spare, -1)
    zero_blocks = jnp.concatenate([tail_start, spare]).astype(jnp.int32)
    return pstart, block_e, n_used, nb_max, zero_blocks


ROW_UNROLL = 8


def _slot_ids(meta, pstart):
    experts = meta[0:TOP_K]
    ranks = meta[TOP_K:2 * TOP_K]
    onehot = experts[..., None] == jnp.arange(N_EXPERTS, dtype=jnp.int32)
    return ranks + jnp.sum(jnp.where(onehot, pstart, 0), axis=-1)


def _for_row_groups(tm, fn, read=None):
    def group(g, c):
        t0 = pl.multiple_of(g * ROW_UNROLL, ROW_UNROLL)
        items = [(t0 + r, k) for r in range(ROW_UNROLL) for k in range(TOP_K)]
        if read is None:
            for t, k in items:
                fn(t, k)
        else:
            vals = [read(t, k) for t, k in items]
            for (t, k), v in zip(items, vals):
                fn(t, k, v)
        return c

    lax.fori_loop(0, tm // ROW_UNROLL, group, 0)


def _dispatch_body(nsteps, tail_ref, s0_ref, s1_ref, h_ref, xs_ref, pk_ref, zero_ref, sems, zsem):
    i = pl.program_id(0)
    tm = h_ref.shape[0]
    slot_refs = (s0_ref, s1_ref)
    cur = i % 2
    blk_tiles = MOE_ROWS * SUBLANES

    @pl.when(i == 0)
    def _():
        zero_ref[...] = jnp.zeros_like(zero_ref)
        for e in range(tail_ref.shape[0]):
            @pl.when(tail_ref[e] >= 0)
            def _():
                start = pl.multiple_of(tail_ref[e] * SUBLANES, blk_tiles)
                pltpu.make_async_copy(zero_ref, xs_ref.at[pl.ds(start, blk_tiles)], zsem).start()
        for e in range(tail_ref.shape[0]):
            @pl.when(tail_ref[e] >= 0)
            def _():
                pltpu.make_async_copy(zero_ref, xs_ref.at[pl.ds(0, blk_tiles)], zsem).wait()

    _store_rows_as_tiles(pk_ref.at[cur], h_ref[...])

    def send(t, k, slot):
        pltpu.make_async_copy(_tile_of_row(pk_ref.at[cur], t), _tile_of_row(xs_ref, slot),
                              sems.at[cur]).start(priority=k)

    def wait_buffer(buf):
        _for_row_groups(tm, lambda t, k: pltpu.make_async_copy(
            _tile_of_row(pk_ref.at[buf], t), _tile_of_row(xs_ref, 0), sems.at[buf]).wait())

    _for_row_groups(tm, send, read=lambda t, k: slot_refs[k][0, t])

    @pl.when(i > 0)
    def _():
        wait_buffer(1 - cur)

    @pl.when(i == nsteps - 1)
    def _():
        wait_buffer(cur)


def _dispatch(h1, slot_ids, tail_start, nslots, tm, total):
    assert total % tm == 0 and total <= h1.shape[0] and tm % ROW_UNROLL == 0
    slot_spec = pl.BlockSpec((1, tm), lambda i, tl: (0, i), memory_space=pltpu.SMEM)
    return pl.pallas_call(
        functools.partial(_dispatch_body, total // tm),
        grid_spec=pltpu.PrefetchScalarGridSpec(
            num_scalar_prefetch=1,
            grid=(total // tm,),
            in_specs=[slot_spec, slot_spec, pl.BlockSpec((tm, D_MODEL), lambda i, tl: (i, 0))],
            out_specs=pl.BlockSpec(memory_space=pl.ANY),
            scratch_shapes=[pltpu.VMEM((2, tm * SUBLANES, LANES), F32), pltpu.VMEM((MOE_ROWS * SUBLANES, LANES), F32),
                            pltpu.SemaphoreType.DMA((2,)), pltpu.SemaphoreType.DMA(())],
        ),
        out_shape=jax.ShapeDtypeStruct((nslots * SUBLANES, LANES), F32),
        compiler_params=_cparams(("arbitrary",)),
        name="moe_dispatch",
    )(tail_start, slot_ids[0:1], slot_ids[1:2], h1)


def _expert_schedule(block_e, n_used):
    nb = block_e.shape[0]
    idx = jnp.arange(nb, dtype=jnp.int32)
    first = (idx < n_used[0]) & ((idx == 0) | (block_e != jnp.roll(block_e, 1)))
    parity = (jnp.cumsum(first.astype(jnp.int32)) - 1) % 2
    pos = jnp.where(first, idx, nb)
    at_or_after = jnp.flip(lax.cummin(jnp.flip(pos)))
    nxt = jnp.concatenate([at_or_after[1:], jnp.full((1,), nb, jnp.int32)])
    nexte = jnp.where(nxt < nb, block_e[jnp.minimum(nxt, nb - 1)], -1)
    return first.astype(jnp.int32), nexte.astype(jnp.int32), parity.astype(jnp.int32)


def _expert_body(be_ref, nu_ref, first_ref, nexte_ref, par_ref, xs_ref, wg_hbm, wu_hbm, wd_hbm, ys_ref,
                 wgf_ref, wuf_ref, wdf_ref, wgb_ref, wub_ref, wdb_ref, sems):
    i = pl.program_id(0)
    hbm = (wg_hbm, wu_hbm, wd_hbm)
    stage = (wgf_ref, wuf_ref, wdf_ref)

    def weight_copies(e, buf):
        return [pltpu.make_async_copy(hbm[w].at[e], stage[w].at[buf], sems.at[buf, w]) for w in range(3)]

    def prepare(b):
        @pl.when(first_ref[b] == 1)
        def _():
            buf = par_ref[b]

            @pl.when(b == 0)
            def _():
                for c in weight_copies(be_ref[0], buf):
                    c.start()

            for c in weight_copies(be_ref[b], buf):
                c.wait()

            @pl.when(nexte_ref[b] >= 0)
            def _():
                for c in weight_copies(nexte_ref[b], 1 - buf):
                    c.start()

            wgb_ref[buf] = wgf_ref[buf].astype(BF16)
            wub_ref[buf] = wuf_ref[buf].astype(BF16)
            wdb_ref[buf] = wdf_ref[buf].astype(BF16)

    def ffn(b, sub):
        buf = par_ref[b]
        x = _load_rows_from_tiles(xs_ref, MOE_ROWS, sub * MOE_ROWS).astype(BF16)
        yield
        g = jnp.dot(x, wgb_ref[buf], preferred_element_type=F32)
        u = jnp.dot(x, wub_ref[buf], preferred_element_type=F32)
        yield
        y = jnp.dot(((g * _sigmoid(g)) * u).astype(BF16), wdb_ref[buf], preferred_element_type=F32)
        yield
        _store_rows_as_tiles(ys_ref, y, sub * MOE_ROWS)

    def run(gens):
        while gens:
            gens = [g for g in gens if next(g, True) is None]

    def zero(sub):
        ys_ref[pl.ds(sub * MOE_ROWS * SUBLANES, MOE_ROWS * SUBLANES), :] = jnp.zeros(
            (MOE_ROWS * SUBLANES, LANES), F32)

    blocks = [EXPERT_BLOCKS_PER_STEP * i + sub for sub in range(EXPERT_BLOCKS_PER_STEP)]
    for b in blocks:
        prepare(b)
    n_live = jnp.clip(nu_ref[0] - blocks[0], 0, EXPERT_BLOCKS_PER_STEP)
    for live in range(EXPERT_BLOCKS_PER_STEP + 1):
        @pl.when(n_live == live)
        def _():
            run([ffn(blocks[sub], sub) for sub in range(live)])
            for sub in range(live, EXPERT_BLOCKS_PER_STEP):
                zero(sub)


def _experts(xs, block_e, n_used, w_gate, w_up, w_down, nb_max):
    assert nb_max % EXPERT_BLOCKS_PER_STEP == 0
    first, nexte, parity = _expert_schedule(block_e, n_used)
    rows = lambda i, *_: (i, 0)
    any_spec = pl.BlockSpec(memory_space=pl.ANY)
    blk = (EXPERT_BLOCKS_PER_STEP * MOE_ROWS * SUBLANES, LANES)
    return pl.pallas_call(
        _expert_body,
        grid_spec=pltpu.PrefetchScalarGridSpec(
            num_scalar_prefetch=5,
            grid=(nb_max // EXPERT_BLOCKS_PER_STEP,),
            in_specs=[pl.BlockSpec(blk, rows), any_spec, any_spec, any_spec],
            out_specs=pl.BlockSpec(blk, rows),
            scratch_shapes=[
                pltpu.VMEM((2, D_MODEL, D_EXPERT), F32), pltpu.VMEM((2, D_MODEL, D_EXPERT), F32),
                pltpu.VMEM((2, D_EXPERT, D_MODEL), F32),
                pltpu.VMEM((2, D_MODEL, D_EXPERT), BF16), pltpu.VMEM((2, D_MODEL, D_EXPERT), BF16),
                pltpu.VMEM((2, D_EXPERT, D_MODEL), BF16),
                pltpu.SemaphoreType.DMA((2, 3)),
            ],
        ),
        out_shape=jax.ShapeDtypeStruct(xs.shape, F32),
        compiler_params=_cparams(("arbitrary",)),
        name="moe_experts",
    )(block_e, n_used, first, nexte, parity, xs, w_gate, w_up, w_down)


def _combine_body(nsteps, s0_ref, s1_ref, n0_ref, n1_ref, h_ref, w_ref, g2_ref, b2_ref, ys_ref, o_ref, buf_ref, sems):
    i = pl.program_id(0)
    tm = o_ref.shape[0]
    cur = i % 2

    def fetch(slot_refs, buf):
        _for_row_groups(tm, lambda t, k, slot: pltpu.make_async_copy(
            _tile_of_row(ys_ref, slot), _tile_of_row(buf_ref.at[buf, k], t),
            sems.at[buf]).start(priority=k), read=lambda t, k: slot_refs[k][0, t])

    @pl.when(i == 0)
    def _():
        fetch((s0_ref, s1_ref), cur)

    @pl.when(i + 1 < nsteps)
    def _():
        fetch((n0_ref, n1_ref), 1 - cur)

    _for_row_groups(tm, lambda t, k: pltpu.make_async_copy(
        _tile_of_row(ys_ref, 0), _tile_of_row(buf_ref.at[cur, k], t), sems.at[cur]).wait())
    w = w_ref[...]
    f = (w[:, 0:1] * _load_rows_from_tiles(buf_ref.at[cur, 0], tm)
         + w[:, 1:2] * _load_rows_from_tiles(buf_ref.at[cur, 1], tm))
    o_ref[...] = _layer_norm_rows(ALPHA * h_ref[...] + f, g2_ref[...], b2_ref[...])


def _combine(h1, wcol, slot_ids, ys, ln2_g, ln2_b, tm, row0, nrows):
    assert nrows % tm == 0 and row0 % tm == 0 and tm % ROW_UNROLL == 0
    off = row0 // tm
    nsteps = nrows // tm
    slot_spec = pl.BlockSpec((1, tm), lambda i: (0, i + off), memory_space=pltpu.SMEM)
    next_spec = pl.BlockSpec((1, tm), lambda i: (0, jnp.minimum(i + 1, nsteps - 1) + off), memory_space=pltpu.SMEM)
    return pl.pallas_call(
        functools.partial(_combine_body, nsteps),
        grid=(nsteps,),
        in_specs=[
            slot_spec, slot_spec, next_spec, next_spec,
            pl.BlockSpec((tm, D_MODEL), lambda i: (i + off, 0)),
            pl.BlockSpec((tm, LANES), lambda i: (i + off, 0)),
            pl.BlockSpec((1, D_MODEL), lambda i: (0, 0)),
            pl.BlockSpec((1, D_MODEL), lambda i: (0, 0)),
            pl.BlockSpec(memory_space=pl.ANY),
        ],
        out_specs=pl.BlockSpec((tm, D_MODEL), lambda i: (i, 0)),
        scratch_shapes=[pltpu.VMEM((2, TOP_K, tm * SUBLANES, LANES), F32), pltpu.SemaphoreType.DMA((2,))],
        out_shape=jax.ShapeDtypeStruct((nrows, D_MODEL), F32),
        compiler_params=_cparams(("arbitrary",)),
        name="moe_combine",
    )(slot_ids[0:1], slot_ids[1:2], slot_ids[0:1], slot_ids[1:2], h1, wcol, ln2_g.reshape(1, -1),
      ln2_b.reshape(1, -1), ys)


PROJ_ROWS = 512
POST_ROWS = 512
DISPATCH_ROWS = 384
COMBINE_ROWS = 256


def kernel(x_prompt, x_sample, state_gla, cache_swa_k, cache_swa_v, meta_tokens, ln_emb_g, ln_emb_b, rel_bias, w_in,
           gk_up, gk_bias, gla_norm_g, sinks, w_out, ln1_g, ln1_b, w_router_group, b_router_group, w_router_expert,
           b_router_expert, w_gate, w_up, w_down, ln2_g, ln2_b):
    nbatch, seq, d = x_prompt.shape
    nseq = x_sample.shape[0]
    assert w_in.shape[0] == DEPTH == 1 and d == D_MODEL and x_sample.shape[1] == 1
    assert seq % BLK == 0 and nseq == BLK and meta_tokens.shape[0] == N_META
    nblk = seq // BLK
    n_prompt = nbatch * seq
    total = n_prompt + nseq
    kvw = ATT_KV_HEADS * HEAD_DIM

    xp = x_prompt.reshape(n_prompt, d)
    xs = x_sample.reshape(nseq, d)
    extra = jnp.concatenate([xs, jnp.zeros((BLK - N_META, d), xs.dtype), meta_tokens.astype(xs.dtype)], axis=0)
    w_cat = _prep_w_in(w_in[0])
    pp = _ln_proj(xp, ln_emb_g, ln_emb_b, w_cat, gk_up[0], gk_bias[0], PROJ_ROWS)
    pe = _ln_proj(extra, ln_emb_g, ln_emb_b, w_cat, gk_up[0], gk_bias[0], 2 * BLK)

    yg, gla_p = _gla_prompt(pp, pe, gla_norm_g[0], nbatch, nblk)
    yg_s, gla_s = _gla_step(pe, gla_norm_g[0], state_gla[0])
    mg, k_win, v_win = _swa_prompt(pp, pe, yg, rel_bias, sinks[0], nbatch, nblk)
    mg_s, k_s, v_s = _swa_step(pe, yg_s, cache_swa_k[0].reshape(nseq, WINDOW, kvw),
                               cache_swa_v[0].reshape(nseq, WINDOW, kvw), rel_bias, sinks[0])

    prm = _prep_post_params(ln_emb_g, ln_emb_b, w_out[0], ln1_g[0], ln1_b[0], w_router_group[0], b_router_group[0],
                            w_router_expert[0], b_router_expert[0])
    carry0 = jnp.zeros((N_EXPERTS, LANES), F32)
    rows_alloc = n_prompt + POST_ROWS
    h1, meta, wcol, carry1 = _post(mg, xp, prm, POST_ROWS, 0, rows_alloc, carry0, zero_tail=True)
    h1, meta, wcol, carry2 = _post(mg_s, xs, prm, nseq, n_prompt, rows_alloc, carry1, prev=(h1, meta, wcol))

    counts = carry2[:, 0].astype(jnp.int32)
    pstart, block_e, n_used, nb_max, tail_start = _moe_plan(counts, TOP_K * total)
    slot_ids = _slot_ids(meta, pstart)
    xs_sorted = _dispatch(h1, slot_ids, tail_start, nb_max * MOE_ROWS, DISPATCH_ROWS, total)
    ys = _experts(xs_sorted, block_e, n_used, w_gate[0], w_up[0], w_down[0], nb_max)
    y_p = _combine(h1, wcol, slot_ids, ys, ln2_g[0], ln2_b[0], COMBINE_ROWS, 0, n_prompt)
    y_s = _combine(h1, wcol, slot_ids, ys, ln2_g[0], ln2_b[0], nseq, n_prompt, nseq)

    kv_shape = (1, nbatch, WINDOW, ATT_KV_HEADS, HEAD_DIM)
    k_p = k_win.reshape(kv_shape)
    v_p = v_win.reshape(kv_shape)
    return (y_p.reshape(nbatch, seq, d), y_s.reshape(nseq, 1, d), gla_p[None], k_p, v_p, gla_s[None],
            k_s.reshape(cache_swa_k.shape), v_s.reshape(cache_swa_v.shape))
```

```python
import functools
import math

import jax
import jax.numpy as jnp
import numpy as np
from jax import lax
from jax.experimental import pallas as pl
from jax.experimental.pallas import tpu as pltpu

F32 = jnp.float32
BF16 = jnp.bfloat16

D_MODEL = 1024
N_META = 16
LN_EPS = 1e-5
GLA_HEADS = 4
GLA_DK = 128
GLA_DV = 256
GLA_RANK = 16
GLA_TAU = 16.0
HEAD_DIM = 64
ATT_HEADS = 16
ATT_KV_HEADS = 4
GQA_GROUP = 4
WINDOW = 128
REL_BUCKETS = 32
REL_MAX_DIST = 128
N_GROUPS = 4
EXPERTS_PER_GROUP = 8
N_EXPERTS = 32
TOP_K = 2
D_EXPERT = 512
DEPTH = 1
ALPHA = (2.0 * DEPTH) ** 0.25
LOG2E = math.log2(math.e)

LANES = 128
BLK = 128
VMEM_LIMIT = 56 * 1024 * 1024


def _cparams(sem):
    return pltpu.CompilerParams(dimension_semantics=sem, vmem_limit_bytes=VMEM_LIMIT)


def _layer_norm_rows(x, g, b):
    mu = jnp.mean(x, axis=-1, keepdims=True)
    xc = x - mu
    var = jnp.mean(xc * xc, axis=-1, keepdims=True)
    return xc * lax.rsqrt(var + LN_EPS) * g + b


_PROJ_OUTS = (
    ("lr", LANES, F32, None),
    ("qg", GLA_HEADS * GLA_DK, BF16, GLA_DK ** -0.5),
    ("kg", GLA_HEADS * GLA_DK, BF16, None),
    ("vg", GLA_HEADS * GLA_DV, BF16, None),
    ("rg", GLA_HEADS * GLA_DV, BF16, None),
    ("qa", ATT_HEADS * HEAD_DIM, BF16, HEAD_DIM ** -0.5 * LOG2E),
    ("ka", ATT_KV_HEADS * HEAD_DIM, F32, None),
    ("va", ATT_KV_HEADS * HEAD_DIM, F32, None),
    ("ga", D_MODEL, BF16, None),
    ("gb", D_MODEL, BF16, None),
)
_PROJ_W = sum(w for _, w, _, _ in _PROJ_OUTS)


def _prep_w_in(w_in):
    sizes = (512, 512, 1024, 1024, GLA_RANK, 1024, 256, 256, 1024, 1024)
    offs = np.cumsum((0,) + sizes)
    a = w_in[:, : offs[4]]
    lr = w_in[:, offs[4]: offs[5]]
    b = w_in[:, offs[5]:]
    pad = jnp.zeros((w_in.shape[0], LANES - GLA_RANK), w_in.dtype)
    return jnp.concatenate([lr, pad, a, b], axis=1).astype(BF16)


LAMAX_ROWS = 8


def _log_sigmoid(x):
    return jnp.minimum(x, 0.0) - jnp.log(1.0 + jnp.exp(-jnp.abs(x)))


def _ln_proj_body(x_ref, g_ref, b_ref, w_ref, gkup_ref, gkb_ref, *out_refs):
    la_ref, lamax_ref = out_refs[-2:]
    tm = x_ref.shape[0]
    xn = _layer_norm_rows(x_ref[...], g_ref[...], b_ref[...]).astype(BF16)
    c0 = 0
    plain_refs = iter(out_refs[:-2])
    for name, width, dtype, scale in _PROJ_OUTS:
        acc = jnp.dot(xn, w_ref[:, c0:c0 + width], preferred_element_type=F32)
        c0 += width
        if name == "lr":
            x = jnp.dot(acc.astype(BF16), gkup_ref[...], preferred_element_type=F32) + gkb_ref[...]
            la = _log_sigmoid(x) * (1.0 / GLA_TAU)
            la_ref[...] = la
            rows = []
            for r in range(tm // BLK):
                blk_max = jnp.max(jnp.max(jnp.abs(la[r * BLK:(r + 1) * BLK]), axis=0, keepdims=True), axis=1, keepdims=True)
                rows.append(jnp.broadcast_to(blk_max, (1, LANES)))
            rows.append(jnp.zeros((LAMAX_ROWS - tm // BLK, LANES), F32))
            lamax_ref[...] = jnp.concatenate(rows, axis=0)
            continue
        if scale is not None:
            acc = acc * scale
        next(plain_refs)[...] = acc.astype(dtype)


def _ln_proj(x2d, ln_g, ln_b, w_cat, gk_up, gk_bias, tm):
    m = x2d.shape[0]
    assert m % tm == 0 and tm % BLK == 0 and tm // BLK < LAMAX_ROWS
    names = [n for n, _, _, _ in _PROJ_OUTS if n != "lr"] + ["la", "lamax"]
    ladim = GLA_HEADS * GLA_DK
    gkup = jnp.concatenate([gk_up, jnp.zeros((LANES - GLA_RANK, ladim), gk_up.dtype)], axis=0).astype(BF16)
    out_shape = [jax.ShapeDtypeStruct((m, w), dt) for n, w, dt, _ in _PROJ_OUTS if n != "lr"]
    out_specs = [pl.BlockSpec((tm, w), lambda i: (i, 0)) for n, w, _, _ in _PROJ_OUTS if n != "lr"]
    out_shape += [jax.ShapeDtypeStruct((m, ladim), F32), jax.ShapeDtypeStruct((m // tm * LAMAX_ROWS, LANES), F32)]
    out_specs += [pl.BlockSpec((tm, ladim), lambda i: (i, 0)), pl.BlockSpec((LAMAX_ROWS, LANES), lambda i: (i, 0))]
    outs = pl.pallas_call(
        _ln_proj_body,
        grid=(m // tm,),
        in_specs=[
            pl.BlockSpec((tm, D_MODEL), lambda i: (i, 0)),
            pl.BlockSpec((1, D_MODEL), lambda i: (0, 0)),
            pl.BlockSpec((1, D_MODEL), lambda i: (0, 0)),
            pl.BlockSpec((D_MODEL, _PROJ_W), lambda i: (0, 0), pipeline_mode=pl.Buffered(1)),
            pl.BlockSpec((LANES, ladim), lambda i: (0, 0)),
            pl.BlockSpec((1, ladim), lambda i: (0, 0)),
        ],
        out_specs=out_specs,
        out_shape=out_shape,
        compiler_params=_cparams(("arbitrary",)),
        name="ln_proj",
    )(x2d, ln_g.reshape(1, -1), ln_b.reshape(1, -1), w_cat, gkup, gk_bias.reshape(1, -1))
    res = dict(zip(names, outs))
    res["lamax"] = res["lamax"].reshape(m // tm, LAMAX_ROWS, LANES)[:, :tm // BLK, 0].reshape(m // BLK)
    return res


_GLA_LEVELS = tuple(2 ** i for i in range(int(math.log2(BLK))))
GLA_SAFE_EXPONENT = 60.0


def _sigmoid(x):
    return 0.5 * jnp.tanh(0.5 * x) + 0.5


def _split_dot(a01, x):
    hi = x.astype(BF16)
    lo = (x - hi.astype(F32)).astype(BF16)
    n = x.shape[1]
    both = jnp.dot(a01, jnp.concatenate([hi, lo], axis=1), preferred_element_type=F32)
    return both[:, :n] + both[:, n:]


def _gla_anchor_exponent(b, la, s, row):
    if s == 1:
        return jnp.where(row % 2 == 1, la, 0.0)
    if s == 2:
        la_dn = pltpu.roll(la, 1, axis=0)
        la_up = pltpu.roll(la, BLK - 1, axis=0)
        r = row % 4
        return jnp.where(r == 0, la_up, jnp.where(r == 1, 0.0, jnp.where(r == 2, la, la + la_dn)))
    nb = BLK // (2 * s)
    b3 = b.reshape(nb, 2 * s, b.shape[-1])
    anchor = jnp.broadcast_to(b3[:, s - 1:s, :], b3.shape).reshape(b.shape)
    return -jnp.abs(b - anchor)


def _gla_body(nblk, lamax_ref, qm, km, vm, rm, lam, gam, qp, kp, vp, rp, lap, gap, gn_ref, tri_ref,
              y_ref, s_out_ref, s_ref):
    c = pl.program_id(1)
    is_meta = c == 0
    n_prompt_blocks = pl.num_programs(0) * nblk
    blk_max = lamax_ref[jnp.where(is_meta, n_prompt_blocks, pl.program_id(0) * nblk + c - 1)]

    @pl.when(is_meta)
    def _():
        s_ref[...] = jnp.zeros_like(s_ref)

    def pick(m_ref, p_ref):
        return jnp.where(is_meta, m_ref[...], p_ref[...])

    row = lax.broadcasted_iota(jnp.int32, (BLK, GLA_DK), 0)
    col_t = lax.broadcasted_iota(jnp.int32, (BLK, BLK), 1)
    row_t = lax.broadcasted_iota(jnp.int32, (BLK, BLK), 0)
    live = jnp.logical_or(jnp.logical_not(is_meta), row >= BLK - N_META)
    tri = tri_ref[...]
    q_all, k_all, v_all, r_all = pick(qm, qp), pick(km, kp), pick(vm, vp), pick(rm, rp)
    ga_all = pick(gam, gap)
    la_all = pick(lam, lap)
    nt = (((1,), (1,)), ((), ()))
    mid = BLK // 2 - 1

    def head(h, single_anchor):
        dk = slice(h * GLA_DK, (h + 1) * GLA_DK)
        dv = slice(h * GLA_DV, (h + 1) * GLA_DV)
        la = jnp.where(live, la_all[:, dk], 0.0)
        q = q_all[:, dk].astype(F32)
        k = jnp.where(live, k_all[:, dk].astype(F32), 0.0)
        v = v_all[:, dv]
        b = _split_dot(tri, la)
        yield
        b_last = b[BLK - 1:BLK, :]
        s_old = s_ref[h]
        if single_anchor:
            b_mid = b[mid:mid + 1, :]
            qe = q * jnp.exp(b - b_mid)
            ke = k * jnp.exp(b_mid - b)
            a = jnp.where(row_t >= col_t,
                          lax.dot_general(qe.astype(BF16), ke.astype(BF16), nt, preferred_element_type=F32), 0.0)
            qg = qe * jnp.exp(b_mid)
            kd = ke * jnp.exp(b_last - b_mid)
        else:
            a = jnp.where(row_t == col_t,
                          lax.dot_general(q.astype(BF16), k.astype(BF16), nt, preferred_element_type=F32), 0.0)
            for s in _GLA_LEVELS:
                e = jnp.exp(_gla_anchor_exponent(b, la, s, row))
                upper = (row // s) % 2 == 1
                q_s = jnp.where(upper, q * e, 0.0).astype(BF16)
                k_s = jnp.where(upper, 0.0, k * e).astype(BF16)
                p = lax.dot_general(q_s, k_s, nt, preferred_element_type=F32)
                a = a + jnp.where(row_t // (2 * s) == col_t // (2 * s), p, 0.0)
            qg = q * jnp.exp(b)
            kd = k * jnp.exp(b_last - b)
        yield
        o = jnp.dot(qg.astype(BF16), s_old.astype(BF16), preferred_element_type=F32)
        lhs = jnp.concatenate([jnp.transpose(kd).astype(BF16), a.astype(BF16)], axis=0)
        both = jnp.dot(lhs, v, preferred_element_type=F32)
        yield
        decay_col = jnp.transpose(jnp.broadcast_to(jnp.exp(b_last), (BLK, GLA_DK)))[:, :1]
        s_ref[h] = decay_col * s_old + both[:GLA_DK]
        o = o + both[GLA_DK:]
        o = o * lax.rsqrt(jnp.mean(o * o, axis=-1, keepdims=True) + LN_EPS) * gn_ref[...]
        r = r_all[:, dv].astype(F32)
        y = o * (r * _sigmoid(r)) * _sigmoid(ga_all[:, dv].astype(F32))
        y_ref[:, dv] = y.astype(y_ref.dtype)

    mild = blk_max * (BLK // 2) <= GLA_SAFE_EXPONENT

    def all_heads(single_anchor):
        running = [head(h, single_anchor) for h in range(GLA_HEADS)]
        while running:
            running = [g for g in running if next(g, True) is None]

    @pl.when(mild)
    def _():
        all_heads(True)

    @pl.when(jnp.logical_not(mild))
    def _():
        all_heads(False)

    @pl.when(c == nblk)
    def _():
        s_out_ref[...] = s_ref[...]


def _tri_incl():
    i = np.arange(BLK)
    return jnp.asarray((i[None, :] <= i[:, None]).astype(np.float32), dtype=BF16)


def _gla_prompt(pp, pe, gnorm, nbatch, nblk):
    names = ("qg", "kg", "vg", "rg", "la", "ga")
    lamax = jnp.concatenate([pp["lamax"], pe["lamax"][1:2]])
    m_specs = [pl.BlockSpec((BLK, pe[n].shape[1]), lambda b, c, lm: (1, 0)) for n in names]
    p_specs = [pl.BlockSpec((BLK, pp[n].shape[1]), lambda b, c, lm: (b * nblk + jnp.maximum(c - 1, 0), 0))
               for n in names]
    w_specs = [
        pl.BlockSpec((1, GLA_DV), lambda b, c, lm: (0, 0)),
        pl.BlockSpec((BLK, BLK), lambda b, c, lm: (0, 0)),
    ]
    y, s_fin = pl.pallas_call(
        functools.partial(_gla_body, nblk),
        grid_spec=pltpu.PrefetchScalarGridSpec(
            num_scalar_prefetch=1,
            grid=(nbatch, nblk + 1),
            in_specs=m_specs + p_specs + w_specs,
            out_specs=[
                pl.BlockSpec((BLK, D_MODEL), lambda b, c, lm: (b * nblk + jnp.maximum(c - 1, 0), 0)),
                pl.BlockSpec((None, GLA_HEADS, GLA_DK, GLA_DV), lambda b, c, lm: (b, 0, 0, 0)),
            ],
            scratch_shapes=[pltpu.VMEM((GLA_HEADS, GLA_DK, GLA_DV), F32)],
        ),
        out_shape=[
            jax.ShapeDtypeStruct((nbatch * nblk * BLK, D_MODEL), BF16),
            jax.ShapeDtypeStruct((nbatch, GLA_HEADS, GLA_DK, GLA_DV), F32),
        ],
        compiler_params=_cparams(("arbitrary", "arbitrary")),
        name="gla_prompt",
    )(lamax, *[pe[n] for n in names], *[pp[n] for n in names], gnorm.reshape(1, -1), _tri_incl())
    return y, s_fin


GLA_STEP_SEQS = 16


def _gla_step_body(q_ref, k_ref, v_ref, r_ref, la_ref, ga_ref, gn_ref, s_in_ref,
                   y_ref, s_out_ref, at_ref, kt_ref, qt_ref):
    g = pl.program_id(0)
    nseq = q_ref.shape[0]

    @pl.when(g == 0)
    def _():
        a = jnp.exp(la_ref[...])
        for h in range(GLA_HEADS):
            dk = slice(h * GLA_DK, (h + 1) * GLA_DK)
            at_ref[h] = jnp.transpose(a[:, dk])
            kt_ref[h] = jnp.transpose(k_ref[:, dk].astype(F32))
            qt_ref[h] = jnp.transpose(q_ref[:, dk].astype(F32))

    lane = lax.broadcasted_iota(jnp.int32, (GLA_DK, nseq), 1)
    ones = jnp.ones((nseq, GLA_DV), BF16)
    grp = pl.ds(pl.multiple_of(g * GLA_STEP_SEQS, GLA_STEP_SEQS), GLA_STEP_SEQS)
    r_grp = r_ref[grp, :].astype(F32)
    ga_grp = ga_ref[grp, :].astype(F32)
    for i in range(GLA_STEP_SEQS):
        n = g * GLA_STEP_SEQS + i
        sel = lane == n
        for h in range(GLA_HEADS):
            dv = slice(h * GLA_DV, (h + 1) * GLA_DV)
            a_sel = jnp.where(sel, at_ref[h], 0.0)
            k_sel = jnp.where(sel, kt_ref[h], 0.0).astype(BF16)
            q_sel = jnp.where(sel, qt_ref[h], 0.0).astype(BF16)
            decay = _split_dot_rhs(a_sel, ones)
            kv = jnp.dot(k_sel, v_ref[:, dv], preferred_element_type=F32)
            q_b = jnp.dot(q_sel, ones, preferred_element_type=F32)
            s_new = decay * s_in_ref[i, h] + kv
            s_out_ref[i, h] = s_new
            o = jnp.sum(q_b * s_new, axis=0, keepdims=True)
            o = o * lax.rsqrt(jnp.mean(o * o, axis=-1, keepdims=True) + LN_EPS) * gn_ref[...]
            r = r_grp[i:i + 1, dv]
            ga = ga_grp[i:i + 1, dv]
            y_ref[i:i + 1, dv] = (o * (r * _sigmoid(r)) * _sigmoid(ga)).astype(y_ref.dtype)


def _split_dot_rhs(x, b01):
    hi = x.astype(BF16)
    lo = (x - hi.astype(F32)).astype(BF16)
    return jnp.dot(hi, b01, preferred_element_type=F32) + jnp.dot(lo, b01, preferred_element_type=F32)


def _gla_step(pe, gnorm, state):
    nseq = state.shape[0]
    assert nseq == BLK and nseq % GLA_STEP_SEQS == 0
    names = ("qg", "kg", "vg", "rg", "la", "ga")
    t_specs = [pl.BlockSpec((nseq, pe[n].shape[1]), lambda g: (0, 0)) for n in names]
    st_spec = pl.BlockSpec((GLA_STEP_SEQS, GLA_HEADS, GLA_DK, GLA_DV), lambda g: (g, 0, 0, 0))
    return pl.pallas_call(
        _gla_step_body,
        grid=(nseq // GLA_STEP_SEQS,),
        in_specs=t_specs + [
            pl.BlockSpec((1, GLA_DV), lambda g: (0, 0)),
            st_spec,
        ],
        out_specs=[pl.BlockSpec((GLA_STEP_SEQS, D_MODEL), lambda g: (g, 0)), st_spec],
        out_shape=[jax.ShapeDtypeStruct((nseq, D_MODEL), F32), jax.ShapeDtypeStruct(state.shape, F32)],
        scratch_shapes=[pltpu.VMEM((GLA_HEADS, GLA_DK, nseq), F32) for _ in range(3)],
        compiler_params=_cparams(("arbitrary",)),
        name="gla_step",
    )(*[pe[n] for n in names], gnorm.reshape(1, -1), state)


HALF = LANES // 2
SWA_BLOCKS_PER_STEP = 4


def _rel_bucket(dist):
    max_exact = REL_BUCKETS // 2
    d = jnp.maximum(dist, 0)
    large = max_exact + (jnp.log(jnp.maximum(d, 1).astype(F32) / max_exact)
                         / math.log(REL_MAX_DIST / max_exact) * (REL_BUCKETS - max_exact)).astype(jnp.int32)
    large = jnp.minimum(large, REL_BUCKETS - 1)
    return jnp.where(d < max_exact, d, large)


def _bias_lookup(rel_bias, dist):
    onehot = (_rel_bucket(dist)[..., None] == jnp.arange(REL_BUCKETS)).astype(F32)
    bias = jnp.einsum("...b,bh->h...", onehot, rel_bias.astype(F32), precision=lax.Precision.HIGHEST)
    return bias * LOG2E


def _swa_bias_tables(rel_bias):
    q = jnp.arange(BLK)[:, None]
    c = jnp.arange(2 * BLK)[None, :]
    dist = BLK + q - c
    bias = _bias_lookup(rel_bias, dist)
    inside = (dist >= 0) & (dist < WINDOW)
    first = inside & (c >= BLK - N_META)
    neg = jnp.float32(-jnp.inf)
    return jnp.stack([jnp.where(first[None], bias, neg), jnp.where(inside[None], bias, neg)])


def _dup_tiles(x):
    lane = lax.broadcasted_iota(jnp.int32, (x.shape[0], LANES), 1)
    low = lane < HALF
    out = []
    for t in range(2):
        tile = x[:, t * LANES:(t + 1) * LANES]
        swapped = pltpu.roll(tile, HALF, axis=1)
        out += [jnp.where(low, tile, swapped).astype(BF16), jnp.where(low, swapped, tile).astype(BF16)]
    return out


def _group_row_heads(j):
    tiles = [j * (GQA_GROUP // 2) + pair for pair in range(GQA_GROUP // 2)]
    return [2 * t for t in tiles] + [2 * t + 1 for t in tiles]


def _swa_body(nblk, q_ref, kc_ref, vc_ref, kp_ref, vp_ref, km_ref, vm_ref, gb_ref, yg_ref, tb_ref, sink_ref,
              o_ref, kw_ref, vw_ref):
    step = pl.program_id(1)
    nsub = SWA_BLOCKS_PER_STEP
    last = slice((nsub - 1) * BLK, nsub * BLK)

    @pl.when(step == nblk // nsub - 1)
    def _():
        kw_ref[...] = kc_ref[last, :]
        vw_ref[...] = vc_ref[last, :]

    npair = GQA_GROUP // 2
    low = lax.broadcasted_iota(jnp.int32, (BLK, LANES), 1) < HALF
    ones = jnp.ones((2 * BLK, LANES), BF16)
    seg = lax.broadcasted_iota(jnp.int32, (GQA_GROUP * BLK, 1), 0) // BLK
    nt = (((1,), (1,)), ((), ()))

    def block(sub):
        rows = slice(sub * BLK, (sub + 1) * BLK)
        if sub == 0:
            k_prev = jnp.where(step == 0, km_ref[...], kp_ref[...])
            v_prev = jnp.where(step == 0, vm_ref[...], vp_ref[...])
            variant = jnp.minimum(step, 1)
        else:
            before = slice((sub - 1) * BLK, sub * BLK)
            k_prev, v_prev, variant = kc_ref[before, :], vc_ref[before, :], 1
        k_tiles = _dup_tiles(jnp.concatenate([k_prev, kc_ref[rows, :]], axis=0))
        v_tiles = _dup_tiles(jnp.concatenate([v_prev, vc_ref[rows, :]], axis=0))
        for j in range(ATT_KV_HEADS):
            tiles = [j * npair + pair for pair in range(npair)]
            q_t = [q_ref[rows, t * LANES:(t + 1) * LANES] for t in tiles]
            zero = jnp.zeros_like(q_t[0])
            q_st = jnp.concatenate([jnp.where(low, q, zero) for q in q_t] + [jnp.where(low, zero, q) for q in q_t],
                                   axis=0)
            heads = _group_row_heads(j)
            s = lax.dot_general(q_st, k_tiles[j], nt, preferred_element_type=F32)
            yield
            s = s + jnp.concatenate([tb_ref[variant, h] for h in heads], axis=0)
            sink = jnp.full((GQA_GROUP * BLK, 1), sink_ref[heads[0]], F32)
            for i in range(1, GQA_GROUP):
                sink = jnp.where(seg == i, sink_ref[heads[i]], sink)
            m = jnp.maximum(jnp.max(s, axis=-1, keepdims=True), sink)
            p = jnp.exp2(s - m).astype(BF16)
            pv = jnp.dot(p, jnp.concatenate([v_tiles[j], ones], axis=1), preferred_element_type=F32)
            yield
            o = pv[:, :LANES] / (pv[:, LANES:] + jnp.exp2(sink - m))
            for pair, t in enumerate(tiles):
                cols = slice(t * LANES, (t + 1) * LANES)
                gate = _sigmoid(gb_ref[rows, cols].astype(F32))
                even = o[pair * BLK:(pair + 1) * BLK]
                odd = o[(npair + pair) * BLK:(npair + pair + 1) * BLK]
                o_ref[rows, cols] = (gate * jnp.where(low, even, odd)
                                     + yg_ref[rows, cols].astype(F32)).astype(o_ref.dtype)

    running = [block(sub) for sub in range(nsub)]
    while running:
        running = [g for g in running if next(g, True) is None]


def _swa_prompt(pp, pe, yg, rel_bias, sinks, nbatch, nblk):
    tb = _swa_bias_tables(rel_bias)
    kvw = ATT_KV_HEADS * HEAD_DIM
    sinks2 = sinks.astype(F32) * LOG2E
    nsub = SWA_BLOCKS_PER_STEP
    assert nblk % nsub == 0
    steps = nblk // nsub
    rows = nsub * BLK
    cur = lambda b, c: (b * steps + c, 0)
    prev = lambda b, c: (b * nblk + jnp.maximum(c * nsub - 1, 0), 0)
    return pl.pallas_call(
        functools.partial(_swa_body, nblk),
        grid=(nbatch, steps),
        in_specs=[
            pl.BlockSpec((rows, D_MODEL), cur),
            pl.BlockSpec((rows, kvw), cur), pl.BlockSpec((rows, kvw), cur),
            pl.BlockSpec((BLK, kvw), prev), pl.BlockSpec((BLK, kvw), prev),
            pl.BlockSpec((BLK, kvw), lambda b, c: (1, 0)), pl.BlockSpec((BLK, kvw), lambda b, c: (1, 0)),
            pl.BlockSpec((rows, D_MODEL), cur),
            pl.BlockSpec((rows, D_MODEL), cur),
            pl.BlockSpec(tb.shape, lambda b, c: (0, 0, 0, 0)),
            pl.BlockSpec(memory_space=pltpu.SMEM),
        ],
        out_specs=[pl.BlockSpec((rows, D_MODEL), cur),
                   pl.BlockSpec((None, BLK, kvw), lambda b, c: (b, 0, 0)),
                   pl.BlockSpec((None, BLK, kvw), lambda b, c: (b, 0, 0))],
        out_shape=[jax.ShapeDtypeStruct((nbatch * nblk * BLK, D_MODEL), BF16),
                   jax.ShapeDtypeStruct((nbatch, BLK, kvw), F32), jax.ShapeDtypeStruct((nbatch, BLK, kvw), F32)],
        compiler_params=_cparams(("arbitrary", "arbitrary")),
        name="swa_prompt",
    )(pp["qa"], pp["ka"], pp["va"], pp["ka"], pp["va"], pe["ka"], pe["va"], pp["gb"], yg, tb, sinks2)


SWA_STEP_SEQS = 8
Q_TILES = ATT_HEADS // 2


def _swa_step_body(q_ref, kn_ref, vn_ref, ck_ref, cv_ref, gb_ref, yg_ref, tb_ref, sink_ref,
                   o_ref, ko_ref, vo_ref):
    row = lax.broadcasted_iota(jnp.int32, (WINDOW, ATT_KV_HEADS * HEAD_DIM), 0)
    low = lax.broadcasted_iota(jnp.int32, (Q_TILES, LANES), 1) < HALF
    mine16 = (lax.broadcasted_iota(jnp.int32, (2 * Q_TILES, LANES), 0) % Q_TILES) // (GQA_GROUP // 2)
    nt = (((1,), (1,)), ((), ()))
    seqs = range(SWA_STEP_SEQS)
    k_wide, v_wide = [], []
    for i in seqs:
        k_win = jnp.where(row == WINDOW - 1, kn_ref[i:i + 1, :], pltpu.roll(ck_ref[i], WINDOW - 1, axis=0))
        v_win = jnp.where(row == WINDOW - 1, vn_ref[i:i + 1, :], pltpu.roll(cv_ref[i], WINDOW - 1, axis=0))
        ko_ref[i] = k_win
        vo_ref[i] = v_win
        k_wide.append(jnp.concatenate(_dup_tiles(k_win), axis=1))
        v_wide.append(jnp.concatenate(_dup_tiles(v_win), axis=1))
    scores = []
    for i in seqs:
        q8 = q_ref[i]
        q16 = jnp.concatenate([jnp.where(low, q8, 0.0), jnp.where(low, 0.0, q8)], axis=0)
        q_wide = jnp.concatenate([jnp.where(mine16 == j, q16, 0.0) for j in range(ATT_KV_HEADS)], axis=1)
        scores.append(lax.dot_general(q_wide.astype(BF16), k_wide[i], nt, preferred_element_type=F32))
    probs, invs = [], []
    for i in seqs:
        s = scores[i] + tb_ref[...]
        sink = sink_ref[...]
        m = jnp.maximum(jnp.max(s, axis=-1, keepdims=True), sink)
        p = jnp.exp2(s - m)
        invs.append(1.0 / (jnp.sum(p, axis=-1, keepdims=True) + jnp.exp2(sink - m)))
        probs.append(p.astype(BF16))
    outs = [jnp.dot(probs[i], v_wide[i], preferred_element_type=F32) for i in seqs]
    for i in seqs:
        o = jnp.zeros((2 * Q_TILES, LANES), F32)
        for j in range(ATT_KV_HEADS):
            o = jnp.where(mine16 == j, outs[i][:, j * LANES:(j + 1) * LANES], o)
        o = o * invs[i]
        o = jnp.where(low, o[:Q_TILES], o[Q_TILES:])
        o_ref[i] = _sigmoid(gb_ref[i]) * o + yg_ref[i]


def _swa_step(pe, yg_s, cache_k, cache_v, rel_bias, sinks):
    nseq = cache_k.shape[0]
    kvw = ATT_KV_HEADS * HEAD_DIM
    as_tiles = lambda x: x[:nseq].astype(F32).reshape(nseq, Q_TILES, LANES)
    dist = (WINDOW - 1) - jnp.arange(WINDOW)
    bias = _bias_lookup(rel_bias, dist)
    tb = jnp.concatenate([bias[0::2], bias[1::2]], axis=0)
    sk = jnp.concatenate([sinks[0::2], sinks[1::2]])[:, None].astype(F32) * LOG2E
    g = SWA_STEP_SEQS
    tile_spec = pl.BlockSpec((g, Q_TILES, LANES), lambda s: (s, 0, 0))
    win_spec = pl.BlockSpec((g, WINDOW, kvw), lambda s: (s, 0, 0))
    new_spec = pl.BlockSpec((g, kvw), lambda s: (s, 0))
    o, ko, vo = pl.pallas_call(
        _swa_step_body,
        grid=(nseq // g,),
        in_specs=[tile_spec, new_spec, new_spec, win_spec, win_spec, tile_spec, tile_spec,
                  pl.BlockSpec(tb.shape, lambda s: (0, 0)), pl.BlockSpec(sk.shape, lambda s: (0, 0))],
        out_specs=[tile_spec, win_spec, win_spec],
        out_shape=[jax.ShapeDtypeStruct((nseq, Q_TILES, LANES), F32),
                   jax.ShapeDtypeStruct(cache_k.shape, F32), jax.ShapeDtypeStruct(cache_v.shape, F32)],
        compiler_params=_cparams(("arbitrary",)),
        name="swa_step",
    )(as_tiles(pe["qa"]), pe["ka"], pe["va"], cache_k, cache_v, as_tiles(pe["gb"]),
      yg_s.reshape(nseq, Q_TILES, LANES), tb, sk)
    return o.reshape(nseq, D_MODEL), ko, vo


ROUTER_ROWS = 40
META_ROWS = 8


def _split3_nt(a_hi, a_lo, x):
    nt = (((1,), (1,)), ((), ()))
    x_hi = x.astype(BF16)
    x_lo = (x - x_hi.astype(F32)).astype(BF16)
    return (lax.dot_general(a_hi, x_hi, nt, preferred_element_type=F32)
            + lax.dot_general(a_hi, x_lo, nt, preferred_element_type=F32)
            + lax.dot_general(a_lo, x_hi, nt, preferred_element_type=F32))


def _first_argmax_rows(v, ridx, nrows):
    vmax = jnp.max(v, axis=0, keepdims=True)
    idx = jnp.min(jnp.where(v == vmax, ridx, nrows), axis=0, keepdims=True)
    return vmax, idx


def _post_body(nsteps, *refs):
    h1_ref, meta_ref, wcol_ref = refs[-5:-2]
    i = pl.program_id(0)

    @pl.when(i < nsteps)
    def _():
        _post_tile(i, *refs)

    @pl.when(i >= nsteps)
    def _():
        h1_ref[...] = jnp.zeros_like(h1_ref)
        meta_ref[...] = jnp.zeros_like(meta_ref)
        wcol_ref[...] = jnp.zeros_like(wcol_ref)


def _post_tile(i, mg_ref, x_ref, lng_ref, lnb_ref, wo_ref, g1_ref, b1_ref, wrh_ref, wrl_ref, rb_ref, ut_ref,
               cin_ref, *rest):
    h1_ref, meta_ref, wcol_ref, cout_ref, carry_ref = rest[-5:]

    @pl.when(i == 0)
    def _():
        carry_ref[...] = cin_ref[...]

    tm = x_ref.shape[0]
    h = _layer_norm_rows(x_ref[...], lng_ref[...], lnb_ref[...])
    acc = jnp.dot(mg_ref[...].astype(BF16), wo_ref[...], preferred_element_type=F32)
    h1 = _layer_norm_rows(ALPHA * h + acc, g1_ref[...], b1_ref[...])
    h1_ref[...] = h1

    lt = _split3_nt(wrh_ref[...], wrl_ref[...], h1) + rb_ref[:, :1]
    ridx = lax.broadcasted_iota(jnp.int32, (EXPERTS_PER_GROUP, tm), 0)
    neg = jnp.float32(-jnp.inf)
    g_log = jnp.where(ridx < N_GROUPS, lt[N_EXPERTS:N_EXPERTS + EXPERTS_PER_GROUP], neg)
    g_max, grp = _first_argmax_rows(g_log, ridx, EXPERTS_PER_GROUP)
    p_grp = 1.0 / jnp.sum(jnp.exp(g_log - g_max), axis=0, keepdims=True)
    e_in = lt[0:EXPERTS_PER_GROUP]
    for gi in range(1, N_GROUPS):
        e_in = jnp.where(grp == gi, lt[gi * EXPERTS_PER_GROUP:(gi + 1) * EXPERTS_PER_GROUP], e_in)
    v0, i0 = _first_argmax_rows(e_in, ridx, EXPERTS_PER_GROUP)
    v1, i1 = _first_argmax_rows(jnp.where(ridx == i0, neg, e_in), ridx, EXPERTS_PER_GROUP)
    t = jnp.exp(v1 - v0)
    w0 = p_grp / (1.0 + t)
    w1 = p_grp * t / (1.0 + t)
    e0 = grp * EXPERTS_PER_GROUP + i0
    e1 = grp * EXPERTS_PER_GROUP + i1

    eidx = lax.broadcasted_iota(jnp.int32, (N_EXPERTS, tm), 0)
    hit0 = eidx == e0
    hit1 = eidx == e1
    oh = jnp.where(jnp.logical_or(hit0, hit1), 1.0, 0.0)
    before = jnp.dot(oh.astype(BF16), ut_ref[...], preferred_element_type=F32) + carry_ref[:, :1]
    r0 = jnp.sum(jnp.where(hit0, before, 0.0), axis=0, keepdims=True).astype(jnp.int32)
    r1 = jnp.sum(jnp.where(hit1, before, 0.0), axis=0, keepdims=True).astype(jnp.int32)
    carry_ref[...] = carry_ref[...] + jnp.sum(oh, axis=1, keepdims=True)
    cout_ref[...] = carry_ref[...]

    zi = jnp.zeros((META_ROWS - 4, tm), jnp.int32)
    meta_ref[...] = jnp.concatenate([e0, e1, r0, r1, zi], axis=0)
    wt = jnp.concatenate([w0, w1, jnp.zeros((LANES - 2, tm), F32)], axis=0)
    wcol_ref[...] = jnp.transpose(wt)


def _post(mg, x2d, prm, tm, row0, total_rows, carry_in, prev=None, zero_tail=False):
    m = x2d.shape[0]
    assert m % tm == 0 and row0 % tm == 0 and total_rows % tm == 0
    off = row0 // tm
    nsteps = m // tm
    last = nsteps - 1
    ut = jnp.asarray(np.triu(np.ones((tm, tm), np.float32), 1), dtype=BF16)
    full = lambda shape: pl.BlockSpec(shape, lambda i: (0,) * len(shape))
    in_specs = [
        pl.BlockSpec((tm, D_MODEL), lambda i: (jnp.minimum(i, last), 0)),
        pl.BlockSpec((tm, D_MODEL), lambda i: (jnp.minimum(i, last), 0)),
        full((1, D_MODEL)), full((1, D_MODEL)),
        full((D_MODEL, D_MODEL)),
        full((1, D_MODEL)), full((1, D_MODEL)),
        full((ROUTER_ROWS, D_MODEL)), full((ROUTER_ROWS, D_MODEL)), full((ROUTER_ROWS, LANES)),
        full((tm, tm)),
        full((N_EXPERTS, LANES)),
    ]
    args = [mg, x2d, prm["ln_emb_g"], prm["ln_emb_b"], prm["w_out"], prm["ln1_g"], prm["ln1_b"],
            prm["wr_hi"], prm["wr_lo"], prm["r_bias"], ut, carry_in]
    aliases = {}
    if prev is not None:
        for k, buf in enumerate(prev):
            in_specs.append(pl.BlockSpec(memory_space=pl.ANY))
            aliases[len(args)] = k
            args.append(buf)
    out_shape = [
        jax.ShapeDtypeStruct((total_rows, D_MODEL), F32),
        jax.ShapeDtypeStruct((META_ROWS, total_rows), jnp.int32),
        jax.ShapeDtypeStruct((total_rows, LANES), F32),
        jax.ShapeDtypeStruct((N_EXPERTS, LANES), F32),
    ]
    out_specs = [
        pl.BlockSpec((tm, D_MODEL), lambda i: (i + off, 0)),
        pl.BlockSpec((META_ROWS, tm), lambda i: (0, i + off)),
        pl.BlockSpec((tm, LANES), lambda i: (i + off, 0)),
        full((N_EXPERTS, LANES)),
    ]
    if prev is not None:
        assert len(prev) == 3
    return pl.pallas_call(
        functools.partial(_post_body, nsteps),
        grid=(nsteps + int(zero_tail),),
        in_specs=in_specs,
        out_specs=out_specs,
        out_shape=out_shape,
        input_output_aliases=aliases,
        scratch_shapes=[pltpu.VMEM((N_EXPERTS, LANES), F32)],
        compiler_params=_cparams(("arbitrary",)),
        name="post_attn",
    )(*args)


def _prep_post_params(ln_emb_g, ln_emb_b, w_out, ln1_g, ln1_b, w_rg, b_rg, w_re, b_re):
    row = lambda v: v.reshape(1, -1)
    wr = jnp.concatenate([w_re.T, w_rg.T, jnp.zeros((ROUTER_ROWS - N_EXPERTS - N_GROUPS, D_MODEL), F32)], axis=0)
    wr_hi = wr.astype(BF16)
    wr_lo = (wr - wr_hi.astype(F32)).astype(BF16)
    rb = jnp.concatenate([b_re, b_rg, jnp.zeros((ROUTER_ROWS - N_EXPERTS - N_GROUPS,), F32)])
    return dict(ln_emb_g=row(ln_emb_g), ln_emb_b=row(ln_emb_b), w_out=w_out.astype(BF16), ln1_g=row(ln1_g),
                ln1_b=row(ln1_b), wr_hi=wr_hi, wr_lo=wr_lo,
                r_bias=jnp.broadcast_to(rb[:, None], (ROUTER_ROWS, LANES)))


MOE_ROWS = 256
EXPERT_BLOCKS_PER_STEP = 2
assert EXPERT_BLOCKS_PER_STEP <= 2
SUBLANES = 8
assert D_MODEL == SUBLANES * LANES


def _store_rows_as_tiles(ref, x, first_row=0):
    n = x.shape[0]
    for c in range(SUBLANES):
        ref[pl.ds(first_row * SUBLANES + c, n, stride=SUBLANES), :] = x[:, c * LANES:(c + 1) * LANES]


def _load_rows_from_tiles(ref, n, first_row=0):
    return jnp.concatenate([ref[pl.ds(first_row * SUBLANES + c, n, stride=SUBLANES), :] for c in range(SUBLANES)],
                           axis=1)


def _tile_of_row(ref, r):
    return ref.at[pl.ds(pl.multiple_of(r * SUBLANES, SUBLANES), SUBLANES)]


def _moe_plan(counts, total_assign):
    nb_max = -(-total_assign // MOE_ROWS) + N_EXPERTS
    nb_max += -nb_max % EXPERT_BLOCKS_PER_STEP
    padded = (counts + MOE_ROWS - 1) // MOE_ROWS * MOE_ROWS
    pend = jnp.cumsum(padded)
    pstart = (pend - padded).astype(jnp.int32)
    block_start = jnp.arange(nb_max, dtype=jnp.int32) * MOE_ROWS
    n_ended = jnp.sum((pend[None, :] <= block_start[:, None]).astype(jnp.int32), axis=1)
    block_e = jnp.minimum(n_ended, N_EXPERTS - 1).astype(jnp.int32)
    n_used = (pend[-1] // MOE_ROWS).astype(jnp.int32).reshape(1)
    tail_start = jnp.where(padded > 0, pend - MOE_ROWS, -1)
    spare = pend[-1] + jnp.arange(N_EXPERTS + EXPERT_BLOCKS_PER_STEP, dtype=pend.dtype) * MOE_ROWS
    spare = jnp.where(spare < nb_max * MOE_ROWS, spare, -1)
    zero_blocks = jnp.concatenate([tail_start, spare]).astype(jnp.int32)
    return pstart, block_e, n_used, nb_max, zero_blocks


ROW_UNROLL = 8


def _slot_ids(meta, pstart):
    experts = meta[0:TOP_K]
    ranks = meta[TOP_K:2 * TOP_K]
    onehot = experts[..., None] == jnp.arange(N_EXPERTS, dtype=jnp.int32)
    return ranks + jnp.sum(jnp.where(onehot, pstart, 0), axis=-1)


def _for_row_groups(tm, fn, read=None):
    def group(g, c):
        t0 = pl.multiple_of(g * ROW_UNROLL, ROW_UNROLL)
        items = [(t0 + r, k) for r in range(ROW_UNROLL) for k in range(TOP_K)]
        if read is None:
            for t, k in items:
                fn(t, k)
        else:
            vals = [read(t, k) for t, k in items]
            for (t, k), v in zip(items, vals):
                fn(t, k, v)
        return c

    lax.fori_loop(0, tm // ROW_UNROLL, group, 0)


def _dispatch_body(nsteps, tail_ref, s0_ref, s1_ref, h_ref, xs_ref, pk_ref, zero_ref, sems, zsem):
    i = pl.program_id(0)
    tm = h_ref.shape[0]
    slot_refs = (s0_ref, s1_ref)
    cur = i % 2
    blk_tiles = MOE_ROWS * SUBLANES

    @pl.when(i == 0)
    def _():
        zero_ref[...] = jnp.zeros_like(zero_ref)
        for e in range(tail_ref.shape[0]):
            @pl.when(tail_ref[e] >= 0)
            def _():
                start = pl.multiple_of(tail_ref[e] * SUBLANES, blk_tiles)
                pltpu.make_async_copy(zero_ref, xs_ref.at[pl.ds(start, blk_tiles)], zsem).start()
        for e in range(tail_ref.shape[0]):
            @pl.when(tail_ref[e] >= 0)
            def _():
                pltpu.make_async_copy(zero_ref, xs_ref.at[pl.ds(0, blk_tiles)], zsem).wait()

    _store_rows_as_tiles(pk_ref.at[cur], h_ref[...])

    def send(t, k, slot):
        pltpu.make_async_copy(_tile_of_row(pk_ref.at[cur], t), _tile_of_row(xs_ref, slot),
                              sems.at[cur]).start(priority=k)

    def wait_buffer(buf):
        _for_row_groups(tm, lambda t, k: pltpu.make_async_copy(
            _tile_of_row(pk_ref.at[buf], t), _tile_of_row(xs_ref, 0), sems.at[buf]).wait())

    _for_row_groups(tm, send, read=lambda t, k: slot_refs[k][0, t])

    @pl.when(i > 0)
    def _():
        wait_buffer(1 - cur)

    @pl.when(i == nsteps - 1)
    def _():
        wait_buffer(cur)


def _dispatch(h1, slot_ids, tail_start, nslots, tm, total):
    assert total % tm == 0 and total <= h1.shape[0] and tm % ROW_UNROLL == 0
    slot_spec = pl.BlockSpec((1, tm), lambda i, tl: (0, i), memory_space=pltpu.SMEM)
    return pl.pallas_call(
        functools.partial(_dispatch_body, total // tm),
        grid_spec=pltpu.PrefetchScalarGridSpec(
            num_scalar_prefetch=1,
            grid=(total // tm,),
            in_specs=[slot_spec, slot_spec, pl.BlockSpec((tm, D_MODEL), lambda i, tl: (i, 0))],
            out_specs=pl.BlockSpec(memory_space=pl.ANY),
            scratch_shapes=[pltpu.VMEM((2, tm * SUBLANES, LANES), F32), pltpu.VMEM((MOE_ROWS * SUBLANES, LANES), F32),
                            pltpu.SemaphoreType.DMA((2,)), pltpu.SemaphoreType.DMA(())],
        ),
        out_shape=jax.ShapeDtypeStruct((nslots * SUBLANES, LANES), F32),
        compiler_params=_cparams(("arbitrary",)),
        name="moe_dispatch",
    )(tail_start, slot_ids[0:1], slot_ids[1:2], h1)


def _expert_schedule(block_e, n_used):
    nb = block_e.shape[0]
    idx = jnp.arange(nb, dtype=jnp.int32)
    first = (idx < n_used[0]) & ((idx == 0) | (block_e != jnp.roll(block_e, 1)))
    parity = (jnp.cumsum(first.astype(jnp.int32)) - 1) % 2
    pos = jnp.where(first, idx, nb)
    at_or_after = jnp.flip(lax.cummin(jnp.flip(pos)))
    nxt = jnp.concatenate([at_or_after[1:], jnp.full((1,), nb, jnp.int32)])
    nexte = jnp.where(nxt < nb, block_e[jnp.minimum(nxt, nb - 1)], -1)
    return first.astype(jnp.int32), nexte.astype(jnp.int32), parity.astype(jnp.int32)


def _expert_body(be_ref, nu_ref, first_ref, nexte_ref, par_ref, xs_ref, wg_hbm, wu_hbm, wd_hbm, ys_ref,
                 wgf_ref, wuf_ref, wdf_ref, wgb_ref, wub_ref, wdb_ref, sems):
    i = pl.program_id(0)
    hbm = (wg_hbm, wu_hbm, wd_hbm)
    stage = (wgf_ref, wuf_ref, wdf_ref)

    def weight_copies(e, buf):
        return [pltpu.make_async_copy(hbm[w].at[e], stage[w].at[buf], sems.at[buf, w]) for w in range(3)]

    def prepare(b):
        @pl.when(first_ref[b] == 1)
        def _():
            buf = par_ref[b]

            @pl.when(b == 0)
            def _():
                for c in weight_copies(be_ref[0], buf):
                    c.start()

            for c in weight_copies(be_ref[b], buf):
                c.wait()

            @pl.when(nexte_ref[b] >= 0)
            def _():
                for c in weight_copies(nexte_ref[b], 1 - buf):
                    c.start()

            wgb_ref[buf] = wgf_ref[buf].astype(BF16)
            wub_ref[buf] = wuf_ref[buf].astype(BF16)
            wdb_ref[buf] = wdf_ref[buf].astype(BF16)

    def ffn(b, sub):
        buf = par_ref[b]
        x = _load_rows_from_tiles(xs_ref, MOE_ROWS, sub * MOE_ROWS).astype(BF16)
        yield
        g = jnp.dot(x, wgb_ref[buf], preferred_element_type=F32)
        u = jnp.dot(x, wub_ref[buf], preferred_element_type=F32)
        yield
        y = jnp.dot(((g * _sigmoid(g)) * u).astype(BF16), wdb_ref[buf], preferred_element_type=F32)
        yield
        _store_rows_as_tiles(ys_ref, y, sub * MOE_ROWS)

    def run(gens):
        while gens:
            gens = [g for g in gens if next(g, True) is None]

    def zero(sub):
        ys_ref[pl.ds(sub * MOE_ROWS * SUBLANES, MOE_ROWS * SUBLANES), :] = jnp.zeros(
            (MOE_ROWS * SUBLANES, LANES), F32)

    blocks = [EXPERT_BLOCKS_PER_STEP * i + sub for sub in range(EXPERT_BLOCKS_PER_STEP)]
    for b in blocks:
        prepare(b)
    n_live = jnp.clip(nu_ref[0] - blocks[0], 0, EXPERT_BLOCKS_PER_STEP)
    for live in range(EXPERT_BLOCKS_PER_STEP + 1):
        @pl.when(n_live == live)
        def _():
            run([ffn(blocks[sub], sub) for sub in range(live)])
            for sub in range(live, EXPERT_BLOCKS_PER_STEP):
                zero(sub)


def _experts(xs, block_e, n_used, w_gate, w_up, w_down, nb_max):
    assert nb_max % EXPERT_BLOCKS_PER_STEP == 0
    first, nexte, parity = _expert_schedule(block_e, n_used)
    rows = lambda i, *_: (i, 0)
    any_spec = pl.BlockSpec(memory_space=pl.ANY)
    blk = (EXPERT_BLOCKS_PER_STEP * MOE_ROWS * SUBLANES, LANES)
    return pl.pallas_call(
        _expert_body,
        grid_spec=pltpu.PrefetchScalarGridSpec(
            num_scalar_prefetch=5,
            grid=(nb_max // EXPERT_BLOCKS_PER_STEP,),
            in_specs=[pl.BlockSpec(blk, rows), any_spec, any_spec, any_spec],
            out_specs=pl.BlockSpec(blk, rows),
            scratch_shapes=[
                pltpu.VMEM((2, D_MODEL, D_EXPERT), F32), pltpu.VMEM((2, D_MODEL, D_EXPERT), F32),
                pltpu.VMEM((2, D_EXPERT, D_MODEL), F32),
                pltpu.VMEM((2, D_MODEL, D_EXPERT), BF16), pltpu.VMEM((2, D_MODEL, D_EXPERT), BF16),
                pltpu.VMEM((2, D_EXPERT, D_MODEL), BF16),
                pltpu.SemaphoreType.DMA((2, 3)),
            ],
        ),
        out_shape=jax.ShapeDtypeStruct(xs.shape, F32),
        compiler_params=_cparams(("arbitrary",)),
        name="moe_experts",
    )(block_e, n_used, first, nexte, parity, xs, w_gate, w_up, w_down)


def _combine_body(nsteps, s0_ref, s1_ref, n0_ref, n1_ref, h_ref, w_ref, g2_ref, b2_ref, ys_ref, o_ref, buf_ref, sems):
    i = pl.program_id(0)
    tm = o_ref.shape[0]
    cur = i % 2

    def fetch(slot_refs, buf):
        _for_row_groups(tm, lambda t, k, slot: pltpu.make_async_copy(
            _tile_of_row(ys_ref, slot), _tile_of_row(buf_ref.at[buf, k], t),
            sems.at[buf]).start(priority=k), read=lambda t, k: slot_refs[k][0, t])

    @pl.when(i == 0)
    def _():
        fetch((s0_ref, s1_ref), cur)

    @pl.when(i + 1 < nsteps)
    def _():
        fetch((n0_ref, n1_ref), 1 - cur)

    _for_row_groups(tm, lambda t, k: pltpu.make_async_copy(
        _tile_of_row(ys_ref, 0), _tile_of_row(buf_ref.at[cur, k], t), sems.at[cur]).wait())
    w = w_ref[...]
    f = (w[:, 0:1] * _load_rows_from_tiles(buf_ref.at[cur, 0], tm)
         + w[:, 1:2] * _load_rows_from_tiles(buf_ref.at[cur, 1], tm))
    o_ref[...] = _layer_norm_rows(ALPHA * h_ref[...] + f, g2_ref[...], b2_ref[...])


def _combine(h1, wcol, slot_ids, ys, ln2_g, ln2_b, tm, row0, nrows):
    assert nrows % tm == 0 and row0 % tm == 0 and tm % ROW_UNROLL == 0
    off = row0 // tm
    nsteps = nrows // tm
    slot_spec = pl.BlockSpec((1, tm), lambda i: (0, i + off), memory_space=pltpu.SMEM)
    next_spec = pl.BlockSpec((1, tm), lambda i: (0, jnp.minimum(i + 1, nsteps - 1) + off), memory_space=pltpu.SMEM)
    return pl.pallas_call(
        functools.partial(_combine_body, nsteps),
        grid=(nsteps,),
        in_specs=[
            slot_spec, slot_spec, next_spec, next_spec,
            pl.BlockSpec((tm, D_MODEL), lambda i: (i + off, 0)),
            pl.BlockSpec((tm, LANES), lambda i: (i + off, 0)),
            pl.BlockSpec((1, D_MODEL), lambda i: (0, 0)),
            pl.BlockSpec((1, D_MODEL), lambda i: (0, 0)),
            pl.BlockSpec(memory_space=pl.ANY),
        ],
        out_specs=pl.BlockSpec((tm, D_MODEL), lambda i: (i, 0)),
        scratch_shapes=[pltpu.VMEM((2, TOP_K, tm * SUBLANES, LANES), F32), pltpu.SemaphoreType.DMA((2,))],
        out_shape=jax.ShapeDtypeStruct((nrows, D_MODEL), F32),
        compiler_params=_cparams(("arbitrary",)),
        name="moe_combine",
    )(slot_ids[0:1], slot_ids[1:2], slot_ids[0:1], slot_ids[1:2], h1, wcol, ln2_g.reshape(1, -1),
      ln2_b.reshape(1, -1), ys)


PROJ_ROWS = 512
POST_ROWS = 512
DISPATCH_ROWS = 384
COMBINE_ROWS = 256


def kernel(x_prompt, x_sample, state_gla, cache_swa_k, cache_swa_v, meta_tokens, ln_emb_g, ln_emb_b, rel_bias, w_in,
           gk_up, gk_bias, gla_norm_g, sinks, w_out, ln1_g, ln1_b, w_router_group, b_router_group, w_router_expert,
           b_router_expert, w_gate, w_up, w_down, ln2_g, ln2_b):
    nbatch, seq, d = x_prompt.shape
    nseq = x_sample.shape[0]
    assert w_in.shape[0] == DEPTH == 1 and d == D_MODEL and x_sample.shape[1] == 1
    assert seq % BLK == 0 and nseq == BLK and meta_tokens.shape[0] == N_META
    nblk = seq // BLK
    n_prompt = nbatch * seq
    total = n_prompt + nseq
    kvw = ATT_KV_HEADS * HEAD_DIM

    xp = x_prompt.reshape(n_prompt, d)
    xs = x_sample.reshape(nseq, d)
    extra = jnp.concatenate([xs, jnp.zeros((BLK - N_META, d), xs.dtype), meta_tokens.astype(xs.dtype)], axis=0)
    w_cat = _prep_w_in(w_in[0])
    pp = _ln_proj(xp, ln_emb_g, ln_emb_b, w_cat, gk_up[0], gk_bias[0], PROJ_ROWS)
    pe = _ln_proj(extra, ln_emb_g, ln_emb_b, w_cat, gk_up[0], gk_bias[0], 2 * BLK)

    yg, gla_p = _gla_prompt(pp, pe, gla_norm_g[0], nbatch, nblk)
    yg_s, gla_s = _gla_step(pe, gla_norm_g[0], state_gla[0])
    mg, k_win, v_win = _swa_prompt(pp, pe, yg, rel_bias, sinks[0], nbatch, nblk)
    mg_s, k_s, v_s = _swa_step(pe, yg_s, cache_swa_k[0].reshape(nseq, WINDOW, kvw),
                               cache_swa_v[0].reshape(nseq, WINDOW, kvw), rel_bias, sinks[0])

    prm = _prep_post_params(ln_emb_g, ln_emb_b, w_out[0], ln1_g[0], ln1_b[0], w_router_group[0], b_router_group[0],
                            w_router_expert[0], b_router_expert[0])
    carry0 = jnp.zeros((N_EXPERTS, LANES), F32)
    rows_alloc = n_prompt + POST_ROWS
    h1, meta, wcol, carry1 = _post(mg, xp, prm, POST_ROWS, 0, rows_alloc, carry0, zero_tail=True)
    h1, meta, wcol, carry2 = _post(mg_s, xs, prm, nseq, n_prompt, rows_alloc, carry1, prev=(h1, meta, wcol))

    counts = carry2[:, 0].astype(jnp.int32)
    pstart, block_e, n_used, nb_max, tail_start = _moe_plan(counts, TOP_K * total)
    slot_ids = _slot_ids(meta, pstart)
    xs_sorted = _dispatch(h1, slot_ids, tail_start, nb_max * MOE_ROWS, DISPATCH_ROWS, total)
    ys = _experts(xs_sorted, block_e, n_used, w_gate[0], w_up[0], w_down[0], nb_max)
    y_p = _combine(h1, wcol, slot_ids, ys, ln2_g[0], ln2_b[0], COMBINE_ROWS, 0, n_prompt)
    y_s = _combine(h1, wcol, slot_ids, ys, ln2_g[0], ln2_b[0], nseq, n_prompt, nseq)

    kv_shape = (1, nbatch, WINDOW, ATT_KV_HEADS, HEAD_DIM)
    k_p = k_win.reshape(kv_shape)
    v_p = v_win.reshape(kv_shape)
    return (y_p.reshape(nbatch, seq, d), y_s.reshape(nseq, 1, d), gla_p[None], k_p, v_p, gla_s[None],
            k_s.reshape(cache_swa_k.shape), v_s.reshape(cache_swa_v.shape))
```

```python
import functools
import math

import jax
import jax.numpy as jnp
import numpy as np
from jax import lax
from jax.experimental import pallas as pl
from jax.experimental.pallas import tpu as pltpu

F32 = jnp.float32
BF16 = jnp.bfloat16

D_MODEL = 1024
N_META = 16
LN_EPS = 1e-5
GLA_HEADS = 4
GLA_DK = 128
GLA_DV = 256
GLA_RANK = 16
GLA_TAU = 16.0
HEAD_DIM = 64
ATT_HEADS = 16
ATT_KV_HEADS = 4
GQA_GROUP = 4
WINDOW = 128
REL_BUCKETS = 32
REL_MAX_DIST = 128
N_GROUPS = 4
EXPERTS_PER_GROUP = 8
N_EXPERTS = 32
TOP_K = 2
D_EXPERT = 512
DEPTH = 1
ALPHA = (2.0 * DEPTH) ** 0.25
LOG2E = math.log2(math.e)

LANES = 128
BLK = 128
VMEM_LIMIT = 56 * 1024 * 1024


def _cparams(sem):
    return pltpu.CompilerParams(dimension_semantics=sem, vmem_limit_bytes=VMEM_LIMIT)


def _layer_norm_rows(x, g, b):
    mu = jnp.mean(x, axis=-1, keepdims=True)
    xc = x - mu
    var = jnp.mean(xc * xc, axis=-1, keepdims=True)
    return xc * lax.rsqrt(var + LN_EPS) * g + b


_PROJ_OUTS = (
    ("lr", LANES, F32, None),
    ("qg", GLA_HEADS * GLA_DK, BF16, GLA_DK ** -0.5),
    ("kg", GLA_HEADS * GLA_DK, BF16, None),
    ("vg", GLA_HEADS * GLA_DV, BF16, None),
    ("rg", GLA_HEADS * GLA_DV, BF16, None),
    ("qa", ATT_HEADS * HEAD_DIM, BF16, HEAD_DIM ** -0.5 * LOG2E),
    ("ka", ATT_KV_HEADS * HEAD_DIM, F32, None),
    ("va", ATT_KV_HEADS * HEAD_DIM, F32, None),
    ("ga", D_MODEL, BF16, None),
    ("gb", D_MODEL, BF16, None),
)
_PROJ_W = sum(w for _, w, _, _ in _PROJ_OUTS)


def _prep_w_in(w_in):
    sizes = (512, 512, 1024, 1024, GLA_RANK, 1024, 256, 256, 1024, 1024)
    offs = np.cumsum((0,) + sizes)
    a = w_in[:, : offs[4]]
    lr = w_in[:, offs[4]: offs[5]]
    b = w_in[:, offs[5]:]
    pad = jnp.zeros((w_in.shape[0], LANES - GLA_RANK), w_in.dtype)
    return jnp.concatenate([lr, pad, a, b], axis=1).astype(BF16)


LAMAX_ROWS = 8


def _log_sigmoid(x):
    return jnp.minimum(x, 0.0) - jnp.log(1.0 + jnp.exp(-jnp.abs(x)))


def _ln_proj_body(x_ref, g_ref, b_ref, w_ref, gkup_ref, gkb_ref, *out_refs):
    la_ref, lamax_ref = out_refs[-2:]
    tm = x_ref.shape[0]
    xn = _layer_norm_rows(x_ref[...], g_ref[...], b_ref[...]).astype(BF16)
    c0 = 0
    plain_refs = iter(out_refs[:-2])
    for name, width, dtype, scale in _PROJ_OUTS:
        acc = jnp.dot(xn, w_ref[:, c0:c0 + width], preferred_element_type=F32)
        c0 += width
        if name == "lr":
            x = jnp.dot(acc.astype(BF16), gkup_ref[...], preferred_element_type=F32) + gkb_ref[...]
            la = _log_sigmoid(x) * (1.0 / GLA_TAU)
            la_ref[...] = la
            rows = []
            for r in range(tm // BLK):
                blk_max = jnp.max(jnp.max(jnp.abs(la[r * BLK:(r + 1) * BLK]), axis=0, keepdims=True), axis=1, keepdims=True)
                rows.append(jnp.broadcast_to(blk_max, (1, LANES)))
            rows.append(jnp.zeros((LAMAX_ROWS - tm // BLK, LANES), F32))
            lamax_ref[...] = jnp.concatenate(rows, axis=0)
            continue
        if scale is not None:
            acc = acc * scale
        next(plain_refs)[...] = acc.astype(dtype)


def _ln_proj(x2d, ln_g, ln_b, w_cat, gk_up, gk_bias, tm):
    m = x2d.shape[0]
    assert m % tm == 0 and tm % BLK == 0 and tm // BLK < LAMAX_ROWS
    names = [n for n, _, _, _ in _PROJ_OUTS if n != "lr"] + ["la", "lamax"]
    ladim = GLA_HEADS * GLA_DK
    gkup = jnp.concatenate([gk_up, jnp.zeros((LANES - GLA_RANK, ladim), gk_up.dtype)], axis=0).astype(BF16)
    out_shape = [jax.ShapeDtypeStruct((m, w), dt) for n, w, dt, _ in _PROJ_OUTS if n != "lr"]
    out_specs = [pl.BlockSpec((tm, w), lambda i: (i, 0)) for n, w, _, _ in _PROJ_OUTS if n != "lr"]
    out_shape += [jax.ShapeDtypeStruct((m, ladim), F32), jax.ShapeDtypeStruct((m // tm * LAMAX_ROWS, LANES), F32)]
    out_specs += [pl.BlockSpec((tm, ladim), lambda i: (i, 0)), pl.BlockSpec((LAMAX_ROWS, LANES), lambda i: (i, 0))]
    outs = pl.pallas_call(
        _ln_proj_body,
        grid=(m // tm,),
        in_specs=[
            pl.BlockSpec((tm, D_MODEL), lambda i: (i, 0)),
            pl.BlockSpec((1, D_MODEL), lambda i: (0, 0)),
            pl.BlockSpec((1, D_MODEL), lambda i: (0, 0)),
            pl.BlockSpec((D_MODEL, _PROJ_W), lambda i: (0, 0), pipeline_mode=pl.Buffered(1)),
            pl.BlockSpec((LANES, ladim), lambda i: (0, 0)),
            pl.BlockSpec((1, ladim), lambda i: (0, 0)),
        ],
        out_specs=out_specs,
        out_shape=out_shape,
        compiler_params=_cparams(("arbitrary",)),
        name="ln_proj",
    )(x2d, ln_g.reshape(1, -1), ln_b.reshape(1, -1), w_cat, gkup, gk_bias.reshape(1, -1))
    res = dict(zip(names, outs))
    res["lamax"] = res["lamax"].reshape(m // tm, LAMAX_ROWS, LANES)[:, :tm // BLK, 0].reshape(m // BLK)
    return res


_GLA_LEVELS = tuple(2 ** i for i in range(int(math.log2(BLK))))
GLA_SAFE_EXPONENT = 60.0


def _sigmoid(x):
    return 0.5 * jnp.tanh(0.5 * x) + 0.5


def _split_dot(a01, x):
    hi = x.astype(BF16)
    lo = (x - hi.astype(F32)).astype(BF16)
    n = x.shape[1]
    both = jnp.dot(a01, jnp.concatenate([hi, lo], axis=1), preferred_element_type=F32)
    return both[:, :n] + both[:, n:]


def _gla_anchor_exponent(b, la, s, row):
    if s == 1:
        return jnp.where(row % 2 == 1, la, 0.0)
    if s == 2:
        la_dn = pltpu.roll(la, 1, axis=0)
        la_up = pltpu.roll(la, BLK - 1, axis=0)
        r = row % 4
        return jnp.where(r == 0, la_up, jnp.where(r == 1, 0.0, jnp.where(r == 2, la, la + la_dn)))
    nb = BLK // (2 * s)
    b3 = b.reshape(nb, 2 * s, b.shape[-1])
    anchor = jnp.broadcast_to(b3[:, s - 1:s, :], b3.shape).reshape(b.shape)
    return -jnp.abs(b - anchor)


def _gla_body(nblk, lamax_ref, qm, km, vm, rm, lam, gam, qp, kp, vp, rp, lap, gap, gn_ref, tri_ref,
              y_ref, s_out_ref, s_ref):
    c = pl.program_id(1)
    is_meta = c == 0
    n_prompt_blocks = pl.num_programs(0) * nblk
    blk_max = lamax_ref[jnp.where(is_meta, n_prompt_blocks, pl.program_id(0) * nblk + c - 1)]

    @pl.when(is_meta)
    def _():
        s_ref[...] = jnp.zeros_like(s_ref)

    def pick(m_ref, p_ref):
        return jnp.where(is_meta, m_ref[...], p_ref[...])

    row = lax.broadcasted_iota(jnp.int32, (BLK, GLA_DK), 0)
    col_t = lax.broadcasted_iota(jnp.int32, (BLK, BLK), 1)
    row_t = lax.broadcasted_iota(jnp.int32, (BLK, BLK), 0)
    live = jnp.logical_or(jnp.logical_not(is_meta), row >= BLK - N_META)
    tri = tri_ref[...]
    q_all, k_all, v_all, r_all = pick(qm, qp), pick(km, kp), pick(vm, vp), pick(rm, rp)
    ga_all = pick(gam, gap)
    la_all = pick(lam, lap)
    nt = (((1,), (1,)), ((), ()))
    mid = BLK // 2 - 1

    def head(h, single_anchor):
        dk = slice(h * GLA_DK, (h + 1) * GLA_DK)
        dv = slice(h * GLA_DV, (h + 1) * GLA_DV)
        la = jnp.where(live, la_all[:, dk], 0.0)
        q = q_all[:, dk].astype(F32)
        k = jnp.where(live, k_all[:, dk].astype(F32), 0.0)
        v = v_all[:, dv]
        b = _split_dot(tri, la)
        yield
        b_last = b[BLK - 1:BLK, :]
        s_old = s_ref[h]
        if single_anchor:
            b_mid = b[mid:mid + 1, :]
            qe = q * jnp.exp(b - b_mid)
            ke = k * jnp.exp(b_mid - b)
            a = jnp.where(row_t >= col_t,
                          lax.dot_general(qe.astype(BF16), ke.astype(BF16), nt, preferred_element_type=F32), 0.0)
            qg = qe * jnp.exp(b_mid)
            kd = ke * jnp.exp(b_last - b_mid)
        else:
            a = jnp.where(row_t == col_t,
                          lax.dot_general(q.astype(BF16), k.astype(BF16), nt, preferred_element_type=F32), 0.0)
            for s in _GLA_LEVELS:
                e = jnp.exp(_gla_anchor_exponent(b, la, s, row))
                upper = (row // s) % 2 == 1
                q_s = jnp.where(upper, q * e, 0.0).astype(BF16)
                k_s = jnp.where(upper, 0.0, k * e).astype(BF16)
                p = lax.dot_general(q_s, k_s, nt, preferred_element_type=F32)
                a = a + jnp.where(row_t // (2 * s) == col_t // (2 * s), p, 0.0)
            qg = q * jnp.exp(b)
            kd = k * jnp.exp(b_last - b)
        yield
        o = jnp.dot(qg.astype(BF16), s_old.astype(BF16), preferred_element_type=F32)
        lhs = jnp.concatenate([jnp.transpose(kd).astype(BF16), a.astype(BF16)], axis=0)
        both = jnp.dot(lhs, v, preferred_element_type=F32)
        yield
        decay_col = jnp.transpose(jnp.broadcast_to(jnp.exp(b_last), (BLK, GLA_DK)))[:, :1]
        s_ref[h] = decay_col * s_old + both[:GLA_DK]
        o = o + both[GLA_DK:]
        o = o * lax.rsqrt(jnp.mean(o * o, axis=-1, keepdims=True) + LN_EPS) * gn_ref[...]
        r = r_all[:, dv].astype(F32)
        y = o * (r * _sigmoid(r)) * _sigmoid(ga_all[:, dv].astype(F32))
        y_ref[:, dv] = y.astype(y_ref.dtype)

    mild = blk_max * (BLK // 2) <= GLA_SAFE_EXPONENT

    def all_heads(single_anchor):
        running = [head(h, single_anchor) for h in range(GLA_HEADS)]
        while running:
            running = [g for g in running if next(g, True) is None]

    @pl.when(mild)
    def _():
        all_heads(True)

    @pl.when(jnp.logical_not(mild))
    def _():
        all_heads(False)

    @pl.when(c == nblk)
    def _():
        s_out_ref[...] = s_ref[...]


def _tri_incl():
    i = np.arange(BLK)
    return jnp.asarray((i[None, :] <= i[:, None]).astype(np.float32), dtype=BF16)


def _gla_prompt(pp, pe, gnorm, nbatch, nblk):
    names = ("qg", "kg", "vg", "rg", "la", "ga")
    lamax = jnp.concatenate([pp["lamax"], pe["lamax"][1:2]])
    m_specs = [pl.BlockSpec((BLK, pe[n].shape[1]), lambda b, c, lm: (1, 0)) for n in names]
    p_specs = [pl.BlockSpec((BLK, pp[n].shape[1]), lambda b, c, lm: (b * nblk + jnp.maximum(c - 1, 0), 0))
               for n in names]
    w_specs = [
        pl.BlockSpec((1, GLA_DV), lambda b, c, lm: (0, 0)),
        pl.BlockSpec((BLK, BLK), lambda b, c, lm: (0, 0)),
    ]
    y, s_fin = pl.pallas_call(
        functools.partial(_gla_body, nblk),
        grid_spec=pltpu.PrefetchScalarGridSpec(
            num_scalar_prefetch=1,
            grid=(nbatch, nblk + 1),
            in_specs=m_specs + p_specs + w_specs,
            out_specs=[
                pl.BlockSpec((BLK, D_MODEL), lambda b, c, lm: (b * nblk + jnp.maximum(c - 1, 0), 0)),
                pl.BlockSpec((None, GLA_HEADS, GLA_DK, GLA_DV), lambda b, c, lm: (b, 0, 0, 0)),
            ],
            scratch_shapes=[pltpu.VMEM((GLA_HEADS, GLA_DK, GLA_DV), F32)],
        ),
        out_shape=[
            jax.ShapeDtypeStruct((nbatch * nblk * BLK, D_MODEL), BF16),
            jax.ShapeDtypeStruct((nbatch, GLA_HEADS, GLA_DK, GLA_DV), F32),
        ],
        compiler_params=_cparams(("arbitrary", "arbitrary")),
        name="gla_prompt",
    )(lamax, *[pe[n] for n in names], *[pp[n] for n in names], gnorm.reshape(1, -1), _tri_incl())
    return y, s_fin


GLA_STEP_SEQS = 16


def _gla_step_body(q_ref, k_ref, v_ref, r_ref, la_ref, ga_ref, gn_ref, s_in_ref,
                   y_ref, s_out_ref, at_ref, kt_ref, qt_ref):
    g = pl.program_id(0)
    nseq = q_ref.shape[0]

    @pl.when(g == 0)
    def _():
        a = jnp.exp(la_ref[...])
        for h in range(GLA_HEADS):
            dk = slice(h * GLA_DK, (h + 1) * GLA_DK)
            at_ref[h] = jnp.transpose(a[:, dk])
            kt_ref[h] = jnp.transpose(k_ref[:, dk].astype(F32))
            qt_ref[h] = jnp.transpose(q_ref[:, dk].astype(F32))

    lane = lax.broadcasted_iota(jnp.int32, (GLA_DK, nseq), 1)
    ones = jnp.ones((nseq, GLA_DV), BF16)
    grp = pl.ds(pl.multiple_of(g * GLA_STEP_SEQS, GLA_STEP_SEQS), GLA_STEP_SEQS)
    r_grp = r_ref[grp, :].astype(F32)
    ga_grp = ga_ref[grp, :].astype(F32)
    for i in range(GLA_STEP_SEQS):
        n = g * GLA_STEP_SEQS + i
        sel = lane == n
        for h in range(GLA_HEADS):
            dv = slice(h * GLA_DV, (h + 1) * GLA_DV)
            a_sel = jnp.where(sel, at_ref[h], 0.0)
            k_sel = jnp.where(sel, kt_ref[h], 0.0).astype(BF16)
            q_sel = jnp.where(sel, qt_ref[h], 0.0).astype(BF16)
            decay = _split_dot_rhs(a_sel, ones)
            kv = jnp.dot(k_sel, v_ref[:, dv], preferred_element_type=F32)
            q_b = jnp.dot(q_sel, ones, preferred_element_type=F32)
            s_new = decay * s_in_ref[i, h] + kv
            s_out_ref[i, h] = s_new
            o = jnp.sum(q_b * s_new, axis=0, keepdims=True)
            o = o * lax.rsqrt(jnp.mean(o * o, axis=-1, keepdims=True) + LN_EPS) * gn_ref[...]
            r = r_grp[i:i + 1, dv]
            ga = ga_grp[i:i + 1, dv]
            y_ref[i:i + 1, dv] = (o * (r * _sigmoid(r)) * _sigmoid(ga)).astype(y_ref.dtype)


def _split_dot_rhs(x, b01):
    hi = x.astype(BF16)
    lo = (x - hi.astype(F32)).astype(BF16)
    return jnp.dot(hi, b01, preferred_element_type=F32) + jnp.dot(lo, b01, preferred_element_type=F32)


def _gla_step(pe, gnorm, state):
    nseq = state.shape[0]
    assert nseq == BLK and nseq % GLA_STEP_SEQS == 0
    names = ("qg", "kg", "vg", "rg", "la", "ga")
    t_specs = [pl.BlockSpec((nseq, pe[n].shape[1]), lambda g: (0, 0)) for n in names]
    st_spec = pl.BlockSpec((GLA_STEP_SEQS, GLA_HEADS, GLA_DK, GLA_DV), lambda g: (g, 0, 0, 0))
    return pl.pallas_call(
        _gla_step_body,
        grid=(nseq // GLA_STEP_SEQS,),
        in_specs=t_specs + [
            pl.BlockSpec((1, GLA_DV), lambda g: (0, 0)),
            st_spec,
        ],
        out_specs=[pl.BlockSpec((GLA_STEP_SEQS, D_MODEL), lambda g: (g, 0)), st_spec],
        out_shape=[jax.ShapeDtypeStruct((nseq, D_MODEL), F32), jax.ShapeDtypeStruct(state.shape, F32)],
        scratch_shapes=[pltpu.VMEM((GLA_HEADS, GLA_DK, nseq), F32) for _ in range(3)],
        compiler_params=_cparams(("arbitrary",)),
        name="gla_step",
    )(*[pe[n] for n in names], gnorm.reshape(1, -1), state)


HALF = LANES // 2
SWA_BLOCKS_PER_STEP = 8


def _rel_bucket(dist):
    max_exact = REL_BUCKETS // 2
    d = jnp.maximum(dist, 0)
    large = max_exact + (jnp.log(jnp.maximum(d, 1).astype(F32) / max_exact)
                         / math.log(REL_MAX_DIST / max_exact) * (REL_BUCKETS - max_exact)).astype(jnp.int32)
    large = jnp.minimum(large, REL_BUCKETS - 1)
    return jnp.where(d < max_exact, d, large)


def _bias_lookup(rel_bias, dist):
    onehot = (_rel_bucket(dist)[..., None] == jnp.arange(REL_BUCKETS)).astype(F32)
    bias = jnp.einsum("...b,bh->h...", onehot, rel_bias.astype(F32), precision=lax.Precision.HIGHEST)
    return bias * LOG2E


def _swa_bias_tables(rel_bias):
    q = jnp.arange(BLK)[:, None]
    c = jnp.arange(2 * BLK)[None, :]
    dist = BLK + q - c
    bias = _bias_lookup(rel_bias, dist)
    inside = (dist >= 0) & (dist < WINDOW)
    first = inside & (c >= BLK - N_META)
    neg = jnp.float32(-jnp.inf)
    return jnp.stack([jnp.where(first[None], bias, neg), jnp.where(inside[None], bias, neg)])


def _dup_tiles(x):
    lane = lax.broadcasted_iota(jnp.int32, (x.shape[0], LANES), 1)
    low = lane < HALF
    out = []
    for t in range(2):
        tile = x[:, t * LANES:(t + 1) * LANES]
        swapped = pltpu.roll(tile, HALF, axis=1)
        out += [jnp.where(low, tile, swapped).astype(BF16), jnp.where(low, swapped, tile).astype(BF16)]
    return out


def _group_row_heads(j):
    tiles = [j * (GQA_GROUP // 2) + pair for pair in range(GQA_GROUP // 2)]
    return [2 * t for t in tiles] + [2 * t + 1 for t in tiles]


def _swa_body(nblk, q_ref, kc_ref, vc_ref, kp_ref, vp_ref, km_ref, vm_ref, gb_ref, yg_ref, tb_ref, sink_ref,
              o_ref, kw_ref, vw_ref):
    step = pl.program_id(1)
    nsub = SWA_BLOCKS_PER_STEP
    last = slice((nsub - 1) * BLK, nsub * BLK)

    @pl.when(step == nblk // nsub - 1)
    def _():
        kw_ref[...] = kc_ref[last, :]
        vw_ref[...] = vc_ref[last, :]

    npair = GQA_GROUP // 2
    low = lax.broadcasted_iota(jnp.int32, (BLK, LANES), 1) < HALF
    ones = jnp.ones((2 * BLK, LANES), BF16)
    seg = lax.broadcasted_iota(jnp.int32, (GQA_GROUP * BLK, 1), 0) // BLK
    nt = (((1,), (1,)), ((), ()))

    def block(sub):
        rows = slice(sub * BLK, (sub + 1) * BLK)
        if sub == 0:
            k_prev = jnp.where(step == 0, km_ref[...], kp_ref[...])
            v_prev = jnp.where(step == 0, vm_ref[...], vp_ref[...])
            variant = jnp.minimum(step, 1)
        else:
            before = slice((sub - 1) * BLK, sub * BLK)
            k_prev, v_prev, variant = kc_ref[before, :], vc_ref[before, :], 1
        k_tiles = _dup_tiles(jnp.concatenate([k_prev, kc_ref[rows, :]], axis=0))
        v_tiles = _dup_tiles(jnp.concatenate([v_prev, vc_ref[rows, :]], axis=0))
        for j in range(ATT_KV_HEADS):
            tiles = [j * npair + pair for pair in range(npair)]
            q_t = [q_ref[rows, t * LANES:(t + 1) * LANES] for t in tiles]
            zero = jnp.zeros_like(q_t[0])
            q_st = jnp.concatenate([jnp.where(low, q, zero) for q in q_t] + [jnp.where(low, zero, q) for q in q_t],
                                   axis=0)
            heads = _group_row_heads(j)
            s = lax.dot_general(q_st, k_tiles[j], nt, preferred_element_type=F32)
            yield
            s = s + jnp.concatenate([tb_ref[variant, h] for h in heads], axis=0)
            sink = jnp.full((GQA_GROUP * BLK, 1), sink_ref[heads[0]], F32)
            for i in range(1, GQA_GROUP):
                sink = jnp.where(seg == i, sink_ref[heads[i]], sink)
            m = jnp.maximum(jnp.max(s, axis=-1, keepdims=True), sink)
            p = jnp.exp2(s - m).astype(BF16)
            pv = jnp.dot(p, jnp.concatenate([v_tiles[j], ones], axis=1), preferred_element_type=F32)
            yield
            o = pv[:, :LANES] / (pv[:, LANES:] + jnp.exp2(sink - m))
            for pair, t in enumerate(tiles):
                cols = slice(t * LANES, (t + 1) * LANES)
                gate = _sigmoid(gb_ref[rows, cols].astype(F32))
                even = o[pair * BLK:(pair + 1) * BLK]
                odd = o[(npair + pair) * BLK:(npair + pair + 1) * BLK]
                o_ref[rows, cols] = (gate * jnp.where(low, even, odd)
                                     + yg_ref[rows, cols].astype(F32)).astype(o_ref.dtype)

    running = [block(sub) for sub in range(nsub)]
    while running:
        running = [g for g in running if next(g, True) is None]


def _swa_prompt(pp, pe, yg, rel_bias, sinks, nbatch, nblk):
    tb = _swa_bias_tables(rel_bias)
    kvw = ATT_KV_HEADS * HEAD_DIM
    sinks2 = sinks.astype(F32) * LOG2E
    nsub = SWA_BLOCKS_PER_STEP
    assert nblk % nsub == 0
    steps = nblk // nsub
    rows = nsub * BLK
    cur = lambda b, c: (b * steps + c, 0)
    prev = lambda b, c: (b * nblk + jnp.maximum(c * nsub - 1, 0), 0)
    return pl.pallas_call(
        functools.partial(_swa_body, nblk),
        grid=(nbatch, steps),
        in_specs=[
            pl.BlockSpec((rows, D_MODEL), cur),
            pl.BlockSpec((rows, kvw), cur), pl.BlockSpec((rows, kvw), cur),
            pl.BlockSpec((BLK, kvw), prev), pl.BlockSpec((BLK, kvw), prev),
            pl.BlockSpec((BLK, kvw), lambda b, c: (1, 0)), pl.BlockSpec((BLK, kvw), lambda b, c: (1, 0)),
            pl.BlockSpec((rows, D_MODEL), cur),
            pl.BlockSpec((rows, D_MODEL), cur),
            pl.BlockSpec(tb.shape, lambda b, c: (0, 0, 0, 0)),
            pl.BlockSpec(memory_space=pltpu.SMEM),
        ],
        out_specs=[pl.BlockSpec((rows, D_MODEL), cur),
                   pl.BlockSpec((None, BLK, kvw), lambda b, c: (b, 0, 0)),
                   pl.BlockSpec((None, BLK, kvw), lambda b, c: (b, 0, 0))],
        out_shape=[jax.ShapeDtypeStruct((nbatch * nblk * BLK, D_MODEL), BF16),
                   jax.ShapeDtypeStruct((nbatch, BLK, kvw), F32), jax.ShapeDtypeStruct((nbatch, BLK, kvw), F32)],
        compiler_params=_cparams(("arbitrary", "arbitrary")),
        name="swa_prompt",
    )(pp["qa"], pp["ka"], pp["va"], pp["ka"], pp["va"], pe["ka"], pe["va"], pp["gb"], yg, tb, sinks2)


SWA_STEP_SEQS = 8
Q_TILES = ATT_HEADS // 2


def _swa_step_body(q_ref, kn_ref, vn_ref, ck_ref, cv_ref, gb_ref, yg_ref, tb_ref, sink_ref,
                   o_ref, ko_ref, vo_ref):
    row = lax.broadcasted_iota(jnp.int32, (WINDOW, ATT_KV_HEADS * HEAD_DIM), 0)
    low = lax.broadcasted_iota(jnp.int32, (Q_TILES, LANES), 1) < HALF
    mine16 = (lax.broadcasted_iota(jnp.int32, (2 * Q_TILES, LANES), 0) % Q_TILES) // (GQA_GROUP // 2)
    nt = (((1,), (1,)), ((), ()))
    seqs = range(SWA_STEP_SEQS)
    k_wide, v_wide = [], []
    for i in seqs:
        k_win = jnp.where(row == WINDOW - 1, kn_ref[i:i + 1, :], pltpu.roll(ck_ref[i], WINDOW - 1, axis=0))
        v_win = jnp.where(row == WINDOW - 1, vn_ref[i:i + 1, :], pltpu.roll(cv_ref[i], WINDOW - 1, axis=0))
        ko_ref[i] = k_win
        vo_ref[i] = v_win
        k_wide.append(jnp.concatenate(_dup_tiles(k_win), axis=1))
        v_wide.append(jnp.concatenate(_dup_tiles(v_win), axis=1))
    scores = []
    for i in seqs:
        q8 = q_ref[i]
        q16 = jnp.concatenate([jnp.where(low, q8, 0.0), jnp.where(low, 0.0, q8)], axis=0)
        q_wide = jnp.concatenate([jnp.where(mine16 == j, q16, 0.0) for j in range(ATT_KV_HEADS)], axis=1)
        scores.append(lax.dot_general(q_wide.astype(BF16), k_wide[i], nt, preferred_element_type=F32))
    probs, invs = [], []
    for i in seqs:
        s = scores[i] + tb_ref[...]
        sink = sink_ref[...]
        m = jnp.maximum(jnp.max(s, axis=-1, keepdims=True), sink)
        p = jnp.exp2(s - m)
        invs.append(1.0 / (jnp.sum(p, axis=-1, keepdims=True) + jnp.exp2(sink - m)))
        probs.append(p.astype(BF16))
    outs = [jnp.dot(probs[i], v_wide[i], preferred_element_type=F32) for i in seqs]
    for i in seqs:
        o = jnp.zeros((2 * Q_TILES, LANES), F32)
        for j in range(ATT_KV_HEADS):
            o = jnp.where(mine16 == j, outs[i][:, j * LANES:(j + 1) * LANES], o)
        o = o * invs[i]
        o = jnp.where(low, o[:Q_TILES], o[Q_TILES:])
        o_ref[i] = _sigmoid(gb_ref[i]) * o + yg_ref[i]


def _swa_step(pe, yg_s, cache_k, cache_v, rel_bias, sinks):
    nseq = cache_k.shape[0]
    kvw = ATT_KV_HEADS * HEAD_DIM
    as_tiles = lambda x: x[:nseq].astype(F32).reshape(nseq, Q_TILES, LANES)
    dist = (WINDOW - 1) - jnp.arange(WINDOW)
    bias = _bias_lookup(rel_bias, dist)
    tb = jnp.concatenate([bias[0::2], bias[1::2]], axis=0)
    sk = jnp.concatenate([sinks[0::2], sinks[1::2]])[:, None].astype(F32) * LOG2E
    g = SWA_STEP_SEQS
    tile_spec = pl.BlockSpec((g, Q_TILES, LANES), lambda s: (s, 0, 0))
    win_spec = pl.BlockSpec((g, WINDOW, kvw), lambda s: (s, 0, 0))
    new_spec = pl.BlockSpec((g, kvw), lambda s: (s, 0))
    o, ko, vo = pl.pallas_call(
        _swa_step_body,
        grid=(nseq // g,),
        in_specs=[tile_spec, new_spec, new_spec, win_spec, win_spec, tile_spec, tile_spec,
                  pl.BlockSpec(tb.shape, lambda s: (0, 0)), pl.BlockSpec(sk.shape, lambda s: (0, 0))],
        out_specs=[tile_spec, win_spec, win_spec],
        out_shape=[jax.ShapeDtypeStruct((nseq, Q_TILES, LANES), F32),
                   jax.ShapeDtypeStruct(cache_k.shape, F32), jax.ShapeDtypeStruct(cache_v.shape, F32)],
        compiler_params=_cparams(("arbitrary",)),
        name="swa_step",
    )(as_tiles(pe["qa"]), pe["ka"], pe["va"], cache_k, cache_v, as_tiles(pe["gb"]),
      yg_s.reshape(nseq, Q_TILES, LANES), tb, sk)
    return o.reshape(nseq, D_MODEL), ko, vo


ROUTER_ROWS = 40
META_ROWS = 8


def _split3_nt(a_hi, a_lo, x):
    nt = (((1,), (1,)), ((), ()))
    x_hi = x.astype(BF16)
    x_lo = (x - x_hi.astype(F32)).astype(BF16)
    return (lax.dot_general(a_hi, x_hi, nt, preferred_element_type=F32)
            + lax.dot_general(a_hi, x_lo, nt, preferred_element_type=F32)
            + lax.dot_general(a_lo, x_hi, nt, preferred_element_type=F32))


def _first_argmax_rows(v, ridx, nrows):
    vmax = jnp.max(v, axis=0, keepdims=True)
    idx = jnp.min(jnp.where(v == vmax, ridx, nrows), axis=0, keepdims=True)
    return vmax, idx


def _post_body(nsteps, *refs):
    h1_ref, meta_ref, wcol_ref = refs[-5:-2]
    i = pl.program_id(0)

    @pl.when(i < nsteps)
    def _():
        _post_tile(i, *refs)

    @pl.when(i >= nsteps)
    def _():
        h1_ref[...] = jnp.zeros_like(h1_ref)
        meta_ref[...] = jnp.zeros_like(meta_ref)
        wcol_ref[...] = jnp.zeros_like(wcol_ref)


def _post_tile(i, mg_ref, x_ref, lng_ref, lnb_ref, wo_ref, g1_ref, b1_ref, wrh_ref, wrl_ref, rb_ref, ut_ref,
               cin_ref, *rest):
    h1_ref, meta_ref, wcol_ref, cout_ref, carry_ref = rest[-5:]

    @pl.when(i == 0)
    def _():
        carry_ref[...] = cin_ref[...]

    tm = x_ref.shape[0]
    h = _layer_norm_rows(x_ref[...], lng_ref[...], lnb_ref[...])
    acc = jnp.dot(mg_ref[...].astype(BF16), wo_ref[...], preferred_element_type=F32)
    h1 = _layer_norm_rows(ALPHA * h + acc, g1_ref[...], b1_ref[...])
    h1_ref[...] = h1

    lt = _split3_nt(wrh_ref[...], wrl_ref[...], h1) + rb_ref[:, :1]
    ridx = lax.broadcasted_iota(jnp.int32, (EXPERTS_PER_GROUP, tm), 0)
    neg = jnp.float32(-jnp.inf)
    g_log = jnp.where(ridx < N_GROUPS, lt[N_EXPERTS:N_EXPERTS + EXPERTS_PER_GROUP], neg)
    g_max, grp = _first_argmax_rows(g_log, ridx, EXPERTS_PER_GROUP)
    p_grp = 1.0 / jnp.sum(jnp.exp(g_log - g_max), axis=0, keepdims=True)
    e_in = lt[0:EXPERTS_PER_GROUP]
    for gi in range(1, N_GROUPS):
        e_in = jnp.where(grp == gi, lt[gi * EXPERTS_PER_GROUP:(gi + 1) * EXPERTS_PER_GROUP], e_in)
    v0, i0 = _first_argmax_rows(e_in, ridx, EXPERTS_PER_GROUP)
    v1, i1 = _first_argmax_rows(jnp.where(ridx == i0, neg, e_in), ridx, EXPERTS_PER_GROUP)
    t = jnp.exp(v1 - v0)
    w0 = p_grp / (1.0 + t)
    w1 = p_grp * t / (1.0 + t)
    e0 = grp * EXPERTS_PER_GROUP + i0
    e1 = grp * EXPERTS_PER_GROUP + i1

    eidx = lax.broadcasted_iota(jnp.int32, (N_EXPERTS, tm), 0)
    hit0 = eidx == e0
    hit1 = eidx == e1
    oh = jnp.where(jnp.logical_or(hit0, hit1), 1.0, 0.0)
    before = jnp.dot(oh.astype(BF16), ut_ref[...], preferred_element_type=F32) + carry_ref[:, :1]
    r0 = jnp.sum(jnp.where(hit0, before, 0.0), axis=0, keepdims=True).astype(jnp.int32)
    r1 = jnp.sum(jnp.where(hit1, before, 0.0), axis=0, keepdims=True).astype(jnp.int32)
    carry_ref[...] = carry_ref[...] + jnp.sum(oh, axis=1, keepdims=True)
    cout_ref[...] = carry_ref[...]

    zi = jnp.zeros((META_ROWS - 4, tm), jnp.int32)
    meta_ref[...] = jnp.concatenate([e0, e1, r0, r1, zi], axis=0)
    wt = jnp.concatenate([w0, w1, jnp.zeros((LANES - 2, tm), F32)], axis=0)
    wcol_ref[...] = jnp.transpose(wt)


def _post(mg, x2d, prm, tm, row0, total_rows, carry_in, prev=None, zero_tail=False):
    m = x2d.shape[0]
    assert m % tm == 0 and row0 % tm == 0 and total_rows % tm == 0
    off = row0 // tm
    nsteps = m // tm
    last = nsteps - 1
    ut = jnp.asarray(np.triu(np.ones((tm, tm), np.float32), 1), dtype=BF16)
    full = lambda shape: pl.BlockSpec(shape, lambda i: (0,) * len(shape))
    in_specs = [
        pl.BlockSpec((tm, D_MODEL), lambda i: (jnp.minimum(i, last), 0)),
        pl.BlockSpec((tm, D_MODEL), lambda i: (jnp.minimum(i, last), 0)),
        full((1, D_MODEL)), full((1, D_MODEL)),
        full((D_MODEL, D_MODEL)),
        full((1, D_MODEL)), full((1, D_MODEL)),
        full((ROUTER_ROWS, D_MODEL)), full((ROUTER_ROWS, D_MODEL)), full((ROUTER_ROWS, LANES)),
        full((tm, tm)),
        full((N_EXPERTS, LANES)),
    ]
    args = [mg, x2d, prm["ln_emb_g"], prm["ln_emb_b"], prm["w_out"], prm["ln1_g"], prm["ln1_b"],
            prm["wr_hi"], prm["wr_lo"], prm["r_bias"], ut, carry_in]
    aliases = {}
    if prev is not None:
        for k, buf in enumerate(prev):
            in_specs.append(pl.BlockSpec(memory_space=pl.ANY))
            aliases[len(args)] = k
            args.append(buf)
    out_shape = [
        jax.ShapeDtypeStruct((total_rows, D_MODEL), F32),
        jax.ShapeDtypeStruct((META_ROWS, total_rows), jnp.int32),
        jax.ShapeDtypeStruct((total_rows, LANES), F32),
        jax.ShapeDtypeStruct((N_EXPERTS, LANES), F32),
    ]
    out_specs = [
        pl.BlockSpec((tm, D_MODEL), lambda i: (i + off, 0)),
        pl.BlockSpec((META_ROWS, tm), lambda i: (0, i + off)),
        pl.BlockSpec((tm, LANES), lambda i: (i + off, 0)),
        full((N_EXPERTS, LANES)),
    ]
    if prev is not None:
        assert len(prev) == 3
    return pl.pallas_call(
        functools.partial(_post_body, nsteps),
        grid=(nsteps + int(zero_tail),),
        in_specs=in_specs,
        out_specs=out_specs,
        out_shape=out_shape,
        input_output_aliases=aliases,
        scratch_shapes=[pltpu.VMEM((N_EXPERTS, LANES), F32)],
        compiler_params=_cparams(("arbitrary",)),
        name="post_attn",
    )(*args)


def _prep_post_params(ln_emb_g, ln_emb_b, w_out, ln1_g, ln1_b, w_rg, b_rg, w_re, b_re):
    row = lambda v: v.reshape(1, -1)
    wr = jnp.concatenate([w_re.T, w_rg.T, jnp.zeros((ROUTER_ROWS - N_EXPERTS - N_GROUPS, D_MODEL), F32)], axis=0)
    wr_hi = wr.astype(BF16)
    wr_lo = (wr - wr_hi.astype(F32)).astype(BF16)
    rb = jnp.concatenate([b_re, b_rg, jnp.zeros((ROUTER_ROWS - N_EXPERTS - N_GROUPS,), F32)])
    return dict(ln_emb_g=row(ln_emb_g), ln_emb_b=row(ln_emb_b), w_out=w_out.astype(BF16), ln1_g=row(ln1_g),
                ln1_b=row(ln1_b), wr_hi=wr_hi, wr_lo=wr_lo,
                r_bias=jnp.broadcast_to(rb[:, None], (ROUTER_ROWS, LANES)))


MOE_ROWS = 256
EXPERT_BLOCKS_PER_STEP = 2
assert EXPERT_BLOCKS_PER_STEP <= 2
SUBLANES = 8
assert D_MODEL == SUBLANES * LANES


def _store_rows_as_tiles(ref, x, first_row=0):
    n = x.shape[0]
    for c in range(SUBLANES):
        ref[pl.ds(first_row * SUBLANES + c, n, stride=SUBLANES), :] = x[:, c * LANES:(c + 1) * LANES]


def _load_rows_from_tiles(ref, n, first_row=0):
    return jnp.concatenate([ref[pl.ds(first_row * SUBLANES + c, n, stride=SUBLANES), :] for c in range(SUBLANES)],
                           axis=1)


def _tile_of_row(ref, r):
    return ref.at[pl.ds(pl.multiple_of(r * SUBLANES, SUBLANES), SUBLANES)]


def _moe_plan(counts, total_assign):
    nb_max = -(-total_assign // MOE_ROWS) + N_EXPERTS
    nb_max += -nb_max % EXPERT_BLOCKS_PER_STEP
    padded = (counts + MOE_ROWS - 1) // MOE_ROWS * MOE_ROWS
    pend = jnp.cumsum(padded)
    pstart = (pend - padded).astype(jnp.int32)
    block_start = jnp.arange(nb_max, dtype=jnp.int32) * MOE_ROWS
    n_ended = jnp.sum((pend[None, :] <= block_start[:, None]).astype(jnp.int32), axis=1)
    block_e = jnp.minimum(n_ended, N_EXPERTS - 1).astype(jnp.int32)
    n_used = (pend[-1] // MOE_ROWS).astype(jnp.int32).reshape(1)
    tail_start = jnp.where(padded > 0, pend - MOE_ROWS, -1)
    spare = pend[-1] + jnp.arange(N_EXPERTS + EXPERT_BLOCKS_PER_STEP, dtype=pend.dtype) * MOE_ROWS
    spare = jnp.where(spare < nb_max * MOE_ROWS, spare, -1)
    zero_blocks = jnp.concatenate([tail_start, spare]).astype(jnp.int32)
    return pstart, block_e, n_used, nb_max, zero_blocks


ROW_UNROLL = 8


def _slot_ids(meta, pstart):
    experts = meta[0:TOP_K]
    ranks = meta[TOP_K:2 * TOP_K]
    onehot = experts[..., None] == jnp.arange(N_EXPERTS, dtype=jnp.int32)
    return ranks + jnp.sum(jnp.where(onehot, pstart, 0), axis=-1)


def _for_row_groups(tm, fn, read=None):
    def group(g, c):
        t0 = pl.multiple_of(g * ROW_UNROLL, ROW_UNROLL)
        items = [(t0 + r, k) for r in range(ROW_UNROLL) for k in range(TOP_K)]
        if read is None:
            for t, k in items:
                fn(t, k)
        else:
            vals = [read(t, k) for t, k in items]
            for (t, k), v in zip(items, vals):
                fn(t, k, v)
        return c

    lax.fori_loop(0, tm // ROW_UNROLL, group, 0)


def _dispatch_body(nsteps, tail_ref, s0_ref, s1_ref, h_ref, xs_ref, pk_ref, zero_ref, sems, zsem):
    i = pl.program_id(0)
    tm = h_ref.shape[0]
    slot_refs = (s0_ref, s1_ref)
    cur = i % 2
    blk_tiles = MOE_ROWS * SUBLANES

    @pl.when(i == 0)
    def _():
        zero_ref[...] = jnp.zeros_like(zero_ref)
        for e in range(tail_ref.shape[0]):
            @pl.when(tail_ref[e] >= 0)
            def _():
                start = pl.multiple_of(tail_ref[e] * SUBLANES, blk_tiles)
                pltpu.make_async_copy(zero_ref, xs_ref.at[pl.ds(start, blk_tiles)], zsem).start()
        for e in range(tail_ref.shape[0]):
            @pl.when(tail_ref[e] >= 0)
            def _():
                pltpu.make_async_copy(zero_ref, xs_ref.at[pl.ds(0, blk_tiles)], zsem).wait()

    _store_rows_as_tiles(pk_ref.at[cur], h_ref[...])

    def send(t, k, slot):
        pltpu.make_async_copy(_tile_of_row(pk_ref.at[cur], t), _tile_of_row(xs_ref, slot),
                              sems.at[cur]).start(priority=k)

    def wait_buffer(buf):
        _for_row_groups(tm, lambda t, k: pltpu.make_async_copy(
            _tile_of_row(pk_ref.at[buf], t), _tile_of_row(xs_ref, 0), sems.at[buf]).wait())

    _for_row_groups(tm, send, read=lambda t, k: slot_refs[k][0, t])

    @pl.when(i > 0)
    def _():
        wait_buffer(1 - cur)

    @pl.when(i == nsteps - 1)
    def _():
        wait_buffer(cur)


def _dispatch(h1, slot_ids, tail_start, nslots, tm, total):
    assert total % tm == 0 and total <= h1.shape[0] and tm % ROW_UNROLL == 0
    slot_spec = pl.BlockSpec((1, tm), lambda i, tl: (0, i), memory_space=pltpu.SMEM)
    return pl.pallas_call(
        functools.partial(_dispatch_body, total // tm),
        grid_spec=pltpu.PrefetchScalarGridSpec(
            num_scalar_prefetch=1,
            grid=(total // tm,),
            in_specs=[slot_spec, slot_spec, pl.BlockSpec((tm, D_MODEL), lambda i, tl: (i, 0))],
            out_specs=pl.BlockSpec(memory_space=pl.ANY),
            scratch_shapes=[pltpu.VMEM((2, tm * SUBLANES, LANES), F32), pltpu.VMEM((MOE_ROWS * SUBLANES, LANES), F32),
                            pltpu.SemaphoreType.DMA((2,)), pltpu.SemaphoreType.DMA(())],
        ),
        out_shape=jax.ShapeDtypeStruct((nslots * SUBLANES, LANES), F32),
        compiler_params=_cparams(("arbitrary",)),
        name="moe_dispatch",
    )(tail_start, slot_ids[0:1], slot_ids[1:2], h1)


def _expert_schedule(block_e, n_used):
    nb = block_e.shape[0]
    idx = jnp.arange(nb, dtype=jnp.int32)
    first = (idx < n_used[0]) & ((idx == 0) | (block_e != jnp.roll(block_e, 1)))
    parity = (jnp.cumsum(first.astype(jnp.int32)) - 1) % 2
    pos = jnp.where(first, idx, nb)
    at_or_after = jnp.flip(lax.cummin(jnp.flip(pos)))
    nxt = jnp.concatenate([at_or_after[1:], jnp.full((1,), nb, jnp.int32)])
    nexte = jnp.where(nxt < nb, block_e[jnp.minimum(nxt, nb - 1)], -1)
    return first.astype(jnp.int32), nexte.astype(jnp.int32), parity.astype(jnp.int32)


def _expert_body(be_ref, nu_ref, first_ref, nexte_ref, par_ref, xs_ref, wg_hbm, wu_hbm, wd_hbm, ys_ref,
                 wgf_ref, wuf_ref, wdf_ref, wgb_ref, wub_ref, wdb_ref, sems):
    i = pl.program_id(0)
    hbm = (wg_hbm, wu_hbm, wd_hbm)
    stage = (wgf_ref, wuf_ref, wdf_ref)

    def weight_copies(e, buf):
        return [pltpu.make_async_copy(hbm[w].at[e], stage[w].at[buf], sems.at[buf, w]) for w in range(3)]

    def prepare(b):
        @pl.when(first_ref[b] == 1)
        def _():
            buf = par_ref[b]

            @pl.when(b == 0)
            def _():
                for c in weight_copies(be_ref[0], buf):
                    c.start()

            for c in weight_copies(be_ref[b], buf):
                c.wait()

            @pl.when(nexte_ref[b] >= 0)
            def _():
                for c in weight_copies(nexte_ref[b], 1 - buf):
                    c.start()

            wgb_ref[buf] = wgf_ref[buf].astype(BF16)
            wub_ref[buf] = wuf_ref[buf].astype(BF16)
            wdb_ref[buf] = wdf_ref[buf].astype(BF16)

    def ffn(b, sub):
        buf = par_ref[b]
        x = _load_rows_from_tiles(xs_ref, MOE_ROWS, sub * MOE_ROWS).astype(BF16)
        yield
        g = jnp.dot(x, wgb_ref[buf], preferred_element_type=F32)
        u = jnp.dot(x, wub_ref[buf], preferred_element_type=F32)
        yield
        y = jnp.dot(((g * _sigmoid(g)) * u).astype(BF16), wdb_ref[buf], preferred_element_type=F32)
        yield
        _store_rows_as_tiles(ys_ref, y, sub * MOE_ROWS)

    def run(gens):
        while gens:
            gens = [g for g in gens if next(g, True) is None]

    def zero(sub):
        ys_ref[pl.ds(sub * MOE_ROWS * SUBLANES, MOE_ROWS * SUBLANES), :] = jnp.zeros(
            (MOE_ROWS * SUBLANES, LANES), F32)

    blocks = [EXPERT_BLOCKS_PER_STEP * i + sub for sub in range(EXPERT_BLOCKS_PER_STEP)]
    for b in blocks:
        prepare(b)
    n_live = jnp.clip(nu_ref[0] - blocks[0], 0, EXPERT_BLOCKS_PER_STEP)
    for live in range(EXPERT_BLOCKS_PER_STEP + 1):
        @pl.when(n_live == live)
        def _():
            run([ffn(blocks[sub], sub) for sub in range(live)])
            for sub in range(live, EXPERT_BLOCKS_PER_STEP):
                zero(sub)


def _experts(xs, block_e, n_used, w_gate, w_up, w_down, nb_max):
    assert nb_max % EXPERT_BLOCKS_PER_STEP == 0
    first, nexte, parity = _expert_schedule(block_e, n_used)
    rows = lambda i, *_: (i, 0)
    any_spec = pl.BlockSpec(memory_space=pl.ANY)
    blk = (EXPERT_BLOCKS_PER_STEP * MOE_ROWS * SUBLANES, LANES)
    return pl.pallas_call(
        _expert_body,
        grid_spec=pltpu.PrefetchScalarGridSpec(
            num_scalar_prefetch=5,
            grid=(nb_max // EXPERT_BLOCKS_PER_STEP,),
            in_specs=[pl.BlockSpec(blk, rows), any_spec, any_spec, any_spec],
            out_specs=pl.BlockSpec(blk, rows),
            scratch_shapes=[
                pltpu.VMEM((2, D_MODEL, D_EXPERT), F32), pltpu.VMEM((2, D_MODEL, D_EXPERT), F32),
                pltpu.VMEM((2, D_EXPERT, D_MODEL), F32),
                pltpu.VMEM((2, D_MODEL, D_EXPERT), BF16), pltpu.VMEM((2, D_MODEL, D_EXPERT), BF16),
                pltpu.VMEM((2, D_EXPERT, D_MODEL), BF16),
                pltpu.SemaphoreType.DMA((2, 3)),
            ],
        ),
        out_shape=jax.ShapeDtypeStruct(xs.shape, F32),
        compiler_params=_cparams(("arbitrary",)),
        name="moe_experts",
    )(block_e, n_used, first, nexte, parity, xs, w_gate, w_up, w_down)


def _combine_body(nsteps, s0_ref, s1_ref, n0_ref, n1_ref, h_ref, w_ref, g2_ref, b2_ref, ys_ref, o_ref, buf_ref, sems):
    i = pl.program_id(0)
    tm = o_ref.shape[0]
    cur = i % 2

    def fetch(slot_refs, buf):
        _for_row_groups(tm, lambda t, k, slot: pltpu.make_async_copy(
            _tile_of_row(ys_ref, slot), _tile_of_row(buf_ref.at[buf, k], t),
            sems.at[buf]).start(priority=k), read=lambda t, k: slot_refs[k][0, t])

    @pl.when(i == 0)
    def _():
        fetch((s0_ref, s1_ref), cur)

    @pl.when(i + 1 < nsteps)
    def _():
        fetch((n0_ref, n1_ref), 1 - cur)

    _for_row_groups(tm, lambda t, k: pltpu.make_async_copy(
        _tile_of_row(ys_ref, 0), _tile_of_row(buf_ref.at[cur, k], t), sems.at[cur]).wait())
    w = w_ref[...]
    f = (w[:, 0:1] * _load_rows_from_tiles(buf_ref.at[cur, 0], tm)
         + w[:, 1:2] * _load_rows_from_tiles(buf_ref.at[cur, 1], tm))
    o_ref[...] = _layer_norm_rows(ALPHA * h_ref[...] + f, g2_ref[...], b2_ref[...])


def _combine(h1, wcol, slot_ids, ys, ln2_g, ln2_b, tm, row0, nrows):
    assert nrows % tm == 0 and row0 % tm == 0 and tm % ROW_UNROLL == 0
    off = row0 // tm
    nsteps = nrows // tm
    slot_spec = pl.BlockSpec((1, tm), lambda i: (0, i + off), memory_space=pltpu.SMEM)
    next_spec = pl.BlockSpec((1, tm), lambda i: (0, jnp.minimum(i + 1, nsteps - 1) + off), memory_space=pltpu.SMEM)
    return pl.pallas_call(
        functools.partial(_combine_body, nsteps),
        grid=(nsteps,),
        in_specs=[
            slot_spec, slot_spec, next_spec, next_spec,
            pl.BlockSpec((tm, D_MODEL), lambda i: (i + off, 0)),
            pl.BlockSpec((tm, LANES), lambda i: (i + off, 0)),
            pl.BlockSpec((1, D_MODEL), lambda i: (0, 0)),
            pl.BlockSpec((1, D_MODEL), lambda i: (0, 0)),
            pl.BlockSpec(memory_space=pl.ANY),
        ],
        out_specs=pl.BlockSpec((tm, D_MODEL), lambda i: (i, 0)),
        scratch_shapes=[pltpu.VMEM((2, TOP_K, tm * SUBLANES, LANES), F32), pltpu.SemaphoreType.DMA((2,))],
        out_shape=jax.ShapeDtypeStruct((nrows, D_MODEL), F32),
        compiler_params=_cparams(("arbitrary",)),
        name="moe_combine",
    )(slot_ids[0:1], slot_ids[1:2], slot_ids[0:1], slot_ids[1:2], h1, wcol, ln2_g.reshape(1, -1),
      ln2_b.reshape(1, -1), ys)


PROJ_ROWS = 512
POST_ROWS = 512
DISPATCH_ROWS = 384
COMBINE_ROWS = 256


def kernel(x_prompt, x_sample, state_gla, cache_swa_k, cache_swa_v, meta_tokens, ln_emb_g, ln_emb_b, rel_bias, w_in,
           gk_up, gk_bias, gla_norm_g, sinks, w_out, ln1_g, ln1_b, w_router_group, b_router_group, w_router_expert,
           b_router_expert, w_gate, w_up, w_down, ln2_g, ln2_b):
    nbatch, seq, d = x_prompt.shape
    nseq = x_sample.shape[0]
    assert w_in.shape[0] == DEPTH == 1 and d == D_MODEL and x_sample.shape[1] == 1
    assert seq % BLK == 0 and nseq == BLK and meta_tokens.shape[0] == N_META
    nblk = seq // BLK
    n_prompt = nbatch * seq
    total = n_prompt + nseq
    kvw = ATT_KV_HEADS * HEAD_DIM

    xp = x_prompt.reshape(n_prompt, d)
    xs = x_sample.reshape(nseq, d)
    extra = jnp.concatenate([xs, jnp.zeros((BLK - N_META, d), xs.dtype), meta_tokens.astype(xs.dtype)], axis=0)
    w_cat = _prep_w_in(w_in[0])
    pp = _ln_proj(xp, ln_emb_g, ln_emb_b, w_cat, gk_up[0], gk_bias[0], PROJ_ROWS)
    pe = _ln_proj(extra, ln_emb_g, ln_emb_b, w_cat, gk_up[0], gk_bias[0], 2 * BLK)

    yg, gla_p = _gla_prompt(pp, pe, gla_norm_g[0], nbatch, nblk)
    yg_s, gla_s = _gla_step(pe, gla_norm_g[0], state_gla[0])
    mg, k_win, v_win = _swa_prompt(pp, pe, yg, rel_bias, sinks[0], nbatch, nblk)
    mg_s, k_s, v_s = _swa_step(pe, yg_s, cache_swa_k[0].reshape(nseq, WINDOW, kvw),
                               cache_swa_v[0].reshape(nseq, WINDOW, kvw), rel_bias, sinks[0])

    prm = _prep_post_params(ln_emb_g, ln_emb_b, w_out[0], ln1_g[0], ln1_b[0], w_router_group[0], b_router_group[0],
                            w_router_expert[0], b_router_expert[0])
    carry0 = jnp.zeros((N_EXPERTS, LANES), F32)
    rows_alloc = n_prompt + POST_ROWS
    h1, meta, wcol, carry1 = _post(mg, xp, prm, POST_ROWS, 0, rows_alloc, carry0, zero_tail=True)
    h1, meta, wcol, carry2 = _post(mg_s, xs, prm, nseq, n_prompt, rows_alloc, carry1, prev=(h1, meta, wcol))

    counts = carry2[:, 0].astype(jnp.int32)
    pstart, block_e, n_used, nb_max, tail_start = _moe_plan(counts, TOP_K * total)
    slot_ids = _slot_ids(meta, pstart)
    xs_sorted = _dispatch(h1, slot_ids, tail_start, nb_max * MOE_ROWS, DISPATCH_ROWS, total)
    ys = _experts(xs_sorted, block_e, n_used, w_gate[0], w_up[0], w_down[0], nb_max)
    y_p = _combine(h1, wcol, slot_ids, ys, ln2_g[0], ln2_b[0], COMBINE_ROWS, 0, n_prompt)
    y_s = _combine(h1, wcol, slot_ids, ys, ln2_g[0], ln2_b[0], nseq, n_prompt, nseq)

    kv_shape = (1, nbatch, WINDOW, ATT_KV_HEADS, HEAD_DIM)
    k_p = k_win.reshape(kv_shape)
    v_p = v_win.reshape(kv_shape)
    return (y_p.reshape(nbatch, seq, d), y_s.reshape(nseq, 1, d), gla_p[None], k_p, v_p, gla_s[None],
            k_s.reshape(cache_swa_k.shape), v_s.reshape(cache_swa_v.shape))
```
